```python
import jax, jax.numpy as jnp
from jax import lax
import numpy as np

D_MODEL = 2048
BATCH = 1
SEQ = 8192
DEPTH = 1

D_MIX = D_MODEL
D_HGRN = D_MIX // 2
D_GLA = D_MIX - D_HGRN
HGRN_DK = 128
HGRN_HEADS = D_HGRN // HGRN_DK
HGRN_DV = D_HGRN // HGRN_HEADS
GLA_HEADS = 4
GLA_DK_TOTAL = D_GLA // 2
GLA_DK = GLA_DK_TOTAL // GLA_HEADS
GLA_DV = D_GLA // GLA_HEADS
GLA_GATE_RANK = 16
GLA_GATE_TAU = 16.0
PLE_DIM = 256
CHUNK = 64
EPS = 1e-6

IN_SPLITS = [D_HGRN, D_HGRN, D_HGRN, D_HGRN,
             GLA_DK_TOTAL, GLA_DK_TOTAL, D_GLA, D_GLA,
             GLA_GATE_RANK]
D_IN = sum(IN_SPLITS)

kernel_name = "hgrn2_gla_parallel_heads_ple"


def rms_norm(x, w):
    xf = x.astype(jnp.float32)
    y = xf * lax.rsqrt(jnp.mean(xf * xf, axis=-1, keepdims=True) + EPS)
    return (y * w.astype(jnp.float32)).astype(x.dtype)


def chunk_gated_linear_attention(q, k, v, log_g, scale):
    B, T, H, dk = q.shape
    dv = v.shape[-1]
    n = T // CHUNK

    def to_chunks(a):
        a = a.astype(jnp.float32).reshape(B, n, CHUNK, H, a.shape[-1])
        return jnp.transpose(a, (1, 0, 3, 2, 4))

    qc, kc, vc, gc = to_chunks(q * scale), to_chunks(k), to_chunks(v), to_chunks(log_g)
    causal = jnp.tril(jnp.ones((CHUNK, CHUNK), dtype=bool))[:, :, None]

    def step(S, inp):
        q_c, k_c, v_c, g_c = inp
        b = jnp.cumsum(g_c, axis=2)
        diff = b[:, :, :, None, :] - b[:, :, None, :, :]
        decay = jnp.where(causal, jnp.exp(jnp.where(causal, diff, 0.0)), 0.0)
        attn = jnp.einsum('bhik,bhjk,bhijk->bhij', q_c, k_c, decay)
        o = (jnp.einsum('bhij,bhjv->bhiv', attn, v_c)
             + jnp.einsum('bhik,bhkv->bhiv', q_c * jnp.exp(b), S))
        b_last = b[:, :, -1:, :]
        S = (jnp.exp(b_last[:, :, 0, :])[..., None] * S
             + jnp.einsum('bhjk,bhjv->bhkv', k_c * jnp.exp(b_last - b), v_c))
        return S, o

    S0 = jnp.zeros((B, H, dk, dv), jnp.float32)
    _, o = lax.scan(step, S0, (qc, kc, vc, gc))
    return jnp.transpose(o, (1, 0, 3, 2, 4)).reshape(B, T, H, dv)


def head_readout(o, norm_w, gate, n_heads, dv, dtype):
    B, T = o.shape[:2]
    o = rms_norm(o, norm_w.reshape(n_heads, dv))
    g = jax.nn.silu(gate.astype(jnp.float32)).reshape(B, T, n_heads, dv)
    return (o * g).reshape(B, T, n_heads * dv).astype(dtype)


def setup_inputs(seed: int = 0) -> dict:
    key = jax.random.key(seed)
    ks = jax.random.split(key, 16)
    f32 = jnp.float32
    nrm = lambda k, shape, s: (jax.random.normal(k, shape, f32) * s)
    gain = lambda k, shape: 1.0 + 0.02 * jax.random.normal(k, shape, f32)
    return {
        "x": jax.random.normal(ks[0], (BATCH, SEQ, D_MODEL), f32),
        "p": jax.random.normal(ks[1], (DEPTH, BATCH, SEQ, PLE_DIM), f32),
        "norm_mix_w": gain(ks[2], (DEPTH, D_MODEL)),
        "w_in": nrm(ks[3], (DEPTH, D_MODEL, D_IN), D_MODEL ** -0.5),
        "hgrn_lb": nrm(ks[4], (DEPTH + 1, D_HGRN), 0.1),
        "gla_gate_w2": nrm(ks[5], (DEPTH, GLA_GATE_RANK, GLA_DK_TOTAL), GLA_GATE_RANK ** -0.5),
        "gla_gate_b": nrm(ks[6], (DEPTH, GLA_DK_TOTAL), 0.01),
        "hgrn_norm_w": gain(ks[7], (DEPTH, D_HGRN)),
        "gla_norm_w": gain(ks[8], (DEPTH, D_GLA)),
        "w_out": nrm(ks[9], (DEPTH, D_MIX, D_MODEL), D_MIX ** -0.5),
        "ple_proj": nrm(ks[10], (DEPTH, PLE_DIM, D_MODEL), PLE_DIM ** -0.5),
        "ple_norm_w": gain(ks[11], (DEPTH, D_MODEL)),
        "ple_gate_norm_w": gain(ks[12], (DEPTH, D_MODEL)),
        "ple_gate_w": nrm(ks[13], (DEPTH, D_MODEL, D_MODEL), D_MODEL ** -0.5),
        "ple_gate_b": nrm(ks[14], (DEPTH, D_MODEL), 0.01),
        "final_norm_w": gain(ks[15], (D_MODEL,)),
    }


def reference(x, p, norm_mix_w, w_in, hgrn_lb, gla_gate_w2, gla_gate_b, hgrn_norm_w,
              gla_norm_w, w_out, ple_proj, ple_norm_w, ple_gate_norm_w, ple_gate_w,
              ple_gate_b, final_norm_w):
    B, T, _ = x.shape
    dtype = x.dtype
    lb_all = jnp.cumsum(jax.nn.softmax(hgrn_lb.astype(jnp.float32), axis=0), axis=0)
    offsets = np.cumsum(IN_SPLITS)[:-1].tolist()
    h = x
    for i in range(DEPTH):
        u = rms_norm(h, norm_mix_w[i])
        proj = u @ w_in[i]
        hq, hf, hi, hz, gq, gk, gv, gz, glr = jnp.split(proj, offsets, axis=-1)

        lb = lb_all[i]
        f = lb + (1.0 - lb) * jax.nn.sigmoid(hf.astype(jnp.float32))
        hshape = (B, T, HGRN_HEADS, HGRN_DK)
        o_h = chunk_gated_linear_attention(
            jax.nn.silu(hq.astype(jnp.float32)).reshape(hshape),
            (1.0 - f).reshape(hshape),
            hi.reshape(B, T, HGRN_HEADS, HGRN_DV),
            jnp.log(f).reshape(hshape),
            HGRN_DK ** -0.5)
        y_h = head_readout(o_h, hgrn_norm_w[i], hz, HGRN_HEADS, HGRN_DV, dtype)

        gate_logit = glr.astype(jnp.float32) @ gla_gate_w2[i].astype(jnp.float32) + gla_gate_b[i]
        log_a = jax.nn.log_sigmoid(gate_logit) / GLA_GATE_TAU
        gshape = (B, T, GLA_HEADS, GLA_DK)
        o_g = chunk_gated_linear_attention(
            gq.reshape(gshape), gk.reshape(gshape),
            gv.reshape(B, T, GLA_HEADS, GLA_DV), log_a.reshape(gshape),
            GLA_DK ** -0.5)
        y_g = head_readout(o_g, gla_norm_w[i], gz, GLA_HEADS, GLA_DV, dtype)

        y = jnp.concatenate([y_h, y_g], axis=-1) @ w_out[i]
        h = h + y.astype(dtype)

        e = rms_norm(p[i] @ ple_proj[i], ple_norm_w[i])
        gate = jax.nn.sigmoid(rms_norm(h, ple_gate_norm_w[i]) @ ple_gate_w[i] + ple_gate_b[i])
        h = h + (gate * e).astype(dtype)
    return rms_norm(h, final_norm_w)
```

```python
import functools

import jax
import jax.numpy as jnp
from jax import lax
from jax.experimental import pallas as pl
from jax.experimental.pallas import tpu as pltpu

D_MODEL = 2048
D_HGRN = 1024
HGRN_HEADS = 8
HGRN_DK = 128
HGRN_DV = 128
GLA_HEADS = 4
GLA_DK = 128
GLA_DV = 256
GLA_DK_TOTAL = 512
D_GLA = 1024
GLA_GATE_RANK = 16
GLA_GATE_TAU = 16.0
PLE_DIM = 256
EPS = 1e-6

LANE = 128
D_MAIN = 4 * D_HGRN + 2 * GLA_DK_TOTAL + 2 * D_GLA
D_IN_PAD = D_MAIN + LANE
CHUNK = 64
VMEM_LIMIT = 48 * 1024 * 1024

_NT = (((1,), (1,)), ((), ()))
_TN = (((0,), (0,)), ((), ()))


def _sigmoid(x):
    return 1.0 / (1.0 + jnp.exp(-x))


def _rms(x, w):
    return x * lax.rsqrt(jnp.mean(x * x, axis=-1, keepdims=True) + EPS) * w


def _inproj_kernel(x_ref, nw_ref, w_ref, o_ref, u_ref):
    @pl.when(pl.program_id(1) == 0)
    def _():
        u_ref[...] = _rms(x_ref[...], nw_ref[...]).astype(jnp.bfloat16)

    o_ref[...] = jnp.dot(u_ref[...], w_ref[...], preferred_element_type=jnp.float32)


def _inproj(x2, norm_w, w_bf, tm=1024, tn=384):
    T, D = x2.shape
    N = w_bf.shape[1]
    return pl.pallas_call(
        _inproj_kernel,
        grid=(T // tm, N // tn),
        in_specs=[
            pl.BlockSpec((tm, D), lambda i, j: (i, 0)),
            pl.BlockSpec((1, D), lambda i, j: (0, 0)),
            pl.BlockSpec((D, tn), lambda i, j: (0, j)),
        ],
        out_specs=pl.BlockSpec((tm, tn), lambda i, j: (i, j)),
        out_shape=jax.ShapeDtypeStruct((T, N), jnp.float32),
        scratch_shapes=[pltpu.VMEM((tm, D), jnp.bfloat16)],
        compiler_params=pltpu.CompilerParams(
            dimension_semantics=("parallel", "arbitrary"),
            vmem_limit_bytes=VMEM_LIMIT),
        name="inproj",
    )(x2, norm_w, w_bf)


def _chunk_step(q, k, v, g, st_ref, tri, causal):
    bf = jnp.bfloat16
    g_hi = g.astype(bf)
    g_lo = (g - g_hi.astype(jnp.float32)).astype(bf)
    b = (jnp.dot(tri, g_hi, preferred_element_type=jnp.float32)
         + jnp.dot(tri, g_lo, preferred_element_type=jnp.float32))
    eb = jnp.exp(b)
    qh = (q * eb).astype(bf)
    kt = (k * jnp.exp(-b)).astype(bf)
    vb = v.astype(bf)
    st = st_ref[...]
    attn = lax.dot_general(qh, kt, _NT, preferred_element_type=jnp.float32)
    attn = jnp.where(causal, attn, 0.0).astype(bf)
    o = (jnp.dot(attn, vb, preferred_element_type=jnp.float32)
         + lax.dot_general(qh, st.astype(bf), _NT, preferred_element_type=jnp.float32))
    upd = lax.dot_general(vb, kt, _TN, preferred_element_type=jnp.float32)
    st_ref[...] = (st + upd) * eb[-1:, :]
    return o


def _tri_masks(C):
    row = lax.broadcasted_iota(jnp.int32, (C, C), 0)
    col = lax.broadcasted_iota(jnp.int32, (C, C), 1)
    causal = row >= col
    return causal.astype(jnp.bfloat16), causal


def _readout(o, nw, z):
    return _rms(o, nw) * (z * _sigmoid(z))


def _hgrn_kernel(lb_ref, hq_ref, hf_ref, hi_ref, hz_ref, nw_ref, y_ref, st_ref, *, n_chunks):
    @pl.when(pl.program_id(1) == 0)
    def _():
        st_ref[...] = jnp.zeros_like(st_ref)

    lbp = lb_ref[...]
    e = jnp.exp(lbp - jnp.max(lbp, axis=0, keepdims=True))
    lb = e[0:1, :] / jnp.sum(e, axis=0, keepdims=True)
    tri, causal = _tri_masks(CHUNK)
    scale = HGRN_DK ** -0.5
    for c in range(n_chunks):
        sl = pl.ds(c * CHUNK, CHUNK)
        hq = hq_ref[sl, :]
        f = lb + (1.0 - lb) * _sigmoid(hf_ref[sl, :])
        q = hq * _sigmoid(hq) * scale
        o = _chunk_step(q, 1.0 - f, hi_ref[sl, :], jnp.log(f), st_ref, tri, causal)
        y_ref[sl, :] = _readout(o, nw_ref[...], hz_ref[sl, :]).astype(y_ref.dtype)


def _gla_kernel(gq_ref, gk_ref, gv_ref, gz_ref, lr_ref, w2_ref, gb_ref, nw_ref, y_ref, st_ref,
                *, n_chunks):
    @pl.when(pl.program_id(1) == 0)
    def _():
        st_ref[...] = jnp.zeros_like(st_ref)

    tri, causal = _tri_masks(CHUNK)
    scale = GLA_DK ** -0.5
    w2 = w2_ref[...].astype(jnp.bfloat16)
    for c in range(n_chunks):
        sl = pl.ds(c * CHUNK, CHUNK)
        logit = jnp.dot(lr_ref[sl, :].astype(jnp.bfloat16), w2,
                        preferred_element_type=jnp.float32) + gb_ref[...]
        log_sig = jnp.minimum(logit, 0.0) - jnp.log(1.0 + jnp.exp(-jnp.abs(logit)))
        g = log_sig / GLA_GATE_TAU
        o = _chunk_step(gq_ref[sl, :] * scale, gk_ref[sl, :], gv_ref[sl, :], g, st_ref, tri, causal)
        y_ref[sl, :] = _readout(o, nw_ref[...], gz_ref[sl, :]).astype(y_ref.dtype)


def _hgrn(proj, hgrn_lb, norm_w, tt=512):
    T = proj.shape[0]
    col = lambda base: (lambda h, t: (t, base + h))
    blk = lambda base: pl.BlockSpec((tt, HGRN_DK), col(base))
    nb = D_HGRN // HGRN_DK
    return pl.pallas_call(
        functools.partial(_hgrn_kernel, n_chunks=tt // CHUNK),
        grid=(HGRN_HEADS, T // tt),
        in_specs=[
            pl.BlockSpec((2, HGRN_DK), lambda h, t: (0, h)),
            blk(0), blk(nb), blk(2 * nb), blk(3 * nb),
            pl.BlockSpec((1, HGRN_DV), lambda h, t: (0, h)),
        ],
        out_specs=pl.BlockSpec((tt, HGRN_DV), lambda h, t: (t, h)),
        out_shape=jax.ShapeDtypeStruct((T, D_HGRN), jnp.bfloat16),
        scratch_shapes=[pltpu.VMEM((HGRN_DV, HGRN_DK), jnp.float32)],
        compiler_params=pltpu.CompilerParams(
            dimension_semantics=("parallel", "arbitrary"),
            vmem_limit_bytes=VMEM_LIMIT),
        name="hgrn_scan",
    )(hgrn_lb, proj, proj, proj, proj, norm_w)


def _gla(proj, w2_pad, gate_b, norm_w, tt=512):
    T = proj.shape[0]
    q0 = 4 * D_HGRN // GLA_DK
    k0 = q0 + GLA_HEADS
    v0 = (4 * D_HGRN + 2 * GLA_DK_TOTAL) // GLA_DV
    z0 = v0 + GLA_HEADS
    lr0 = D_MAIN // LANE
    return pl.pallas_call(
        functools.partial(_gla_kernel, n_chunks=tt // CHUNK),
        grid=(GLA_HEADS, T // tt),
        in_specs=[
            pl.BlockSpec((tt, GLA_DK), lambda h, t: (t, q0 + h)),
            pl.BlockSpec((tt, GLA_DK), lambda h, t: (t, k0 + h)),
            pl.BlockSpec((tt, GLA_DV), lambda h, t: (t, v0 + h)),
            pl.BlockSpec((tt, GLA_DV), lambda h, t: (t, z0 + h)),
            pl.BlockSpec((tt, LANE), lambda h, t: (t, lr0)),
            pl.BlockSpec((LANE, GLA_DK), lambda h, t: (0, h)),
            pl.BlockSpec((1, GLA_DK), lambda h, t: (0, h)),
            pl.BlockSpec((1, GLA_DV), lambda h, t: (0, h)),
        ],
        out_specs=pl.BlockSpec((tt, GLA_DV), lambda h, t: (t, h)),
        out_shape=jax.ShapeDtypeStruct((T, D_GLA), jnp.bfloat16),
        scratch_shapes=[pltpu.VMEM((GLA_DV, GLA_DK), jnp.float32)],
        compiler_params=pltpu.CompilerParams(
            dimension_semantics=("parallel", "arbitrary"),
            vmem_limit_bytes=VMEM_LIMIT),
        name="gla_scan",
    )(proj, proj, proj, proj, proj, w2_pad, gate_b, norm_w)


def _out_kernel(x_ref, yh_ref, yg_ref, p_ref, woh_ref, wog_ref, pp_ref, pnw_ref, gnw_ref,
                gw_ref, gb_ref, fnw_ref, o_ref):
    f32 = jnp.float32
    h = (x_ref[...]
         + jnp.dot(yh_ref[...], woh_ref[...], preferred_element_type=f32)
         + jnp.dot(yg_ref[...], wog_ref[...], preferred_element_type=f32))
    e = _rms(jnp.dot(p_ref[...].astype(jnp.bfloat16), pp_ref[...], preferred_element_type=f32),
             pnw_ref[...])
    hn = _rms(h, gnw_ref[...]).astype(jnp.bfloat16)
    gate = _sigmoid(jnp.dot(hn, gw_ref[...], preferred_element_type=f32) + gb_ref[...])
    o_ref[...] = _rms(h + gate * e, fnw_ref[...])


def _outproj(x2, yh, yg, p2, woh, wog, pp, pnw, gnw, gw, gb, fnw, tm=256):
    T, D = x2.shape
    row = lambda w: pl.BlockSpec((tm, w), lambda i: (i, 0))
    full = lambda a: pl.BlockSpec(a.shape, lambda i: (0, 0), pipeline_mode=pl.Buffered(1))
    return pl.pallas_call(
        _out_kernel,
        grid=(T // tm,),
        in_specs=[row(D), row(D_HGRN), row(D_GLA), row(PLE_DIM),
                  full(woh), full(wog), full(pp), full(pnw), full(gnw), full(gw), full(gb),
                  full(fnw)],
        out_specs=row(D),
        out_shape=jax.ShapeDtypeStruct((T, D), jnp.float32),
        compiler_params=pltpu.CompilerParams(
            dimension_semantics=("parallel",),
            vmem_limit_bytes=VMEM_LIMIT),
        name="outproj_ple",
    )(x2, yh, yg, p2, woh, wog, pp, pnw, gnw, gw, gb, fnw)


def kernel(x, p, norm_mix_w, w_in, hgrn_lb, gla_gate_w2, gla_gate_b, hgrn_norm_w, gla_norm_w,
           w_out, ple_proj, ple_norm_w, ple_gate_norm_w, ple_gate_w, ple_gate_b, final_norm_w):
    B, T, D = x.shape
    bf = jnp.bfloat16
    x2 = x.reshape(B * T, D)
    p2 = p[0].reshape(B * T, PLE_DIM)
    row = lambda a: a.reshape(1, -1)

    w_in_bf = jnp.pad(w_in[0].astype(bf), ((0, 0), (0, D_IN_PAD - w_in.shape[-1])))
    proj = _inproj(x2, row(norm_mix_w[0]), w_in_bf)

    y_h = _hgrn(proj, hgrn_lb, row(hgrn_norm_w[0]))
    w2_pad = jnp.pad(gla_gate_w2[0], ((0, LANE - GLA_GATE_RANK), (0, 0)))
    y_g = _gla(proj, w2_pad, row(gla_gate_b[0]), row(gla_norm_w[0]))

    w_out_bf = w_out[0].astype(bf)
    out = _outproj(x2, y_h, y_g, p2, w_out_bf[:D_HGRN], w_out_bf[D_HGRN:],
                   ple_proj[0].astype(bf), row(ple_norm_w[0]), row(ple_gate_norm_w[0]),
                   ple_gate_w[0].astype(bf), row(ple_gate_b[0]), row(final_norm_w))
    return out.reshape(B, T, D)
```

```python
import jax
import jax.numpy as jnp
from jax import lax
from jax.experimental import pallas as pl
from jax.experimental.pallas import tpu as pltpu

D_MODEL = 2048
D_HGRN = 1024
HGRN_HEADS = 8
HGRN_DK = 128
HGRN_DV = 128
GLA_HEADS = 4
GLA_DK = 128
GLA_DV = 256
GLA_DK_TOTAL = 512
D_GLA = 1024
GLA_GATE_RANK = 16
GLA_GATE_TAU = 16.0
PLE_DIM = 256
EPS = 1e-6

LANE = 128
D_MAIN = 4 * D_HGRN + 2 * GLA_DK_TOTAL + 2 * D_GLA
D_IN_PAD = D_MAIN + LANE
CHUNK = 64
VMEM_LIMIT = 48 * 1024 * 1024

_NT = (((1,), (1,)), ((), ()))
_TN = (((0,), (0,)), ((), ()))


def _sigmoid(x):
    return 1.0 / (1.0 + jnp.exp(-x))


def _rms(x, w):
    return x * lax.rsqrt(jnp.mean(x * x, axis=-1, keepdims=True) + EPS) * w


def _inproj_kernel(x_ref, nw_ref, w_ref, o_ref, u_ref):
    @pl.when(pl.program_id(1) == 0)
    def _():
        u_ref[...] = _rms(x_ref[...], nw_ref[...]).astype(jnp.bfloat16)

    o_ref[...] = jnp.dot(u_ref[...], w_ref[...], preferred_element_type=jnp.float32)


def _inproj(x2, norm_w, w_bf, tm=1024, tn=384):
    T, D = x2.shape
    N = w_bf.shape[1]
    return pl.pallas_call(
        _inproj_kernel,
        grid=(T // tm, N // tn),
        in_specs=[
            pl.BlockSpec((tm, D), lambda i, j: (i, 0)),
            pl.BlockSpec((1, D), lambda i, j: (0, 0)),
            pl.BlockSpec((D, tn), lambda i, j: (0, j)),
        ],
        out_specs=pl.BlockSpec((tm, tn), lambda i, j: (i, j)),
        out_shape=jax.ShapeDtypeStruct((T, N), jnp.float32),
        scratch_shapes=[pltpu.VMEM((tm, D), jnp.bfloat16)],
        compiler_params=pltpu.CompilerParams(
            dimension_semantics=("parallel", "arbitrary"),
            vmem_limit_bytes=VMEM_LIMIT),
        name="inproj",
    )(x2, norm_w, w_bf)


def _scan_tile(q, k, v, g, st_ref):
    bf = jnp.bfloat16
    f32 = jnp.float32
    n_heads, dv, dk = st_ref.shape
    tt, wk = g.shape
    blk = 2 * CHUNK
    n_blk = tt // blk
    n_chk = tt // CHUNK
    row = lax.broadcasted_iota(jnp.int32, (blk, blk), 0)
    col = lax.broadcasted_iota(jnp.int32, (blk, blk), 1)
    causal = (col <= row) & (col >= (row & -CHUNK))
    tri = causal.astype(bf)

    g_hi = g.astype(bf)
    g_lo = (g - g_hi.astype(f32)).astype(bf)
    g2 = jnp.concatenate([g_hi, g_lo], axis=1)
    b = []
    for r in range(n_blk):
        b2 = jnp.dot(tri, g2[r * blk:(r + 1) * blk], preferred_element_type=f32)
        b.append(b2[:, :wk] + b2[:, wk:])
    b = jnp.concatenate(b, axis=0)
    eb = jnp.exp(b)
    qh = (q * eb).astype(bf)
    kt = (k * jnp.exp(-b)).astype(bf)
    vb = v.astype(bf)

    hk = lambda a, h, lo, hi: a[lo:hi, h * dk:(h + 1) * dk]
    hv = lambda a, h, lo, hi: a[lo:hi, h * dv:(h + 1) * dv]
    heads = range(n_heads)

    attn = [[lax.dot_general(hk(qh, h, r * blk, (r + 1) * blk), hk(kt, h, r * blk, (r + 1) * blk),
                             _NT, preferred_element_type=f32) for r in range(n_blk)] for h in heads]
    upd = [[lax.dot_general(hv(vb, h, c * CHUNK, (c + 1) * CHUNK), hk(kt, h, c * CHUNK, (c + 1) * CHUNK),
                            _TN, preferred_element_type=f32) for c in range(n_chk)] for h in heads]
    intra = [[jnp.dot(jnp.where(causal, attn[h][r], 0.0).astype(bf), hv(vb, h, r * blk, (r + 1) * blk),
                      preferred_element_type=f32) for r in range(n_blk)] for h in heads]
    st_in = []
    for h in heads:
        st = st_ref[h]
        st_in.append([])
        for c in range(n_chk):
            st_in[h].append(st.astype(bf))
            st = (st + upd[h][c]) * hk(eb, h, (c + 1) * CHUNK - 1, (c + 1) * CHUNK)
        st_ref[h] = st
    inter = [[lax.dot_general(hk(qh, h, c * CHUNK, (c + 1) * CHUNK), st_in[h][c], _NT,
                              preferred_element_type=f32) for c in range(n_chk)] for h in heads]
    return jnp.concatenate(
        [jnp.concatenate(intra[h], axis=0) + jnp.concatenate(inter[h], axis=0) for h in heads], axis=1)


def _readout(o, nw, z, n_heads):
    dv = o.shape[1] // n_heads
    on = jnp.concatenate([_rms(o[:, h * dv:(h + 1) * dv], nw[:, h * dv:(h + 1) * dv])
                          for h in range(n_heads)], axis=1)
    return on * (z * _sigmoid(z))


def _hgrn_kernel(lb_ref, hq_ref, hf_ref, hi_ref, hz_ref, nw_ref, y_ref, st_ref):
    @pl.when(pl.program_id(1) == 0)
    def _():
        st_ref[...] = jnp.zeros_like(st_ref)

    lbp = lb_ref[...]
    e = jnp.exp(lbp - jnp.max(lbp, axis=0, keepdims=True))
    lb = e[0:1, :] / jnp.sum(e, axis=0, keepdims=True)
    hq = hq_ref[...]
    f = lb + (1.0 - lb) * _sigmoid(hf_ref[...])
    q = hq * _sigmoid(hq) * (HGRN_DK ** -0.5)
    o = _scan_tile(q, 1.0 - f, hi_ref[...], jnp.log(f), st_ref)
    y_ref[...] = _readout(o, nw_ref[...], hz_ref[...], st_ref.shape[0]).astype(y_ref.dtype)


def _gla_kernel(gq_ref, gk_ref, gv_ref, gz_ref, lr_ref, w2_ref, gb_ref, nw_ref, y_ref, st_ref):
    @pl.when(pl.program_id(1) == 0)
    def _():
        st_ref[...] = jnp.zeros_like(st_ref)

    logit = jnp.dot(lr_ref[...].astype(jnp.bfloat16), w2_ref[...].astype(jnp.bfloat16),
                    preferred_element_type=jnp.float32) + gb_ref[...]
    log_sig = jnp.minimum(logit, 0.0) - jnp.log(1.0 + jnp.exp(-jnp.abs(logit)))
    g = log_sig / GLA_GATE_TAU
    o = _scan_tile(gq_ref[...] * (GLA_DK ** -0.5), gk_ref[...], gv_ref[...], g, st_ref)
    y_ref[...] = _readout(o, nw_ref[...], gz_ref[...], st_ref.shape[0]).astype(y_ref.dtype)


def _hgrn(proj, hgrn_lb, norm_w, tt=512, hp=4):
    T = proj.shape[0]
    w = hp * HGRN_DK
    nb = D_HGRN // w
    blk = lambda sec: pl.BlockSpec((tt, w), lambda h, t: (t, sec * nb + h))
    return pl.pallas_call(
        _hgrn_kernel,
        grid=(HGRN_HEADS // hp, T // tt),
        in_specs=[
            pl.BlockSpec((2, w), lambda h, t: (0, h)),
            blk(0), blk(1), blk(2), blk(3),
            pl.BlockSpec((1, w), lambda h, t: (0, h)),
        ],
        out_specs=pl.BlockSpec((tt, w), lambda h, t: (t, h)),
        out_shape=jax.ShapeDtypeStruct((T, D_HGRN), jnp.bfloat16),
        scratch_shapes=[pltpu.VMEM((hp, HGRN_DV, HGRN_DK), jnp.float32)],
        compiler_params=pltpu.CompilerParams(
            dimension_semantics=("parallel", "arbitrary"),
            vmem_limit_bytes=VMEM_LIMIT),
        name="hgrn_scan",
    )(hgrn_lb, proj, proj, proj, proj, norm_w)


def _gla(proj, w2_pad, gate_b, norm_w, tt=512, hp=4):
    T = proj.shape[0]
    wk, wv = hp * GLA_DK, hp * GLA_DV
    q0 = 4 * D_HGRN // wk
    k0 = q0 + GLA_DK_TOTAL // wk
    v0 = (4 * D_HGRN + 2 * GLA_DK_TOTAL) // wv
    z0 = v0 + D_GLA // wv
    lr0 = D_MAIN // LANE
    return pl.pallas_call(
        _gla_kernel,
        grid=(GLA_HEADS // hp, T // tt),
        in_specs=[
            pl.BlockSpec((tt, wk), lambda h, t: (t, q0 + h)),
            pl.BlockSpec((tt, wk), lambda h, t: (t, k0 + h)),
            pl.BlockSpec((tt, wv), lambda h, t: (t, v0 + h)),
            pl.BlockSpec((tt, wv), lambda h, t: (t, z0 + h)),
            pl.BlockSpec((tt, LANE), lambda h, t: (t, lr0)),
            pl.BlockSpec((LANE, wk), lambda h, t: (0, h)),
            pl.BlockSpec((1, wk), lambda h, t: (0, h)),
            pl.BlockSpec((1, wv), lambda h, t: (0, h)),
        ],
        out_specs=pl.BlockSpec((tt, wv), lambda h, t: (t, h)),
        out_shape=jax.ShapeDtypeStruct((T, D_GLA), jnp.bfloat16),
        scratch_shapes=[pltpu.VMEM((hp, GLA_DV, GLA_DK), jnp.float32)],
        compiler_params=pltpu.CompilerParams(
            dimension_semantics=("parallel", "arbitrary"),
            vmem_limit_bytes=VMEM_LIMIT),
        name="gla_scan",
    )(proj, proj, proj, proj, proj, w2_pad, gate_b, norm_w)


def _out_kernel(x_ref, yh_ref, yg_ref, p_ref, woh_ref, wog_ref, pp_ref, pnw_ref, gnw_ref,
                gw_ref, gb_ref, fnw_ref, o_ref):
    f32 = jnp.float32
    h = (x_ref[...]
         + jnp.dot(yh_ref[...], woh_ref[...], preferred_element_type=f32)
         + jnp.dot(yg_ref[...], wog_ref[...], preferred_element_type=f32))
    e = _rms(jnp.dot(p_ref[...].astype(jnp.bfloat16), pp_ref[...], preferred_element_type=f32),
             pnw_ref[...])
    hn = _rms(h, gnw_ref[...]).astype(jnp.bfloat16)
    gate = _sigmoid(jnp.dot(hn, gw_ref[...], preferred_element_type=f32) + gb_ref[...])
    o_ref[...] = _rms(h + gate * e, fnw_ref[...])


def _outproj(x2, yh, yg, p2, woh, wog, pp, pnw, gnw, gw, gb, fnw, tm=256):
    T, D = x2.shape
    row = lambda w: pl.BlockSpec((tm, w), lambda i: (i, 0))
    full = lambda a: pl.BlockSpec(a.shape, lambda i: (0, 0), pipeline_mode=pl.Buffered(1))
    return pl.pallas_call(
        _out_kernel,
        grid=(T // tm,),
        in_specs=[row(D), row(D_HGRN), row(D_GLA), row(PLE_DIM),
                  full(woh), full(wog), full(pp), full(pnw), full(gnw), full(gw), full(gb),
                  full(fnw)],
        out_specs=row(D),
        out_shape=jax.ShapeDtypeStruct((T, D), jnp.float32),
        compiler_params=pltpu.CompilerParams(
            dimension_semantics=("parallel",),
            vmem_limit_bytes=VMEM_LIMIT),
        name="outproj_ple",
    )(x2, yh, yg, p2, woh, wog, pp, pnw, gnw, gw, gb, fnw)


def kernel(x, p, norm_mix_w, w_in, hgrn_lb, gla_gate_w2, gla_gate_b, hgrn_norm_w, gla_norm_w,
           w_out, ple_proj, ple_norm_w, ple_gate_norm_w, ple_gate_w, ple_gate_b, final_norm_w):
    B, T, D = x.shape
    bf = jnp.bfloat16
    x2 = x.reshape(B * T, D)
    p2 = p[0].reshape(B * T, PLE_DIM)
    row = lambda a: a.reshape(1, -1)

    w_in_bf = jnp.pad(w_in[0].astype(bf), ((0, 0), (0, D_IN_PAD - w_in.shape[-1])))
    proj = _inproj(x2, row(norm_mix_w[0]), w_in_bf)

    y_h = _hgrn(proj, hgrn_lb, row(hgrn_norm_w[0]))
    w2_pad = jnp.pad(gla_gate_w2[0], ((0, LANE - GLA_GATE_RANK), (0, 0)))
    y_g = _gla(proj, w2_pad, row(gla_gate_b[0]), row(gla_norm_w[0]))

    w_out_bf = w_out[0].astype(bf)
    out = _outproj(x2, y_h, y_g, p2, w_out_bf[:D_HGRN], w_out_bf[D_HGRN:],
                   ple_proj[0].astype(bf), row(ple_norm_w[0]), row(ple_gate_norm_w[0]),
                   ple_gate_w[0].astype(bf), row(ple_gate_b[0]), row(final_norm_w))
    return out.reshape(B, T, D)
```

```python
import jax
import jax.numpy as jnp
from jax import lax
from jax.experimental import pallas as pl
from jax.experimental.pallas import tpu as pltpu

D_MODEL = 2048
D_HGRN = 1024
HGRN_HEADS = 8
HGRN_DK = 128
HGRN_DV = 128
GLA_HEADS = 4
GLA_DK = 128
GLA_DV = 256
GLA_DK_TOTAL = 512
D_GLA = 1024
GLA_GATE_RANK = 16
GLA_GATE_TAU = 16.0
PLE_DIM = 256
EPS = 1e-6

LANE = 128
D_MAIN = 4 * D_HGRN + 2 * GLA_DK_TOTAL + 2 * D_GLA
CHUNK = 64
VMEM_LIMIT = 48 * 1024 * 1024

_NT = (((1,), (1,)), ((), ()))
_TN = (((0,), (0,)), ((), ()))


def _sigmoid(x):
    return 1.0 / (1.0 + jnp.exp(-x))


def _rms(x, w):
    return x * lax.rsqrt(jnp.mean(x * x, axis=-1, keepdims=True) + EPS) * w


def _inproj_kernel(x_ref, nw_ref, w_ref, wlr_ref, o_ref, lr_ref, u_ref):
    @pl.when(pl.program_id(1) == 0)
    def _():
        u_ref[...] = _rms(x_ref[...], nw_ref[...]).astype(jnp.bfloat16)
        lr_ref[...] = jnp.dot(u_ref[...], wlr_ref[...], preferred_element_type=jnp.float32)

    o_ref[...] = jnp.dot(u_ref[...], w_ref[...], preferred_element_type=jnp.float32)


def _inproj(x2, norm_w, w_bf, wlr_bf, tm=1024, tn=1024):
    T, D = x2.shape
    return pl.pallas_call(
        _inproj_kernel,
        grid=(T // tm, D_MAIN // tn),
        in_specs=[
            pl.BlockSpec((tm, D), lambda i, j: (i, 0)),
            pl.BlockSpec((1, D), lambda i, j: (0, 0)),
            pl.BlockSpec((D, tn), lambda i, j: (0, j)),
            pl.BlockSpec((D, LANE), lambda i, j: (0, 0)),
        ],
        out_specs=[pl.BlockSpec((tm, tn), lambda i, j: (i, j)),
                   pl.BlockSpec((tm, LANE), lambda i, j: (i, 0))],
        out_shape=[jax.ShapeDtypeStruct((T, D_MAIN), jnp.float32),
                   jax.ShapeDtypeStruct((T, LANE), jnp.float32)],
        scratch_shapes=[pltpu.VMEM((tm, D), jnp.bfloat16)],
        compiler_params=pltpu.CompilerParams(
            dimension_semantics=("parallel", "arbitrary"),
            vmem_limit_bytes=VMEM_LIMIT),
        name="inproj",
    )(x2, norm_w, w_bf, wlr_bf)


def _scan_tile(q, k, v, g, st_ref):
    bf = jnp.bfloat16
    f32 = jnp.float32
    n_heads, dv, dk = st_ref.shape
    tt, wk = g.shape
    blk = 2 * CHUNK
    n_blk = tt // blk
    n_chk = tt // CHUNK
    row = lax.broadcasted_iota(jnp.int32, (blk, blk), 0)
    col = lax.broadcasted_iota(jnp.int32, (blk, blk), 1)
    causal = (col <= row) & (col >= (row & -CHUNK))
    tri = causal.astype(bf)

    g_hi = g.astype(bf)
    g_lo = (g - g_hi.astype(f32)).astype(bf)
    g2 = jnp.concatenate([g_hi, g_lo], axis=1)
    b = []
    for r in range(n_blk):
        b2 = jnp.dot(tri, g2[r * blk:(r + 1) * blk], preferred_element_type=f32)
        b.append(b2[:, :wk] + b2[:, wk:])
    b = jnp.concatenate(b, axis=0)
    eb = jnp.exp(b)
    qh = (q * eb).astype(bf)
    kt = (k * jnp.exp(-b)).astype(bf)
    vb = v.astype(bf)

    hk = lambda a, h, lo, hi: a[lo:hi, h * dk:(h + 1) * dk]
    hv = lambda a, h, lo, hi: a[lo:hi, h * dv:(h + 1) * dv]
    heads = range(n_heads)

    attn = [[lax.dot_general(hk(qh, h, r * blk, (r + 1) * blk), hk(kt, h, r * blk, (r + 1) * blk),
                             _NT, preferred_element_type=f32) for r in range(n_blk)] for h in heads]
    upd = [[lax.dot_general(hv(vb, h, c * CHUNK, (c + 1) * CHUNK), hk(kt, h, c * CHUNK, (c + 1) * CHUNK),
                            _TN, preferred_element_type=f32) for c in range(n_chk)] for h in heads]
    intra = [[jnp.dot(jnp.where(causal, attn[h][r], 0.0).astype(bf), hv(vb, h, r * blk, (r + 1) * blk),
                      preferred_element_type=f32) for r in range(n_blk)] for h in heads]
    st_in = []
    for h in heads:
        st = st_ref[h]
        st_in.append([])
        for c in range(n_chk):
            st_in[h].append(st.astype(bf))
            st = (st + upd[h][c]) * hk(eb, h, (c + 1) * CHUNK - 1, (c + 1) * CHUNK)
        st_ref[h] = st
    inter = [[lax.dot_general(hk(qh, h, c * CHUNK, (c + 1) * CHUNK), st_in[h][c], _NT,
                              preferred_element_type=f32) for c in range(n_chk)] for h in heads]
    return jnp.concatenate(
        [jnp.concatenate(intra[h], axis=0) + jnp.concatenate(inter[h], axis=0) for h in heads], axis=1)


def _readout(o, nw, z, n_heads):
    dv = o.shape[1] // n_heads
    on = jnp.concatenate([_rms(o[:, h * dv:(h + 1) * dv], nw[:, h * dv:(h + 1) * dv])
                          for h in range(n_heads)], axis=1)
    return on * (z * _sigmoid(z))


def _hgrn_kernel(lb_ref, hq_ref, hf_ref, hi_ref, hz_ref, nw_ref, y_ref, st_ref):
    @pl.when(pl.program_id(1) == 0)
    def _():
        st_ref[...] = jnp.zeros_like(st_ref)

    lbp = lb_ref[...]
    e = jnp.exp(lbp - jnp.max(lbp, axis=0, keepdims=True))
    lb = e[0:1, :] / jnp.sum(e, axis=0, keepdims=True)
    hq = hq_ref[...]
    f = lb + (1.0 - lb) * _sigmoid(hf_ref[...])
    q = hq * _sigmoid(hq) * (HGRN_DK ** -0.5)
    o = _scan_tile(q, 1.0 - f, hi_ref[...], jnp.log(f), st_ref)
    y_ref[...] = _readout(o, nw_ref[...], hz_ref[...], st_ref.shape[0]).astype(y_ref.dtype)


def _gla_kernel(gq_ref, gk_ref, gv_ref, gz_ref, lr_ref, w2_ref, gb_ref, nw_ref, y_ref, st_ref):
    @pl.when(pl.program_id(1) == 0)
    def _():
        st_ref[...] = jnp.zeros_like(st_ref)

    logit = jnp.dot(lr_ref[...].astype(jnp.bfloat16), w2_ref[...].astype(jnp.bfloat16),
                    preferred_element_type=jnp.float32) + gb_ref[...]
    log_sig = jnp.minimum(logit, 0.0) - jnp.log(1.0 + jnp.exp(-jnp.abs(logit)))
    g = log_sig / GLA_GATE_TAU
    o = _scan_tile(gq_ref[...] * (GLA_DK ** -0.5), gk_ref[...], gv_ref[...], g, st_ref)
    y_ref[...] = _readout(o, nw_ref[...], gz_ref[...], st_ref.shape[0]).astype(y_ref.dtype)


def _hgrn(proj, hgrn_lb, norm_w, tt=512, hp=4):
    T = proj.shape[0]
    w = hp * HGRN_DK
    nb = D_HGRN // w
    blk = lambda sec: pl.BlockSpec((tt, w), lambda h, t: (t, sec * nb + h))
    return pl.pallas_call(
        _hgrn_kernel,
        grid=(HGRN_HEADS // hp, T // tt),
        in_specs=[
            pl.BlockSpec((2, w), lambda h, t: (0, h)),
            blk(0), blk(1), blk(2), blk(3),
            pl.BlockSpec((1, w), lambda h, t: (0, h)),
        ],
        out_specs=pl.BlockSpec((tt, w), lambda h, t: (t, h)),
        out_shape=jax.ShapeDtypeStruct((T, D_HGRN), jnp.bfloat16),
        scratch_shapes=[pltpu.VMEM((hp, HGRN_DV, HGRN_DK), jnp.float32)],
        compiler_params=pltpu.CompilerParams(
            dimension_semantics=("parallel", "arbitrary"),
            vmem_limit_bytes=VMEM_LIMIT),
        name="hgrn_scan",
    )(hgrn_lb, proj, proj, proj, proj, norm_w)


def _gla(proj, lr, w2_pad, gate_b, norm_w, tt=512, hp=4):
    T = proj.shape[0]
    wk, wv = hp * GLA_DK, hp * GLA_DV
    q0 = 4 * D_HGRN // wk
    k0 = q0 + GLA_DK_TOTAL // wk
    v0 = (4 * D_HGRN + 2 * GLA_DK_TOTAL) // wv
    z0 = v0 + D_GLA // wv
    return pl.pallas_call(
        _gla_kernel,
        grid=(GLA_HEADS // hp, T // tt),
        in_specs=[
            pl.BlockSpec((tt, wk), lambda h, t: (t, q0 + h)),
            pl.BlockSpec((tt, wk), lambda h, t: (t, k0 + h)),
            pl.BlockSpec((tt, wv), lambda h, t: (t, v0 + h)),
            pl.BlockSpec((tt, wv), lambda h, t: (t, z0 + h)),
            pl.BlockSpec((tt, LANE), lambda h, t: (t, 0)),
            pl.BlockSpec((LANE, wk), lambda h, t: (0, h)),
            pl.BlockSpec((1, wk), lambda h, t: (0, h)),
            pl.BlockSpec((1, wv), lambda h, t: (0, h)),
        ],
        out_specs=pl.BlockSpec((tt, wv), lambda h, t: (t, h)),
        out_shape=jax.ShapeDtypeStruct((T, D_GLA), jnp.bfloat16),
        scratch_shapes=[pltpu.VMEM((hp, GLA_DV, GLA_DK), jnp.float32)],
        compiler_params=pltpu.CompilerParams(
            dimension_semantics=("parallel", "arbitrary"),
            vmem_limit_bytes=VMEM_LIMIT),
        name="gla_scan",
    )(proj, proj, proj, proj, lr, w2_pad, gate_b, norm_w)


def _out_kernel(x_ref, yh_ref, yg_ref, p_ref, woh_ref, wog_ref, pp_ref, pnw_ref, gnw_ref,
                gw_ref, gb_ref, fnw_ref, o_ref):
    f32 = jnp.float32
    h = (x_ref[...]
         + jnp.dot(yh_ref[...], woh_ref[...], preferred_element_type=f32)
         + jnp.dot(yg_ref[...], wog_ref[...], preferred_element_type=f32))
    e = _rms(jnp.dot(p_ref[...].astype(jnp.bfloat16), pp_ref[...], preferred_element_type=f32),
             pnw_ref[...])
    hn = _rms(h, gnw_ref[...]).astype(jnp.bfloat16)
    gate = _sigmoid(jnp.dot(hn, gw_ref[...], preferred_element_type=f32) + gb_ref[...])
    o_ref[...] = _rms(h + gate * e, fnw_ref[...])


def _outproj(x2, yh, yg, p2, woh, wog, pp, pnw, gnw, gw, gb, fnw, tm=256):
    T, D = x2.shape
    row = lambda w: pl.BlockSpec((tm, w), lambda i: (i, 0))
    full = lambda a: pl.BlockSpec(a.shape, lambda i: (0, 0), pipeline_mode=pl.Buffered(1))
    return pl.pallas_call(
        _out_kernel,
        grid=(T // tm,),
        in_specs=[row(D), row(D_HGRN), row(D_GLA), row(PLE_DIM),
                  full(woh), full(wog), full(pp), full(pnw), full(gnw), full(gw), full(gb),
                  full(fnw)],
        out_specs=row(D),
        out_shape=jax.ShapeDtypeStruct((T, D), jnp.float32),
        compiler_params=pltpu.CompilerParams(
            dimension_semantics=("parallel",),
            vmem_limit_bytes=VMEM_LIMIT),
        name="outproj_ple",
    )(x2, yh, yg, p2, woh, wog, pp, pnw, gnw, gw, gb, fnw)


def kernel(x, p, norm_mix_w, w_in, hgrn_lb, gla_gate_w2, gla_gate_b, hgrn_norm_w, gla_norm_w,
           w_out, ple_proj, ple_norm_w, ple_gate_norm_w, ple_gate_w, ple_gate_b, final_norm_w):
    B, T, D = x.shape
    bf = jnp.bfloat16
    x2 = x.reshape(B * T, D)
    p2 = p[0].reshape(B * T, PLE_DIM)
    row = lambda a: a.reshape(1, -1)

    wlr_bf = jnp.pad(w_in[0][:, D_MAIN:].astype(bf), ((0, 0), (0, LANE - GLA_GATE_RANK)))
    proj, lr = _inproj(x2, row(norm_mix_w[0]), w_in[0].astype(bf), wlr_bf)

    y_h = _hgrn(proj, hgrn_lb, row(hgrn_norm_w[0]))
    w2_pad = jnp.pad(gla_gate_w2[0], ((0, LANE - GLA_GATE_RANK), (0, 0)))
    y_g = _gla(proj, lr, w2_pad, row(gla_gate_b[0]), row(gla_norm_w[0]))

    w_out_bf = w_out[0].astype(bf)
    out = _outproj(x2, y_h, y_g, p2, w_out_bf[:D_HGRN], w_out_bf[D_HGRN:],
                   ple_proj[0].astype(bf), row(ple_norm_w[0]), row(ple_gate_norm_w[0]),
                   ple_gate_w[0].astype(bf), row(ple_gate_b[0]), row(final_norm_w))
    return out.reshape(B, T, D)
```

```python
import jax
import jax.numpy as jnp
from jax import lax
from jax.experimental import pallas as pl
from jax.experimental.pallas import tpu as pltpu

D_MODEL = 2048
D_HGRN = 1024
HGRN_HEADS = 8
HGRN_DK = 128
HGRN_DV = 128
GLA_HEADS = 4
GLA_DK = 128
GLA_DV = 256
GLA_DK_TOTAL = 512
D_GLA = 1024
GLA_GATE_RANK = 16
GLA_GATE_TAU = 16.0
PLE_DIM = 256
EPS = 1e-6

LANE = 128
D_MAIN = 4 * D_HGRN + 2 * GLA_DK_TOTAL + 2 * D_GLA
CHUNK = 64
VMEM_LIMIT = 48 * 1024 * 1024

_NT = (((1,), (1,)), ((), ()))
_TN = (((0,), (0,)), ((), ()))


def _sigmoid(x):
    return 1.0 / (1.0 + jnp.exp(-x))


def _rms(x, w):
    return x * lax.rsqrt(jnp.mean(x * x, axis=-1, keepdims=True) + EPS) * w


def _norm_kernel(x_ref, nw_ref, wlr_ref, u_ref, lr_ref):
    u = _rms(x_ref[...], nw_ref[...]).astype(jnp.bfloat16)
    u_ref[...] = u
    lr_ref[...] = lax.dot_general(u, wlr_ref[...].astype(jnp.bfloat16), _NT,
                                  preferred_element_type=jnp.float32)


def _norm(x2, norm_w, wlr_t, tm=512):
    T, D = x2.shape
    return pl.pallas_call(
        _norm_kernel,
        grid=(T // tm,),
        in_specs=[
            pl.BlockSpec((tm, D), lambda i: (i, 0)),
            pl.BlockSpec((1, D), lambda i: (0, 0)),
            pl.BlockSpec((LANE, D), lambda i: (0, 0)),
        ],
        out_specs=[pl.BlockSpec((tm, D), lambda i: (i, 0)),
                   pl.BlockSpec((tm, LANE), lambda i: (i, 0))],
        out_shape=[jax.ShapeDtypeStruct((T, D), jnp.bfloat16),
                   jax.ShapeDtypeStruct((T, LANE), jnp.float32)],
        compiler_params=pltpu.CompilerParams(
            dimension_semantics=("parallel",),
            vmem_limit_bytes=VMEM_LIMIT),
        name="mix_norm",
    )(x2, norm_w, wlr_t)


def _inproj_kernel(u_ref, w_ref, o_ref, wb_ref):
    @pl.when(pl.program_id(1) == 0)
    def _():
        wb_ref[...] = w_ref[...].astype(jnp.bfloat16)

    o_ref[...] = lax.dot_general(u_ref[...], wb_ref[...], _NT, preferred_element_type=jnp.float32)


def _inproj(u, w_t, tm=1024, tn=1024):
    T, D = u.shape
    return pl.pallas_call(
        _inproj_kernel,
        grid=(D_MAIN // tn, T // tm),
        in_specs=[
            pl.BlockSpec((tm, D), lambda j, i: (i, 0)),
            pl.BlockSpec((tn, D), lambda j, i: (j, 0)),
        ],
        out_specs=pl.BlockSpec((tm, tn), lambda j, i: (i, j)),
        out_shape=jax.ShapeDtypeStruct((T, D_MAIN), jnp.float32),
        scratch_shapes=[pltpu.VMEM((tn, D), jnp.bfloat16)],
        compiler_params=pltpu.CompilerParams(
            dimension_semantics=("parallel", "arbitrary"),
            vmem_limit_bytes=VMEM_LIMIT),
        name="inproj",
    )(u, w_t)


def _scan_tile(q, k, v, g, st_ref):
    bf = jnp.bfloat16
    f32 = jnp.float32
    n_heads, dv, dk = st_ref.shape
    tt, wk = g.shape
    blk = 2 * CHUNK
    n_blk = tt // blk
    n_chk = tt // CHUNK
    row = lax.broadcasted_iota(jnp.int32, (blk, blk), 0)
    col = lax.broadcasted_iota(jnp.int32, (blk, blk), 1)
    causal = (col <= row) & (col >= (row & -CHUNK))
    tri = causal.astype(bf)

    g_hi = g.astype(bf)
    g_lo = (g - g_hi.astype(f32)).astype(bf)
    g2 = jnp.concatenate([g_hi, g_lo], axis=1)
    b = []
    for r in range(n_blk):
        b2 = jnp.dot(tri, g2[r * blk:(r + 1) * blk], preferred_element_type=f32)
        b.append(b2[:, :wk] + b2[:, wk:])
    b = jnp.concatenate(b, axis=0)
    eb = jnp.exp(b)
    qh = (q * eb).astype(bf)
    kt = (k * jnp.exp(-b)).astype(bf)
    vb = v.astype(bf)

    hk = lambda a, h, lo, hi: a[lo:hi, h * dk:(h + 1) * dk]
    hv = lambda a, h, lo, hi: a[lo:hi, h * dv:(h + 1) * dv]
    heads = range(n_heads)

    attn = [[lax.dot_general(hk(qh, h, r * blk, (r + 1) * blk), hk(kt, h, r * blk, (r + 1) * blk),
                             _NT, preferred_element_type=f32) for r in range(n_blk)] for h in heads]
    upd = [[lax.dot_general(hv(vb, h, c * CHUNK, (c + 1) * CHUNK), hk(kt, h, c * CHUNK, (c + 1) * CHUNK),
                            _TN, preferred_element_type=f32) for c in range(n_chk)] for h in heads]
    intra = [[jnp.dot(jnp.where(causal, attn[h][r], 0.0).astype(bf), hv(vb, h, r * blk, (r + 1) * blk),
                      preferred_element_type=f32) for r in range(n_blk)] for h in heads]
    st_in = []
    for h in heads:
        st = st_ref[h]
        st_in.append([])
        for c in range(n_chk):
            st_in[h].append(st.astype(bf))
            st = (st + upd[h][c]) * hk(eb, h, (c + 1) * CHUNK - 1, (c + 1) * CHUNK)
        st_ref[h] = st
    inter = [[lax.dot_general(hk(qh, h, c * CHUNK, (c + 1) * CHUNK), st_in[h][c], _NT,
                              preferred_element_type=f32) for c in range(n_chk)] for h in heads]
    return jnp.concatenate(
        [jnp.concatenate(intra[h], axis=0) + jnp.concatenate(inter[h], axis=0) for h in heads], axis=1)


def _readout(o, nw, z, n_heads):
    dv = o.shape[1] // n_heads
    on = jnp.concatenate([_rms(o[:, h * dv:(h + 1) * dv], nw[:, h * dv:(h + 1) * dv])
                          for h in range(n_heads)], axis=1)
    return on * (z * _sigmoid(z))


def _hgrn_kernel(lb_ref, hq_ref, hf_ref, hi_ref, hz_ref, nw_ref, y_ref, st_ref):
    @pl.when(pl.program_id(1) == 0)
    def _():
        st_ref[...] = jnp.zeros_like(st_ref)

    lbp = lb_ref[...]
    e = jnp.exp(lbp - jnp.max(lbp, axis=0, keepdims=True))
    lb = e[0:1, :] / jnp.sum(e, axis=0, keepdims=True)
    hq = hq_ref[...]
    f = lb + (1.0 - lb) * _sigmoid(hf_ref[...])
    q = hq * _sigmoid(hq) * (HGRN_DK ** -0.5)
    o = _scan_tile(q, 1.0 - f, hi_ref[...], jnp.log(f), st_ref)
    y_ref[...] = _readout(o, nw_ref[...], hz_ref[...], st_ref.shape[0]).astype(y_ref.dtype)


def _gla_kernel(gq_ref, gk_ref, gv_ref, gz_ref, lr_ref, w2_ref, gb_ref, nw_ref, y_ref, st_ref):
    @pl.when(pl.program_id(1) == 0)
    def _():
        st_ref[...] = jnp.zeros_like(st_ref)

    logit = jnp.dot(lr_ref[...].astype(jnp.bfloat16), w2_ref[...].astype(jnp.bfloat16),
                    preferred_element_type=jnp.float32) + gb_ref[...]
    log_sig = jnp.minimum(logit, 0.0) - jnp.log(1.0 + jnp.exp(-jnp.abs(logit)))
    g = log_sig / GLA_GATE_TAU
    o = _scan_tile(gq_ref[...] * (GLA_DK ** -0.5), gk_ref[...], gv_ref[...], g, st_ref)
    y_ref[...] = _readout(o, nw_ref[...], gz_ref[...], st_ref.shape[0]).astype(y_ref.dtype)


def _hgrn(proj, hgrn_lb, norm_w, tt=512, hp=4):
    T = proj.shape[0]
    w = hp * HGRN_DK
    nb = D_HGRN // w
    blk = lambda sec: pl.BlockSpec((tt, w), lambda h, t: (t, sec * nb + h))
    return pl.pallas_call(
        _hgrn_kernel,
        grid=(HGRN_HEADS // hp, T // tt),
        in_specs=[
            pl.BlockSpec((2, w), lambda h, t: (0, h)),
            blk(0), blk(1), blk(2), blk(3),
            pl.BlockSpec((1, w), lambda h, t: (0, h)),
        ],
        out_specs=pl.BlockSpec((tt, w), lambda h, t: (t, h)),
        out_shape=jax.ShapeDtypeStruct((T, D_HGRN), jnp.bfloat16),
        scratch_shapes=[pltpu.VMEM((hp, HGRN_DV, HGRN_DK), jnp.float32)],
        compiler_params=pltpu.CompilerParams(
            dimension_semantics=("parallel", "arbitrary"),
            vmem_limit_bytes=VMEM_LIMIT),
        name="hgrn_scan",
    )(hgrn_lb, proj, proj, proj, proj, norm_w)


def _gla(proj, lr, w2_pad, gate_b, norm_w, tt=512, hp=4):
    T = proj.shape[0]
    wk, wv = hp * GLA_DK, hp * GLA_DV
    q0 = 4 * D_HGRN // wk
    k0 = q0 + GLA_DK_TOTAL // wk
    v0 = (4 * D_HGRN + 2 * GLA_DK_TOTAL) // wv
    z0 = v0 + D_GLA // wv
    return pl.pallas_call(
        _gla_kernel,
        grid=(GLA_HEADS // hp, T // tt),
        in_specs=[
            pl.BlockSpec((tt, wk), lambda h, t: (t, q0 + h)),
            pl.BlockSpec((tt, wk), lambda h, t: (t, k0 + h)),
            pl.BlockSpec((tt, wv), lambda h, t: (t, v0 + h)),
            pl.BlockSpec((tt, wv), lambda h, t: (t, z0 + h)),
            pl.BlockSpec((tt, LANE), lambda h, t: (t, 0)),
            pl.BlockSpec((LANE, wk), lambda h, t: (0, h)),
            pl.BlockSpec((1, wk), lambda h, t: (0, h)),
            pl.BlockSpec((1, wv), lambda h, t: (0, h)),
        ],
        out_specs=pl.BlockSpec((tt, wv), lambda h, t: (t, h)),
        out_shape=jax.ShapeDtypeStruct((T, D_GLA), jnp.bfloat16),
        scratch_shapes=[pltpu.VMEM((hp, GLA_DV, GLA_DK), jnp.float32)],
        compiler_params=pltpu.CompilerParams(
            dimension_semantics=("parallel", "arbitrary"),
            vmem_limit_bytes=VMEM_LIMIT),
        name="gla_scan",
    )(proj, proj, proj, proj, lr, w2_pad, gate_b, norm_w)


def _out_kernel(x_ref, yh_ref, yg_ref, p_ref, woh_ref, wog_ref, pp_ref, pnw_ref, gnw_ref,
                gw_ref, gb_ref, fnw_ref, o_ref):
    f32 = jnp.float32
    h = (x_ref[...]
         + jnp.dot(yh_ref[...], woh_ref[...], preferred_element_type=f32)
         + jnp.dot(yg_ref[...], wog_ref[...], preferred_element_type=f32))
    e = _rms(jnp.dot(p_ref[...].astype(jnp.bfloat16), pp_ref[...], preferred_element_type=f32),
             pnw_ref[...])
    hn = _rms(h, gnw_ref[...]).astype(jnp.bfloat16)
    gate = _sigmoid(jnp.dot(hn, gw_ref[...], preferred_element_type=f32) + gb_ref[...])
    o_ref[...] = _rms(h + gate * e, fnw_ref[...])


def _outproj(x2, yh, yg, p2, woh, wog, pp, pnw, gnw, gw, gb, fnw, tm=256):
    T, D = x2.shape
    row = lambda w: pl.BlockSpec((tm, w), lambda i: (i, 0))
    full = lambda a: pl.BlockSpec(a.shape, lambda i: (0, 0), pipeline_mode=pl.Buffered(1))
    return pl.pallas_call(
        _out_kernel,
        grid=(T // tm,),
        in_specs=[row(D), row(D_HGRN), row(D_GLA), row(PLE_DIM),
                  full(woh), full(wog), full(pp), full(pnw), full(gnw), full(gw), full(gb),
                  full(fnw)],
        out_specs=row(D),
        out_shape=jax.ShapeDtypeStruct((T, D), jnp.float32),
        compiler_params=pltpu.CompilerParams(
            dimension_semantics=("parallel",),
            vmem_limit_bytes=VMEM_LIMIT),
        name="outproj_ple",
    )(x2, yh, yg, p2, woh, wog, pp, pnw, gnw, gw, gb, fnw)


def kernel(x, p, norm_mix_w, w_in, hgrn_lb, gla_gate_w2, gla_gate_b, hgrn_norm_w, gla_norm_w,
           w_out, ple_proj, ple_norm_w, ple_gate_norm_w, ple_gate_w, ple_gate_b, final_norm_w):
    B, T, D = x.shape
    bf = jnp.bfloat16
    x2 = x.reshape(B * T, D)
    p2 = p[0].reshape(B * T, PLE_DIM)
    row = lambda a: a.reshape(1, -1)

    w_t = jnp.swapaxes(w_in[0], 0, 1)
    wlr_t = jnp.pad(w_t[D_MAIN:], ((0, LANE - GLA_GATE_RANK), (0, 0)))
    u, lr = _norm(x2, row(norm_mix_w[0]), wlr_t)
    proj = _inproj(u, w_t)

    y_h = _hgrn(proj, hgrn_lb, row(hgrn_norm_w[0]))
    w2_pad = jnp.pad(gla_gate_w2[0], ((0, LANE - GLA_GATE_RANK), (0, 0)))
    y_g = _gla(proj, lr, w2_pad, row(gla_gate_b[0]), row(gla_norm_w[0]))

    w_out_bf = w_out[0].astype(bf)
    out = _outproj(x2, y_h, y_g, p2, w_out_bf[:D_HGRN], w_out_bf[D_HGRN:],
                   ple_proj[0].astype(bf), row(ple_norm_w[0]), row(ple_gate_norm_w[0]),
                   ple_gate_w[0].astype(bf), row(ple_gate_b[0]), row(final_norm_w))
    return out.reshape(B, T, D)
```

```python
import jax
import jax.numpy as jnp
from jax import lax
from jax.experimental import pallas as pl
from jax.experimental.pallas import tpu as pltpu

D_MODEL = 2048
D_HGRN = 1024
HGRN_HEADS = 8
HGRN_DK = 128
HGRN_DV = 128
GLA_HEADS = 4
GLA_DK = 128
GLA_DV = 256
GLA_DK_TOTAL = 512
D_GLA = 1024
GLA_GATE_RANK = 16
GLA_GATE_TAU = 16.0
PLE_DIM = 256
EPS = 1e-6

LANE = 128
SUBLANE = 8
D_MAIN = 4 * D_HGRN + 2 * GLA_DK_TOTAL + 2 * D_GLA
CHUNK = 64
DECAY_RANGE_LIMIT = 60.0
VMEM_LIMIT = 48 * 1024 * 1024

_NT = (((1,), (1,)), ((), ()))
_TN = (((0,), (0,)), ((), ()))


def _sigmoid(x):
    return 1.0 / (1.0 + jnp.exp(-x))


def _rms(x, w):
    return x * lax.rsqrt(jnp.mean(x * x, axis=-1, keepdims=True) + EPS) * w


def _norm_kernel(x_ref, nw_ref, wlr_ref, u_ref, lr_ref):
    u = _rms(x_ref[...], nw_ref[...]).astype(jnp.bfloat16)
    u_ref[...] = u
    lr_ref[...] = lax.dot_general(u, wlr_ref[...].astype(jnp.bfloat16), _NT,
                                  preferred_element_type=jnp.float32)


def _norm(x2, norm_w, wlr_t, tm=512):
    T, D = x2.shape
    return pl.pallas_call(
        _norm_kernel,
        grid=(T // tm,),
        in_specs=[
            pl.BlockSpec((tm, D), lambda i: (i, 0)),
            pl.BlockSpec((1, D), lambda i: (0, 0)),
            pl.BlockSpec((LANE, D), lambda i: (0, 0)),
        ],
        out_specs=[pl.BlockSpec((tm, D), lambda i: (i, 0)),
                   pl.BlockSpec((tm, LANE), lambda i: (i, 0))],
        out_shape=[jax.ShapeDtypeStruct((T, D), jnp.bfloat16),
                   jax.ShapeDtypeStruct((T, LANE), jnp.float32)],
        compiler_params=pltpu.CompilerParams(
            dimension_semantics=("parallel",),
            vmem_limit_bytes=VMEM_LIMIT),
        name="mix_norm",
    )(x2, norm_w, wlr_t)


def _inproj_kernel(u_ref, w_ref, o_ref, wb_ref):
    @pl.when(pl.program_id(1) == 0)
    def _():
        wb_ref[...] = w_ref[...].astype(jnp.bfloat16)

    o_ref[...] = lax.dot_general(u_ref[...], wb_ref[...], _NT, preferred_element_type=jnp.float32)


def _inproj(u, w_t, tm=1024, tn=1024):
    T, D = u.shape
    return pl.pallas_call(
        _inproj_kernel,
        grid=(D_MAIN // tn, T // tm),
        in_specs=[
            pl.BlockSpec((tm, D), lambda j, i: (i, 0)),
            pl.BlockSpec((tn, D), lambda j, i: (j, 0)),
        ],
        out_specs=pl.BlockSpec((tm, tn), lambda j, i: (i, j)),
        out_shape=jax.ShapeDtypeStruct((T, D_MAIN), jnp.float32),
        scratch_shapes=[pltpu.VMEM((tn, D), jnp.bfloat16)],
        compiler_params=pltpu.CompilerParams(
            dimension_semantics=("parallel", "arbitrary"),
            vmem_limit_bytes=VMEM_LIMIT),
        name="inproj",
    )(u, w_t)


def _scan_tile(q, k, v, g, st_ref):
    bf = jnp.bfloat16
    f32 = jnp.float32
    n_heads, dv, dk = st_ref.shape
    tt, wk = g.shape
    blk = 2 * CHUNK
    n_blk = tt // blk
    n_chk = tt // CHUNK
    row = lax.broadcasted_iota(jnp.int32, (blk, blk), 0)
    col = lax.broadcasted_iota(jnp.int32, (blk, blk), 1)
    causal = (col <= row) & (col >= (row & -CHUNK))
    tri = causal.astype(bf)

    g_hi = g.astype(bf)
    g_lo = (g - g_hi.astype(f32)).astype(bf)
    g2 = jnp.concatenate([g_hi, g_lo], axis=1)
    b = []
    for r in range(n_blk):
        b2 = jnp.dot(tri, g2[r * blk:(r + 1) * blk], preferred_element_type=f32)
        b.append(b2[:, :wk] + b2[:, wk:])
    b = jnp.concatenate(b, axis=0)
    eb = jnp.exp(b)
    qh = (q * eb).astype(bf)
    kt = (k * jnp.exp(-b)).astype(bf)
    vb = v.astype(bf)

    hk = lambda a, h, lo, hi: a[lo:hi, h * dk:(h + 1) * dk]
    hv = lambda a, h, lo, hi: a[lo:hi, h * dv:(h + 1) * dv]
    heads = range(n_heads)

    attn = [[lax.dot_general(hk(qh, h, r * blk, (r + 1) * blk), hk(kt, h, r * blk, (r + 1) * blk),
                             _NT, preferred_element_type=f32) for r in range(n_blk)] for h in heads]
    upd = [[lax.dot_general(hv(vb, h, c * CHUNK, (c + 1) * CHUNK), hk(kt, h, c * CHUNK, (c + 1) * CHUNK),
                            _TN, preferred_element_type=f32) for c in range(n_chk)] for h in heads]
    intra = [[jnp.dot(jnp.where(causal, attn[h][r], 0.0).astype(bf), hv(vb, h, r * blk, (r + 1) * blk),
                      preferred_element_type=f32) for r in range(n_blk)] for h in heads]
    st_in, st_out = [], []
    for h in heads:
        st = st_ref[h]
        st_in.append([])
        for c in range(n_chk):
            st_in[h].append(st.astype(bf))
            st = (st + upd[h][c]) * hk(eb, h, (c + 1) * CHUNK - 1, (c + 1) * CHUNK)
        st_out.append(st)
    inter = [[lax.dot_general(hk(qh, h, c * CHUNK, (c + 1) * CHUNK), st_in[h][c], _NT,
                              preferred_element_type=f32) for c in range(n_chk)] for h in heads]
    o = jnp.concatenate(
        [jnp.concatenate(intra[h], axis=0) + jnp.concatenate(inter[h], axis=0) for h in heads], axis=1)
    return o, st_out, jnp.min(b)


def _scan_tile_exact(qkg_ref, v_ref, st_ref, o_ref):
    bf = jnp.bfloat16
    f32 = jnp.float32
    n_heads, dv, dk = st_ref.shape
    tt = o_ref.shape[0]
    rowk = lax.broadcasted_iota(jnp.int32, (CHUNK, dk), 0)
    rowc = lax.broadcasted_iota(jnp.int32, (CHUNK, CHUNK), 0)
    colc = lax.broadcasted_iota(jnp.int32, (CHUNK, CHUNK), 1)
    tri = (colc <= rowc).astype(bf)

    for h in range(n_heads):
        ks = slice(h * dk, (h + 1) * dk)
        vs = slice(h * dv, (h + 1) * dv)

        def chunk(c, carry, ks=ks, vs=vs, h=h):
            r0 = pl.multiple_of(c * CHUNK, CHUNK)
            rows = pl.ds(r0, CHUNK)
            qc = qkg_ref[0, rows, ks]
            kc = qkg_ref[1, rows, ks]
            gc = qkg_ref[2, rows, ks]
            g_hi = gc.astype(bf)
            g_lo = (gc - g_hi.astype(f32)).astype(bf)
            b = (jnp.dot(tri, g_hi, preferred_element_type=f32)
                 + jnp.dot(tri, g_lo, preferred_element_type=f32))
            qkg_ref[2, rows, ks] = b

            def columns(jg, attn):
                j0 = pl.multiple_of(jg * SUBLANE, SUBLANE)
                b_rows = qkg_ref[2, pl.ds(r0 + j0, SUBLANE), ks]
                k_rows = qkg_ref[1, pl.ds(r0 + j0, SUBLANE), ks]
                for r in range(SUBLANE):
                    j = j0 + r
                    decay = jnp.exp(jnp.where(rowk >= j, b - b_rows[r:r + 1], 0.0))
                    colv = jnp.sum(qc * k_rows[r:r + 1] * decay, axis=1, keepdims=True)
                    attn = jnp.where(colc == j, colv, attn)
                return attn

            attn = lax.fori_loop(0, CHUNK // SUBLANE, columns, jnp.zeros((CHUNK, CHUNK), f32))
            attn = jnp.where(colc <= rowc, attn, 0.0).astype(bf)
            vb = v_ref[rows, vs].astype(bf)
            st = st_ref[h]
            b_last = b[CHUNK - 1:CHUNK, :]
            o_ref[rows, vs] = (
                jnp.dot(attn, vb, preferred_element_type=f32)
                + lax.dot_general((qc * jnp.exp(b)).astype(bf), st.astype(bf), _NT,
                                  preferred_element_type=f32))
            k_dec = (kc * jnp.exp(b_last - b)).astype(bf)
            st_ref[h] = st * jnp.exp(b_last) + lax.dot_general(vb, k_dec, _TN,
                                                               preferred_element_type=f32)
            return carry

        lax.fori_loop(0, tt // CHUNK, chunk, 0)


def _readout(o, nw, z, n_heads):
    dv = o.shape[1] // n_heads
    on = jnp.concatenate([_rms(o[:, h * dv:(h + 1) * dv], nw[:, h * dv:(h + 1) * dv])
                          for h in range(n_heads)], axis=1)
    return on * (z * _sigmoid(z))


def _scan_and_readout(qkg, v_ref, z_ref, nw_ref, y_ref, st_ref, qkg_ref, o_ref):
    n_heads = st_ref.shape[0]
    q, k, g = qkg()
    o, st_new, b_min = _scan_tile(q, k, v_ref[...], g, st_ref)
    y_ref[...] = _readout(o, nw_ref[...], z_ref[...], n_heads).astype(y_ref.dtype)
    in_range = b_min >= -DECAY_RANGE_LIMIT

    @pl.when(in_range)
    def _():
        for h in range(n_heads):
            st_ref[h] = st_new[h]

    @pl.when(jnp.logical_not(in_range))
    def _():
        q, k, g = qkg()
        qkg_ref[0] = q
        qkg_ref[1] = k
        qkg_ref[2] = g
        _scan_tile_exact(qkg_ref, v_ref, st_ref, o_ref)
        y_ref[...] = _readout(o_ref[...], nw_ref[...], z_ref[...], n_heads).astype(y_ref.dtype)


def _hgrn_kernel(lb_ref, hq_ref, hf_ref, hi_ref, hz_ref, nw_ref, y_ref, st_ref, qkg_ref, o_ref):
    @pl.when(pl.program_id(1) == 0)
    def _():
        st_ref[...] = jnp.zeros_like(st_ref)

    def qkg():
        lbp = lb_ref[...]
        e = jnp.exp(lbp - jnp.max(lbp, axis=0, keepdims=True))
        lb = e[0:1, :] / jnp.sum(e, axis=0, keepdims=True)
        hq = hq_ref[...]
        f = lb + (1.0 - lb) * _sigmoid(hf_ref[...])
        return hq * _sigmoid(hq) * (HGRN_DK ** -0.5), 1.0 - f, jnp.log(f)

    _scan_and_readout(qkg, hi_ref, hz_ref, nw_ref, y_ref, st_ref, qkg_ref, o_ref)


def _gla_kernel(gq_ref, gk_ref, gv_ref, gz_ref, lr_ref, w2_ref, gb_ref, nw_ref, y_ref, st_ref,
                qkg_ref, o_ref):
    @pl.when(pl.program_id(1) == 0)
    def _():
        st_ref[...] = jnp.zeros_like(st_ref)

    def qkg():
        logit = jnp.dot(lr_ref[...].astype(jnp.bfloat16), w2_ref[...].astype(jnp.bfloat16),
                        preferred_element_type=jnp.float32) + gb_ref[...]
        log_sig = jnp.minimum(logit, 0.0) - jnp.log(1.0 + jnp.exp(-jnp.abs(logit)))
        return gq_ref[...] * (GLA_DK ** -0.5), gk_ref[...], log_sig / GLA_GATE_TAU

    _scan_and_readout(qkg, gv_ref, gz_ref, nw_ref, y_ref, st_ref, qkg_ref, o_ref)


def _hgrn(proj, hgrn_lb, norm_w, tt=512, hp=4):
    T = proj.shape[0]
    w = hp * HGRN_DK
    nb = D_HGRN // w
    blk = lambda sec: pl.BlockSpec((tt, w), lambda h, t: (t, sec * nb + h))
    return pl.pallas_call(
        _hgrn_kernel,
        grid=(HGRN_HEADS // hp, T // tt),
        in_specs=[
            pl.BlockSpec((2, w), lambda h, t: (0, h)),
            blk(0), blk(1), blk(2), blk(3),
            pl.BlockSpec((1, w), lambda h, t: (0, h)),
        ],
        out_specs=pl.BlockSpec((tt, w), lambda h, t: (t, h)),
        out_shape=jax.ShapeDtypeStruct((T, D_HGRN), jnp.bfloat16),
        scratch_shapes=[pltpu.VMEM((hp, HGRN_DV, HGRN_DK), jnp.float32),
                        pltpu.VMEM((3, tt, w), jnp.float32),
                        pltpu.VMEM((tt, w), jnp.float32)],
        compiler_params=pltpu.CompilerParams(
            dimension_semantics=("parallel", "arbitrary"),
            vmem_limit_bytes=VMEM_LIMIT),
        name="hgrn_scan",
    )(hgrn_lb, proj, proj, proj, proj, norm_w)


def _gla(proj, lr, w2_pad, gate_b, norm_w, tt=512, hp=4):
    T = proj.shape[0]
    wk, wv = hp * GLA_DK, hp * GLA_DV
    q0 = 4 * D_HGRN // wk
    k0 = q0 + GLA_DK_TOTAL // wk
    v0 = (4 * D_HGRN + 2 * GLA_DK_TOTAL) // wv
    z0 = v0 + D_GLA // wv
    return pl.pallas_call(
        _gla_kernel,
        grid=(GLA_HEADS // hp, T // tt),
        in_specs=[
            pl.BlockSpec((tt, wk), lambda h, t: (t, q0 + h)),
            pl.BlockSpec((tt, wk), lambda h, t: (t, k0 + h)),
            pl.BlockSpec((tt, wv), lambda h, t: (t, v0 + h)),
            pl.BlockSpec((tt, wv), lambda h, t: (t, z0 + h)),
            pl.BlockSpec((tt, LANE), lambda h, t: (t, 0)),
            pl.BlockSpec((LANE, wk), lambda h, t: (0, h)),
            pl.BlockSpec((1, wk), lambda h, t: (0, h)),
            pl.BlockSpec((1, wv), lambda h, t: (0, h)),
        ],
        out_specs=pl.BlockSpec((tt, wv), lambda h, t: (t, h)),
        out_shape=jax.ShapeDtypeStruct((T, D_GLA), jnp.bfloat16),
        scratch_shapes=[pltpu.VMEM((hp, GLA_DV, GLA_DK), jnp.float32),
                        pltpu.VMEM((3, tt, wk), jnp.float32),
                        pltpu.VMEM((tt, wv), jnp.float32)],
        compiler_params=pltpu.CompilerParams(
            dimension_semantics=("parallel", "arbitrary"),
            vmem_limit_bytes=VMEM_LIMIT),
        name="gla_scan",
    )(proj, proj, proj, proj, lr, w2_pad, gate_b, norm_w)


def _out_kernel(x_ref, yh_ref, yg_ref, p_ref, woh_ref, wog_ref, pp_ref, pnw_ref, gnw_ref,
                gw_ref, gb_ref, fnw_ref, o_ref):
    f32 = jnp.float32
    h = (x_ref[...]
         + jnp.dot(yh_ref[...], woh_ref[...], preferred_element_type=f32)
         + jnp.dot(yg_ref[...], wog_ref[...], preferred_element_type=f32))
    e = _rms(jnp.dot(p_ref[...].astype(jnp.bfloat16), pp_ref[...], preferred_element_type=f32),
             pnw_ref[...])
    hn = _rms(h, gnw_ref[...]).astype(jnp.bfloat16)
    gate = _sigmoid(jnp.dot(hn, gw_ref[...], preferred_element_type=f32) + gb_ref[...])
    o_ref[...] = _rms(h + gate * e, fnw_ref[...])


def _outproj(x2, yh, yg, p2, woh, wog, pp, pnw, gnw, gw, gb, fnw, tm=256):
    T, D = x2.shape
    row = lambda w: pl.BlockSpec((tm, w), lambda i: (i, 0))
    full = lambda a: pl.BlockSpec(a.shape, lambda i: (0, 0), pipeline_mode=pl.Buffered(1))
    return pl.pallas_call(
        _out_kernel,
        grid=(T // tm,),
        in_specs=[row(D), row(D_HGRN), row(D_GLA), row(PLE_DIM),
                  full(woh), full(wog), full(pp), full(pnw), full(gnw), full(gw), full(gb),
                  full(fnw)],
        out_specs=row(D),
        out_shape=jax.ShapeDtypeStruct((T, D), jnp.float32),
        compiler_params=pltpu.CompilerParams(
            dimension_semantics=("parallel",),
            vmem_limit_bytes=VMEM_LIMIT),
        name="outproj_ple",
    )(x2, yh, yg, p2, woh, wog, pp, pnw, gnw, gw, gb, fnw)


def kernel(x, p, norm_mix_w, w_in, hgrn_lb, gla_gate_w2, gla_gate_b, hgrn_norm_w, gla_norm_w,
           w_out, ple_proj, ple_norm_w, ple_gate_norm_w, ple_gate_w, ple_gate_b, final_norm_w):
    B, T, D = x.shape
    bf = jnp.bfloat16
    x2 = x.reshape(B * T, D)
    p2 = p[0].reshape(B * T, PLE_DIM)
    row = lambda a: a.reshape(1, -1)

    w_t = jnp.swapaxes(w_in[0], 0, 1)
    wlr_t = jnp.pad(w_t[D_MAIN:], ((0, LANE - GLA_GATE_RANK), (0, 0)))
    u, lr = _norm(x2, row(norm_mix_w[0]), wlr_t)
    proj = _inproj(u, w_t)

    y_h = _hgrn(proj, hgrn_lb, row(hgrn_norm_w[0]))
    w2_pad = jnp.pad(gla_gate_w2[0], ((0, LANE - GLA_GATE_RANK), (0, 0)))
    y_g = _gla(proj, lr, w2_pad, row(gla_gate_b[0]), row(gla_norm_w[0]))

    w_out_bf = w_out[0].astype(bf)
    out = _outproj(x2, y_h, y_g, p2, w_out_bf[:D_HGRN], w_out_bf[D_HGRN:],
                   ple_proj[0].astype(bf), row(ple_norm_w[0]), row(ple_gate_norm_w[0]),
                   ple_gate_w[0].astype(bf), row(ple_gate_b[0]), row(final_norm_w))
    return out.reshape(B, T, D)
```

```python
import jax
import jax.numpy as jnp
from jax import lax
from jax.experimental import pallas as pl
from jax.experimental.pallas import tpu as pltpu

D_MODEL = 2048
D_HGRN = 1024
HGRN_HEADS = 8
HGRN_DK = 128
HGRN_DV = 128
GLA_HEADS = 4
GLA_DK = 128
GLA_DV = 256
GLA_DK_TOTAL = 512
D_GLA = 1024
GLA_GATE_RANK = 16
GLA_GATE_TAU = 16.0
PLE_DIM = 256
EPS = 1e-6

LANE = 128
SUBLANE = 8
D_MAIN = 4 * D_HGRN + 2 * GLA_DK_TOTAL + 2 * D_GLA
CHUNK = 64
DECAY_RANGE_LIMIT = 60.0
VMEM_LIMIT = 48 * 1024 * 1024

_NT = (((1,), (1,)), ((), ()))
_TN = (((0,), (0,)), ((), ()))


def _sigmoid(x):
    return 1.0 / (1.0 + jnp.exp(-x))


def _rms(x, w):
    return x * lax.rsqrt(jnp.mean(x * x, axis=-1, keepdims=True) + EPS) * w


def _norm_kernel(x_ref, nw_ref, wlr_ref, u_ref, lr_ref):
    u = _rms(x_ref[...], nw_ref[...]).astype(jnp.bfloat16)
    u_ref[...] = u
    lr_ref[...] = lax.dot_general(u, wlr_ref[...].astype(jnp.bfloat16), _NT,
                                  preferred_element_type=jnp.float32)


def _norm(x2, norm_w, wlr_t, tm=512):
    T, D = x2.shape
    return pl.pallas_call(
        _norm_kernel,
        grid=(T // tm,),
        in_specs=[
            pl.BlockSpec((tm, D), lambda i: (i, 0)),
            pl.BlockSpec((1, D), lambda i: (0, 0)),
            pl.BlockSpec((LANE, D), lambda i: (0, 0)),
        ],
        out_specs=[pl.BlockSpec((tm, D), lambda i: (i, 0)),
                   pl.BlockSpec((tm, LANE), lambda i: (i, 0))],
        out_shape=[jax.ShapeDtypeStruct((T, D), jnp.bfloat16),
                   jax.ShapeDtypeStruct((T, LANE), jnp.float32)],
        compiler_params=pltpu.CompilerParams(
            dimension_semantics=("parallel",),
            vmem_limit_bytes=VMEM_LIMIT),
        name="mix_norm",
    )(x2, norm_w, wlr_t)


def _inproj_kernel(u_ref, w_ref, o_ref, wb_ref):
    @pl.when(pl.program_id(1) == 0)
    def _():
        wb_ref[...] = w_ref[...].astype(jnp.bfloat16)

    o_ref[...] = lax.dot_general(u_ref[...], wb_ref[...], _NT, preferred_element_type=jnp.float32)


def _inproj(u, w_t, tm=1024, tn=1024):
    T, D = u.shape
    return pl.pallas_call(
        _inproj_kernel,
        grid=(D_MAIN // tn, T // tm),
        in_specs=[
            pl.BlockSpec((tm, D), lambda j, i: (i, 0)),
            pl.BlockSpec((tn, D), lambda j, i: (j, 0)),
        ],
        out_specs=pl.BlockSpec((tm, tn), lambda j, i: (i, j)),
        out_shape=jax.ShapeDtypeStruct((T, D_MAIN), jnp.float32),
        scratch_shapes=[pltpu.VMEM((tn, D), jnp.bfloat16)],
        compiler_params=pltpu.CompilerParams(
            dimension_semantics=("parallel", "arbitrary"),
            vmem_limit_bytes=VMEM_LIMIT),
        name="inproj",
    )(u, w_t)


def _scan_tile(q, k, v, g, st_ref):
    bf = jnp.bfloat16
    f32 = jnp.float32
    n_heads, dv, dk = st_ref.shape
    tt, wk = g.shape
    blk = 2 * CHUNK
    n_blk = tt // blk
    n_chk = tt // CHUNK
    row = lax.broadcasted_iota(jnp.int32, (blk, blk), 0)
    col = lax.broadcasted_iota(jnp.int32, (blk, blk), 1)
    causal = (col <= row) & (col >= (row & -CHUNK))
    tri = causal.astype(bf)

    g_hi = g.astype(bf)
    g_lo = (g - g_hi.astype(f32)).astype(bf)
    g2 = jnp.concatenate([g_hi, g_lo], axis=1)
    b = []
    for r in range(n_blk):
        b2 = jnp.dot(tri, g2[r * blk:(r + 1) * blk], preferred_element_type=f32)
        b.append(b2[:, :wk] + b2[:, wk:])
    b = jnp.concatenate(b, axis=0)
    eb = jnp.exp(b)
    qh = (q * eb).astype(bf)
    kt = (k * jnp.exp(-b)).astype(bf)
    vb = v.astype(bf)

    hk = lambda a, h, lo, hi: a[lo:hi, h * dk:(h + 1) * dk]
    hv = lambda a, h, lo, hi: a[lo:hi, h * dv:(h + 1) * dv]
    heads = range(n_heads)

    attn = [[lax.dot_general(hk(qh, h, r * blk, (r + 1) * blk), hk(kt, h, r * blk, (r + 1) * blk),
                             _NT, preferred_element_type=f32) for r in range(n_blk)] for h in heads]
    upd = [[lax.dot_general(hv(vb, h, c * CHUNK, (c + 1) * CHUNK), hk(kt, h, c * CHUNK, (c + 1) * CHUNK),
                            _TN, preferred_element_type=f32) for c in range(n_chk)] for h in heads]
    intra = [[jnp.dot(jnp.where(causal, attn[h][r], 0.0).astype(bf), hv(vb, h, r * blk, (r + 1) * blk),
                      preferred_element_type=f32) for r in range(n_blk)] for h in heads]
    st_in, st_out = [], []
    for h in heads:
        st = st_ref[h]
        st_in.append([])
        for c in range(n_chk):
            st_in[h].append(st.astype(bf))
            st = (st + upd[h][c]) * hk(eb, h, (c + 1) * CHUNK - 1, (c + 1) * CHUNK)
        st_out.append(st)
    inter = [[lax.dot_general(hk(qh, h, c * CHUNK, (c + 1) * CHUNK), st_in[h][c], _NT,
                              preferred_element_type=f32) for c in range(n_chk)] for h in heads]
    o = jnp.concatenate(
        [jnp.concatenate(intra[h], axis=0) + jnp.concatenate(inter[h], axis=0) for h in heads], axis=1)
    return o, st_out, jnp.min(b)


def _scan_tile_exact(qkg_ref, v_ref, st_ref, o_ref):
    bf = jnp.bfloat16
    f32 = jnp.float32
    n_heads, dv, dk = st_ref.shape
    tt = o_ref.shape[0]
    rowk = lax.broadcasted_iota(jnp.int32, (CHUNK, dk), 0)
    rowc = lax.broadcasted_iota(jnp.int32, (CHUNK, CHUNK), 0)
    colc = lax.broadcasted_iota(jnp.int32, (CHUNK, CHUNK), 1)
    tri = (colc <= rowc).astype(bf)

    for h in range(n_heads):
        ks = slice(h * dk, (h + 1) * dk)
        vs = slice(h * dv, (h + 1) * dv)

        def chunk(c, carry, ks=ks, vs=vs, h=h):
            r0 = pl.multiple_of(c * CHUNK, CHUNK)
            rows = pl.ds(r0, CHUNK)
            qc = qkg_ref[0, rows, ks]
            kc = qkg_ref[1, rows, ks]
            gc = qkg_ref[2, rows, ks]
            g_hi = gc.astype(bf)
            g_lo = (gc - g_hi.astype(f32)).astype(bf)
            b = (jnp.dot(tri, g_hi, preferred_element_type=f32)
                 + jnp.dot(tri, g_lo, preferred_element_type=f32))
            qkg_ref[2, rows, ks] = b

            def columns(jg, attn):
                j0 = pl.multiple_of(jg * SUBLANE, SUBLANE)
                b_rows = qkg_ref[2, pl.ds(r0 + j0, SUBLANE), ks]
                k_rows = qkg_ref[1, pl.ds(r0 + j0, SUBLANE), ks]
                for r in range(SUBLANE):
                    j = j0 + r
                    decay = jnp.exp(jnp.where(rowk >= j, b - b_rows[r:r + 1], 0.0))
                    colv = jnp.sum(qc * k_rows[r:r + 1] * decay, axis=1, keepdims=True)
                    attn = jnp.where(colc == j, colv, attn)
                return attn

            attn = lax.fori_loop(0, CHUNK // SUBLANE, columns, jnp.zeros((CHUNK, CHUNK), f32))
            attn = jnp.where(colc <= rowc, attn, 0.0).astype(bf)
            vb = v_ref[rows, vs].astype(bf)
            st = st_ref[h]
            b_last = b[CHUNK - 1:CHUNK, :]
            o_ref[rows, vs] = (
                jnp.dot(attn, vb, preferred_element_type=f32)
                + lax.dot_general((qc * jnp.exp(b)).astype(bf), st.astype(bf), _NT,
                                  preferred_element_type=f32))
            k_dec = (kc * jnp.exp(b_last - b)).astype(bf)
            st_ref[h] = st * jnp.exp(b_last) + lax.dot_general(vb, k_dec, _TN,
                                                               preferred_element_type=f32)
            return carry

        lax.fori_loop(0, tt // CHUNK, chunk, 0)


def _readout(o, nw, z, n_heads):
    dv = o.shape[1] // n_heads
    on = jnp.concatenate([_rms(o[:, h * dv:(h + 1) * dv], nw[:, h * dv:(h + 1) * dv])
                          for h in range(n_heads)], axis=1)
    return on * (z * _sigmoid(z))


def _scan_and_readout(qkg, v_ref, z_ref, nw_ref, y_ref, st_ref, qkg_ref, o_ref):
    @pl.when(pl.program_id(1) == 0)
    def _():
        st_ref[...] = jnp.zeros_like(st_ref)

    n_heads = st_ref.shape[0]
    q, k, g = qkg()
    o, st_new, b_min = _scan_tile(q, k, v_ref[...], g, st_ref)
    y_ref[...] = _readout(o, nw_ref[...], z_ref[...], n_heads).astype(y_ref.dtype)
    in_range = b_min >= -DECAY_RANGE_LIMIT

    @pl.when(in_range)
    def _():
        for h in range(n_heads):
            st_ref[h] = st_new[h]

    @pl.when(jnp.logical_not(in_range))
    def _():
        q, k, g = qkg()
        qkg_ref[0] = q
        qkg_ref[1] = k
        qkg_ref[2] = g
        _scan_tile_exact(qkg_ref, v_ref, st_ref, o_ref)
        y_ref[...] = _readout(o_ref[...], nw_ref[...], z_ref[...], n_heads).astype(y_ref.dtype)


def _hgrn_kernel(lb_ref, hq_ref, hf_ref, hi_ref, hz_ref, nw_ref, y_ref, st_ref, qkg_ref, o_ref):
    def qkg():
        lbp = lb_ref[...]
        e = jnp.exp(lbp - jnp.max(lbp, axis=0, keepdims=True))
        lb = e[0:1, :] / jnp.sum(e, axis=0, keepdims=True)
        hq = hq_ref[...]
        f = lb + (1.0 - lb) * _sigmoid(hf_ref[...])
        return hq * _sigmoid(hq) * (HGRN_DK ** -0.5), 1.0 - f, jnp.log(f)

    _scan_and_readout(qkg, hi_ref, hz_ref, nw_ref, y_ref, st_ref, qkg_ref, o_ref)


def _gla_kernel(gq_ref, gk_ref, gv_ref, gz_ref, lr_ref, w2_ref, gb_ref, nw_ref, y_ref, st_ref,
                qkg_ref, o_ref):
    def qkg():
        logit = jnp.dot(lr_ref[...].astype(jnp.bfloat16), w2_ref[...].astype(jnp.bfloat16),
                        preferred_element_type=jnp.float32) + gb_ref[...]
        log_sig = jnp.minimum(logit, 0.0) - jnp.log(1.0 + jnp.exp(-jnp.abs(logit)))
        return gq_ref[...] * (GLA_DK ** -0.5), gk_ref[...], log_sig / GLA_GATE_TAU

    _scan_and_readout(qkg, gv_ref, gz_ref, nw_ref, y_ref, st_ref, qkg_ref, o_ref)


def _hgrn(proj, hgrn_lb, norm_w, tt=512, hp=4):
    T = proj.shape[0]
    w = hp * HGRN_DK
    nb = D_HGRN // w
    blk = lambda sec: pl.BlockSpec((tt, w), lambda h, t: (t, sec * nb + h))
    return pl.pallas_call(
        _hgrn_kernel,
        grid=(HGRN_HEADS // hp, T // tt),
        in_specs=[
            pl.BlockSpec((2, w), lambda h, t: (0, h)),
            blk(0), blk(1), blk(2), blk(3),
            pl.BlockSpec((1, w), lambda h, t: (0, h)),
        ],
        out_specs=pl.BlockSpec((tt, w), lambda h, t: (t, h)),
        out_shape=jax.ShapeDtypeStruct((T, D_HGRN), jnp.bfloat16),
        scratch_shapes=[pltpu.VMEM((hp, HGRN_DV, HGRN_DK), jnp.float32),
                        pltpu.VMEM((3, tt, w), jnp.float32),
                        pltpu.VMEM((tt, w), jnp.float32)],
        compiler_params=pltpu.CompilerParams(
            dimension_semantics=("parallel", "arbitrary"),
            vmem_limit_bytes=VMEM_LIMIT),
        name="hgrn_scan",
    )(hgrn_lb, proj, proj, proj, proj, norm_w)


def _gla(proj, lr, w2_pad, gate_b, norm_w, tt=512, hp=4):
    T = proj.shape[0]
    wk, wv = hp * GLA_DK, hp * GLA_DV
    q0 = 4 * D_HGRN // wk
    k0 = q0 + GLA_DK_TOTAL // wk
    v0 = (4 * D_HGRN + 2 * GLA_DK_TOTAL) // wv
    z0 = v0 + D_GLA // wv
    return pl.pallas_call(
        _gla_kernel,
        grid=(GLA_HEADS // hp, T // tt),
        in_specs=[
            pl.BlockSpec((tt, wk), lambda h, t: (t, q0 + h)),
            pl.BlockSpec((tt, wk), lambda h, t: (t, k0 + h)),
            pl.BlockSpec((tt, wv), lambda h, t: (t, v0 + h)),
            pl.BlockSpec((tt, wv), lambda h, t: (t, z0 + h)),
            pl.BlockSpec((tt, LANE), lambda h, t: (t, 0)),
            pl.BlockSpec((LANE, wk), lambda h, t: (0, h)),
            pl.BlockSpec((1, wk), lambda h, t: (0, h)),
            pl.BlockSpec((1, wv), lambda h, t: (0, h)),
        ],
        out_specs=pl.BlockSpec((tt, wv), lambda h, t: (t, h)),
        out_shape=jax.ShapeDtypeStruct((T, D_GLA), jnp.bfloat16),
        scratch_shapes=[pltpu.VMEM((hp, GLA_DV, GLA_DK), jnp.float32),
                        pltpu.VMEM((3, tt, wk), jnp.float32),
                        pltpu.VMEM((tt, wv), jnp.float32)],
        compiler_params=pltpu.CompilerParams(
            dimension_semantics=("parallel", "arbitrary"),
            vmem_limit_bytes=VMEM_LIMIT),
        name="gla_scan",
    )(proj, proj, proj, proj, lr, w2_pad, gate_b, norm_w)


OUT_SUB = 2
OUT_VMEM_LIMIT = 58 * 1024 * 1024


def _out_kernel(x_ref, yh_ref, yg_ref, p_ref, woh_ref, wog_ref, pp_ref, pnw_ref, gnw_ref,
                gw_ref, gb_ref, fnw_ref, o_ref):
    f32 = jnp.float32
    bf = jnp.bfloat16
    rows = o_ref.shape[0] // OUT_SUB
    subs = [slice(s * rows, (s + 1) * rows) for s in range(OUT_SUB)]
    h = [x_ref[r, :]
         + jnp.dot(yh_ref[r, :], woh_ref[...], preferred_element_type=f32)
         + jnp.dot(yg_ref[r, :], wog_ref[...], preferred_element_type=f32) for r in subs]
    e = [_rms(jnp.dot(p_ref[r, :].astype(bf), pp_ref[...], preferred_element_type=f32), pnw_ref[...])
         for r in subs]
    hn = [_rms(hs, gnw_ref[...]).astype(bf) for hs in h]
    acc = [jnp.dot(hs, gw_ref[...], preferred_element_type=f32) for hs in hn]
    for r, hs, es, a in zip(subs, h, e, acc):
        o_ref[r, :] = _rms(hs + _sigmoid(a + gb_ref[...]) * es, fnw_ref[...])


def _outproj(x2, yh, yg, p2, wo, pp, pnw, gnw, gw, gb, fnw, tm=512):
    T, D = x2.shape
    row = lambda w: pl.BlockSpec((tm, w), lambda i: (i, 0))
    once = pl.Buffered(1)
    full = lambda a: pl.BlockSpec(a.shape, lambda i: (0, 0), pipeline_mode=once)
    return pl.pallas_call(
        _out_kernel,
        grid=(T // tm,),
        in_specs=[row(D), row(D_HGRN), row(D_GLA), row(PLE_DIM),
                  pl.BlockSpec((D_HGRN, D), lambda i: (0, 0), pipeline_mode=once),
                  pl.BlockSpec((D_GLA, D), lambda i: (D_HGRN // D_GLA, 0), pipeline_mode=once),
                  full(pp), full(pnw), full(gnw), full(gw), full(gb), full(fnw)],
        out_specs=row(D),
        out_shape=jax.ShapeDtypeStruct((T, D), jnp.float32),
        compiler_params=pltpu.CompilerParams(
            dimension_semantics=("parallel",),
            vmem_limit_bytes=OUT_VMEM_LIMIT),
        name="outproj_ple",
    )(x2, yh, yg, p2, wo, wo, pp, pnw, gnw, gw, gb, fnw)


def kernel(x, p, norm_mix_w, w_in, hgrn_lb, gla_gate_w2, gla_gate_b, hgrn_norm_w, gla_norm_w,
           w_out, ple_proj, ple_norm_w, ple_gate_norm_w, ple_gate_w, ple_gate_b, final_norm_w):
    B, T, D = x.shape
    bf = jnp.bfloat16
    x2 = x.reshape(B * T, D)
    p2 = p[0].reshape(B * T, PLE_DIM)
    row = lambda a: a.reshape(1, -1)

    w_t = jnp.swapaxes(w_in[0], 0, 1)
    wlr_t = jnp.pad(w_t[D_MAIN:], ((0, LANE - GLA_GATE_RANK), (0, 0)))
    u, lr = _norm(x2, row(norm_mix_w[0]), wlr_t)
    proj = _inproj(u, w_t)

    y_h = _hgrn(proj, hgrn_lb, row(hgrn_norm_w[0]))
    w2_pad = jnp.pad(gla_gate_w2[0], ((0, LANE - GLA_GATE_RANK), (0, 0)))
    y_g = _gla(proj, lr, w2_pad, row(gla_gate_b[0]), row(gla_norm_w[0]))

    out = _outproj(x2, y_h, y_g, p2, w_out[0].astype(bf), ple_proj[0].astype(bf),
                   row(ple_norm_w[0]), row(ple_gate_norm_w[0]), ple_gate_w[0].astype(bf),
                   row(ple_gate_b[0]), row(final_norm_w))
    return out.reshape(B, T, D)
```

```python
import jax
import jax.numpy as jnp
from jax import lax
from jax.experimental import pallas as pl
from jax.experimental.pallas import tpu as pltpu

D_MODEL = 2048
D_HGRN = 1024
HGRN_HEADS = 8
HGRN_DK = 128
HGRN_DV = 128
GLA_HEADS = 4
GLA_DK = 128
GLA_DV = 256
GLA_DK_TOTAL = 512
D_GLA = 1024
GLA_GATE_RANK = 16
GLA_GATE_TAU = 16.0
PLE_DIM = 256
EPS = 1e-6

LANE = 128
SUBLANE = 8
D_MAIN = 4 * D_HGRN + 2 * GLA_DK_TOTAL + 2 * D_GLA
PANEL = 1024
CHUNK = 64
DECAY_RANGE_LIMIT = 86.0
LOG2E = 1.4426950408889634
VMEM_LIMIT = 48 * 1024 * 1024

_NT = (((1,), (1,)), ((), ()))
_TN = (((0,), (0,)), ((), ()))


def _sigmoid(x):
    return 0.5 + 0.5 * jnp.tanh(0.5 * x)


def _silu(x):
    h = 0.5 * x
    return h + h * jnp.tanh(h)


def _rms(x, w):
    return x * lax.rsqrt(jnp.mean(x * x, axis=-1, keepdims=True) + EPS) * w


def _norm_kernel(x_ref, nw_ref, wlr_ref, u_ref, lr_ref):
    u = _rms(x_ref[...], nw_ref[...]).astype(jnp.bfloat16)
    u_ref[...] = u
    lr_ref[...] = lax.dot_general(u, wlr_ref[...].astype(jnp.bfloat16), _NT,
                                  preferred_element_type=jnp.float32)


def _norm(x2, norm_w, wlr_t, tm=512):
    T, D = x2.shape
    return pl.pallas_call(
        _norm_kernel,
        grid=(T // tm,),
        in_specs=[
            pl.BlockSpec((tm, D), lambda i: (i, 0)),
            pl.BlockSpec((1, D), lambda i: (0, 0)),
            pl.BlockSpec((LANE, D), lambda i: (0, 0)),
        ],
        out_specs=[pl.BlockSpec((tm, D), lambda i: (i, 0)),
                   pl.BlockSpec((tm, LANE), lambda i: (i, 0))],
        out_shape=[jax.ShapeDtypeStruct((T, D), jnp.bfloat16),
                   jax.ShapeDtypeStruct((T, LANE), jnp.float32)],
        compiler_params=pltpu.CompilerParams(
            dimension_semantics=("parallel",),
            vmem_limit_bytes=VMEM_LIMIT),
        name="mix_norm",
    )(x2, norm_w, wlr_t)


def _inproj_kernel(u_ref, w_ref, o_ref, wb_ref):
    @pl.when(pl.program_id(1) == 0)
    def _():
        wb_ref[...] = w_ref[...].astype(jnp.bfloat16)

    o_ref[...] = lax.dot_general(u_ref[...], wb_ref[...], _NT,
                                 preferred_element_type=jnp.float32).astype(o_ref.dtype)


def _inproj(u, w_t, panel_of, n_panels, out_dtype, name, tm=1024):
    T, D = u.shape
    return pl.pallas_call(
        _inproj_kernel,
        grid=(n_panels, T // tm),
        in_specs=[
            pl.BlockSpec((tm, D), lambda j, i: (i, 0)),
            pl.BlockSpec((PANEL, D), lambda j, i: (panel_of(j), 0)),
        ],
        out_specs=pl.BlockSpec((tm, PANEL), lambda j, i: (i, j)),
        out_shape=jax.ShapeDtypeStruct((T, n_panels * PANEL), out_dtype),
        scratch_shapes=[pltpu.VMEM((PANEL, D), jnp.bfloat16)],
        compiler_params=pltpu.CompilerParams(
            dimension_semantics=("parallel", "arbitrary"),
            vmem_limit_bytes=VMEM_LIMIT),
        name=name,
    )(u, w_t)


def _scan_stages(qkg, v, state, finish):
    bf = jnp.bfloat16
    f32 = jnp.float32
    dv, dk = state.shape
    blk = 2 * CHUNK
    row = lax.broadcasted_iota(jnp.int32, (blk, blk), 0)
    col = lax.broadcasted_iota(jnp.int32, (blk, blk), 1)
    causal = (col <= row) & (col >= (row & -CHUNK))
    res = {}

    def gates():
        q, k, g = qkg()
        g_hi = g.astype(bf)
        g_lo = (g - g_hi.astype(f32)).astype(bf)
        g2 = jnp.concatenate([g_hi, g_lo], axis=1)
        tri = causal.astype(bf)
        res.update(q=q, k=k, n_blk=g.shape[0] // blk, n_chk=g.shape[0] // CHUNK)
        res['b2'] = [jnp.dot(tri, g2[r * blk:(r + 1) * blk], preferred_element_type=f32)
                     for r in range(res['n_blk'])]

    def decay():
        b = jnp.concatenate([b2[:, :dk] + b2[:, dk:] for b2 in res.pop('b2')], axis=0)
        eb = jnp.exp2(b)
        res['qh'] = (res.pop('q') * eb).astype(bf)
        res['kt'] = (res.pop('k') / eb).astype(bf)
        res['vb'] = v().astype(bf)
        res['eb_last'] = [eb[(c + 1) * CHUNK - 1:(c + 1) * CHUNK, :] for c in range(res['n_chk'])]
        res['b_min'] = jnp.min(b)

    def scores():
        qh, kt, vb = res['qh'], res['kt'], res['vb']
        res['attn'] = [lax.dot_general(qh[r * blk:(r + 1) * blk], kt[r * blk:(r + 1) * blk], _NT,
                                       preferred_element_type=f32) for r in range(res['n_blk'])]
        res['upd'] = [lax.dot_general(vb[c * CHUNK:(c + 1) * CHUNK], kt[c * CHUNK:(c + 1) * CHUNK], _TN,
                                      preferred_element_type=f32) for c in range(res['n_chk'])]

    def carry():
        res['attn'] = [jnp.where(causal, a, 0.0).astype(bf) for a in res['attn']]
        st, st_in = state, []
        for upd, eb_last in zip(res.pop('upd'), res.pop('eb_last')):
            st_in.append(st.astype(bf))
            st = (st + upd) * eb_last
        res.update(st_in=st_in, state=st)

    def outputs():
        qh, vb = res.pop('qh'), res.pop('vb')
        res['intra'] = [jnp.dot(a, vb[r * blk:(r + 1) * blk], preferred_element_type=f32)
                        for r, a in enumerate(res.pop('attn'))]
        res['inter'] = [lax.dot_general(qh[c * CHUNK:(c + 1) * CHUNK], s, _NT, preferred_element_type=f32)
                        for c, s in enumerate(res.pop('st_in'))]

    def done():
        finish(jnp.concatenate(res.pop('intra'), axis=0) + jnp.concatenate(res.pop('inter'), axis=0))

    return [gates, decay, scores, carry, outputs, done], res


def _scan_tile_exact(qkg_ref, v_ref, st_ref, o_ref):
    bf = jnp.bfloat16
    f32 = jnp.float32
    n_heads, dv, dk = st_ref.shape
    tt = o_ref.shape[0]
    rowk = lax.broadcasted_iota(jnp.int32, (CHUNK, dk), 0)
    rowc = lax.broadcasted_iota(jnp.int32, (CHUNK, CHUNK), 0)
    colc = lax.broadcasted_iota(jnp.int32, (CHUNK, CHUNK), 1)
    tri = (colc <= rowc).astype(bf)

    for h in range(n_heads):
        ks = slice(h * dk, (h + 1) * dk)
        vs = slice(h * dv, (h + 1) * dv)

        def chunk(c, carry, ks=ks, vs=vs, h=h):
            r0 = pl.multiple_of(c * CHUNK, CHUNK)
            rows = pl.ds(r0, CHUNK)
            qc = qkg_ref[0, rows, ks]
            kc = qkg_ref[1, rows, ks]
            gc = qkg_ref[2, rows, ks]
            g_hi = gc.astype(bf)
            g_lo = (gc - g_hi.astype(f32)).astype(bf)
            b = (jnp.dot(tri, g_hi, preferred_element_type=f32)
                 + jnp.dot(tri, g_lo, preferred_element_type=f32))
            qkg_ref[2, rows, ks] = b

            def columns(jg, attn):
                j0 = pl.multiple_of(jg * SUBLANE, SUBLANE)
                b_rows = qkg_ref[2, pl.ds(r0 + j0, SUBLANE), ks]
                k_rows = qkg_ref[1, pl.ds(r0 + j0, SUBLANE), ks]
                for r in range(SUBLANE):
                    j = j0 + r
                    decay = jnp.exp2(jnp.where(rowk >= j, b - b_rows[r:r + 1], 0.0))
                    colv = jnp.sum(qc * k_rows[r:r + 1] * decay, axis=1, keepdims=True)
                    attn = jnp.where(colc == j, colv, attn)
                return attn

            attn = lax.fori_loop(0, CHUNK // SUBLANE, columns, jnp.zeros((CHUNK, CHUNK), f32))
            attn = jnp.where(colc <= rowc, attn, 0.0).astype(bf)
            vb = v_ref[rows, vs].astype(bf)
            st = st_ref[h]
            b_last = b[CHUNK - 1:CHUNK, :]
            o_ref[rows, vs] = (
                jnp.dot(attn, vb, preferred_element_type=f32)
                + lax.dot_general((qc * jnp.exp2(b)).astype(bf), st.astype(bf), _NT,
                                  preferred_element_type=f32))
            k_dec = (kc * jnp.exp2(b_last - b)).astype(bf)
            st_ref[h] = st * jnp.exp2(b_last) + lax.dot_general(vb, k_dec, _TN,
                                                                preferred_element_type=f32)
            return carry

        lax.fori_loop(0, tt // CHUNK, chunk, 0)


def _readout(o, z, n_heads, dk):
    dv = o.shape[1] // n_heads

    def head(oh):
        return oh * lax.rsqrt(jnp.mean(oh * oh, axis=-1, keepdims=True) + EPS * dk)

    on = jnp.concatenate([head(o[:, h * dv:(h + 1) * dv]) for h in range(n_heads)], axis=1)
    return on.astype(z.dtype) * _silu(z)


def _scan_and_readout(qkg, v_ref, z_ref, y_ref, st_ref, qkg_ref, o_ref):
    @pl.when(pl.program_id(1) == 0)
    def _():
        st_ref[...] = jnp.zeros_like(st_ref)

    n_heads, dv, dk = st_ref.shape
    heads = []
    for h in range(n_heads):
        ks = slice(h * dk, (h + 1) * dk)
        vs = slice(h * dv, (h + 1) * dv)

        def finish(o, vs=vs):
            y_ref[:, vs] = _readout(o, z_ref[:, vs], 1, dk).astype(y_ref.dtype)

        heads.append(_scan_stages(lambda ks=ks: qkg(ks), lambda vs=vs: v_ref[:, vs], st_ref[h], finish))
    n_stages = len(heads[0][0])
    for tick in range(n_heads + n_stages - 1):
        for h in range(n_heads):
            if 0 <= tick - h < n_stages:
                heads[h][0][tick - h]()
    st_new = [res['state'] for _, res in heads]
    b_min = heads[0][1]['b_min']
    for _, res in heads[1:]:
        b_min = jnp.minimum(b_min, res['b_min'])
    in_range = b_min >= -DECAY_RANGE_LIMIT

    @pl.when(in_range)
    def _():
        for h in range(n_heads):
            st_ref[h] = st_new[h]

    @pl.when(jnp.logical_not(in_range))
    def _():
        q, k, g = qkg(slice(None))
        qkg_ref[0] = q
        qkg_ref[1] = k
        qkg_ref[2] = g
        _scan_tile_exact(qkg_ref, v_ref, st_ref, o_ref)
        y_ref[...] = _readout(o_ref[...], z_ref[...], n_heads, dk).astype(y_ref.dtype)


def _hgrn_kernel(lb_ref, hq_ref, hf_ref, hi_ref, hz_ref, y_ref, st_ref, qkg_ref, o_ref):
    def qkg(cols):
        lbp = lb_ref[:, cols]
        e = jnp.exp(lbp - jnp.max(lbp, axis=0, keepdims=True))
        lb = e[0:1, :] / jnp.sum(e, axis=0, keepdims=True)
        f = lb + (1.0 - lb) * _sigmoid(hf_ref[:, cols])
        return _silu(hq_ref[:, cols]), 1.0 - f, jnp.log2(f)

    _scan_and_readout(qkg, hi_ref, hz_ref, y_ref, st_ref, qkg_ref, o_ref)


def _gla_kernel(gq_ref, gk_ref, gv_ref, gz_ref, lr_ref, w2_ref, gb_ref, y_ref, st_ref,
                qkg_ref, o_ref):
    def qkg(cols):
        logit = jnp.dot(lr_ref[...].astype(jnp.bfloat16), w2_ref[:, cols].astype(jnp.bfloat16),
                        preferred_element_type=jnp.float32) + gb_ref[:, cols]
        log2_sig = (jnp.minimum(logit, 0.0) * LOG2E
                    - jnp.log2(1.0 + jnp.exp2(jnp.abs(logit) * -LOG2E)))
        return gq_ref[:, cols], gk_ref[:, cols], log2_sig / GLA_GATE_TAU

    _scan_and_readout(qkg, gv_ref, gz_ref, y_ref, st_ref, qkg_ref, o_ref)


def _hgrn(qf, vz, hgrn_lb, tt=512, hp=8):
    T = qf.shape[0]
    w = hp * HGRN_DK
    nb = D_HGRN // w
    blk = lambda sec: pl.BlockSpec((tt, w), lambda h, t: (t, sec * nb + h))
    return pl.pallas_call(
        _hgrn_kernel,
        grid=(HGRN_HEADS // hp, T // tt),
        in_specs=[
            pl.BlockSpec((2, w), lambda h, t: (0, h)),
            blk(0), blk(1), blk(0), blk(1),
        ],
        out_specs=pl.BlockSpec((tt, w), lambda h, t: (t, h)),
        out_shape=jax.ShapeDtypeStruct((T, D_HGRN), jnp.bfloat16),
        scratch_shapes=[pltpu.VMEM((hp, HGRN_DV, HGRN_DK), jnp.float32),
                        pltpu.VMEM((3, tt, w), jnp.float32),
                        pltpu.VMEM((tt, w), jnp.float32)],
        compiler_params=pltpu.CompilerParams(
            dimension_semantics=("parallel", "arbitrary"),
            vmem_limit_bytes=VMEM_LIMIT),
        name="hgrn_scan",
    )(hgrn_lb, qf, qf, vz, vz)


def _gla(qf, vz, lr, w2_pad, gate_b, tt=512, hp=4):
    T = qf.shape[0]
    wk, wv = hp * GLA_DK, hp * GLA_DV
    q0 = 2 * D_HGRN // wk
    k0 = q0 + GLA_DK_TOTAL // wk
    v0 = 2 * D_HGRN // wv
    z0 = v0 + D_GLA // wv
    return pl.pallas_call(
        _gla_kernel,
        grid=(GLA_HEADS // hp, T // tt),
        in_specs=[
            pl.BlockSpec((tt, wk), lambda h, t: (t, q0 + h)),
            pl.BlockSpec((tt, wk), lambda h, t: (t, k0 + h)),
            pl.BlockSpec((tt, wv), lambda h, t: (t, v0 + h)),
            pl.BlockSpec((tt, wv), lambda h, t: (t, z0 + h)),
            pl.BlockSpec((tt, LANE), lambda h, t: (t, 0)),
            pl.BlockSpec((LANE, wk), lambda h, t: (0, h)),
            pl.BlockSpec((1, wk), lambda h, t: (0, h)),
        ],
        out_specs=pl.BlockSpec((tt, wv), lambda h, t: (t, h)),
        out_shape=jax.ShapeDtypeStruct((T, D_GLA), jnp.bfloat16),
        scratch_shapes=[pltpu.VMEM((hp, GLA_DV, GLA_DK), jnp.float32),
                        pltpu.VMEM((3, tt, wk), jnp.float32),
                        pltpu.VMEM((tt, wv), jnp.float32)],
        compiler_params=pltpu.CompilerParams(
            dimension_semantics=("parallel", "arbitrary"),
            vmem_limit_bytes=VMEM_LIMIT),
        name="gla_scan",
    )(qf, qf, vz, vz, lr, w2_pad, gate_b)


OUT_SUB = 2
OUT_VMEM_LIMIT = 58 * 1024 * 1024


def _out_kernel(x_ref, yh_ref, yg_ref, p_ref, woh_ref, wog_ref, pp_ref, pnw_ref, gnw_ref,
                gw_ref, gb_ref, fnw_ref, o_ref):
    f32 = jnp.float32
    bf = jnp.bfloat16
    rows = o_ref.shape[0] // OUT_SUB
    subs = [slice(s * rows, (s + 1) * rows) for s in range(OUT_SUB)]
    h = [x_ref[r, :]
         + jnp.dot(yh_ref[r, :], woh_ref[...], preferred_element_type=f32)
         + jnp.dot(yg_ref[r, :], wog_ref[...], preferred_element_type=f32) for r in subs]
    e = [_rms(jnp.dot(p_ref[r, :].astype(bf), pp_ref[...], preferred_element_type=f32), pnw_ref[...])
         for r in subs]
    hn = [_rms(hs, gnw_ref[...]).astype(bf) for hs in h]
    acc = [jnp.dot(hs, gw_ref[...], preferred_element_type=f32) for hs in hn]
    for r, hs, es, a in zip(subs, h, e, acc):
        o_ref[r, :] = _rms(hs + _sigmoid(a + gb_ref[...]) * es, fnw_ref[...])


def _outproj(x2, yh, yg, p2, wo, pp, pnw, gnw, gw, gb, fnw, tm=512):
    T, D = x2.shape
    row = lambda w: pl.BlockSpec((tm, w), lambda i: (i, 0))
    once = pl.Buffered(1)
    full = lambda a: pl.BlockSpec(a.shape, lambda i: (0, 0), pipeline_mode=once)
    return pl.pallas_call(
        _out_kernel,
        grid=(T // tm,),
        in_specs=[row(D), row(D_HGRN), row(D_GLA), row(PLE_DIM),
                  pl.BlockSpec((D_HGRN, D), lambda i: (0, 0), pipeline_mode=once),
                  pl.BlockSpec((D_GLA, D), lambda i: (D_HGRN // D_GLA, 0), pipeline_mode=once),
                  full(pp), full(pnw), full(gnw), full(gw), full(gb), full(fnw)],
        out_specs=row(D),
        out_shape=jax.ShapeDtypeStruct((T, D), jnp.float32),
        compiler_params=pltpu.CompilerParams(
            dimension_semantics=("parallel",),
            vmem_limit_bytes=OUT_VMEM_LIMIT),
        name="outproj_ple",
    )(x2, yh, yg, p2, wo, wo, pp, pnw, gnw, gw, gb, fnw)


def kernel(x, p, norm_mix_w, w_in, hgrn_lb, gla_gate_w2, gla_gate_b, hgrn_norm_w, gla_norm_w,
           w_out, ple_proj, ple_norm_w, ple_gate_norm_w, ple_gate_w, ple_gate_b, final_norm_w):
    B, T, D = x.shape
    bf = jnp.bfloat16
    x2 = x.reshape(B * T, D)
    p2 = p[0].reshape(B * T, PLE_DIM)
    row = lambda a: a.reshape(1, -1)

    w_t = jnp.swapaxes(w_in[0], 0, 1)
    wlr_t = jnp.pad(w_t[D_MAIN:], ((0, LANE - GLA_GATE_RANK), (0, 0)))
    u, lr = _norm(x2, row(norm_mix_w[0]), wlr_t)
    qf = _inproj(u, w_t, lambda j: j + 2 * (j // 2), 3, jnp.float32, "inproj_qf")
    vz = _inproj(u, w_t, lambda j: j + 2 + j // 2, 4, bf, "inproj_vz")

    y_h = _hgrn(qf, vz, hgrn_lb)
    w2_pad = jnp.pad(gla_gate_w2[0], ((0, LANE - GLA_GATE_RANK), (0, 0)))
    y_g = _gla(qf, vz, lr, w2_pad, row(gla_gate_b[0]))

    head_gain = jnp.concatenate([hgrn_norm_w[0], gla_norm_w[0]])[:, None]
    out = _outproj(x2, y_h, y_g, p2, (w_out[0] * head_gain).astype(bf), ple_proj[0].astype(bf),
                   row(ple_norm_w[0]), row(ple_gate_norm_w[0]), ple_gate_w[0].astype(bf),
                   row(ple_gate_b[0]), row(final_norm_w))
    return out.reshape(B, T, D)
```

```python
import functools

import jax
import jax.numpy as jnp
from jax import lax
from jax.experimental import pallas as pl
from jax.experimental.pallas import tpu as pltpu

D_MODEL = 2048
D_HGRN = 1024
HGRN_HEADS = 8
HGRN_DK = 128
HGRN_DV = 128
GLA_HEADS = 4
GLA_DK = 128
GLA_DV = 256
GLA_DK_TOTAL = 512
D_GLA = 1024
GLA_GATE_RANK = 16
GLA_GATE_TAU = 16.0
PLE_DIM = 256
EPS = 1e-6

LANE = 128
SUBLANE = 8
D_MAIN = 4 * D_HGRN + 2 * GLA_DK_TOTAL + 2 * D_GLA
PANEL = 1024
CHUNK = 64
DECAY_RANGE_LIMIT = 86.0
LOG2E = 1.4426950408889634
VMEM_LIMIT = 48 * 1024 * 1024
INPROJ_VMEM_LIMIT = 58 * 1024 * 1024

_NT = (((1,), (1,)), ((), ()))
_TN = (((0,), (0,)), ((), ()))


def _sigmoid(x):
    return 0.5 + 0.5 * jnp.tanh(0.5 * x)


def _silu(x):
    h = 0.5 * x
    return h + h * jnp.tanh(h)


def _rms(x, w):
    return x * lax.rsqrt(jnp.mean(x * x, axis=-1, keepdims=True) + EPS) * w


def _inproj_kernel(*refs, with_gate_rank):
    if with_gate_rank:
        x_ref, nw_ref, w_ref, wlr_ref, o_ref, lr_ref, wb_ref = refs
    else:
        x_ref, nw_ref, w_ref, o_ref, wb_ref = refs
    bf = jnp.bfloat16

    @pl.when(pl.program_id(1) == 0)
    def _():
        wb_ref[...] = (w_ref[...] * nw_ref[...]).astype(bf)

    def body(gate_rank):
        x = x_ref[...]
        r = lax.rsqrt(jnp.mean(x * x, axis=-1, keepdims=True) + EPS)
        xb = x.astype(bf)
        o_ref[...] = (lax.dot_general(xb, wb_ref[...], _NT, preferred_element_type=jnp.float32)
                      * r).astype(o_ref.dtype)
        if gate_rank:
            wlr = (wlr_ref[...] * nw_ref[...]).astype(bf)
            lr_ref[...] = lax.dot_general(xb, wlr, _NT, preferred_element_type=jnp.float32) * r

    if with_gate_rank:
        first_panel = pl.program_id(0) == 0
        pl.when(first_panel)(lambda: body(True))
        pl.when(jnp.logical_not(first_panel))(lambda: body(False))
    else:
        body(False)


def _inproj(x2, norm_w, w_t, wlr_t, panel_of, n_panels, out_dtype, name, tm=1024):
    T, D = x2.shape
    n_rows = T // tm
    with_gate_rank = wlr_t is not None
    in_specs = [pl.BlockSpec((tm, D), lambda j, i: (i, 0)),
                pl.BlockSpec((1, D), lambda j, i: (0, 0)),
                pl.BlockSpec((PANEL, D), lambda j, i: (panel_of(j), 0))]
    out_specs = [pl.BlockSpec((tm, PANEL), lambda j, i: (i, j))]
    out_shape = [jax.ShapeDtypeStruct((T, n_panels * PANEL), out_dtype)]
    args = [x2, norm_w, w_t]
    if with_gate_rank:
        in_specs.append(pl.BlockSpec((LANE, D), lambda j, i: (0, 0)))
        out_specs.append(pl.BlockSpec((tm, LANE), lambda j, i: (jnp.where(j == 0, i, n_rows - 1), 0)))
        out_shape.append(jax.ShapeDtypeStruct((T, LANE), jnp.float32))
        args.append(wlr_t)
    return pl.pallas_call(
        functools.partial(_inproj_kernel, with_gate_rank=with_gate_rank),
        grid=(n_panels, n_rows),
        in_specs=in_specs,
        out_specs=out_specs,
        out_shape=out_shape,
        scratch_shapes=[pltpu.VMEM((PANEL, D), jnp.bfloat16)],
        compiler_params=pltpu.CompilerParams(
            dimension_semantics=("parallel", "arbitrary"),
            vmem_limit_bytes=INPROJ_VMEM_LIMIT),
        name=name,
    )(*args)


def _scan_stages(qkg, v, state, finish):
    bf = jnp.bfloat16
    f32 = jnp.float32
    dv, dk = state.shape
    blk = 2 * CHUNK
    row = lax.broadcasted_iota(jnp.int32, (blk, blk), 0)
    col = lax.broadcasted_iota(jnp.int32, (blk, blk), 1)
    causal = (col <= row) & (col >= (row & -CHUNK))
    res = {}

    def gates():
        q, k, g = qkg()
        g_hi = g.astype(bf)
        g_lo = (g - g_hi.astype(f32)).astype(bf)
        g2 = jnp.concatenate([g_hi, g_lo], axis=1)
        tri = causal.astype(bf)
        res.update(q=q, k=k, n_blk=g.shape[0] // blk, n_chk=g.shape[0] // CHUNK)
        res['b2'] = [jnp.dot(tri, g2[r * blk:(r + 1) * blk], preferred_element_type=f32)
                     for r in range(res['n_blk'])]

    def decay():
        b = jnp.concatenate([b2[:, :dk] + b2[:, dk:] for b2 in res.pop('b2')], axis=0)
        eb = jnp.exp2(b)
        res['qh'] = (res.pop('q') * eb).astype(bf)
        res['kt'] = (res.pop('k') / eb).astype(bf)
        res['vb'] = v().astype(bf)
        res['eb_last'] = [eb[(c + 1) * CHUNK - 1:(c + 1) * CHUNK, :] for c in range(res['n_chk'])]
        res['b_min'] = jnp.min(b)

    def scores():
        qh, kt, vb = res['qh'], res['kt'], res['vb']
        res['attn'] = [lax.dot_general(qh[r * blk:(r + 1) * blk], kt[r * blk:(r + 1) * blk], _NT,
                                       preferred_element_type=f32) for r in range(res['n_blk'])]
        res['upd'] = [lax.dot_general(vb[c * CHUNK:(c + 1) * CHUNK], kt[c * CHUNK:(c + 1) * CHUNK], _TN,
                                      preferred_element_type=f32) for c in range(res['n_chk'])]

    def carry():
        res['attn'] = [jnp.where(causal, a, 0.0).astype(bf) for a in res['attn']]
        st, st_in = state, []
        for upd, eb_last in zip(res.pop('upd'), res.pop('eb_last')):
            st_in.append(st.astype(bf))
            st = (st + upd) * eb_last
        res.update(st_in=st_in, state=st)

    def outputs():
        qh, vb = res.pop('qh'), res.pop('vb')
        res['intra'] = [jnp.dot(a, vb[r * blk:(r + 1) * blk], preferred_element_type=f32)
                        for r, a in enumerate(res.pop('attn'))]
        res['inter'] = [lax.dot_general(qh[c * CHUNK:(c + 1) * CHUNK], s, _NT, preferred_element_type=f32)
                        for c, s in enumerate(res.pop('st_in'))]

    def done():
        finish(jnp.concatenate(res.pop('intra'), axis=0) + jnp.concatenate(res.pop('inter'), axis=0))

    return [gates, decay, scores, carry, outputs, done], res


def _scan_tile_exact(qkg_ref, v_ref, st_ref, o_ref):
    bf = jnp.bfloat16
    f32 = jnp.float32
    n_heads, dv, dk = st_ref.shape
    tt = o_ref.shape[0]
    rowk = lax.broadcasted_iota(jnp.int32, (CHUNK, dk), 0)
    rowc = lax.broadcasted_iota(jnp.int32, (CHUNK, CHUNK), 0)
    colc = lax.broadcasted_iota(jnp.int32, (CHUNK, CHUNK), 1)
    tri = (colc <= rowc).astype(bf)

    for h in range(n_heads):
        ks = slice(h * dk, (h + 1) * dk)
        vs = slice(h * dv, (h + 1) * dv)

        def chunk(c, carry, ks=ks, vs=vs, h=h):
            r0 = pl.multiple_of(c * CHUNK, CHUNK)
            rows = pl.ds(r0, CHUNK)
            qc = qkg_ref[0, rows, ks]
            kc = qkg_ref[1, rows, ks]
            gc = qkg_ref[2, rows, ks]
            g_hi = gc.astype(bf)
            g_lo = (gc - g_hi.astype(f32)).astype(bf)
            b = (jnp.dot(tri, g_hi, preferred_element_type=f32)
                 + jnp.dot(tri, g_lo, preferred_element_type=f32))
            qkg_ref[2, rows, ks] = b

            def columns(jg, attn):
                j0 = pl.multiple_of(jg * SUBLANE, SUBLANE)
                b_rows = qkg_ref[2, pl.ds(r0 + j0, SUBLANE), ks]
                k_rows = qkg_ref[1, pl.ds(r0 + j0, SUBLANE), ks]
                for r in range(SUBLANE):
                    j = j0 + r
                    decay = jnp.exp2(jnp.where(rowk >= j, b - b_rows[r:r + 1], 0.0))
                    colv = jnp.sum(qc * k_rows[r:r + 1] * decay, axis=1, keepdims=True)
                    attn = jnp.where(colc == j, colv, attn)
                return attn

            attn = lax.fori_loop(0, CHUNK // SUBLANE, columns, jnp.zeros((CHUNK, CHUNK), f32))
            attn = jnp.where(colc <= rowc, attn, 0.0).astype(bf)
            vb = v_ref[rows, vs].astype(bf)
            st = st_ref[h]
            b_last = b[CHUNK - 1:CHUNK, :]
            o_ref[rows, vs] = (
                jnp.dot(attn, vb, preferred_element_type=f32)
                + lax.dot_general((qc * jnp.exp2(b)).astype(bf), st.astype(bf), _NT,
                                  preferred_element_type=f32))
            k_dec = (kc * jnp.exp2(b_last - b)).astype(bf)
            st_ref[h] = st * jnp.exp2(b_last) + lax.dot_general(vb, k_dec, _TN,
                                                                preferred_element_type=f32)
            return carry

        lax.fori_loop(0, tt // CHUNK, chunk, 0)


def _readout(o, z, n_heads, dk):
    dv = o.shape[1] // n_heads

    def head(oh):
        return oh * lax.rsqrt(jnp.mean(oh * oh, axis=-1, keepdims=True) + EPS * dk)

    on = jnp.concatenate([head(o[:, h * dv:(h + 1) * dv]) for h in range(n_heads)], axis=1)
    return on.astype(z.dtype) * _silu(z)


def _scan_and_readout(qkg, v_ref, z_ref, y_ref, st_ref, qkg_ref, o_ref):
    @pl.when(pl.program_id(1) == 0)
    def _():
        st_ref[...] = jnp.zeros_like(st_ref)

    n_heads, dv, dk = st_ref.shape
    heads = []
    for h in range(n_heads):
        ks = slice(h * dk, (h + 1) * dk)
        vs = slice(h * dv, (h + 1) * dv)

        def finish(o, vs=vs):
            y_ref[:, vs] = _readout(o, z_ref[:, vs], 1, dk).astype(y_ref.dtype)

        heads.append(_scan_stages(lambda ks=ks: qkg(ks), lambda vs=vs: v_ref[:, vs], st_ref[h], finish))
    n_stages = len(heads[0][0])
    for tick in range(n_heads + n_stages - 1):
        for h in range(n_heads):
            if 0 <= tick - h < n_stages:
                heads[h][0][tick - h]()
    st_new = [res['state'] for _, res in heads]
    b_min = heads[0][1]['b_min']
    for _, res in heads[1:]:
        b_min = jnp.minimum(b_min, res['b_min'])
    in_range = b_min >= -DECAY_RANGE_LIMIT

    @pl.when(in_range)
    def _():
        for h in range(n_heads):
            st_ref[h] = st_new[h]

    @pl.when(jnp.logical_not(in_range))
    def _():
        q, k, g = qkg(slice(None))
        qkg_ref[0] = q
        qkg_ref[1] = k
        qkg_ref[2] = g
        _scan_tile_exact(qkg_ref, v_ref, st_ref, o_ref)
        y_ref[...] = _readout(o_ref[...], z_ref[...], n_heads, dk).astype(y_ref.dtype)


def _hgrn_kernel(lb_ref, hq_ref, hf_ref, hi_ref, hz_ref, y_ref, st_ref, qkg_ref, o_ref):
    def qkg(cols):
        lbp = lb_ref[:, cols]
        e = jnp.exp(lbp - jnp.max(lbp, axis=0, keepdims=True))
        lb = e[0:1, :] / jnp.sum(e, axis=0, keepdims=True)
        f = lb + (1.0 - lb) * _sigmoid(hf_ref[:, cols])
        return _silu(hq_ref[:, cols]), 1.0 - f, jnp.log2(f)

    _scan_and_readout(qkg, hi_ref, hz_ref, y_ref, st_ref, qkg_ref, o_ref)


def _gla_kernel(gq_ref, gk_ref, gv_ref, gz_ref, lr_ref, w2_ref, gb_ref, y_ref, st_ref,
                qkg_ref, o_ref):
    def qkg(cols):
        logit = jnp.dot(lr_ref[...].astype(jnp.bfloat16), w2_ref[:, cols].astype(jnp.bfloat16),
                        preferred_element_type=jnp.float32) + gb_ref[:, cols]
        log2_sig = (jnp.minimum(logit, 0.0) * LOG2E
                    - jnp.log2(1.0 + jnp.exp2(jnp.abs(logit) * -LOG2E)))
        return gq_ref[:, cols], gk_ref[:, cols], log2_sig / GLA_GATE_TAU

    _scan_and_readout(qkg, gv_ref, gz_ref, y_ref, st_ref, qkg_ref, o_ref)


def _hgrn(qf, vz, hgrn_lb, tt=512, hp=8):
    T = qf.shape[0]
    w = hp * HGRN_DK
    nb = D_HGRN // w
    blk = lambda sec: pl.BlockSpec((tt, w), lambda h, t: (t, sec * nb + h))
    return pl.pallas_call(
        _hgrn_kernel,
        grid=(HGRN_HEADS // hp, T // tt),
        in_specs=[
            pl.BlockSpec((2, w), lambda h, t: (0, h)),
            blk(0), blk(1), blk(0), blk(1),
        ],
        out_specs=pl.BlockSpec((tt, w), lambda h, t: (t, h)),
        out_shape=jax.ShapeDtypeStruct((T, D_HGRN), jnp.bfloat16),
        scratch_shapes=[pltpu.VMEM((hp, HGRN_DV, HGRN_DK), jnp.float32),
                        pltpu.VMEM((3, tt, w), jnp.float32),
                        pltpu.VMEM((tt, w), jnp.float32)],
        compiler_params=pltpu.CompilerParams(
            dimension_semantics=("parallel", "arbitrary"),
            vmem_limit_bytes=VMEM_LIMIT),
        name="hgrn_scan",
    )(hgrn_lb, qf, qf, vz, vz)


def _gla(qf, vz, lr, w2_pad, gate_b, tt=512, hp=4):
    T = qf.shape[0]
    wk, wv = hp * GLA_DK, hp * GLA_DV
    q0 = 2 * D_HGRN // wk
    k0 = q0 + GLA_DK_TOTAL // wk
    v0 = 2 * D_HGRN // wv
    z0 = v0 + D_GLA // wv
    return pl.pallas_call(
        _gla_kernel,
        grid=(GLA_HEADS // hp, T // tt),
        in_specs=[
            pl.BlockSpec((tt, wk), lambda h, t: (t, q0 + h)),
            pl.BlockSpec((tt, wk), lambda h, t: (t, k0 + h)),
            pl.BlockSpec((tt, wv), lambda h, t: (t, v0 + h)),
            pl.BlockSpec((tt, wv), lambda h, t: (t, z0 + h)),
            pl.BlockSpec((tt, LANE), lambda h, t: (t, 0)),
            pl.BlockSpec((LANE, wk), lambda h, t: (0, h)),
            pl.BlockSpec((1, wk), lambda h, t: (0, h)),
        ],
        out_specs=pl.BlockSpec((tt, wv), lambda h, t: (t, h)),
        out_shape=jax.ShapeDtypeStruct((T, D_GLA), jnp.bfloat16),
        scratch_shapes=[pltpu.VMEM((hp, GLA_DV, GLA_DK), jnp.float32),
                        pltpu.VMEM((3, tt, wk), jnp.float32),
                        pltpu.VMEM((tt, wv), jnp.float32)],
        compiler_params=pltpu.CompilerParams(
            dimension_semantics=("parallel", "arbitrary"),
            vmem_limit_bytes=VMEM_LIMIT),
        name="gla_scan",
    )(qf, qf, vz, vz, lr, w2_pad, gate_b)


OUT_SUB = 2
OUT_VMEM_LIMIT = 58 * 1024 * 1024


def _out_kernel(x_ref, yh_ref, yg_ref, p_ref, woh_ref, wog_ref, pp_ref, pnw_ref, gnw_ref,
                gw_ref, gb_ref, fnw_ref, o_ref):
    f32 = jnp.float32
    bf = jnp.bfloat16
    rows = o_ref.shape[0] // OUT_SUB
    subs = [slice(s * rows, (s + 1) * rows) for s in range(OUT_SUB)]
    h = [x_ref[r, :]
         + jnp.dot(yh_ref[r, :], woh_ref[...], preferred_element_type=f32)
         + jnp.dot(yg_ref[r, :], wog_ref[...], preferred_element_type=f32) for r in subs]
    e = [_rms(jnp.dot(p_ref[r, :].astype(bf), pp_ref[...], preferred_element_type=f32), pnw_ref[...])
         for r in subs]
    hn = [_rms(hs, gnw_ref[...]).astype(bf) for hs in h]
    acc = [jnp.dot(hs, gw_ref[...], preferred_element_type=f32) for hs in hn]
    for r, hs, es, a in zip(subs, h, e, acc):
        o_ref[r, :] = _rms(hs + _sigmoid(a + gb_ref[...]) * es, fnw_ref[...])


def _outproj(x2, yh, yg, p2, wo, pp, pnw, gnw, gw, gb, fnw, tm=512):
    T, D = x2.shape
    row = lambda w: pl.BlockSpec((tm, w), lambda i: (i, 0))
    once = pl.Buffered(1)
    full = lambda a: pl.BlockSpec(a.shape, lambda i: (0, 0), pipeline_mode=once)
    return pl.pallas_call(
        _out_kernel,
        grid=(T // tm,),
        in_specs=[row(D), row(D_HGRN), row(D_GLA), row(PLE_DIM),
                  pl.BlockSpec((D_HGRN, D), lambda i: (0, 0), pipeline_mode=once),
                  pl.BlockSpec((D_GLA, D), lambda i: (D_HGRN // D_GLA, 0), pipeline_mode=once),
                  full(pp), full(pnw), full(gnw), full(gw), full(gb), full(fnw)],
        out_specs=row(D),
        out_shape=jax.ShapeDtypeStruct((T, D), jnp.float32),
        compiler_params=pltpu.CompilerParams(
            dimension_semantics=("parallel",),
            vmem_limit_bytes=OUT_VMEM_LIMIT),
        name="outproj_ple",
    )(x2, yh, yg, p2, wo, wo, pp, pnw, gnw, gw, gb, fnw)


def kernel(x, p, norm_mix_w, w_in, hgrn_lb, gla_gate_w2, gla_gate_b, hgrn_norm_w, gla_norm_w,
           w_out, ple_proj, ple_norm_w, ple_gate_norm_w, ple_gate_w, ple_gate_b, final_norm_w):
    B, T, D = x.shape
    bf = jnp.bfloat16
    x2 = x.reshape(B * T, D)
    p2 = p[0].reshape(B * T, PLE_DIM)
    row = lambda a: a.reshape(1, -1)

    w_t = jnp.swapaxes(w_in[0], 0, 1)
    wlr_t = jnp.pad(w_t[D_MAIN:], ((0, LANE - GLA_GATE_RANK), (0, 0)))
    nw = row(norm_mix_w[0])
    qf, lr = _inproj(x2, nw, w_t, wlr_t, lambda j: j + 2 * (j // 2), 3, jnp.float32,
                     "inproj_qf")
    (vz,) = _inproj(x2, nw, w_t, None, lambda j: j + 2 + j // 2, 4, bf, "inproj_vz")

    y_h = _hgrn(qf, vz, hgrn_lb)
    w2_pad = jnp.pad(gla_gate_w2[0], ((0, LANE - GLA_GATE_RANK), (0, 0)))
    y_g = _gla(qf, vz, lr, w2_pad, row(gla_gate_b[0]))

    head_gain = jnp.concatenate([hgrn_norm_w[0], gla_norm_w[0]])[:, None]
    out = _outproj(x2, y_h, y_g, p2, (w_out[0] * head_gain).astype(bf), ple_proj[0].astype(bf),
                   row(ple_norm_w[0]), row(ple_gate_norm_w[0]), ple_gate_w[0].astype(bf),
                   row(ple_gate_b[0]), row(final_norm_w))
    return out.reshape(B, T, D)
```

```python
import functools

import jax
import jax.numpy as jnp
from jax import lax
from jax.experimental import pallas as pl
from jax.experimental.pallas import tpu as pltpu

D_MODEL = 2048
D_HGRN = 1024
HGRN_HEADS = 8
HGRN_DK = 128
HGRN_DV = 128
GLA_HEADS = 4
GLA_DK = 128
GLA_DV = 256
GLA_DK_TOTAL = 512
D_GLA = 1024
GLA_GATE_RANK = 16
GLA_GATE_TAU = 16.0
PLE_DIM = 256
EPS = 1e-6

LANE = 128
SUBLANE = 8
D_MAIN = 4 * D_HGRN + 2 * GLA_DK_TOTAL + 2 * D_GLA
PANEL = 1024
CHUNK = 64
DECAY_RANGE_LIMIT = 86.0
LOG2E = 1.4426950408889634
VMEM_LIMIT = 48 * 1024 * 1024
INPROJ_VMEM_LIMIT = 58 * 1024 * 1024

_NT = (((1,), (1,)), ((), ()))
_TN = (((0,), (0,)), ((), ()))


def _sigmoid(x):
    return 0.5 + 0.5 * jnp.tanh(0.5 * x)


def _silu_of_half(h):
    return h + h * jnp.tanh(h)


def _rms(x, w):
    return x * lax.rsqrt(jnp.mean(x * x, axis=-1, keepdims=True) + EPS) * w


def _inproj_kernel(*refs, with_gate_rank, half_panels):
    if with_gate_rank:
        x_ref, nw_ref, w_ref, wlr_ref, o_ref, lr_ref, wb_ref = refs
    else:
        x_ref, nw_ref, w_ref, o_ref, wb_ref = refs
    bf = jnp.bfloat16

    @pl.when(pl.program_id(1) == 0)
    def _():
        wb_ref[...] = (w_ref[...] * nw_ref[...]).astype(bf)

    def body(gate_rank):
        x = x_ref[...]
        r = lax.rsqrt(jnp.mean(x * x, axis=-1, keepdims=True) + EPS)
        xb = x.astype(bf)
        half = functools.reduce(jnp.logical_or, [pl.program_id(0) == jp for jp in half_panels])
        o_ref[...] = (lax.dot_general(xb, wb_ref[...], _NT, preferred_element_type=jnp.float32)
                      * (r * jnp.where(half, 0.5, 1.0))).astype(o_ref.dtype)
        if gate_rank:
            wlr = (wlr_ref[...] * nw_ref[...]).astype(bf)
            lr_ref[...] = lax.dot_general(xb, wlr, _NT, preferred_element_type=jnp.float32) * r

    if with_gate_rank:
        first_panel = pl.program_id(0) == 0
        pl.when(first_panel)(lambda: body(True))
        pl.when(jnp.logical_not(first_panel))(lambda: body(False))
    else:
        body(False)


def _inproj(x2, norm_w, w_t, wlr_t, panel_of, n_panels, half_panels, out_dtype, name, tm=1024):
    T, D = x2.shape
    n_rows = T // tm
    with_gate_rank = wlr_t is not None
    in_specs = [pl.BlockSpec((tm, D), lambda j, i: (i, 0)),
                pl.BlockSpec((1, D), lambda j, i: (0, 0)),
                pl.BlockSpec((PANEL, D), lambda j, i: (panel_of(j), 0))]
    out_specs = [pl.BlockSpec((tm, PANEL), lambda j, i: (i, j))]
    out_shape = [jax.ShapeDtypeStruct((T, n_panels * PANEL), out_dtype)]
    args = [x2, norm_w, w_t]
    if with_gate_rank:
        in_specs.append(pl.BlockSpec((LANE, D), lambda j, i: (0, 0)))
        out_specs.append(pl.BlockSpec((tm, LANE), lambda j, i: (jnp.where(j == 0, i, n_rows - 1), 0)))
        out_shape.append(jax.ShapeDtypeStruct((T, LANE), jnp.float32))
        args.append(wlr_t)
    return pl.pallas_call(
        functools.partial(_inproj_kernel, with_gate_rank=with_gate_rank, half_panels=half_panels),
        grid=(n_panels, n_rows),
        in_specs=in_specs,
        out_specs=out_specs,
        out_shape=out_shape,
        scratch_shapes=[pltpu.VMEM((PANEL, D), jnp.bfloat16)],
        compiler_params=pltpu.CompilerParams(
            dimension_semantics=("parallel", "arbitrary"),
            vmem_limit_bytes=INPROJ_VMEM_LIMIT),
        name=name,
    )(*args)


def _scan_stages(qkg, v, state, finish):
    bf = jnp.bfloat16
    f32 = jnp.float32
    dv, dk = state.shape
    blk = 2 * CHUNK
    row = lax.broadcasted_iota(jnp.int32, (blk, blk), 0)
    col = lax.broadcasted_iota(jnp.int32, (blk, blk), 1)
    causal = (col <= row) & (col >= (row & -CHUNK))
    res = {}

    def gates():
        q, k, g = qkg()
        g_hi = g.astype(bf)
        g_lo = (g - g_hi.astype(f32)).astype(bf)
        g2 = jnp.concatenate([g_hi, g_lo], axis=1)
        tri = causal.astype(bf)
        res.update(q=q, k=k, n_blk=g.shape[0] // blk, n_chk=g.shape[0] // CHUNK)
        res['b2'] = [jnp.dot(tri, g2[r * blk:(r + 1) * blk], preferred_element_type=f32)
                     for r in range(res['n_blk'])]

    def decay():
        b = jnp.concatenate([b2[:, :dk] + b2[:, dk:] for b2 in res.pop('b2')], axis=0)
        eb = jnp.exp2(b)
        res['qh'] = (res.pop('q') * eb).astype(bf)
        res['kt'] = (res.pop('k') / eb).astype(bf)
        res['vb'] = v().astype(bf)
        res['eb_last'] = [eb[(c + 1) * CHUNK - 1:(c + 1) * CHUNK, :] for c in range(res['n_chk'])]
        res['b_min'] = jnp.min(jnp.concatenate(
            [b[(c + 1) * CHUNK - 1:(c + 1) * CHUNK, :] for c in range(res['n_chk'])], axis=0))

    def scores():
        qh, kt, vb = res['qh'], res['kt'], res['vb']
        res['attn'] = [lax.dot_general(qh[r * blk:(r + 1) * blk], kt[r * blk:(r + 1) * blk], _NT,
                                       preferred_element_type=f32) for r in range(res['n_blk'])]
        res['upd'] = [lax.dot_general(vb[c * CHUNK:(c + 1) * CHUNK], kt[c * CHUNK:(c + 1) * CHUNK], _TN,
                                      preferred_element_type=f32) for c in range(res['n_chk'])]

    def carry():
        res['attn'] = [jnp.where(causal, a, 0.0).astype(bf) for a in res['attn']]
        st, st_in = state, []
        for upd, eb_last in zip(res.pop('upd'), res.pop('eb_last')):
            st_in.append(st.astype(bf))
            st = (st + upd) * eb_last
        res.update(st_in=st_in, state=st)

    def outputs():
        qh, vb = res.pop('qh'), res.pop('vb')
        res['intra'] = [jnp.dot(a, vb[r * blk:(r + 1) * blk], preferred_element_type=f32)
                        for r, a in enumerate(res.pop('attn'))]
        res['inter'] = [lax.dot_general(qh[c * CHUNK:(c + 1) * CHUNK], s, _NT, preferred_element_type=f32)
                        for c, s in enumerate(res.pop('st_in'))]

    def done():
        finish(jnp.concatenate(res.pop('intra'), axis=0) + jnp.concatenate(res.pop('inter'), axis=0))

    return [gates, decay, scores, carry, outputs, done], res


def _scan_tile_exact(qkg_ref, v_ref, st_ref, o_ref):
    bf = jnp.bfloat16
    f32 = jnp.float32
    n_heads, dv, dk = st_ref.shape
    tt = o_ref.shape[0]
    rowk = lax.broadcasted_iota(jnp.int32, (CHUNK, dk), 0)
    rowc = lax.broadcasted_iota(jnp.int32, (CHUNK, CHUNK), 0)
    colc = lax.broadcasted_iota(jnp.int32, (CHUNK, CHUNK), 1)
    tri = (colc <= rowc).astype(bf)

    for h in range(n_heads):
        ks = slice(h * dk, (h + 1) * dk)
        vs = slice(h * dv, (h + 1) * dv)

        def chunk(c, carry, ks=ks, vs=vs, h=h):
            r0 = pl.multiple_of(c * CHUNK, CHUNK)
            rows = pl.ds(r0, CHUNK)
            qc = qkg_ref[0, rows, ks]
            kc = qkg_ref[1, rows, ks]
            gc = qkg_ref[2, rows, ks]
            g_hi = gc.astype(bf)
            g_lo = (gc - g_hi.astype(f32)).astype(bf)
            b = (jnp.dot(tri, g_hi, preferred_element_type=f32)
                 + jnp.dot(tri, g_lo, preferred_element_type=f32))
            qkg_ref[2, rows, ks] = b

            def columns(jg, attn):
                j0 = pl.multiple_of(jg * SUBLANE, SUBLANE)
                b_rows = qkg_ref[2, pl.ds(r0 + j0, SUBLANE), ks]
                k_rows = qkg_ref[1, pl.ds(r0 + j0, SUBLANE), ks]
                for r in range(SUBLANE):
                    j = j0 + r
                    decay = jnp.exp2(jnp.where(rowk >= j, b - b_rows[r:r + 1], 0.0))
                    colv = jnp.sum(qc * k_rows[r:r + 1] * decay, axis=1, keepdims=True)
                    attn = jnp.where(colc == j, colv, attn)
                return attn

            attn = lax.fori_loop(0, CHUNK // SUBLANE, columns, jnp.zeros((CHUNK, CHUNK), f32))
            attn = jnp.where(colc <= rowc, attn, 0.0).astype(bf)
            vb = v_ref[rows, vs].astype(bf)
            st = st_ref[h]
            b_last = b[CHUNK - 1:CHUNK, :]
            o_ref[rows, vs] = (
                jnp.dot(attn, vb, preferred_element_type=f32)
                + lax.dot_general((qc * jnp.exp2(b)).astype(bf), st.astype(bf), _NT,
                                  preferred_element_type=f32))
            k_dec = (kc * jnp.exp2(b_last - b)).astype(bf)
            st_ref[h] = st * jnp.exp2(b_last) + lax.dot_general(vb, k_dec, _TN,
                                                                preferred_element_type=f32)
            return carry

        lax.fori_loop(0, tt // CHUNK, chunk, 0)


def _readout(o, z, n_heads, dk):
    dv = o.shape[1] // n_heads

    def head(oh):
        return oh * lax.rsqrt(jnp.mean(oh * oh, axis=-1, keepdims=True) + EPS * dk)

    on = jnp.concatenate([head(o[:, h * dv:(h + 1) * dv]) for h in range(n_heads)], axis=1)
    return on.astype(z.dtype) * _silu_of_half(z)


def _scan_and_readout(qkg, v_ref, z_ref, y_ref, st_ref, qkg_ref, o_ref):
    @pl.when(pl.program_id(1) == 0)
    def _():
        st_ref[...] = jnp.zeros_like(st_ref)

    n_heads, dv, dk = st_ref.shape
    heads = []
    for h in range(n_heads):
        ks = slice(h * dk, (h + 1) * dk)
        vs = slice(h * dv, (h + 1) * dv)

        def finish(o, vs=vs):
            y_ref[:, vs] = _readout(o, z_ref[:, vs], 1, dk).astype(y_ref.dtype)

        heads.append(_scan_stages(lambda ks=ks: qkg(ks), lambda vs=vs: v_ref[:, vs], st_ref[h], finish))
    n_stages = len(heads[0][0])
    for tick in range(n_heads + n_stages - 1):
        for h in range(n_heads):
            if 0 <= tick - h < n_stages:
                heads[h][0][tick - h]()
    st_new = [res['state'] for _, res in heads]
    b_min = heads[0][1]['b_min']
    for _, res in heads[1:]:
        b_min = jnp.minimum(b_min, res['b_min'])
    in_range = b_min >= -DECAY_RANGE_LIMIT

    @pl.when(in_range)
    def _():
        for h in range(n_heads):
            st_ref[h] = st_new[h]

    @pl.when(jnp.logical_not(in_range))
    def _():
        q, k, g = qkg(slice(None))
        qkg_ref[0] = q
        qkg_ref[1] = k
        qkg_ref[2] = g
        _scan_tile_exact(qkg_ref, v_ref, st_ref, o_ref)
        y_ref[...] = _readout(o_ref[...], z_ref[...], n_heads, dk).astype(y_ref.dtype)


def _hgrn_kernel(lb_ref, hq_ref, hf_ref, hi_ref, hz_ref, y_ref, st_ref, qkg_ref, o_ref):
    def qkg(cols):
        lbp = lb_ref[:, cols]
        e = jnp.exp(lbp - jnp.max(lbp, axis=0, keepdims=True))
        lb = e[0:1, :] / jnp.sum(e, axis=0, keepdims=True)
        f = (0.5 + 0.5 * lb) + (0.5 - 0.5 * lb) * jnp.tanh(hf_ref[:, cols])
        return _silu_of_half(hq_ref[:, cols]), 1.0 - f, jnp.log2(f)

    _scan_and_readout(qkg, hi_ref, hz_ref, y_ref, st_ref, qkg_ref, o_ref)


def _gla_kernel(gq_ref, gk_ref, gv_ref, gz_ref, lr_ref, w2_ref, gb_ref, y_ref, st_ref,
                qkg_ref, o_ref):
    def qkg(cols):
        logit = jnp.dot(lr_ref[...].astype(jnp.bfloat16), w2_ref[:, cols].astype(jnp.bfloat16),
                        preferred_element_type=jnp.float32) + gb_ref[:, cols]
        log2_sig = (jnp.minimum(logit, 0.0) * LOG2E
                    - jnp.log2(1.0 + jnp.exp2(jnp.abs(logit) * -LOG2E)))
        return gq_ref[:, cols], gk_ref[:, cols], log2_sig / GLA_GATE_TAU

    _scan_and_readout(qkg, gv_ref, gz_ref, y_ref, st_ref, qkg_ref, o_ref)


def _hgrn(qf, vz, hgrn_lb, tt=512, hp=8):
    T = qf.shape[0]
    w = hp * HGRN_DK
    nb = D_HGRN // w
    blk = lambda sec: pl.BlockSpec((tt, w), lambda h, t: (t, sec * nb + h))
    return pl.pallas_call(
        _hgrn_kernel,
        grid=(HGRN_HEADS // hp, T // tt),
        in_specs=[
            pl.BlockSpec((2, w), lambda h, t: (0, h)),
            blk(0), blk(1), blk(0), blk(1),
        ],
        out_specs=pl.BlockSpec((tt, w), lambda h, t: (t, h)),
        out_shape=jax.ShapeDtypeStruct((T, D_HGRN), jnp.bfloat16),
        scratch_shapes=[pltpu.VMEM((hp, HGRN_DV, HGRN_DK), jnp.float32),
                        pltpu.VMEM((3, tt, w), jnp.float32),
                        pltpu.VMEM((tt, w), jnp.float32)],
        compiler_params=pltpu.CompilerParams(
            dimension_semantics=("parallel", "arbitrary"),
            vmem_limit_bytes=VMEM_LIMIT),
        name="hgrn_scan",
    )(hgrn_lb, qf, qf, vz, vz)


def _gla(qf, vz, lr, w2_pad, gate_b, tt=512, hp=4):
    T = qf.shape[0]
    wk, wv = hp * GLA_DK, hp * GLA_DV
    q0 = 2 * D_HGRN // wk
    k0 = q0 + GLA_DK_TOTAL // wk
    v0 = 2 * D_HGRN // wv
    z0 = v0 + D_GLA // wv
    return pl.pallas_call(
        _gla_kernel,
        grid=(GLA_HEADS // hp, T // tt),
        in_specs=[
            pl.BlockSpec((tt, wk), lambda h, t: (t, q0 + h)),
            pl.BlockSpec((tt, wk), lambda h, t: (t, k0 + h)),
            pl.BlockSpec((tt, wv), lambda h, t: (t, v0 + h)),
            pl.BlockSpec((tt, wv), lambda h, t: (t, z0 + h)),
            pl.BlockSpec((tt, LANE), lambda h, t: (t, 0)),
            pl.BlockSpec((LANE, wk), lambda h, t: (0, h)),
            pl.BlockSpec((1, wk), lambda h, t: (0, h)),
        ],
        out_specs=pl.BlockSpec((tt, wv), lambda h, t: (t, h)),
        out_shape=jax.ShapeDtypeStruct((T, D_GLA), jnp.bfloat16),
        scratch_shapes=[pltpu.VMEM((hp, GLA_DV, GLA_DK), jnp.float32),
                        pltpu.VMEM((3, tt, wk), jnp.float32),
                        pltpu.VMEM((tt, wv), jnp.float32)],
        compiler_params=pltpu.CompilerParams(
            dimension_semantics=("parallel", "arbitrary"),
            vmem_limit_bytes=VMEM_LIMIT),
        name="gla_scan",
    )(qf, qf, vz, vz, lr, w2_pad, gate_b)


OUT_SUB = 2
OUT_VMEM_LIMIT = 58 * 1024 * 1024


def _out_kernel(x_ref, yh_ref, yg_ref, p_ref, woh_ref, wog_ref, pp_ref, pnw_ref, gnw_ref,
                gw_ref, gb_ref, fnw_ref, o_ref):
    f32 = jnp.float32
    bf = jnp.bfloat16
    rows = o_ref.shape[0] // OUT_SUB
    subs = [slice(s * rows, (s + 1) * rows) for s in range(OUT_SUB)]
    h = [x_ref[r, :]
         + jnp.dot(yh_ref[r, :], woh_ref[...], preferred_element_type=f32)
         + jnp.dot(yg_ref[r, :], wog_ref[...], preferred_element_type=f32) for r in subs]
    e = [_rms(jnp.dot(p_ref[r, :].astype(bf), pp_ref[...], preferred_element_type=f32), pnw_ref[...])
         for r in subs]
    hn = [_rms(hs, gnw_ref[...]).astype(bf) for hs in h]
    acc = [jnp.dot(hs, gw_ref[...], preferred_element_type=f32) for hs in hn]
    for r, hs, es, a in zip(subs, h, e, acc):
        o_ref[r, :] = _rms(hs + _sigmoid(a + gb_ref[...]) * es, fnw_ref[...])


def _outproj(x2, yh, yg, p2, wo, pp, pnw, gnw, gw, gb, fnw, tm=512):
    T, D = x2.shape
    row = lambda w: pl.BlockSpec((tm, w), lambda i: (i, 0))
    once = pl.Buffered(1)
    full = lambda a: pl.BlockSpec(a.shape, lambda i: (0, 0), pipeline_mode=once)
    return pl.pallas_call(
        _out_kernel,
        grid=(T // tm,),
        in_specs=[row(D), row(D_HGRN), row(D_GLA), row(PLE_DIM),
                  pl.BlockSpec((D_HGRN, D), lambda i: (0, 0), pipeline_mode=once),
                  pl.BlockSpec((D_GLA, D), lambda i: (D_HGRN // D_GLA, 0), pipeline_mode=once),
                  full(pp), full(pnw), full(gnw), full(gw), full(gb), full(fnw)],
        out_specs=row(D),
        out_shape=jax.ShapeDtypeStruct((T, D), jnp.float32),
        compiler_params=pltpu.CompilerParams(
            dimension_semantics=("parallel",),
            vmem_limit_bytes=OUT_VMEM_LIMIT),
        name="outproj_ple",
    )(x2, yh, yg, p2, wo, wo, pp, pnw, gnw, gw, gb, fnw)


def kernel(x, p, norm_mix_w, w_in, hgrn_lb, gla_gate_w2, gla_gate_b, hgrn_norm_w, gla_norm_w,
           w_out, ple_proj, ple_norm_w, ple_gate_norm_w, ple_gate_w, ple_gate_b, final_norm_w):
    B, T, D = x.shape
    bf = jnp.bfloat16
    x2 = x.reshape(B * T, D)
    p2 = p[0].reshape(B * T, PLE_DIM)
    row = lambda a: a.reshape(1, -1)

    w_t = jnp.swapaxes(w_in[0], 0, 1)
    wlr_t = jnp.pad(w_t[D_MAIN:], ((0, LANE - GLA_GATE_RANK), (0, 0)))
    nw = row(norm_mix_w[0])
    qf, lr = _inproj(x2, nw, w_t, wlr_t, lambda j: j + 2 * (j // 2), 3, (0, 1), jnp.float32,
                     "inproj_qf")
    (vz,) = _inproj(x2, nw, w_t, None, lambda j: j + 2 + j // 2, 4, (1, 3), bf,
                    "inproj_vz")

    y_h = _hgrn(qf, vz, hgrn_lb)
    w2_pad = jnp.pad(gla_gate_w2[0], ((0, LANE - GLA_GATE_RANK), (0, 0)))
    y_g = _gla(qf, vz, lr, w2_pad, row(gla_gate_b[0]))

    head_gain = jnp.concatenate([hgrn_norm_w[0], gla_norm_w[0]])[:, None]
    out = _outproj(x2, y_h, y_g, p2, (w_out[0] * head_gain).astype(bf), ple_proj[0].astype(bf),
                   row(ple_norm_w[0]), row(ple_gate_norm_w[0]), ple_gate_w[0].astype(bf),
                   row(ple_gate_b[0]), row(final_norm_w))
    return out.reshape(B, T, D)
```

```python
import functools

import jax
import jax.numpy as jnp
from jax import lax
from jax.experimental import pallas as pl
from jax.experimental.pallas import tpu as pltpu

D_MODEL = 2048
D_HGRN = 1024
HGRN_HEADS = 8
HGRN_DK = 128
HGRN_DV = 128
GLA_HEADS = 4
GLA_DK = 128
GLA_DV = 256
GLA_DK_TOTAL = 512
D_GLA = 1024
GLA_GATE_RANK = 16
GLA_GATE_TAU = 16.0
PLE_DIM = 256
EPS = 1e-6

LANE = 128
SUBLANE = 8
D_MAIN = 4 * D_HGRN + 2 * GLA_DK_TOTAL + 2 * D_GLA
PANEL = 1024
CHUNK = 64
DECAY_RANGE_LIMIT = 86.0
LOG2E = 1.4426950408889634
VMEM_LIMIT = 48 * 1024 * 1024
INPROJ_VMEM_LIMIT = 58 * 1024 * 1024

_NT = (((1,), (1,)), ((), ()))
_TN = (((0,), (0,)), ((), ()))


def _sigmoid(x):
    return 0.5 + 0.5 * jnp.tanh(0.5 * x)


def _silu_of_half(h):
    return h + h * jnp.tanh(h)


def _rms(x, w):
    return x * lax.rsqrt(jnp.mean(x * x, axis=-1, keepdims=True) + EPS) * w


def _is_half_panel(half_panels):
    return functools.reduce(jnp.logical_or, [pl.program_id(0) == jp for jp in half_panels])


def _cast_weight_panel(w_ref, nw_ref, wb_ref):
    @pl.when(pl.program_id(1) == 0)
    def _():
        wb_ref[...] = (w_ref[...] * nw_ref[...]).astype(jnp.bfloat16)


def _inproj_first_kernel(x_ref, nw_ref, w_ref, wlr_ref, o_ref, lr_ref, xb_ref, rs_ref, wb_ref, *,
                         half_panels):
    bf = jnp.bfloat16
    _cast_weight_panel(w_ref, nw_ref, wb_ref)

    def body(first_panel):
        x = x_ref[...]
        r = lax.rsqrt(jnp.mean(x * x, axis=-1, keepdims=True) + EPS)
        xb = x.astype(bf)
        o_ref[...] = (lax.dot_general(xb, wb_ref[...], _NT, preferred_element_type=jnp.float32)
                      * (r * jnp.where(_is_half_panel(half_panels), 0.5, 1.0))).astype(o_ref.dtype)
        if first_panel:
            xb_ref[...] = xb
            rs_ref[...] = jnp.broadcast_to(r, rs_ref.shape)
            wlr = (wlr_ref[...] * nw_ref[...]).astype(bf)
            lr_ref[...] = lax.dot_general(xb, wlr, _NT, preferred_element_type=jnp.float32) * r

    first = pl.program_id(0) == 0
    pl.when(first)(lambda: body(True))
    pl.when(jnp.logical_not(first))(lambda: body(False))


def _inproj_rest_kernel(xb_ref, rs_ref, nw_ref, w_ref, o_ref, wb_ref, *, half_panels):
    _cast_weight_panel(w_ref, nw_ref, wb_ref)
    r = rs_ref[:, 0:1] * jnp.where(_is_half_panel(half_panels), 0.5, 1.0)
    o_ref[...] = (lax.dot_general(xb_ref[...], wb_ref[...], _NT, preferred_element_type=jnp.float32)
                  * r).astype(o_ref.dtype)


def _inproj_params():
    return pltpu.CompilerParams(dimension_semantics=("parallel", "arbitrary"),
                                vmem_limit_bytes=INPROJ_VMEM_LIMIT)


def _inproj_first(x2, norm_w, w_t, wlr_t, panel_of, n_panels, half_panels, name, tm=1024):
    T, D = x2.shape
    n_rows = T // tm
    once = lambda j, i: (jnp.where(j == 0, i, n_rows - 1), 0)
    return pl.pallas_call(
        functools.partial(_inproj_first_kernel, half_panels=half_panels),
        grid=(n_panels, n_rows),
        in_specs=[pl.BlockSpec((tm, D), lambda j, i: (i, 0)),
                  pl.BlockSpec((1, D), lambda j, i: (0, 0)),
                  pl.BlockSpec((PANEL, D), lambda j, i: (panel_of(j), 0), pipeline_mode=pl.Buffered(1)),
                  pl.BlockSpec((LANE, D), lambda j, i: (0, 0))],
        out_specs=[pl.BlockSpec((tm, PANEL), lambda j, i: (i, j)),
                   pl.BlockSpec((tm, LANE), once),
                   pl.BlockSpec((tm, D), once),
                   pl.BlockSpec((tm, LANE), once)],
        out_shape=[jax.ShapeDtypeStruct((T, n_panels * PANEL), jnp.float32),
                   jax.ShapeDtypeStruct((T, LANE), jnp.float32),
                   jax.ShapeDtypeStruct((T, D), jnp.bfloat16),
                   jax.ShapeDtypeStruct((T, LANE), jnp.float32)],
        scratch_shapes=[pltpu.VMEM((PANEL, D), jnp.bfloat16)],
        compiler_params=_inproj_params(),
        name=name,
    )(x2, norm_w, w_t, wlr_t)


def _inproj_rest(xb, rs, norm_w, w_t, panel_of, n_panels, half_panels, out_dtype, name, tm=1024):
    T, D = xb.shape
    return pl.pallas_call(
        functools.partial(_inproj_rest_kernel, half_panels=half_panels),
        grid=(n_panels, T // tm),
        in_specs=[pl.BlockSpec((tm, D), lambda j, i: (i, 0)),
                  pl.BlockSpec((tm, LANE), lambda j, i: (i, 0)),
                  pl.BlockSpec((1, D), lambda j, i: (0, 0)),
                  pl.BlockSpec((PANEL, D), lambda j, i: (panel_of(j), 0))],
        out_specs=pl.BlockSpec((tm, PANEL), lambda j, i: (i, j)),
        out_shape=jax.ShapeDtypeStruct((T, n_panels * PANEL), out_dtype),
        scratch_shapes=[pltpu.VMEM((PANEL, D), jnp.bfloat16)],
        compiler_params=_inproj_params(),
        name=name,
    )(xb, rs, norm_w, w_t)


def _scan_stages(qkg, v, state, finish):
    bf = jnp.bfloat16
    f32 = jnp.float32
    dv, dk = state.shape
    blk = 2 * CHUNK
    row = lax.broadcasted_iota(jnp.int32, (blk, blk), 0)
    col = lax.broadcasted_iota(jnp.int32, (blk, blk), 1)
    causal = (col <= row) & (col >= (row & -CHUNK))
    res = {}

    def gates():
        q, k, g = qkg()
        g_hi = g.astype(bf)
        g_lo = (g - g_hi.astype(f32)).astype(bf)
        g2 = jnp.concatenate([g_hi, g_lo], axis=1)
        tri = causal.astype(bf)
        res.update(q=q, k=k, n_blk=g.shape[0] // blk, n_chk=g.shape[0] // CHUNK)
        res['b2'] = [jnp.dot(tri, g2[r * blk:(r + 1) * blk], preferred_element_type=f32)
                     for r in range(res['n_blk'])]

    def decay():
        b = jnp.concatenate([b2[:, :dk] + b2[:, dk:] for b2 in res.pop('b2')], axis=0)
        eb = jnp.exp2(b)
        res['qh'] = (res.pop('q') * eb).astype(bf)
        res['kt'] = (res.pop('k') / eb).astype(bf)
        res['vb'] = v().astype(bf)
        res['eb_last'] = [eb[(c + 1) * CHUNK - 1:(c + 1) * CHUNK, :] for c in range(res['n_chk'])]
        res['b_min'] = jnp.min(jnp.concatenate(
            [b[(c + 1) * CHUNK - 1:(c + 1) * CHUNK, :] for c in range(res['n_chk'])], axis=0))

    def scores():
        qh, kt, vb = res['qh'], res['kt'], res['vb']
        res['attn'] = [lax.dot_general(qh[r * blk:(r + 1) * blk], kt[r * blk:(r + 1) * blk], _NT,
                                       preferred_element_type=f32) for r in range(res['n_blk'])]
        res['upd'] = [lax.dot_general(vb[c * CHUNK:(c + 1) * CHUNK], kt[c * CHUNK:(c + 1) * CHUNK], _TN,
                                      preferred_element_type=f32) for c in range(res['n_chk'])]

    def carry():
        res['attn'] = [jnp.where(causal, a, 0.0).astype(bf) for a in res['attn']]
        st, st_in = state, []
        for upd, eb_last in zip(res.pop('upd'), res.pop('eb_last')):
            st_in.append(st.astype(bf))
            st = (st + upd) * eb_last
        res.update(st_in=st_in, state=st)

    def outputs():
        qh, vb = res.pop('qh'), res.pop('vb')
        res['intra'] = [jnp.dot(a, vb[r * blk:(r + 1) * blk], preferred_element_type=f32)
                        for r, a in enumerate(res.pop('attn'))]
        res['inter'] = [lax.dot_general(qh[c * CHUNK:(c + 1) * CHUNK], s, _NT, preferred_element_type=f32)
                        for c, s in enumerate(res.pop('st_in'))]

    def done():
        finish(jnp.concatenate(res.pop('intra'), axis=0) + jnp.concatenate(res.pop('inter'), axis=0))

    return [gates, decay, scores, carry, outputs, done], res


def _scan_tile_exact(qkg_ref, v_ref, st_ref, o_ref):
    bf = jnp.bfloat16
    f32 = jnp.float32
    n_heads, dv, dk = st_ref.shape
    tt = o_ref.shape[0]
    rowk = lax.broadcasted_iota(jnp.int32, (CHUNK, dk), 0)
    rowc = lax.broadcasted_iota(jnp.int32, (CHUNK, CHUNK), 0)
    colc = lax.broadcasted_iota(jnp.int32, (CHUNK, CHUNK), 1)
    tri = (colc <= rowc).astype(bf)

    for h in range(n_heads):
        ks = slice(h * dk, (h + 1) * dk)
        vs = slice(h * dv, (h + 1) * dv)

        def chunk(c, carry, ks=ks, vs=vs, h=h):
            r0 = pl.multiple_of(c * CHUNK, CHUNK)
            rows = pl.ds(r0, CHUNK)
            qc = qkg_ref[0, rows, ks]
            kc = qkg_ref[1, rows, ks]
            gc = qkg_ref[2, rows, ks]
            g_hi = gc.astype(bf)
            g_lo = (gc - g_hi.astype(f32)).astype(bf)
            b = (jnp.dot(tri, g_hi, preferred_element_type=f32)
                 + jnp.dot(tri, g_lo, preferred_element_type=f32))
            qkg_ref[2, rows, ks] = b

            def columns(jg, attn):
                j0 = pl.multiple_of(jg * SUBLANE, SUBLANE)
                b_rows = qkg_ref[2, pl.ds(r0 + j0, SUBLANE), ks]
                k_rows = qkg_ref[1, pl.ds(r0 + j0, SUBLANE), ks]
                for r in range(SUBLANE):
                    j = j0 + r
                    decay = jnp.exp2(jnp.where(rowk >= j, b - b_rows[r:r + 1], 0.0))
                    colv = jnp.sum(qc * k_rows[r:r + 1] * decay, axis=1, keepdims=True)
                    attn = jnp.where(colc == j, colv, attn)
                return attn

            attn = lax.fori_loop(0, CHUNK // SUBLANE, columns, jnp.zeros((CHUNK, CHUNK), f32))
            attn = jnp.where(colc <= rowc, attn, 0.0).astype(bf)
            vb = v_ref[rows, vs].astype(bf)
            st = st_ref[h]
            b_last = b[CHUNK - 1:CHUNK, :]
            o_ref[rows, vs] = (
                jnp.dot(attn, vb, preferred_element_type=f32)
                + lax.dot_general((qc * jnp.exp2(b)).astype(bf), st.astype(bf), _NT,
                                  preferred_element_type=f32))
            k_dec = (kc * jnp.exp2(b_last - b)).astype(bf)
            st_ref[h] = st * jnp.exp2(b_last) + lax.dot_general(vb, k_dec, _TN,
                                                                preferred_element_type=f32)
            return carry

        lax.fori_loop(0, tt // CHUNK, chunk, 0)


def _readout(o, z, n_heads, dk):
    dv = o.shape[1] // n_heads

    def head(oh):
        return oh * lax.rsqrt(jnp.mean(oh * oh, axis=-1, keepdims=True) + EPS * dk)

    on = jnp.concatenate([head(o[:, h * dv:(h + 1) * dv]) for h in range(n_heads)], axis=1)
    return on.astype(z.dtype) * _silu_of_half(z)


def _scan_and_readout(qkg, v_ref, z_ref, y_ref, st_ref, qkg_ref, o_ref):
    @pl.when(pl.program_id(1) == 0)
    def _():
        st_ref[...] = jnp.zeros_like(st_ref)

    n_heads, dv, dk = st_ref.shape
    heads = []
    for h in range(n_heads):
        ks = slice(h * dk, (h + 1) * dk)
        vs = slice(h * dv, (h + 1) * dv)

        def finish(o, vs=vs):
            y_ref[:, vs] = _readout(o, z_ref[:, vs], 1, dk).astype(y_ref.dtype)

        heads.append(_scan_stages(lambda ks=ks: qkg(ks), lambda vs=vs: v_ref[:, vs], st_ref[h], finish))
    n_stages = len(heads[0][0])
    for tick in range(n_heads + n_stages - 1):
        for h in range(n_heads):
            if 0 <= tick - h < n_stages:
                heads[h][0][tick - h]()
    st_new = [res['state'] for _, res in heads]
    b_min = heads[0][1]['b_min']
    for _, res in heads[1:]:
        b_min = jnp.minimum(b_min, res['b_min'])
    in_range = b_min >= -DECAY_RANGE_LIMIT

    @pl.when(in_range)
    def _():
        for h in range(n_heads):
            st_ref[h] = st_new[h]

    @pl.when(jnp.logical_not(in_range))
    def _():
        q, k, g = qkg(slice(None))
        qkg_ref[0] = q
        qkg_ref[1] = k
        qkg_ref[2] = g
        _scan_tile_exact(qkg_ref, v_ref, st_ref, o_ref)
        y_ref[...] = _readout(o_ref[...], z_ref[...], n_heads, dk).astype(y_ref.dtype)


def _hgrn_kernel(lb_ref, hq_ref, hf_ref, hi_ref, hz_ref, y_ref, st_ref, qkg_ref, o_ref):
    def qkg(cols):
        lbp = lb_ref[:, cols]
        e = jnp.exp(lbp - jnp.max(lbp, axis=0, keepdims=True))
        lb = e[0:1, :] / jnp.sum(e, axis=0, keepdims=True)
        f = (0.5 + 0.5 * lb) + (0.5 - 0.5 * lb) * jnp.tanh(hf_ref[:, cols])
        return _silu_of_half(hq_ref[:, cols]), 1.0 - f, jnp.log2(f)

    _scan_and_readout(qkg, hi_ref, hz_ref, y_ref, st_ref, qkg_ref, o_ref)


def _gla_kernel(gq_ref, gk_ref, gv_ref, gz_ref, lr_ref, w2_ref, gb_ref, y_ref, st_ref,
                qkg_ref, o_ref):
    def qkg(cols):
        logit = jnp.dot(lr_ref[...].astype(jnp.bfloat16), w2_ref[:, cols].astype(jnp.bfloat16),
                        preferred_element_type=jnp.float32) + gb_ref[:, cols]
        log2_sig = (jnp.minimum(logit, 0.0) * LOG2E
                    - jnp.log2(1.0 + jnp.exp2(jnp.abs(logit) * -LOG2E)))
        return gq_ref[:, cols], gk_ref[:, cols], log2_sig / GLA_GATE_TAU

    _scan_and_readout(qkg, gv_ref, gz_ref, y_ref, st_ref, qkg_ref, o_ref)


def _hgrn(qf, vz, hgrn_lb, tt=512, hp=8):
    T = qf.shape[0]
    w = hp * HGRN_DK
    nb = D_HGRN // w
    blk = lambda sec: pl.BlockSpec((tt, w), lambda h, t: (t, sec * nb + h))
    return pl.pallas_call(
        _hgrn_kernel,
        grid=(HGRN_HEADS // hp, T // tt),
        in_specs=[
            pl.BlockSpec((2, w), lambda h, t: (0, h)),
            blk(0), blk(1), blk(0), blk(1),
        ],
        out_specs=pl.BlockSpec((tt, w), lambda h, t: (t, h)),
        out_shape=jax.ShapeDtypeStruct((T, D_HGRN), jnp.bfloat16),
        scratch_shapes=[pltpu.VMEM((hp, HGRN_DV, HGRN_DK), jnp.float32),
                        pltpu.VMEM((3, tt, w), jnp.float32),
                        pltpu.VMEM((tt, w), jnp.float32)],
        compiler_params=pltpu.CompilerParams(
            dimension_semantics=("parallel", "arbitrary"),
            vmem_limit_bytes=VMEM_LIMIT),
        name="hgrn_scan",
    )(hgrn_lb, qf, qf, vz, vz)


def _gla(qf, vz, lr, w2_pad, gate_b, tt=512, hp=4):
    T = qf.shape[0]
    wk, wv = hp * GLA_DK, hp * GLA_DV
    q0 = 2 * D_HGRN // wk
    k0 = q0 + GLA_DK_TOTAL // wk
    v0 = 2 * D_HGRN // wv
    z0 = v0 + D_GLA // wv
    return pl.pallas_call(
        _gla_kernel,
        grid=(GLA_HEADS // hp, T // tt),
        in_specs=[
            pl.BlockSpec((tt, wk), lambda h, t: (t, q0 + h)),
            pl.BlockSpec((tt, wk), lambda h, t: (t, k0 + h)),
            pl.BlockSpec((tt, wv), lambda h, t: (t, v0 + h)),
            pl.BlockSpec((tt, wv), lambda h, t: (t, z0 + h)),
            pl.BlockSpec((tt, LANE), lambda h, t: (t, 0)),
            pl.BlockSpec((LANE, wk), lambda h, t: (0, h)),
            pl.BlockSpec((1, wk), lambda h, t: (0, h)),
        ],
        out_specs=pl.BlockSpec((tt, wv), lambda h, t: (t, h)),
        out_shape=jax.ShapeDtypeStruct((T, D_GLA), jnp.bfloat16),
        scratch_shapes=[pltpu.VMEM((hp, GLA_DV, GLA_DK), jnp.float32),
                        pltpu.VMEM((3, tt, wk), jnp.float32),
                        pltpu.VMEM((tt, wv), jnp.float32)],
        compiler_params=pltpu.CompilerParams(
            dimension_semantics=("parallel", "arbitrary"),
            vmem_limit_bytes=VMEM_LIMIT),
        name="gla_scan",
    )(qf, qf, vz, vz, lr, w2_pad, gate_b)


OUT_SUB = 2
OUT_VMEM_LIMIT = 58 * 1024 * 1024


def _out_kernel(x_ref, yh_ref, yg_ref, p_ref, woh_ref, wog_ref, pp_ref, pnw_ref, gnw_ref,
                gw_ref, gb_ref, fnw_ref, o_ref):
    f32 = jnp.float32
    bf = jnp.bfloat16
    rows = o_ref.shape[0] // OUT_SUB
    subs = [slice(s * rows, (s + 1) * rows) for s in range(OUT_SUB)]
    h = [x_ref[r, :]
         + jnp.dot(yh_ref[r, :], woh_ref[...], preferred_element_type=f32)
         + jnp.dot(yg_ref[r, :], wog_ref[...], preferred_element_type=f32) for r in subs]
    e = [_rms(jnp.dot(p_ref[r, :].astype(bf), pp_ref[...], preferred_element_type=f32), pnw_ref[...])
         for r in subs]
    hn = [_rms(hs, gnw_ref[...]).astype(bf) for hs in h]
    acc = [jnp.dot(hs, gw_ref[...], preferred_element_type=f32) for hs in hn]
    for r, hs, es, a in zip(subs, h, e, acc):
        o_ref[r, :] = _rms(hs + _sigmoid(a + gb_ref[...]) * es, fnw_ref[...])


def _outproj(x2, yh, yg, p2, wo, pp, pnw, gnw, gw, gb, fnw, tm=512):
    T, D = x2.shape
    row = lambda w: pl.BlockSpec((tm, w), lambda i: (i, 0))
    once = pl.Buffered(1)
    full = lambda a: pl.BlockSpec(a.shape, lambda i: (0, 0), pipeline_mode=once)
    return pl.pallas_call(
        _out_kernel,
        grid=(T // tm,),
        in_specs=[row(D), row(D_HGRN), row(D_GLA), row(PLE_DIM),
                  pl.BlockSpec((D_HGRN, D), lambda i: (0, 0), pipeline_mode=once),
                  pl.BlockSpec((D_GLA, D), lambda i: (D_HGRN // D_GLA, 0), pipeline_mode=once),
                  full(pp), full(pnw), full(gnw), full(gw), full(gb), full(fnw)],
        out_specs=row(D),
        out_shape=jax.ShapeDtypeStruct((T, D), jnp.float32),
        compiler_params=pltpu.CompilerParams(
            dimension_semantics=("parallel",),
            vmem_limit_bytes=OUT_VMEM_LIMIT),
        name="outproj_ple",
    )(x2, yh, yg, p2, wo, wo, pp, pnw, gnw, gw, gb, fnw)


def kernel(x, p, norm_mix_w, w_in, hgrn_lb, gla_gate_w2, gla_gate_b, hgrn_norm_w, gla_norm_w,
           w_out, ple_proj, ple_norm_w, ple_gate_norm_w, ple_gate_w, ple_gate_b, final_norm_w):
    B, T, D = x.shape
    bf = jnp.bfloat16
    x2 = x.reshape(B * T, D)
    p2 = p[0].reshape(B * T, PLE_DIM)
    row = lambda a: a.reshape(1, -1)

    w_t = jnp.swapaxes(w_in[0], 0, 1)
    wlr_t = jnp.pad(w_t[D_MAIN:], ((0, LANE - GLA_GATE_RANK), (0, 0)))
    nw = row(norm_mix_w[0])
    qf, lr, xb, rs = _inproj_first(x2, nw, w_t, wlr_t, lambda j: j + 2 * (j // 2), 3, (0, 1),
                                   "inproj_qf")
    vz = _inproj_rest(xb, rs, nw, w_t, lambda j: j + 2 + j // 2, 4, (1, 3), bf,
                      "inproj_vz")

    y_h = _hgrn(qf, vz, hgrn_lb)
    w2_pad = jnp.pad(gla_gate_w2[0], ((0, LANE - GLA_GATE_RANK), (0, 0)))
    y_g = _gla(qf, vz, lr, w2_pad, row(gla_gate_b[0]))

    head_gain = jnp.concatenate([hgrn_norm_w[0], gla_norm_w[0]])[:, None]
    out = _outproj(x2, y_h, y_g, p2, (w_out[0] * head_gain).astype(bf), ple_proj[0].astype(bf),
                   row(ple_norm_w[0]), row(ple_gate_norm_w[0]), ple_gate_w[0].astype(bf),
                   row(ple_gate_b[0]), row(final_norm_w))
    return out.reshape(B, T, D)
```

```python
import functools

import jax
import jax.numpy as jnp
from jax import lax
from jax.experimental import pallas as pl
from jax.experimental.pallas import tpu as pltpu

D_MODEL = 2048
D_HGRN = 1024
HGRN_HEADS = 8
HGRN_DK = 128
HGRN_DV = 128
GLA_HEADS = 4
GLA_DK = 128
GLA_DV = 256
GLA_DK_TOTAL = 512
D_GLA = 1024
GLA_GATE_RANK = 16
GLA_GATE_TAU = 16.0
PLE_DIM = 256
EPS = 1e-6

LANE = 128
SUBLANE = 8
D_MAIN = 4 * D_HGRN + 2 * GLA_DK_TOTAL + 2 * D_GLA
PANEL = 1024
CHUNK = 128
ATTN_BLOCK = 128
DECAY_RANGE_LIMIT = 110.0
LOG2E = 1.4426950408889634
VMEM_LIMIT = 48 * 1024 * 1024
INPROJ_VMEM_LIMIT = 58 * 1024 * 1024

_NT = (((1,), (1,)), ((), ()))
_TN = (((0,), (0,)), ((), ()))


def _sigmoid(x):
    return 0.5 + 0.5 * jnp.tanh(0.5 * x)


def _silu_of_half(h):
    return h + h * jnp.tanh(h)


def _rms(x, w):
    return x * lax.rsqrt(jnp.mean(x * x, axis=-1, keepdims=True) + EPS) * w


def _is_half_panel(half_panels):
    return functools.reduce(jnp.logical_or, [pl.program_id(0) == jp for jp in half_panels])


def _cast_weight_panel(w_ref, nw_ref, wb_ref):
    @pl.when(pl.program_id(1) == 0)
    def _():
        wb_ref[...] = (w_ref[...] * nw_ref[...]).astype(jnp.bfloat16)


def _inproj_first_kernel(x_ref, nw_ref, w_ref, wlr_ref, o_ref, lr_ref, xb_ref, rs_ref, wb_ref, *,
                         half_panels):
    bf = jnp.bfloat16
    _cast_weight_panel(w_ref, nw_ref, wb_ref)

    def body(first_panel):
        x = x_ref[...]
        r = lax.rsqrt(jnp.mean(x * x, axis=-1, keepdims=True) + EPS)
        xb = x.astype(bf)
        o_ref[...] = (lax.dot_general(xb, wb_ref[...], _NT, preferred_element_type=jnp.float32)
                      * (r * jnp.where(_is_half_panel(half_panels), 0.5, 1.0))).astype(o_ref.dtype)
        if first_panel:
            xb_ref[...] = xb
            rs_ref[...] = jnp.broadcast_to(r, rs_ref.shape)
            wlr = (wlr_ref[...] * nw_ref[...]).astype(bf)
            lr_ref[...] = lax.dot_general(xb, wlr, _NT, preferred_element_type=jnp.float32) * r

    first = pl.program_id(0) == 0
    pl.when(first)(lambda: body(True))
    pl.when(jnp.logical_not(first))(lambda: body(False))


def _inproj_rest_kernel(xb_ref, rs_ref, nw_ref, w_ref, o_ref, wb_ref, *, half_panels):
    _cast_weight_panel(w_ref, nw_ref, wb_ref)
    r = rs_ref[:, 0:1] * jnp.where(_is_half_panel(half_panels), 0.5, 1.0)
    o_ref[...] = (lax.dot_general(xb_ref[...], wb_ref[...], _NT, preferred_element_type=jnp.float32)
                  * r).astype(o_ref.dtype)


def _inproj_params():
    return pltpu.CompilerParams(dimension_semantics=("parallel", "arbitrary"),
                                vmem_limit_bytes=INPROJ_VMEM_LIMIT)


def _inproj_first(x2, norm_w, w_t, wlr_t, panel_of, n_panels, half_panels, name, tm=1024):
    T, D = x2.shape
    n_rows = T // tm
    once = lambda j, i: (jnp.where(j == 0, i, n_rows - 1), 0)
    return pl.pallas_call(
        functools.partial(_inproj_first_kernel, half_panels=half_panels),
        grid=(n_panels, n_rows),
        in_specs=[pl.BlockSpec((tm, D), lambda j, i: (i, 0)),
                  pl.BlockSpec((1, D), lambda j, i: (0, 0)),
                  pl.BlockSpec((PANEL, D), lambda j, i: (panel_of(j), 0), pipeline_mode=pl.Buffered(1)),
                  pl.BlockSpec((LANE, D), lambda j, i: (0, 0))],
        out_specs=[pl.BlockSpec((tm, PANEL), lambda j, i: (i, j)),
                   pl.BlockSpec((tm, LANE), once),
                   pl.BlockSpec((tm, D), once),
                   pl.BlockSpec((tm, LANE), once)],
        out_shape=[jax.ShapeDtypeStruct((T, n_panels * PANEL), jnp.float32),
                   jax.ShapeDtypeStruct((T, LANE), jnp.float32),
                   jax.ShapeDtypeStruct((T, D), jnp.bfloat16),
                   jax.ShapeDtypeStruct((T, LANE), jnp.float32)],
        scratch_shapes=[pltpu.VMEM((PANEL, D), jnp.bfloat16)],
        compiler_params=_inproj_params(),
        name=name,
    )(x2, norm_w, w_t, wlr_t)


def _inproj_rest(xb, rs, norm_w, w_t, panel_of, n_panels, half_panels, out_dtype, name, tm=1024):
    T, D = xb.shape
    return pl.pallas_call(
        functools.partial(_inproj_rest_kernel, half_panels=half_panels),
        grid=(n_panels, T // tm),
        in_specs=[pl.BlockSpec((tm, D), lambda j, i: (i, 0)),
                  pl.BlockSpec((tm, LANE), lambda j, i: (i, 0)),
                  pl.BlockSpec((1, D), lambda j, i: (0, 0)),
                  pl.BlockSpec((PANEL, D), lambda j, i: (panel_of(j), 0))],
        out_specs=pl.BlockSpec((tm, PANEL), lambda j, i: (i, j)),
        out_shape=jax.ShapeDtypeStruct((T, n_panels * PANEL), out_dtype),
        scratch_shapes=[pltpu.VMEM((PANEL, D), jnp.bfloat16)],
        compiler_params=_inproj_params(),
        name=name,
    )(xb, rs, norm_w, w_t)


def _scan_stages(qkg, v, state, finish):
    bf = jnp.bfloat16
    f32 = jnp.float32
    dv, dk = state.shape
    blk = ATTN_BLOCK
    row = lax.broadcasted_iota(jnp.int32, (blk, blk), 0)
    col = lax.broadcasted_iota(jnp.int32, (blk, blk), 1)
    causal = (col <= row) & (col >= (row & -CHUNK))
    res = {}

    def gates():
        q, k, g = qkg()
        g_hi = g.astype(bf)
        g_lo = (g - g_hi.astype(f32)).astype(bf)
        g2 = jnp.concatenate([g_hi, g_lo], axis=1)
        tri = causal.astype(bf)
        res.update(q=q, k=k, n_blk=g.shape[0] // blk, n_chk=g.shape[0] // CHUNK)
        res['b2'] = [jnp.dot(tri, g2[r * blk:(r + 1) * blk], preferred_element_type=f32)
                     for r in range(res['n_blk'])]

    def decay():
        b = jnp.concatenate([b2[:, :dk] + b2[:, dk:] for b2 in res.pop('b2')], axis=0)
        eb = jnp.exp2(b)
        res['qh'] = (res.pop('q') * eb).astype(bf)
        res['kt'] = (res.pop('k') / eb).astype(bf)
        res['vb'] = v().astype(bf)
        res['eb_last'] = [eb[(c + 1) * CHUNK - 1:(c + 1) * CHUNK, :] for c in range(res['n_chk'])]
        res['b_min'] = jnp.min(jnp.concatenate(
            [b[(c + 1) * CHUNK - 1:(c + 1) * CHUNK, :] for c in range(res['n_chk'])], axis=0))

    def scores():
        qh, kt, vb = res['qh'], res['kt'], res['vb']
        res['attn'] = [lax.dot_general(qh[r * blk:(r + 1) * blk], kt[r * blk:(r + 1) * blk], _NT,
                                       preferred_element_type=f32) for r in range(res['n_blk'])]
        res['upd'] = [lax.dot_general(vb[c * CHUNK:(c + 1) * CHUNK], kt[c * CHUNK:(c + 1) * CHUNK], _TN,
                                      preferred_element_type=f32) for c in range(res['n_chk'])]

    def carry():
        res['attn'] = [jnp.where(causal, a, 0.0).astype(bf) for a in res['attn']]
        st, st_in = state, []
        for upd, eb_last in zip(res.pop('upd'), res.pop('eb_last')):
            st_in.append(st.astype(bf))
            st = (st + upd) * eb_last
        res.update(st_in=st_in, state=st)

    def outputs():
        qh, vb = res.pop('qh'), res.pop('vb')
        res['intra'] = [jnp.dot(a, vb[r * blk:(r + 1) * blk], preferred_element_type=f32)
                        for r, a in enumerate(res.pop('attn'))]
        res['inter'] = [lax.dot_general(qh[c * CHUNK:(c + 1) * CHUNK], s, _NT, preferred_element_type=f32)
                        for c, s in enumerate(res.pop('st_in'))]

    def done():
        finish(jnp.concatenate(res.pop('intra'), axis=0) + jnp.concatenate(res.pop('inter'), axis=0))

    return [gates, decay, scores, carry, outputs, done], res


def _scan_tile_exact(qkg_ref, v_ref, st_ref, o_ref):
    bf = jnp.bfloat16
    f32 = jnp.float32
    n_heads, dv, dk = st_ref.shape
    tt = o_ref.shape[0]
    rowk = lax.broadcasted_iota(jnp.int32, (CHUNK, dk), 0)
    rowc = lax.broadcasted_iota(jnp.int32, (CHUNK, CHUNK), 0)
    colc = lax.broadcasted_iota(jnp.int32, (CHUNK, CHUNK), 1)
    tri = (colc <= rowc).astype(bf)

    for h in range(n_heads):
        ks = slice(h * dk, (h + 1) * dk)
        vs = slice(h * dv, (h + 1) * dv)

        def chunk(c, carry, ks=ks, vs=vs, h=h):
            r0 = pl.multiple_of(c * CHUNK, CHUNK)
            rows = pl.ds(r0, CHUNK)
            qc = qkg_ref[0, rows, ks]
            kc = qkg_ref[1, rows, ks]
            gc = qkg_ref[2, rows, ks]
            g_hi = gc.astype(bf)
            g_lo = (gc - g_hi.astype(f32)).astype(bf)
            b = (jnp.dot(tri, g_hi, preferred_element_type=f32)
                 + jnp.dot(tri, g_lo, preferred_element_type=f32))
            qkg_ref[2, rows, ks] = b

            def columns(jg, attn):
                j0 = pl.multiple_of(jg * SUBLANE, SUBLANE)
                b_rows = qkg_ref[2, pl.ds(r0 + j0, SUBLANE), ks]
                k_rows = qkg_ref[1, pl.ds(r0 + j0, SUBLANE), ks]
                for r in range(SUBLANE):
                    j = j0 + r
                    decay = jnp.exp2(jnp.where(rowk >= j, b - b_rows[r:r + 1], 0.0))
                    colv = jnp.sum(qc * k_rows[r:r + 1] * decay, axis=1, keepdims=True)
                    attn = jnp.where(colc == j, colv, attn)
                return attn

            attn = lax.fori_loop(0, CHUNK // SUBLANE, columns, jnp.zeros((CHUNK, CHUNK), f32))
            attn = jnp.where(colc <= rowc, attn, 0.0).astype(bf)
            vb = v_ref[rows, vs].astype(bf)
            st = st_ref[h]
            b_last = b[CHUNK - 1:CHUNK, :]
            o_ref[rows, vs] = (
                jnp.dot(attn, vb, preferred_element_type=f32)
                + lax.dot_general((qc * jnp.exp2(b)).astype(bf), st.astype(bf), _NT,
                                  preferred_element_type=f32))
            k_dec = (kc * jnp.exp2(b_last - b)).astype(bf)
            st_ref[h] = st * jnp.exp2(b_last) + lax.dot_general(vb, k_dec, _TN,
                                                                preferred_element_type=f32)
            return carry

        lax.fori_loop(0, tt // CHUNK, chunk, 0)


def _readout(o, z, n_heads, dk):
    dv = o.shape[1] // n_heads

    def head(oh):
        return oh * lax.rsqrt(jnp.mean(oh * oh, axis=-1, keepdims=True) + EPS * dk)

    on = jnp.concatenate([head(o[:, h * dv:(h + 1) * dv]) for h in range(n_heads)], axis=1)
    return on.astype(z.dtype) * _silu_of_half(z)


def _scan_and_readout(qkg, v_ref, z_ref, y_ref, st_ref, qkg_ref, o_ref):
    @pl.when(pl.program_id(1) == 0)
    def _():
        st_ref[...] = jnp.zeros_like(st_ref)

    n_heads, dv, dk = st_ref.shape
    heads = []
    for h in range(n_heads):
        ks = slice(h * dk, (h + 1) * dk)
        vs = slice(h * dv, (h + 1) * dv)

        def finish(o, vs=vs):
            y_ref[:, vs] = _readout(o, z_ref[:, vs], 1, dk).astype(y_ref.dtype)

        heads.append(_scan_stages(lambda ks=ks: qkg(ks), lambda vs=vs: v_ref[:, vs], st_ref[h], finish))
    n_stages = len(heads[0][0])
    for tick in range(n_heads + n_stages - 1):
        for h in range(n_heads):
            if 0 <= tick - h < n_stages:
                heads[h][0][tick - h]()
    st_new = [res['state'] for _, res in heads]
    b_min = heads[0][1]['b_min']
    for _, res in heads[1:]:
        b_min = jnp.minimum(b_min, res['b_min'])
    in_range = b_min >= -DECAY_RANGE_LIMIT

    @pl.when(in_range)
    def _():
        for h in range(n_heads):
            st_ref[h] = st_new[h]

    @pl.when(jnp.logical_not(in_range))
    def _():
        q, k, g = qkg(slice(None))
        qkg_ref[0] = q
        qkg_ref[1] = k
        qkg_ref[2] = g
        _scan_tile_exact(qkg_ref, v_ref, st_ref, o_ref)
        y_ref[...] = _readout(o_ref[...], z_ref[...], n_heads, dk).astype(y_ref.dtype)


def _hgrn_kernel(lb_ref, hq_ref, hf_ref, hi_ref, hz_ref, y_ref, st_ref, qkg_ref, o_ref):
    def qkg(cols):
        lbp = lb_ref[:, cols]
        e = jnp.exp(lbp - jnp.max(lbp, axis=0, keepdims=True))
        lb = e[0:1, :] / jnp.sum(e, axis=0, keepdims=True)
        f = (0.5 + 0.5 * lb) + (0.5 - 0.5 * lb) * jnp.tanh(hf_ref[:, cols])
        return _silu_of_half(hq_ref[:, cols]), 1.0 - f, jnp.log2(f)

    _scan_and_readout(qkg, hi_ref, hz_ref, y_ref, st_ref, qkg_ref, o_ref)


def _gla_kernel(gq_ref, gk_ref, gv_ref, gz_ref, lr_ref, w2_ref, gb_ref, y_ref, st_ref,
                qkg_ref, o_ref):
    def qkg(cols):
        logit = jnp.dot(lr_ref[...].astype(jnp.bfloat16), w2_ref[:, cols].astype(jnp.bfloat16),
                        preferred_element_type=jnp.float32) + gb_ref[:, cols]
        log2_sig = (jnp.minimum(logit, 0.0) * LOG2E
                    - jnp.log2(1.0 + jnp.exp2(jnp.abs(logit) * -LOG2E)))
        return gq_ref[:, cols], gk_ref[:, cols], log2_sig / GLA_GATE_TAU

    _scan_and_readout(qkg, gv_ref, gz_ref, y_ref, st_ref, qkg_ref, o_ref)


def _hgrn(qf, vz, hgrn_lb, tt=512, hp=8):
    T = qf.shape[0]
    w = hp * HGRN_DK
    nb = D_HGRN // w
    blk = lambda sec: pl.BlockSpec((tt, w), lambda h, t: (t, sec * nb + h))
    return pl.pallas_call(
        _hgrn_kernel,
        grid=(HGRN_HEADS // hp, T // tt),
        in_specs=[
            pl.BlockSpec((2, w), lambda h, t: (0, h)),
            blk(0), blk(1), blk(0), blk(1),
        ],
        out_specs=pl.BlockSpec((tt, w), lambda h, t: (t, h)),
        out_shape=jax.ShapeDtypeStruct((T, D_HGRN), jnp.bfloat16),
        scratch_shapes=[pltpu.VMEM((hp, HGRN_DV, HGRN_DK), jnp.float32),
                        pltpu.VMEM((3, tt, w), jnp.float32),
                        pltpu.VMEM((tt, w), jnp.float32)],
        compiler_params=pltpu.CompilerParams(
            dimension_semantics=("parallel", "arbitrary"),
            vmem_limit_bytes=VMEM_LIMIT),
        name="hgrn_scan",
    )(hgrn_lb, qf, qf, vz, vz)


def _gla(qf, vz, lr, w2_pad, gate_b, tt=512, hp=4):
    T = qf.shape[0]
    wk, wv = hp * GLA_DK, hp * GLA_DV
    q0 = 2 * D_HGRN // wk
    k0 = q0 + GLA_DK_TOTAL // wk
    v0 = 2 * D_HGRN // wv
    z0 = v0 + D_GLA // wv
    return pl.pallas_call(
        _gla_kernel,
        grid=(GLA_HEADS // hp, T // tt),
        in_specs=[
            pl.BlockSpec((tt, wk), lambda h, t: (t, q0 + h)),
            pl.BlockSpec((tt, wk), lambda h, t: (t, k0 + h)),
            pl.BlockSpec((tt, wv), lambda h, t: (t, v0 + h)),
            pl.BlockSpec((tt, wv), lambda h, t: (t, z0 + h)),
            pl.BlockSpec((tt, LANE), lambda h, t: (t, 0)),
            pl.BlockSpec((LANE, wk), lambda h, t: (0, h)),
            pl.BlockSpec((1, wk), lambda h, t: (0, h)),
        ],
        out_specs=pl.BlockSpec((tt, wv), lambda h, t: (t, h)),
        out_shape=jax.ShapeDtypeStruct((T, D_GLA), jnp.bfloat16),
        scratch_shapes=[pltpu.VMEM((hp, GLA_DV, GLA_DK), jnp.float32),
                        pltpu.VMEM((3, tt, wk), jnp.float32),
                        pltpu.VMEM((tt, wv), jnp.float32)],
        compiler_params=pltpu.CompilerParams(
            dimension_semantics=("parallel", "arbitrary"),
            vmem_limit_bytes=VMEM_LIMIT),
        name="gla_scan",
    )(qf, qf, vz, vz, lr, w2_pad, gate_b)


OUT_SUB = 2
OUT_VMEM_LIMIT = 58 * 1024 * 1024


def _out_kernel(x_ref, yh_ref, yg_ref, p_ref, woh_ref, wog_ref, pp_ref, pnw_ref, gnw_ref,
                gw_ref, gb_ref, fnw_ref, o_ref):
    f32 = jnp.float32
    bf = jnp.bfloat16
    rows = o_ref.shape[0] // OUT_SUB
    subs = [slice(s * rows, (s + 1) * rows) for s in range(OUT_SUB)]
    h = [x_ref[r, :]
         + jnp.dot(yh_ref[r, :], woh_ref[...], preferred_element_type=f32)
         + jnp.dot(yg_ref[r, :], wog_ref[...], preferred_element_type=f32) for r in subs]
    e = [_rms(jnp.dot(p_ref[r, :].astype(bf), pp_ref[...], preferred_element_type=f32), pnw_ref[...])
         for r in subs]
    hn = [_rms(hs, gnw_ref[...]).astype(bf) for hs in h]
    acc = [jnp.dot(hs, gw_ref[...], preferred_element_type=f32) for hs in hn]
    for r, hs, es, a in zip(subs, h, e, acc):
        o_ref[r, :] = _rms(hs + _sigmoid(a + gb_ref[...]) * es, fnw_ref[...])


def _outproj(x2, yh, yg, p2, wo, pp, pnw, gnw, gw, gb, fnw, tm=512):
    T, D = x2.shape
    row = lambda w: pl.BlockSpec((tm, w), lambda i: (i, 0))
    once = pl.Buffered(1)
    full = lambda a: pl.BlockSpec(a.shape, lambda i: (0, 0), pipeline_mode=once)
    return pl.pallas_call(
        _out_kernel,
        grid=(T // tm,),
        in_specs=[row(D), row(D_HGRN), row(D_GLA), row(PLE_DIM),
                  pl.BlockSpec((D_HGRN, D), lambda i: (0, 0), pipeline_mode=once),
                  pl.BlockSpec((D_GLA, D), lambda i: (D_HGRN // D_GLA, 0), pipeline_mode=once),
                  full(pp), full(pnw), full(gnw), full(gw), full(gb), full(fnw)],
        out_specs=row(D),
        out_shape=jax.ShapeDtypeStruct((T, D), jnp.float32),
        compiler_params=pltpu.CompilerParams(
            dimension_semantics=("parallel",),
            vmem_limit_bytes=OUT_VMEM_LIMIT),
        name="outproj_ple",
    )(x2, yh, yg, p2, wo, wo, pp, pnw, gnw, gw, gb, fnw)


def kernel(x, p, norm_mix_w, w_in, hgrn_lb, gla_gate_w2, gla_gate_b, hgrn_norm_w, gla_norm_w,
           w_out, ple_proj, ple_norm_w, ple_gate_norm_w, ple_gate_w, ple_gate_b, final_norm_w):
    B, T, D = x.shape
    bf = jnp.bfloat16
    x2 = x.reshape(B * T, D)
    p2 = p[0].reshape(B * T, PLE_DIM)
    row = lambda a: a.reshape(1, -1)

    w_t = jnp.swapaxes(w_in[0], 0, 1)
    wlr_t = jnp.pad(w_t[D_MAIN:], ((0, LANE - GLA_GATE_RANK), (0, 0)))
    nw = row(norm_mix_w[0])
    qf, lr, xb, rs = _inproj_first(x2, nw, w_t, wlr_t, lambda j: j + 2 * (j // 2), 3, (0, 1),
                                   "inproj_qf")
    vz = _inproj_rest(xb, rs, nw, w_t, lambda j: j + 2 + j // 2, 4, (1, 3), bf,
                      "inproj_vz")

    y_h = _hgrn(qf, vz, hgrn_lb)
    w2_pad = jnp.pad(gla_gate_w2[0], ((0, LANE - GLA_GATE_RANK), (0, 0)))
    y_g = _gla(qf, vz, lr, w2_pad, row(gla_gate_b[0]))

    head_gain = jnp.concatenate([hgrn_norm_w[0], gla_norm_w[0]])[:, None]
    out = _outproj(x2, y_h, y_g, p2, (w_out[0] * head_gain).astype(bf), ple_proj[0].astype(bf),
                   row(ple_norm_w[0]), row(ple_gate_norm_w[0]), ple_gate_w[0].astype(bf),
                   row(ple_gate_b[0]), row(final_norm_w))
    return out.reshape(B, T, D)
```

```python
import functools

import jax
import jax.numpy as jnp
from jax import lax
from jax.experimental import pallas as pl
from jax.experimental.pallas import tpu as pltpu

D_MODEL = 2048
D_HGRN = 1024
HGRN_HEADS = 8
HGRN_DK = 128
HGRN_DV = 128
GLA_HEADS = 4
GLA_DK = 128
GLA_DV = 256
GLA_DK_TOTAL = 512
D_GLA = 1024
GLA_GATE_RANK = 16
GLA_GATE_TAU = 16.0
PLE_DIM = 256
EPS = 1e-6

LANE = 128
SUBLANE = 8
D_MAIN = 4 * D_HGRN + 2 * GLA_DK_TOTAL + 2 * D_GLA
PANEL = 1024
CHUNK = 128
ATTN_BLOCK = 128
DECAY_RANGE_LIMIT = 110.0
LOG2E = 1.4426950408889634
VMEM_LIMIT = 48 * 1024 * 1024
INPROJ_VMEM_LIMIT = 58 * 1024 * 1024

_NT = (((1,), (1,)), ((), ()))
_TN = (((0,), (0,)), ((), ()))


def _sigmoid(x):
    return 0.5 + 0.5 * jnp.tanh(0.5 * x)


def _silu_of_half(h):
    return h + h * jnp.tanh(h)


def _rms(x, w):
    return x * lax.rsqrt(jnp.mean(x * x, axis=-1, keepdims=True) + EPS) * w


def _is_half_panel(half_panels):
    return functools.reduce(jnp.logical_or, [pl.program_id(0) == jp for jp in half_panels])


def _cast_weight_panel(w_ref, nw_ref, wb_ref):
    @pl.when(pl.program_id(1) == 0)
    def _():
        wb_ref[...] = (w_ref[...] * nw_ref[...]).astype(jnp.bfloat16)


def _inproj_first_kernel(x_ref, nw_ref, w_ref, o_ref, xb_ref, rs_ref, wb_ref, *, half_panels):
    bf = jnp.bfloat16
    _cast_weight_panel(w_ref, nw_ref, wb_ref)

    def body(first_panel):
        x = x_ref[...]
        r = lax.rsqrt(jnp.mean(x * x, axis=-1, keepdims=True) + EPS)
        xb = x.astype(bf)
        o_ref[...] = (lax.dot_general(xb, wb_ref[...], _NT, preferred_element_type=jnp.float32)
                      * (r * jnp.where(_is_half_panel(half_panels), 0.5, 1.0))).astype(o_ref.dtype)
        if first_panel:
            xb_ref[...] = xb
            rs_ref[...] = jnp.broadcast_to(r, rs_ref.shape)

    first = pl.program_id(0) == 0
    pl.when(first)(lambda: body(True))
    pl.when(jnp.logical_not(first))(lambda: body(False))


def _inproj_rest_kernel(xb_ref, rs_ref, nw_ref, w_ref, wlr_ref, o_ref, lr_ref, wb_ref, *, half_panels):
    _cast_weight_panel(w_ref, nw_ref, wb_ref)

    def body(first_panel):
        xb = xb_ref[...]
        r = rs_ref[:, 0:1]
        o_ref[...] = (lax.dot_general(xb, wb_ref[...], _NT, preferred_element_type=jnp.float32)
                      * (r * jnp.where(_is_half_panel(half_panels), 0.5, 1.0))).astype(o_ref.dtype)
        if first_panel:
            wlr = (wlr_ref[...] * nw_ref[...]).astype(jnp.bfloat16)
            lr_ref[...] = lax.dot_general(xb, wlr, _NT, preferred_element_type=jnp.float32) * r

    first = pl.program_id(0) == 0
    pl.when(first)(lambda: body(True))
    pl.when(jnp.logical_not(first))(lambda: body(False))


def _inproj_params():
    return pltpu.CompilerParams(dimension_semantics=("parallel", "arbitrary"),
                                vmem_limit_bytes=INPROJ_VMEM_LIMIT)


def _first_panel_only(n_rows):
    return lambda j, i: (jnp.where(j == 0, i, n_rows - 1), 0)


def _inproj_first(x2, norm_w, w_t, panel_of, n_panels, half_panels, out_dtype, name, tm=1024):
    T, D = x2.shape
    n_rows = T // tm
    return pl.pallas_call(
        functools.partial(_inproj_first_kernel, half_panels=half_panels),
        grid=(n_panels, n_rows),
        in_specs=[pl.BlockSpec((tm, D), lambda j, i: (i, 0)),
                  pl.BlockSpec((1, D), lambda j, i: (0, 0)),
                  pl.BlockSpec((PANEL, D), lambda j, i: (panel_of(j), 0))],
        out_specs=[pl.BlockSpec((tm, PANEL), lambda j, i: (i, j)),
                   pl.BlockSpec((tm, D), _first_panel_only(n_rows)),
                   pl.BlockSpec((tm, LANE), _first_panel_only(n_rows))],
        out_shape=[jax.ShapeDtypeStruct((T, n_panels * PANEL), out_dtype),
                   jax.ShapeDtypeStruct((T, D), jnp.bfloat16),
                   jax.ShapeDtypeStruct((T, LANE), jnp.float32)],
        scratch_shapes=[pltpu.VMEM((PANEL, D), jnp.bfloat16)],
        compiler_params=_inproj_params(),
        name=name,
    )(x2, norm_w, w_t)


def _inproj_rest(xb, rs, norm_w, w_t, wlr_t, panel_of, n_panels, half_panels, out_dtype, name, tm=1024):
    T, D = xb.shape
    n_rows = T // tm
    return pl.pallas_call(
        functools.partial(_inproj_rest_kernel, half_panels=half_panels),
        grid=(n_panels, n_rows),
        in_specs=[pl.BlockSpec((tm, D), lambda j, i: (i, 0)),
                  pl.BlockSpec((tm, LANE), lambda j, i: (i, 0)),
                  pl.BlockSpec((1, D), lambda j, i: (0, 0)),
                  pl.BlockSpec((PANEL, D), lambda j, i: (panel_of(j), 0)),
                  pl.BlockSpec((LANE, D), lambda j, i: (0, 0))],
        out_specs=[pl.BlockSpec((tm, PANEL), lambda j, i: (i, j)),
                   pl.BlockSpec((tm, LANE), _first_panel_only(n_rows))],
        out_shape=[jax.ShapeDtypeStruct((T, n_panels * PANEL), out_dtype),
                   jax.ShapeDtypeStruct((T, LANE), jnp.float32)],
        scratch_shapes=[pltpu.VMEM((PANEL, D), jnp.bfloat16)],
        compiler_params=_inproj_params(),
        name=name,
    )(xb, rs, norm_w, w_t, wlr_t)


def _scan_stages(qkg, v, state, finish):
    bf = jnp.bfloat16
    f32 = jnp.float32
    dv, dk = state.shape
    blk = ATTN_BLOCK
    row = lax.broadcasted_iota(jnp.int32, (blk, blk), 0)
    col = lax.broadcasted_iota(jnp.int32, (blk, blk), 1)
    causal = (col <= row) & (col >= (row & -CHUNK))
    res = {}

    def gates():
        q, k, g = qkg()
        g_hi = g.astype(bf)
        g_lo = (g - g_hi.astype(f32)).astype(bf)
        g2 = jnp.concatenate([g_hi, g_lo], axis=1)
        tri = causal.astype(bf)
        res.update(q=q, k=k, n_blk=g.shape[0] // blk, n_chk=g.shape[0] // CHUNK)
        res['b2'] = [jnp.dot(tri, g2[r * blk:(r + 1) * blk], preferred_element_type=f32)
                     for r in range(res['n_blk'])]

    def decay():
        b = jnp.concatenate([b2[:, :dk] + b2[:, dk:] for b2 in res.pop('b2')], axis=0)
        eb = jnp.exp2(b)
        res['qh'] = (res.pop('q') * eb).astype(bf)
        res['kt'] = (res.pop('k') / eb).astype(bf)
        res['vb'] = v().astype(bf)
        res['eb_last'] = [eb[(c + 1) * CHUNK - 1:(c + 1) * CHUNK, :] for c in range(res['n_chk'])]
        res['b_min'] = jnp.min(jnp.concatenate(
            [b[(c + 1) * CHUNK - 1:(c + 1) * CHUNK, :] for c in range(res['n_chk'])], axis=0))

    def scores():
        qh, kt, vb = res['qh'], res['kt'], res['vb']
        res['attn'] = [lax.dot_general(qh[r * blk:(r + 1) * blk], kt[r * blk:(r + 1) * blk], _NT,
                                       preferred_element_type=f32) for r in range(res['n_blk'])]
        res['upd'] = [lax.dot_general(vb[c * CHUNK:(c + 1) * CHUNK], kt[c * CHUNK:(c + 1) * CHUNK], _TN,
                                      preferred_element_type=f32) for c in range(res['n_chk'])]

    def carry():
        res['attn'] = [jnp.where(causal, a, 0.0).astype(bf) for a in res['attn']]
        st, st_in = state, []
        for upd, eb_last in zip(res.pop('upd'), res.pop('eb_last')):
            st_in.append(st.astype(bf))
            st = (st + upd) * eb_last
        res.update(st_in=st_in, state=st)

    def outputs():
        qh, vb = res.pop('qh'), res.pop('vb')
        res['intra'] = [jnp.dot(a, vb[r * blk:(r + 1) * blk], preferred_element_type=f32)
                        for r, a in enumerate(res.pop('attn'))]
        res['inter'] = [lax.dot_general(qh[c * CHUNK:(c + 1) * CHUNK], s, _NT, preferred_element_type=f32)
                        for c, s in enumerate(res.pop('st_in'))]

    def done():
        finish(jnp.concatenate(res.pop('intra'), axis=0) + jnp.concatenate(res.pop('inter'), axis=0))

    return [gates, decay, scores, carry, outputs, done], res


def _scan_tile_exact(qkg_ref, v_ref, st_ref, o_ref):
    bf = jnp.bfloat16
    f32 = jnp.float32
    n_heads, dv, dk = st_ref.shape
    tt = o_ref.shape[0]
    rowk = lax.broadcasted_iota(jnp.int32, (CHUNK, dk), 0)
    rowc = lax.broadcasted_iota(jnp.int32, (CHUNK, CHUNK), 0)
    colc = lax.broadcasted_iota(jnp.int32, (CHUNK, CHUNK), 1)
    tri = (colc <= rowc).astype(bf)

    for h in range(n_heads):
        ks = slice(h * dk, (h + 1) * dk)
        vs = slice(h * dv, (h + 1) * dv)

        def chunk(c, carry, ks=ks, vs=vs, h=h):
            r0 = pl.multiple_of(c * CHUNK, CHUNK)
            rows = pl.ds(r0, CHUNK)
            qc = qkg_ref[0, rows, ks]
            kc = qkg_ref[1, rows, ks]
            gc = qkg_ref[2, rows, ks]
            g_hi = gc.astype(bf)
            g_lo = (gc - g_hi.astype(f32)).astype(bf)
            b = (jnp.dot(tri, g_hi, preferred_element_type=f32)
                 + jnp.dot(tri, g_lo, preferred_element_type=f32))
            qkg_ref[2, rows, ks] = b

            def columns(jg, attn):
                j0 = pl.multiple_of(jg * SUBLANE, SUBLANE)
                b_rows = qkg_ref[2, pl.ds(r0 + j0, SUBLANE), ks]
                k_rows = qkg_ref[1, pl.ds(r0 + j0, SUBLANE), ks]
                for r in range(SUBLANE):
                    j = j0 + r
                    decay = jnp.exp2(jnp.where(rowk >= j, b - b_rows[r:r + 1], 0.0))
                    colv = jnp.sum(qc * k_rows[r:r + 1] * decay, axis=1, keepdims=True)
                    attn = jnp.where(colc == j, colv, attn)
                return attn

            attn = lax.fori_loop(0, CHUNK // SUBLANE, columns, jnp.zeros((CHUNK, CHUNK), f32))
            attn = jnp.where(colc <= rowc, attn, 0.0).astype(bf)
            vb = v_ref[rows, vs].astype(bf)
            st = st_ref[h]
            b_last = b[CHUNK - 1:CHUNK, :]
            o_ref[rows, vs] = (
                jnp.dot(attn, vb, preferred_element_type=f32)
                + lax.dot_general((qc * jnp.exp2(b)).astype(bf), st.astype(bf), _NT,
                                  preferred_element_type=f32))
            k_dec = (kc * jnp.exp2(b_last - b)).astype(bf)
            st_ref[h] = st * jnp.exp2(b_last) + lax.dot_general(vb, k_dec, _TN,
                                                                preferred_element_type=f32)
            return carry

        lax.fori_loop(0, tt // CHUNK, chunk, 0)


def _readout(o, z, n_heads, dk):
    dv = o.shape[1] // n_heads

    def head(oh):
        return oh * lax.rsqrt(jnp.mean(oh * oh, axis=-1, keepdims=True) + EPS * dk)

    on = jnp.concatenate([head(o[:, h * dv:(h + 1) * dv]) for h in range(n_heads)], axis=1)
    return on.astype(z.dtype) * _silu_of_half(z)


def _scan_and_readout(qkg, v_ref, z_ref, y_ref, st_ref, qkg_ref, o_ref):
    @pl.when(pl.program_id(1) == 0)
    def _():
        st_ref[...] = jnp.zeros_like(st_ref)

    n_heads, dv, dk = st_ref.shape
    heads = []
    for h in range(n_heads):
        ks = slice(h * dk, (h + 1) * dk)
        vs = slice(h * dv, (h + 1) * dv)

        def finish(o, vs=vs):
            y_ref[:, vs] = _readout(o, z_ref[:, vs], 1, dk).astype(y_ref.dtype)

        heads.append(_scan_stages(lambda ks=ks: qkg(ks), lambda vs=vs: v_ref[:, vs], st_ref[h], finish))
    n_stages = len(heads[0][0])
    for tick in range(n_heads + n_stages - 1):
        for h in range(n_heads):
            if 0 <= tick - h < n_stages:
                heads[h][0][tick - h]()
    st_new = [res['state'] for _, res in heads]
    b_min = heads[0][1]['b_min']
    for _, res in heads[1:]:
        b_min = jnp.minimum(b_min, res['b_min'])
    in_range = b_min >= -DECAY_RANGE_LIMIT

    @pl.when(in_range)
    def _():
        for h in range(n_heads):
            st_ref[h] = st_new[h]

    @pl.when(jnp.logical_not(in_range))
    def _():
        q, k, g = qkg(slice(None))
        qkg_ref[0] = q
        qkg_ref[1] = k
        qkg_ref[2] = g
        _scan_tile_exact(qkg_ref, v_ref, st_ref, o_ref)
        y_ref[...] = _readout(o_ref[...], z_ref[...], n_heads, dk).astype(y_ref.dtype)


def _hgrn_kernel(lb_ref, hq_ref, hf_ref, hi_ref, hz_ref, y_ref, st_ref, qkg_ref, o_ref):
    def qkg(cols):
        lbp = lb_ref[:, cols]
        e = jnp.exp(lbp - jnp.max(lbp, axis=0, keepdims=True))
        lb = e[0:1, :] / jnp.sum(e, axis=0, keepdims=True)
        f = (0.5 + 0.5 * lb) + (0.5 - 0.5 * lb) * jnp.tanh(hf_ref[:, cols])
        return _silu_of_half(hq_ref[:, cols]), 1.0 - f, jnp.log2(f)

    _scan_and_readout(qkg, hi_ref, hz_ref, y_ref, st_ref, qkg_ref, o_ref)


def _gla_kernel(gq_ref, gk_ref, gv_ref, gz_ref, lr_ref, w2_ref, gb_ref, y_ref, st_ref,
                qkg_ref, o_ref):
    def qkg(cols):
        logit = jnp.dot(lr_ref[...].astype(jnp.bfloat16), w2_ref[:, cols].astype(jnp.bfloat16),
                        preferred_element_type=jnp.float32) + gb_ref[:, cols]
        log2_sig = (jnp.minimum(logit, 0.0) * LOG2E
                    - jnp.log2(1.0 + jnp.exp2(jnp.abs(logit) * -LOG2E)))
        return gq_ref[:, cols], gk_ref[:, cols], log2_sig / GLA_GATE_TAU

    _scan_and_readout(qkg, gv_ref, gz_ref, y_ref, st_ref, qkg_ref, o_ref)


def _hgrn(qf, vz, hgrn_lb, tt=512, hp=8):
    T = qf.shape[0]
    w = hp * HGRN_DK
    nb = D_HGRN // w
    blk = lambda sec: pl.BlockSpec((tt, w), lambda h, t: (t, sec * nb + h))
    return pl.pallas_call(
        _hgrn_kernel,
        grid=(HGRN_HEADS // hp, T // tt),
        in_specs=[
            pl.BlockSpec((2, w), lambda h, t: (0, h)),
            blk(0), blk(1), blk(0), blk(1),
        ],
        out_specs=pl.BlockSpec((tt, w), lambda h, t: (t, h)),
        out_shape=jax.ShapeDtypeStruct((T, D_HGRN), jnp.bfloat16),
        scratch_shapes=[pltpu.VMEM((hp, HGRN_DV, HGRN_DK), jnp.float32),
                        pltpu.VMEM((3, tt, w), jnp.float32),
                        pltpu.VMEM((tt, w), jnp.float32)],
        compiler_params=pltpu.CompilerParams(
            dimension_semantics=("parallel", "arbitrary"),
            vmem_limit_bytes=VMEM_LIMIT),
        name="hgrn_scan",
    )(hgrn_lb, qf, qf, vz, vz)


def _gla(qf, vz, lr, w2_pad, gate_b, tt=512, hp=4):
    T = qf.shape[0]
    wk, wv = hp * GLA_DK, hp * GLA_DV
    q0 = 2 * D_HGRN // wk
    k0 = q0 + GLA_DK_TOTAL // wk
    v0 = 2 * D_HGRN // wv
    z0 = v0 + D_GLA // wv
    return pl.pallas_call(
        _gla_kernel,
        grid=(GLA_HEADS // hp, T // tt),
        in_specs=[
            pl.BlockSpec((tt, wk), lambda h, t: (t, q0 + h)),
            pl.BlockSpec((tt, wk), lambda h, t: (t, k0 + h)),
            pl.BlockSpec((tt, wv), lambda h, t: (t, v0 + h)),
            pl.BlockSpec((tt, wv), lambda h, t: (t, z0 + h)),
            pl.BlockSpec((tt, LANE), lambda h, t: (t, 0)),
            pl.BlockSpec((LANE, wk), lambda h, t: (0, h)),
            pl.BlockSpec((1, wk), lambda h, t: (0, h)),
        ],
        out_specs=pl.BlockSpec((tt, wv), lambda h, t: (t, h)),
        out_shape=jax.ShapeDtypeStruct((T, D_GLA), jnp.bfloat16),
        scratch_shapes=[pltpu.VMEM((hp, GLA_DV, GLA_DK), jnp.float32),
                        pltpu.VMEM((3, tt, wk), jnp.float32),
                        pltpu.VMEM((tt, wv), jnp.float32)],
        compiler_params=pltpu.CompilerParams(
            dimension_semantics=("parallel", "arbitrary"),
            vmem_limit_bytes=VMEM_LIMIT),
        name="gla_scan",
    )(qf, qf, vz, vz, lr, w2_pad, gate_b)


OUT_SUB = 2
OUT_VMEM_LIMIT = 58 * 1024 * 1024


def _out_kernel(x_ref, yh_ref, yg_ref, p_ref, woh_ref, wog_ref, pp_ref, pnw_ref, gnw_ref,
                gw_ref, gb_ref, fnw_ref, o_ref):
    f32 = jnp.float32
    bf = jnp.bfloat16
    rows = o_ref.shape[0] // OUT_SUB
    subs = [slice(s * rows, (s + 1) * rows) for s in range(OUT_SUB)]
    h = [x_ref[r, :]
         + jnp.dot(yh_ref[r, :], woh_ref[...], preferred_element_type=f32)
         + jnp.dot(yg_ref[r, :], wog_ref[...], preferred_element_type=f32) for r in subs]
    e = [_rms(jnp.dot(p_ref[r, :].astype(bf), pp_ref[...], preferred_element_type=f32), pnw_ref[...])
         for r in subs]
    hn = [_rms(hs, gnw_ref[...]).astype(bf) for hs in h]
    acc = [jnp.dot(hs, gw_ref[...], preferred_element_type=f32) for hs in hn]
    for r, hs, es, a in zip(subs, h, e, acc):
        o_ref[r, :] = _rms(hs + _sigmoid(a + gb_ref[...]) * es, fnw_ref[...])


def _outproj(x2, yh, yg, p2, wo, pp, pnw, gnw, gw, gb, fnw, tm=512):
    T, D = x2.shape
    row = lambda w: pl.BlockSpec((tm, w), lambda i: (i, 0))
    once = pl.Buffered(1)
    full = lambda a: pl.BlockSpec(a.shape, lambda i: (0, 0), pipeline_mode=once)
    return pl.pallas_call(
        _out_kernel,
        grid=(T // tm,),
        in_specs=[row(D), row(D_HGRN), row(D_GLA), row(PLE_DIM),
                  pl.BlockSpec((D_HGRN, D), lambda i: (0, 0), pipeline_mode=once),
                  pl.BlockSpec((D_GLA, D), lambda i: (D_HGRN // D_GLA, 0), pipeline_mode=once),
                  full(pp), full(pnw), full(gnw), full(gw), full(gb), full(fnw)],
        out_specs=row(D),
        out_shape=jax.ShapeDtypeStruct((T, D), jnp.float32),
        compiler_params=pltpu.CompilerParams(
            dimension_semantics=("parallel",),
            vmem_limit_bytes=OUT_VMEM_LIMIT),
        name="outproj_ple",
    )(x2, yh, yg, p2, wo, wo, pp, pnw, gnw, gw, gb, fnw)


def kernel(x, p, norm_mix_w, w_in, hgrn_lb, gla_gate_w2, gla_gate_b, hgrn_norm_w, gla_norm_w,
           w_out, ple_proj, ple_norm_w, ple_gate_norm_w, ple_gate_w, ple_gate_b, final_norm_w):
    B, T, D = x.shape
    bf = jnp.bfloat16
    x2 = x.reshape(B * T, D)
    p2 = p[0].reshape(B * T, PLE_DIM)
    row = lambda a: a.reshape(1, -1)

    w_t = jnp.swapaxes(w_in[0], 0, 1)
    wlr_t = jnp.pad(w_t[D_MAIN:], ((0, LANE - GLA_GATE_RANK), (0, 0)))
    nw = row(norm_mix_w[0])
    vz, xb, rs = _inproj_first(x2, nw, w_t, lambda j: j + 2 + j // 2, 4, (1, 3), bf,
                               "inproj_vz")
    qf, lr = _inproj_rest(xb, rs, nw, w_t, wlr_t, lambda j: j + 2 * (j // 2), 3, (0, 1), jnp.float32,
                          "inproj_qf")

    y_h = _hgrn(qf, vz, hgrn_lb)
    w2_pad = jnp.pad(gla_gate_w2[0], ((0, LANE - GLA_GATE_RANK), (0, 0)))
    y_g = _gla(qf, vz, lr, w2_pad, row(gla_gate_b[0]))

    head_gain = jnp.concatenate([hgrn_norm_w[0], gla_norm_w[0]])[:, None]
    out = _outproj(x2, y_h, y_g, p2, (w_out[0] * head_gain).astype(bf), ple_proj[0].astype(bf),
                   row(ple_norm_w[0]), row(ple_gate_norm_w[0]), ple_gate_w[0].astype(bf),
                   row(ple_gate_b[0]), row(final_norm_w))
    return out.reshape(B, T, D)
```

```python
import functools

import jax
import jax.numpy as jnp
from jax import lax
from jax.experimental import pallas as pl
from jax.experimental.pallas import tpu as pltpu

D_MODEL = 2048
D_HGRN = 1024
HGRN_HEADS = 8
HGRN_DK = 128
HGRN_DV = 128
GLA_HEADS = 4
GLA_DK = 128
GLA_DV = 256
GLA_DK_TOTAL = 512
D_GLA = 1024
GLA_GATE_RANK = 16
GLA_GATE_TAU = 16.0
PLE_DIM = 256
EPS = 1e-6

LANE = 128
SUBLANE = 8
D_MAIN = 4 * D_HGRN + 2 * GLA_DK_TOTAL + 2 * D_GLA
PANEL = 1024
CHUNK = 128
ATTN_BLOCK = 128
DECAY_RANGE_LIMIT = 110.0
LOG2E = 1.4426950408889634
VMEM_LIMIT = 48 * 1024 * 1024
INPROJ_VMEM_LIMIT = 58 * 1024 * 1024

_NT = (((1,), (1,)), ((), ()))
_TN = (((0,), (0,)), ((), ()))


def _sigmoid(x):
    return 0.5 + 0.5 * jnp.tanh(0.5 * x)


def _silu_of_half(h):
    return h + h * jnp.tanh(h)


def _rms(x, w):
    return x * lax.rsqrt(jnp.mean(x * x, axis=-1, keepdims=True) + EPS) * w


def _is_half_panel(half_panels):
    return functools.reduce(jnp.logical_or, [pl.program_id(0) == jp for jp in half_panels])


def _cast_weight_panel(w_ref, nw_ref, wb_ref):
    @pl.when(pl.program_id(1) == 0)
    def _():
        wb_ref[...] = (w_ref[...] * nw_ref[...]).astype(jnp.bfloat16)


def _inproj_first_kernel(x_ref, nw_ref, w_ref, o_ref, xb_ref, rs_ref, wb_ref, *, half_panels):
    bf = jnp.bfloat16
    _cast_weight_panel(w_ref, nw_ref, wb_ref)

    def body(first_panel):
        x = x_ref[...]
        r = lax.rsqrt(jnp.mean(x * x, axis=-1, keepdims=True) + EPS)
        xb = x.astype(bf)
        o_ref[...] = (lax.dot_general(xb, wb_ref[...], _NT, preferred_element_type=jnp.float32)
                      * (r * jnp.where(_is_half_panel(half_panels), 0.5, 1.0))).astype(o_ref.dtype)
        if first_panel:
            xb_ref[...] = xb
            rs_ref[...] = jnp.broadcast_to(r, rs_ref.shape)

    first = pl.program_id(0) == 0
    pl.when(first)(lambda: body(True))
    pl.when(jnp.logical_not(first))(lambda: body(False))


def _inproj_rest_kernel(xb_ref, rs_ref, nw_ref, w_ref, wlr_ref, o_ref, lr_ref, wb_ref, *, half_panels):
    _cast_weight_panel(w_ref, nw_ref, wb_ref)

    def body(first_panel):
        xb = xb_ref[...]
        r = rs_ref[:, 0:1]
        o_ref[...] = (lax.dot_general(xb, wb_ref[...], _NT, preferred_element_type=jnp.float32)
                      * (r * jnp.where(_is_half_panel(half_panels), 0.5, 1.0))).astype(o_ref.dtype)
        if first_panel:
            wlr = (wlr_ref[...] * nw_ref[...]).astype(jnp.bfloat16)
            lr_ref[...] = lax.dot_general(xb, wlr, _NT, preferred_element_type=jnp.float32) * r

    first = pl.program_id(0) == 0
    pl.when(first)(lambda: body(True))
    pl.when(jnp.logical_not(first))(lambda: body(False))


def _inproj_params():
    return pltpu.CompilerParams(dimension_semantics=("parallel", "arbitrary"),
                                vmem_limit_bytes=INPROJ_VMEM_LIMIT)


def _first_panel_only(n_rows):
    return lambda j, i: (jnp.where(j == 0, i, n_rows - 1), 0)


def _inproj_first(x2, norm_w, w_t, panel_of, n_panels, half_panels, out_dtype, name, tm=1024):
    T, D = x2.shape
    n_rows = T // tm
    return pl.pallas_call(
        functools.partial(_inproj_first_kernel, half_panels=half_panels),
        grid=(n_panels, n_rows),
        in_specs=[pl.BlockSpec((tm, D), lambda j, i: (i, 0)),
                  pl.BlockSpec((1, D), lambda j, i: (0, 0)),
                  pl.BlockSpec((PANEL, D), lambda j, i: (panel_of(j), 0))],
        out_specs=[pl.BlockSpec((tm, PANEL), lambda j, i: (i, j)),
                   pl.BlockSpec((tm, D), _first_panel_only(n_rows)),
                   pl.BlockSpec((tm, LANE), _first_panel_only(n_rows))],
        out_shape=[jax.ShapeDtypeStruct((T, n_panels * PANEL), out_dtype),
                   jax.ShapeDtypeStruct((T, D), jnp.bfloat16),
                   jax.ShapeDtypeStruct((T, LANE), jnp.float32)],
        scratch_shapes=[pltpu.VMEM((PANEL, D), jnp.bfloat16)],
        compiler_params=_inproj_params(),
        name=name,
    )(x2, norm_w, w_t)


def _inproj_rest(xb, rs, norm_w, w_t, wlr_t, panel_of, n_panels, half_panels, out_dtype, name, tm=1024):
    T, D = xb.shape
    n_rows = T // tm
    return pl.pallas_call(
        functools.partial(_inproj_rest_kernel, half_panels=half_panels),
        grid=(n_panels, n_rows),
        in_specs=[pl.BlockSpec((tm, D), lambda j, i: (i, 0)),
                  pl.BlockSpec((tm, LANE), lambda j, i: (i, 0)),
                  pl.BlockSpec((1, D), lambda j, i: (0, 0)),
                  pl.BlockSpec((PANEL, D), lambda j, i: (panel_of(j), 0)),
                  pl.BlockSpec((LANE, D), lambda j, i: (0, 0))],
        out_specs=[pl.BlockSpec((tm, PANEL), lambda j, i: (i, j)),
                   pl.BlockSpec((tm, LANE), _first_panel_only(n_rows))],
        out_shape=[jax.ShapeDtypeStruct((T, n_panels * PANEL), out_dtype),
                   jax.ShapeDtypeStruct((T, LANE), jnp.float32)],
        scratch_shapes=[pltpu.VMEM((PANEL, D), jnp.bfloat16)],
        compiler_params=_inproj_params(),
        name=name,
    )(xb, rs, norm_w, w_t, wlr_t)


def _scan_stages(qkg, v, state, finish):
    bf = jnp.bfloat16
    f32 = jnp.float32
    dv, dk = state.shape
    blk = ATTN_BLOCK
    row = lax.broadcasted_iota(jnp.int32, (blk, blk), 0)
    col = lax.broadcasted_iota(jnp.int32, (blk, blk), 1)
    causal = (col <= row) & (col >= (row & -CHUNK))
    res = {}

    def gates():
        q, k, g = qkg()
        g_hi = g.astype(bf)
        g_lo = (g - g_hi.astype(f32)).astype(bf)
        g2 = jnp.concatenate([g_hi, g_lo], axis=1)
        tri = causal.astype(bf)
        res.update(q=q, k=k, n_blk=g.shape[0] // blk, n_chk=g.shape[0] // CHUNK)
        res['b2'] = [jnp.dot(tri, g2[r * blk:(r + 1) * blk], preferred_element_type=f32)
                     for r in range(res['n_blk'])]

    def decay():
        b = jnp.concatenate([b2[:, :dk] + b2[:, dk:] for b2 in res.pop('b2')], axis=0)
        eb = jnp.exp2(b)
        q, k = res.pop('q'), res.pop('k')
        res['qh'] = q * eb.astype(bf) if q.dtype == bf else (q * eb).astype(bf)
        res['kt'] = k * (1.0 / eb).astype(bf) if k.dtype == bf else (k / eb).astype(bf)
        res['vb'] = v().astype(bf)
        res['eb_last'] = [eb[(c + 1) * CHUNK - 1:(c + 1) * CHUNK, :] for c in range(res['n_chk'])]
        res['b_min'] = jnp.min(jnp.concatenate(
            [b[(c + 1) * CHUNK - 1:(c + 1) * CHUNK, :] for c in range(res['n_chk'])], axis=0))

    def scores():
        qh, kt, vb = res['qh'], res['kt'], res['vb']
        res['attn'] = [lax.dot_general(qh[r * blk:(r + 1) * blk], kt[r * blk:(r + 1) * blk], _NT,
                                       preferred_element_type=f32) for r in range(res['n_blk'])]
        res['upd'] = [lax.dot_general(vb[c * CHUNK:(c + 1) * CHUNK], kt[c * CHUNK:(c + 1) * CHUNK], _TN,
                                      preferred_element_type=f32) for c in range(res['n_chk'])]

    def carry():
        res['attn'] = [jnp.where(causal, a, 0.0).astype(bf) for a in res['attn']]
        st, st_in = state, []
        for upd, eb_last in zip(res.pop('upd'), res.pop('eb_last')):
            st_in.append(st.astype(bf))
            st = (st + upd) * eb_last
        res.update(st_in=st_in, state=st)

    def outputs():
        qh, vb = res.pop('qh'), res.pop('vb')
        res['intra'] = [jnp.dot(a, vb[r * blk:(r + 1) * blk], preferred_element_type=f32)
                        for r, a in enumerate(res.pop('attn'))]
        res['inter'] = [lax.dot_general(qh[c * CHUNK:(c + 1) * CHUNK], s, _NT, preferred_element_type=f32)
                        for c, s in enumerate(res.pop('st_in'))]

    def done():
        finish(jnp.concatenate(res.pop('intra'), axis=0) + jnp.concatenate(res.pop('inter'), axis=0))

    return [gates, decay, scores, carry, outputs, done], res


def _scan_tile_exact(qkg_ref, v_ref, st_ref, o_ref):
    bf = jnp.bfloat16
    f32 = jnp.float32
    n_heads, dv, dk = st_ref.shape
    tt = o_ref.shape[0]
    rowk = lax.broadcasted_iota(jnp.int32, (CHUNK, dk), 0)
    rowc = lax.broadcasted_iota(jnp.int32, (CHUNK, CHUNK), 0)
    colc = lax.broadcasted_iota(jnp.int32, (CHUNK, CHUNK), 1)
    tri = (colc <= rowc).astype(bf)

    for h in range(n_heads):
        ks = slice(h * dk, (h + 1) * dk)
        vs = slice(h * dv, (h + 1) * dv)

        def chunk(c, carry, ks=ks, vs=vs, h=h):
            r0 = pl.multiple_of(c * CHUNK, CHUNK)
            rows = pl.ds(r0, CHUNK)
            qc = qkg_ref[0, rows, ks]
            kc = qkg_ref[1, rows, ks]
            gc = qkg_ref[2, rows, ks]
            g_hi = gc.astype(bf)
            g_lo = (gc - g_hi.astype(f32)).astype(bf)
            b = (jnp.dot(tri, g_hi, preferred_element_type=f32)
                 + jnp.dot(tri, g_lo, preferred_element_type=f32))
            qkg_ref[2, rows, ks] = b

            def columns(jg, attn):
                j0 = pl.multiple_of(jg * SUBLANE, SUBLANE)
                b_rows = qkg_ref[2, pl.ds(r0 + j0, SUBLANE), ks]
                k_rows = qkg_ref[1, pl.ds(r0 + j0, SUBLANE), ks]
                for r in range(SUBLANE):
                    j = j0 + r
                    decay = jnp.exp2(jnp.where(rowk >= j, b - b_rows[r:r + 1], 0.0))
                    colv = jnp.sum(qc * k_rows[r:r + 1] * decay, axis=1, keepdims=True)
                    attn = jnp.where(colc == j, colv, attn)
                return attn

            attn = lax.fori_loop(0, CHUNK // SUBLANE, columns, jnp.zeros((CHUNK, CHUNK), f32))
            attn = jnp.where(colc <= rowc, attn, 0.0).astype(bf)
            vb = v_ref[rows, vs].astype(bf)
            st = st_ref[h]
            b_last = b[CHUNK - 1:CHUNK, :]
            o_ref[rows, vs] = (
                jnp.dot(attn, vb, preferred_element_type=f32)
                + lax.dot_general((qc * jnp.exp2(b)).astype(bf), st.astype(bf), _NT,
                                  preferred_element_type=f32))
            k_dec = (kc * jnp.exp2(b_last - b)).astype(bf)
            st_ref[h] = st * jnp.exp2(b_last) + lax.dot_general(vb, k_dec, _TN,
                                                                preferred_element_type=f32)
            return carry

        lax.fori_loop(0, tt // CHUNK, chunk, 0)


def _readout(o, z, n_heads, dk):
    dv = o.shape[1] // n_heads

    def head(oh):
        return oh * lax.rsqrt(jnp.mean(oh * oh, axis=-1, keepdims=True) + EPS * dk)

    on = jnp.concatenate([head(o[:, h * dv:(h + 1) * dv]) for h in range(n_heads)], axis=1)
    return on.astype(z.dtype) * _silu_of_half(z)


def _scan_and_readout(qkg, v_ref, z_ref, y_ref, st_ref, qkg_ref, o_ref):
    @pl.when(pl.program_id(1) == 0)
    def _():
        st_ref[...] = jnp.zeros_like(st_ref)

    n_heads, dv, dk = st_ref.shape
    heads = []
    for h in range(n_heads):
        ks = slice(h * dk, (h + 1) * dk)
        vs = slice(h * dv, (h + 1) * dv)

        def finish(o, vs=vs):
            y_ref[:, vs] = _readout(o, z_ref[:, vs], 1, dk).astype(y_ref.dtype)

        heads.append(_scan_stages(lambda ks=ks: qkg(ks), lambda vs=vs: v_ref[:, vs], st_ref[h], finish))
    n_stages = len(heads[0][0])
    for tick in range(n_heads + n_stages - 1):
        for h in range(n_heads):
            if 0 <= tick - h < n_stages:
                heads[h][0][tick - h]()
    st_new = [res['state'] for _, res in heads]
    b_min = heads[0][1]['b_min']
    for _, res in heads[1:]:
        b_min = jnp.minimum(b_min, res['b_min'])
    in_range = b_min >= -DECAY_RANGE_LIMIT

    @pl.when(in_range)
    def _():
        for h in range(n_heads):
            st_ref[h] = st_new[h]

    @pl.when(jnp.logical_not(in_range))
    def _():
        q, k, g = qkg(slice(None))
        qkg_ref[0] = q.astype(jnp.float32)
        qkg_ref[1] = k.astype(jnp.float32)
        qkg_ref[2] = g
        _scan_tile_exact(qkg_ref, v_ref, st_ref, o_ref)
        y_ref[...] = _readout(o_ref[...], z_ref[...], n_heads, dk).astype(y_ref.dtype)


def _hgrn_kernel(lb_ref, hq_ref, hf_ref, hi_ref, hz_ref, y_ref, st_ref, qkg_ref, o_ref):
    def qkg(cols):
        lbp = lb_ref[:, cols]
        e = jnp.exp(lbp - jnp.max(lbp, axis=0, keepdims=True))
        lb = e[0:1, :] / jnp.sum(e, axis=0, keepdims=True)
        f = (0.5 + 0.5 * lb) + (0.5 - 0.5 * lb) * jnp.tanh(hf_ref[:, cols])
        return _silu_of_half(hq_ref[:, cols]), 1.0 - f, jnp.log2(f)

    _scan_and_readout(qkg, hi_ref, hz_ref, y_ref, st_ref, qkg_ref, o_ref)


def _gla_kernel(gq_ref, gk_ref, gv_ref, gz_ref, lr_ref, w2_ref, gb_ref, y_ref, st_ref,
                qkg_ref, o_ref):
    def qkg(cols):
        logit = jnp.dot(lr_ref[...].astype(jnp.bfloat16), w2_ref[:, cols].astype(jnp.bfloat16),
                        preferred_element_type=jnp.float32) + gb_ref[:, cols]
        log2_sig = (jnp.minimum(logit, 0.0) * LOG2E
                    - jnp.log2(1.0 + jnp.exp2(jnp.abs(logit) * -LOG2E)))
        return gq_ref[:, cols], gk_ref[:, cols], log2_sig / GLA_GATE_TAU

    _scan_and_readout(qkg, gv_ref, gz_ref, y_ref, st_ref, qkg_ref, o_ref)


def _scan_params():
    return pltpu.CompilerParams(dimension_semantics=("parallel", "arbitrary"),
                                vmem_limit_bytes=VMEM_LIMIT)


def _hgrn(hf, pb, hgrn_lb, tt=512, hp=8):
    T = hf.shape[0]
    w = hp * HGRN_DK
    nb = D_HGRN // w
    blk = lambda sec: pl.BlockSpec((tt, w), lambda h, t: (t, sec * nb + h))
    return pl.pallas_call(
        _hgrn_kernel,
        grid=(HGRN_HEADS // hp, T // tt),
        in_specs=[
            pl.BlockSpec((2, w), lambda h, t: (0, h)),
            blk(0), blk(0), blk(1), blk(2),
        ],
        out_specs=pl.BlockSpec((tt, w), lambda h, t: (t, h)),
        out_shape=jax.ShapeDtypeStruct((T, D_HGRN), jnp.bfloat16),
        scratch_shapes=[pltpu.VMEM((hp, HGRN_DV, HGRN_DK), jnp.float32),
                        pltpu.VMEM((3, tt, w), jnp.float32),
                        pltpu.VMEM((tt, w), jnp.float32)],
        compiler_params=_scan_params(),
        name="hgrn_scan",
    )(hgrn_lb, pb, hf, pb, pb)


def _gla(pb, lr, w2_pad, gate_b, tt=512, hp=4):
    T = pb.shape[0]
    wk, wv = hp * GLA_DK, hp * GLA_DV
    q0 = 3 * PANEL // wk
    k0 = q0 + GLA_DK_TOTAL // wk
    v0 = 4 * PANEL // wv
    z0 = v0 + D_GLA // wv
    return pl.pallas_call(
        _gla_kernel,
        grid=(GLA_HEADS // hp, T // tt),
        in_specs=[
            pl.BlockSpec((tt, wk), lambda h, t: (t, q0 + h)),
            pl.BlockSpec((tt, wk), lambda h, t: (t, k0 + h)),
            pl.BlockSpec((tt, wv), lambda h, t: (t, v0 + h)),
            pl.BlockSpec((tt, wv), lambda h, t: (t, z0 + h)),
            pl.BlockSpec((tt, LANE), lambda h, t: (t, 0)),
            pl.BlockSpec((LANE, wk), lambda h, t: (0, h)),
            pl.BlockSpec((1, wk), lambda h, t: (0, h)),
        ],
        out_specs=pl.BlockSpec((tt, wv), lambda h, t: (t, h)),
        out_shape=jax.ShapeDtypeStruct((T, D_GLA), jnp.bfloat16),
        scratch_shapes=[pltpu.VMEM((hp, GLA_DV, GLA_DK), jnp.float32),
                        pltpu.VMEM((3, tt, wk), jnp.float32),
                        pltpu.VMEM((tt, wv), jnp.float32)],
        compiler_params=_scan_params(),
        name="gla_scan",
    )(pb, pb, pb, pb, lr, w2_pad, gate_b)


OUT_SUB = 2
OUT_VMEM_LIMIT = 58 * 1024 * 1024


def _out_kernel(x_ref, yh_ref, yg_ref, p_ref, woh_ref, wog_ref, pp_ref, pnw_ref, gnw_ref,
                gw_ref, gb_ref, fnw_ref, o_ref):
    f32 = jnp.float32
    bf = jnp.bfloat16
    rows = o_ref.shape[0] // OUT_SUB
    subs = [slice(s * rows, (s + 1) * rows) for s in range(OUT_SUB)]
    h = [x_ref[r, :]
         + jnp.dot(yh_ref[r, :], woh_ref[...], preferred_element_type=f32)
         + jnp.dot(yg_ref[r, :], wog_ref[...], preferred_element_type=f32) for r in subs]
    e = [_rms(jnp.dot(p_ref[r, :].astype(bf), pp_ref[...], preferred_element_type=f32), pnw_ref[...])
         for r in subs]
    hn = [_rms(hs, gnw_ref[...]).astype(bf) for hs in h]
    acc = [jnp.dot(hs, gw_ref[...], preferred_element_type=f32) for hs in hn]
    for r, hs, es, a in zip(subs, h, e, acc):
        o_ref[r, :] = _rms(hs + _sigmoid(a + gb_ref[...]) * es, fnw_ref[...])


def _outproj(x2, yh, yg, p2, wo, pp, pnw, gnw, gw, gb, fnw, tm=512):
    T, D = x2.shape
    row = lambda w: pl.BlockSpec((tm, w), lambda i: (i, 0))
    once = pl.Buffered(1)
    full = lambda a: pl.BlockSpec(a.shape, lambda i: (0, 0), pipeline_mode=once)
    return pl.pallas_call(
        _out_kernel,
        grid=(T // tm,),
        in_specs=[row(D), row(D_HGRN), row(D_GLA), row(PLE_DIM),
                  pl.BlockSpec((D_HGRN, D), lambda i: (0, 0), pipeline_mode=once),
                  pl.BlockSpec((D_GLA, D), lambda i: (D_HGRN // D_GLA, 0), pipeline_mode=once),
                  full(pp), full(pnw), full(gnw), full(gw), full(gb), full(fnw)],
        out_specs=row(D),
        out_shape=jax.ShapeDtypeStruct((T, D), jnp.float32),
        compiler_params=pltpu.CompilerParams(
            dimension_semantics=("parallel",),
            vmem_limit_bytes=OUT_VMEM_LIMIT),
        name="outproj_ple",
    )(x2, yh, yg, p2, wo, wo, pp, pnw, gnw, gw, gb, fnw)


def kernel(x, p, norm_mix_w, w_in, hgrn_lb, gla_gate_w2, gla_gate_b, hgrn_norm_w, gla_norm_w,
           w_out, ple_proj, ple_norm_w, ple_gate_norm_w, ple_gate_w, ple_gate_b, final_norm_w):
    B, T, D = x.shape
    bf = jnp.bfloat16
    x2 = x.reshape(B * T, D)
    p2 = p[0].reshape(B * T, PLE_DIM)
    row = lambda a: a.reshape(1, -1)

    w_t = jnp.swapaxes(w_in[0], 0, 1)
    wlr_t = jnp.pad(w_t[D_MAIN:], ((0, LANE - GLA_GATE_RANK), (0, 0)))
    nw = row(norm_mix_w[0])
    hf, xb, rs = _inproj_first(x2, nw, w_t, lambda j: j + 1, 1, (0,), jnp.float32, "inproj_f")
    pb, lr = _inproj_rest(xb, rs, nw, w_t, wlr_t, lambda j: j + jnp.minimum(j, 1), 6, (0, 2, 5), bf,
                          "inproj_b")

    y_h = _hgrn(hf, pb, hgrn_lb)
    w2_pad = jnp.pad(gla_gate_w2[0], ((0, LANE - GLA_GATE_RANK), (0, 0)))
    y_g = _gla(pb, lr, w2_pad, row(gla_gate_b[0]))

    head_gain = jnp.concatenate([hgrn_norm_w[0], gla_norm_w[0]])[:, None]
    out = _outproj(x2, y_h, y_g, p2, (w_out[0] * head_gain).astype(bf), ple_proj[0].astype(bf),
                   row(ple_norm_w[0]), row(ple_gate_norm_w[0]), ple_gate_w[0].astype(bf),
                   row(ple_gate_b[0]), row(final_norm_w))
    return out.reshape(B, T, D)
```

```python
import functools

import jax
import jax.numpy as jnp
from jax import lax
from jax.experimental import pallas as pl
from jax.experimental.pallas import tpu as pltpu

D_MODEL = 2048
D_HGRN = 1024
HGRN_HEADS = 8
HGRN_DK = 128
HGRN_DV = 128
GLA_HEADS = 4
GLA_DK = 128
GLA_DV = 256
GLA_DK_TOTAL = 512
D_GLA = 1024
GLA_GATE_RANK = 16
GLA_GATE_TAU = 16.0
PLE_DIM = 256
EPS = 1e-6

LANE = 128
SUBLANE = 8
D_MAIN = 4 * D_HGRN + 2 * GLA_DK_TOTAL + 2 * D_GLA
PANEL = 1024
CHUNK = 128
ATTN_BLOCK = 128
DECAY_RANGE_LIMIT = 110.0
LOG2E = 1.4426950408889634
VMEM_LIMIT = 48 * 1024 * 1024
INPROJ_VMEM_LIMIT = 62 * 1024 * 1024

_NT = (((1,), (1,)), ((), ()))
_TN = (((0,), (0,)), ((), ()))


def _sigmoid(x):
    return 0.5 + 0.5 * jnp.tanh(0.5 * x)


def _silu_of_half(h):
    return h + h * jnp.tanh(h)


def _rms(x, w):
    return x * lax.rsqrt(jnp.mean(x * x, axis=-1, keepdims=True) + EPS) * w


def _is_half_panel(half_panels):
    return functools.reduce(jnp.logical_or, [pl.program_id(0) == jp for jp in half_panels])


def _cast_weight_panel(w_ref, nw_ref, wb_ref):
    @pl.when(pl.program_id(1) == 0)
    def _():
        wb_ref[...] = (w_ref[...] * nw_ref[...]).astype(jnp.bfloat16)


def _inproj_first_kernel(x_ref, nw_ref, w_ref, o_ref, xb_ref, rs_ref, wb_ref, *, half_panels):
    bf = jnp.bfloat16
    _cast_weight_panel(w_ref, nw_ref, wb_ref)

    def body(first_panel):
        x = x_ref[...]
        r = lax.rsqrt(jnp.mean(x * x, axis=-1, keepdims=True) + EPS)
        xb = x.astype(bf)
        o_ref[...] = (lax.dot_general(xb, wb_ref[...], _NT, preferred_element_type=jnp.float32)
                      * (r * jnp.where(_is_half_panel(half_panels), 0.5, 1.0))).astype(o_ref.dtype)
        if first_panel:
            xb_ref[...] = xb
            rs_ref[...] = jnp.broadcast_to(r, rs_ref.shape)

    first = pl.program_id(0) == 0
    pl.when(first)(lambda: body(True))
    pl.when(jnp.logical_not(first))(lambda: body(False))


def _inproj_rest_kernel(xb_ref, rs_ref, nw_ref, w_ref, wlr_ref, o_ref, lr_ref, wb_ref, *, half_panels):
    _cast_weight_panel(w_ref, nw_ref, wb_ref)

    def body(first_panel):
        xb = xb_ref[...]
        r = rs_ref[:, 0:1]
        o_ref[...] = (lax.dot_general(xb, wb_ref[...], _NT, preferred_element_type=jnp.float32)
                      * (r * jnp.where(_is_half_panel(half_panels), 0.5, 1.0))).astype(o_ref.dtype)
        if first_panel:
            wlr = (wlr_ref[...] * nw_ref[...]).astype(jnp.bfloat16)
            lr_ref[...] = lax.dot_general(xb, wlr, _NT, preferred_element_type=jnp.float32) * r

    first = pl.program_id(0) == 0
    pl.when(first)(lambda: body(True))
    pl.when(jnp.logical_not(first))(lambda: body(False))


def _inproj_params():
    return pltpu.CompilerParams(dimension_semantics=("parallel", "arbitrary"),
                                vmem_limit_bytes=INPROJ_VMEM_LIMIT)


def _first_panel_only(n_rows):
    return lambda j, i: (jnp.where(j == 0, i, n_rows - 1), 0)


def _inproj_first(x2, norm_w, w_t, panel_of, n_panels, half_panels, out_dtype, name, tm=1024):
    T, D = x2.shape
    n_rows = T // tm
    return pl.pallas_call(
        functools.partial(_inproj_first_kernel, half_panels=half_panels),
        grid=(n_panels, n_rows),
        in_specs=[pl.BlockSpec((tm, D), lambda j, i: (i, 0)),
                  pl.BlockSpec((1, D), lambda j, i: (0, 0)),
                  pl.BlockSpec((PANEL, D), lambda j, i: (panel_of(j), 0))],
        out_specs=[pl.BlockSpec((tm, PANEL), lambda j, i: (i, j)),
                   pl.BlockSpec((tm, D), _first_panel_only(n_rows)),
                   pl.BlockSpec((tm, LANE), _first_panel_only(n_rows))],
        out_shape=[jax.ShapeDtypeStruct((T, n_panels * PANEL), out_dtype),
                   jax.ShapeDtypeStruct((T, D), jnp.bfloat16),
                   jax.ShapeDtypeStruct((T, LANE), jnp.float32)],
        scratch_shapes=[pltpu.VMEM((PANEL, D), jnp.bfloat16)],
        compiler_params=_inproj_params(),
        name=name,
    )(x2, norm_w, w_t)


def _inproj_rest(xb, rs, norm_w, w_t, wlr_t, panel_of, n_panels, half_panels, out_dtype, name, tm=2048):
    T, D = xb.shape
    n_rows = T // tm
    return pl.pallas_call(
        functools.partial(_inproj_rest_kernel, half_panels=half_panels),
        grid=(n_panels, n_rows),
        in_specs=[pl.BlockSpec((tm, D), lambda j, i: (i, 0)),
                  pl.BlockSpec((tm, LANE), lambda j, i: (i, 0)),
                  pl.BlockSpec((1, D), lambda j, i: (0, 0)),
                  pl.BlockSpec((PANEL, D), lambda j, i: (panel_of(j), 0)),
                  pl.BlockSpec((LANE, D), lambda j, i: (0, 0))],
        out_specs=[pl.BlockSpec((tm, PANEL), lambda j, i: (i, j)),
                   pl.BlockSpec((tm, LANE), _first_panel_only(n_rows))],
        out_shape=[jax.ShapeDtypeStruct((T, n_panels * PANEL), out_dtype),
                   jax.ShapeDtypeStruct((T, LANE), jnp.float32)],
        scratch_shapes=[pltpu.VMEM((PANEL, D), jnp.bfloat16)],
        compiler_params=_inproj_params(),
        name=name,
    )(xb, rs, norm_w, w_t, wlr_t)


def _scan_stages(qkg, v, state, finish):
    bf = jnp.bfloat16
    f32 = jnp.float32
    dv, dk = state.shape
    blk = ATTN_BLOCK
    row = lax.broadcasted_iota(jnp.int32, (blk, blk), 0)
    col = lax.broadcasted_iota(jnp.int32, (blk, blk), 1)
    causal = (col <= row) & (col >= (row & -CHUNK))
    res = {}

    def gates():
        q, k, g = qkg()
        g_hi = g.astype(bf)
        g_lo = (g - g_hi.astype(f32)).astype(bf)
        g2 = jnp.concatenate([g_hi, g_lo], axis=1)
        tri = causal.astype(bf)
        res.update(q=q, k=k, n_blk=g.shape[0] // blk, n_chk=g.shape[0] // CHUNK)
        res['b2'] = [jnp.dot(tri, g2[r * blk:(r + 1) * blk], preferred_element_type=f32)
                     for r in range(res['n_blk'])]

    def decay():
        b = jnp.concatenate([b2[:, :dk] + b2[:, dk:] for b2 in res.pop('b2')], axis=0)
        eb = jnp.exp2(b)
        q, k = res.pop('q'), res.pop('k')
        res['qh'] = q * eb.astype(bf) if q.dtype == bf else (q * eb).astype(bf)
        res['kt'] = k * (1.0 / eb).astype(bf) if k.dtype == bf else (k / eb).astype(bf)
        res['vb'] = v().astype(bf)
        res['eb_last'] = [eb[(c + 1) * CHUNK - 1:(c + 1) * CHUNK, :] for c in range(res['n_chk'])]
        res['b_min'] = jnp.min(jnp.concatenate(
            [b[(c + 1) * CHUNK - 1:(c + 1) * CHUNK, :] for c in range(res['n_chk'])], axis=0))

    def scores():
        qh, kt, vb = res['qh'], res['kt'], res['vb']
        res['attn'] = [lax.dot_general(qh[r * blk:(r + 1) * blk], kt[r * blk:(r + 1) * blk], _NT,
                                       preferred_element_type=f32) for r in range(res['n_blk'])]
        res['upd'] = [lax.dot_general(vb[c * CHUNK:(c + 1) * CHUNK], kt[c * CHUNK:(c + 1) * CHUNK], _TN,
                                      preferred_element_type=f32) for c in range(res['n_chk'])]

    def carry():
        res['attn'] = [jnp.where(causal, a, 0.0).astype(bf) for a in res['attn']]
        st, st_in = state, []
        for upd, eb_last in zip(res.pop('upd'), res.pop('eb_last')):
            st_in.append(st.astype(bf))
            st = (st + upd) * eb_last
        res.update(st_in=st_in, state=st)

    def outputs():
        qh, vb = res.pop('qh'), res.pop('vb')
        res['intra'] = [jnp.dot(a, vb[r * blk:(r + 1) * blk], preferred_element_type=f32)
                        for r, a in enumerate(res.pop('attn'))]
        res['inter'] = [lax.dot_general(qh[c * CHUNK:(c + 1) * CHUNK], s, _NT, preferred_element_type=f32)
                        for c, s in enumerate(res.pop('st_in'))]

    def done():
        finish(jnp.concatenate(res.pop('intra'), axis=0) + jnp.concatenate(res.pop('inter'), axis=0))

    return [gates, decay, scores, carry, outputs, done], res


def _scan_tile_exact(qkg_ref, v_ref, st_ref, o_ref):
    bf = jnp.bfloat16
    f32 = jnp.float32
    n_heads, dv, dk = st_ref.shape
    tt = o_ref.shape[0]
    rowk = lax.broadcasted_iota(jnp.int32, (CHUNK, dk), 0)
    rowc = lax.broadcasted_iota(jnp.int32, (CHUNK, CHUNK), 0)
    colc = lax.broadcasted_iota(jnp.int32, (CHUNK, CHUNK), 1)
    tri = (colc <= rowc).astype(bf)

    for h in range(n_heads):
        ks = slice(h * dk, (h + 1) * dk)
        vs = slice(h * dv, (h + 1) * dv)

        def chunk(c, carry, ks=ks, vs=vs, h=h):
            r0 = pl.multiple_of(c * CHUNK, CHUNK)
            rows = pl.ds(r0, CHUNK)
            qc = qkg_ref[0, rows, ks]
            kc = qkg_ref[1, rows, ks]
            gc = qkg_ref[2, rows, ks]
            g_hi = gc.astype(bf)
            g_lo = (gc - g_hi.astype(f32)).astype(bf)
            b = (jnp.dot(tri, g_hi, preferred_element_type=f32)
                 + jnp.dot(tri, g_lo, preferred_element_type=f32))
            qkg_ref[2, rows, ks] = b

            def columns(jg, attn):
                j0 = pl.multiple_of(jg * SUBLANE, SUBLANE)
                b_rows = qkg_ref[2, pl.ds(r0 + j0, SUBLANE), ks]
                k_rows = qkg_ref[1, pl.ds(r0 + j0, SUBLANE), ks]
                for r in range(SUBLANE):
                    j = j0 + r
                    decay = jnp.exp2(jnp.where(rowk >= j, b - b_rows[r:r + 1], 0.0))
                    colv = jnp.sum(qc * k_rows[r:r + 1] * decay, axis=1, keepdims=True)
                    attn = jnp.where(colc == j, colv, attn)
                return attn

            attn = lax.fori_loop(0, CHUNK // SUBLANE, columns, jnp.zeros((CHUNK, CHUNK), f32))
            attn = jnp.where(colc <= rowc, attn, 0.0).astype(bf)
            vb = v_ref[rows, vs].astype(bf)
            st = st_ref[h]
            b_last = b[CHUNK - 1:CHUNK, :]
            o_ref[rows, vs] = (
                jnp.dot(attn, vb, preferred_element_type=f32)
                + lax.dot_general((qc * jnp.exp2(b)).astype(bf), st.astype(bf), _NT,
                                  preferred_element_type=f32))
            k_dec = (kc * jnp.exp2(b_last - b)).astype(bf)
            st_ref[h] = st * jnp.exp2(b_last) + lax.dot_general(vb, k_dec, _TN,
                                                                preferred_element_type=f32)
            return carry

        lax.fori_loop(0, tt // CHUNK, chunk, 0)


def _readout(o, z, n_heads, dk):
    dv = o.shape[1] // n_heads

    def head(oh):
        return oh * lax.rsqrt(jnp.mean(oh * oh, axis=-1, keepdims=True) + EPS * dk)

    on = jnp.concatenate([head(o[:, h * dv:(h + 1) * dv]) for h in range(n_heads)], axis=1)
    return on.astype(z.dtype) * _silu_of_half(z)


def _scan_and_readout(qkg, v_ref, z_ref, y_ref, st_ref, qkg_ref, o_ref):
    @pl.when(pl.program_id(1) == 0)
    def _():
        st_ref[...] = jnp.zeros_like(st_ref)

    n_heads, dv, dk = st_ref.shape
    heads = []
    for h in range(n_heads):
        ks = slice(h * dk, (h + 1) * dk)
        vs = slice(h * dv, (h + 1) * dv)

        def finish(o, vs=vs):
            y_ref[:, vs] = _readout(o, z_ref[:, vs], 1, dk).astype(y_ref.dtype)

        heads.append(_scan_stages(lambda ks=ks: qkg(ks), lambda vs=vs: v_ref[:, vs], st_ref[h], finish))
    n_stages = len(heads[0][0])
    for tick in range(n_heads + n_stages - 1):
        for h in range(n_heads):
            if 0 <= tick - h < n_stages:
                heads[h][0][tick - h]()
    st_new = [res['state'] for _, res in heads]
    b_min = heads[0][1]['b_min']
    for _, res in heads[1:]:
        b_min = jnp.minimum(b_min, res['b_min'])
    in_range = b_min >= -DECAY_RANGE_LIMIT

    @pl.when(in_range)
    def _():
        for h in range(n_heads):
            st_ref[h] = st_new[h]

    @pl.when(jnp.logical_not(in_range))
    def _():
        q, k, g = qkg(slice(None))
        qkg_ref[0] = q.astype(jnp.float32)
        qkg_ref[1] = k.astype(jnp.float32)
        qkg_ref[2] = g
        _scan_tile_exact(qkg_ref, v_ref, st_ref, o_ref)
        y_ref[...] = _readout(o_ref[...], z_ref[...], n_heads, dk).astype(y_ref.dtype)


def _hgrn_kernel(lb_ref, hq_ref, hf_ref, hi_ref, hz_ref, y_ref, st_ref, qkg_ref, o_ref):
    def qkg(cols):
        lbp = lb_ref[:, cols]
        e = jnp.exp(lbp - jnp.max(lbp, axis=0, keepdims=True))
        lb = e[0:1, :] / jnp.sum(e, axis=0, keepdims=True)
        f = (0.5 + 0.5 * lb) + (0.5 - 0.5 * lb) * jnp.tanh(hf_ref[:, cols])
        return _silu_of_half(hq_ref[:, cols]), 1.0 - f, jnp.log2(f)

    _scan_and_readout(qkg, hi_ref, hz_ref, y_ref, st_ref, qkg_ref, o_ref)


def _gla_kernel(gq_ref, gk_ref, gv_ref, gz_ref, lr_ref, w2_ref, gb_ref, y_ref, st_ref,
                qkg_ref, o_ref):
    def qkg(cols):
        logit = jnp.dot(lr_ref[...].astype(jnp.bfloat16), w2_ref[:, cols].astype(jnp.bfloat16),
                        preferred_element_type=jnp.float32) + gb_ref[:, cols]
        log2_sig = (jnp.minimum(logit, 0.0) * LOG2E
                    - jnp.log2(1.0 + jnp.exp2(jnp.abs(logit) * -LOG2E)))
        return gq_ref[:, cols], gk_ref[:, cols], log2_sig / GLA_GATE_TAU

    _scan_and_readout(qkg, gv_ref, gz_ref, y_ref, st_ref, qkg_ref, o_ref)


def _scan_params():
    return pltpu.CompilerParams(dimension_semantics=("parallel", "arbitrary"),
                                vmem_limit_bytes=VMEM_LIMIT)


def _hgrn(hf, pb, hgrn_lb, tt=512, hp=8):
    T = hf.shape[0]
    w = hp * HGRN_DK
    nb = D_HGRN // w
    blk = lambda sec: pl.BlockSpec((tt, w), lambda h, t: (t, sec * nb + h))
    return pl.pallas_call(
        _hgrn_kernel,
        grid=(HGRN_HEADS // hp, T // tt),
        in_specs=[
            pl.BlockSpec((2, w), lambda h, t: (0, h)),
            blk(0), blk(0), blk(1), blk(2),
        ],
        out_specs=pl.BlockSpec((tt, w), lambda h, t: (t, h)),
        out_shape=jax.ShapeDtypeStruct((T, D_HGRN), jnp.bfloat16),
        scratch_shapes=[pltpu.VMEM((hp, HGRN_DV, HGRN_DK), jnp.float32),
                        pltpu.VMEM((3, tt, w), jnp.float32),
                        pltpu.VMEM((tt, w), jnp.float32)],
        compiler_params=_scan_params(),
        name="hgrn_scan",
    )(hgrn_lb, pb, hf, pb, pb)


def _gla(pb, lr, w2_pad, gate_b, tt=512, hp=4):
    T = pb.shape[0]
    wk, wv = hp * GLA_DK, hp * GLA_DV
    q0 = 3 * PANEL // wk
    k0 = q0 + GLA_DK_TOTAL // wk
    v0 = 4 * PANEL // wv
    z0 = v0 + D_GLA // wv
    return pl.pallas_call(
        _gla_kernel,
        grid=(GLA_HEADS // hp, T // tt),
        in_specs=[
            pl.BlockSpec((tt, wk), lambda h, t: (t, q0 + h)),
            pl.BlockSpec((tt, wk), lambda h, t: (t, k0 + h)),
            pl.BlockSpec((tt, wv), lambda h, t: (t, v0 + h)),
            pl.BlockSpec((tt, wv), lambda h, t: (t, z0 + h)),
            pl.BlockSpec((tt, LANE), lambda h, t: (t, 0)),
            pl.BlockSpec((LANE, wk), lambda h, t: (0, h)),
            pl.BlockSpec((1, wk), lambda h, t: (0, h)),
        ],
        out_specs=pl.BlockSpec((tt, wv), lambda h, t: (t, h)),
        out_shape=jax.ShapeDtypeStruct((T, D_GLA), jnp.bfloat16),
        scratch_shapes=[pltpu.VMEM((hp, GLA_DV, GLA_DK), jnp.float32),
                        pltpu.VMEM((3, tt, wk), jnp.float32),
                        pltpu.VMEM((tt, wv), jnp.float32)],
        compiler_params=_scan_params(),
        name="gla_scan",
    )(pb, pb, pb, pb, lr, w2_pad, gate_b)


OUT_SUB = 2
OUT_VMEM_LIMIT = 58 * 1024 * 1024


def _out_kernel(x_ref, yh_ref, yg_ref, p_ref, woh_ref, wog_ref, pp_ref, pnw_ref, gnw_ref,
                gw_ref, gb_ref, fnw_ref, o_ref):
    f32 = jnp.float32
    bf = jnp.bfloat16
    rows = o_ref.shape[0] // OUT_SUB
    subs = [slice(s * rows, (s + 1) * rows) for s in range(OUT_SUB)]
    h = [x_ref[r, :]
         + jnp.dot(yh_ref[r, :], woh_ref[...], preferred_element_type=f32)
         + jnp.dot(yg_ref[r, :], wog_ref[...], preferred_element_type=f32) for r in subs]
    e = [_rms(jnp.dot(p_ref[r, :].astype(bf), pp_ref[...], preferred_element_type=f32), pnw_ref[...])
         for r in subs]
    hn = [_rms(hs, gnw_ref[...]).astype(bf) for hs in h]
    acc = [jnp.dot(hs, gw_ref[...], preferred_element_type=f32) for hs in hn]
    for r, hs, es, a in zip(subs, h, e, acc):
        o_ref[r, :] = _rms(hs + _sigmoid(a + gb_ref[...]) * es, fnw_ref[...])


def _outproj(x2, yh, yg, p2, wo, pp, pnw, gnw, gw, gb, fnw, tm=512):
    T, D = x2.shape
    row = lambda w: pl.BlockSpec((tm, w), lambda i: (i, 0))
    once = pl.Buffered(1)
    full = lambda a: pl.BlockSpec(a.shape, lambda i: (0, 0), pipeline_mode=once)
    return pl.pallas_call(
        _out_kernel,
        grid=(T // tm,),
        in_specs=[row(D), row(D_HGRN), row(D_GLA), row(PLE_DIM),
                  pl.BlockSpec((D_HGRN, D), lambda i: (0, 0), pipeline_mode=once),
                  pl.BlockSpec((D_GLA, D), lambda i: (D_HGRN // D_GLA, 0), pipeline_mode=once),
                  full(pp), full(pnw), full(gnw), full(gw), full(gb), full(fnw)],
        out_specs=row(D),
        out_shape=jax.ShapeDtypeStruct((T, D), jnp.float32),
        compiler_params=pltpu.CompilerParams(
            dimension_semantics=("parallel",),
            vmem_limit_bytes=OUT_VMEM_LIMIT),
        name="outproj_ple",
    )(x2, yh, yg, p2, wo, wo, pp, pnw, gnw, gw, gb, fnw)


def kernel(x, p, norm_mix_w, w_in, hgrn_lb, gla_gate_w2, gla_gate_b, hgrn_norm_w, gla_norm_w,
           w_out, ple_proj, ple_norm_w, ple_gate_norm_w, ple_gate_w, ple_gate_b, final_norm_w):
    B, T, D = x.shape
    bf = jnp.bfloat16
    x2 = x.reshape(B * T, D)
    p2 = p[0].reshape(B * T, PLE_DIM)
    row = lambda a: a.reshape(1, -1)

    w_t = jnp.swapaxes(w_in[0], 0, 1)
    wlr_t = jnp.pad(w_t[D_MAIN:], ((0, LANE - GLA_GATE_RANK), (0, 0)))
    nw = row(norm_mix_w[0])
    hf, xb, rs = _inproj_first(x2, nw, w_t, lambda j: j + 1, 1, (0,), jnp.float32, "inproj_f")
    pb, lr = _inproj_rest(xb, rs, nw, w_t, wlr_t, lambda j: j + jnp.minimum(j, 1), 6, (0, 2, 5), bf,
                          "inproj_b")

    y_h = _hgrn(hf, pb, hgrn_lb)
    w2_pad = jnp.pad(gla_gate_w2[0], ((0, LANE - GLA_GATE_RANK), (0, 0)))
    y_g = _gla(pb, lr, w2_pad, row(gla_gate_b[0]))

    head_gain = jnp.concatenate([hgrn_norm_w[0], gla_norm_w[0]])[:, None]
    out = _outproj(x2, y_h, y_g, p2, (w_out[0] * head_gain).astype(bf), ple_proj[0].astype(bf),
                   row(ple_norm_w[0]), row(ple_gate_norm_w[0]), ple_gate_w[0].astype(bf),
                   row(ple_gate_b[0]), row(final_norm_w))
    return out.reshape(B, T, D)
```

```python
import functools

import jax
import jax.numpy as jnp
from jax import lax
from jax.experimental import pallas as pl
from jax.experimental.pallas import tpu as pltpu

D_MODEL = 2048
D_HGRN = 1024
HGRN_HEADS = 8
HGRN_DK = 128
HGRN_DV = 128
GLA_HEADS = 4
GLA_DK = 128
GLA_DV = 256
GLA_DK_TOTAL = 512
D_GLA = 1024
GLA_GATE_RANK = 16
GLA_GATE_TAU = 16.0
PLE_DIM = 256
EPS = 1e-6

LANE = 128
SUBLANE = 8
D_MAIN = 4 * D_HGRN + 2 * GLA_DK_TOTAL + 2 * D_GLA
PANEL = 1024
CHUNK = 128
ATTN_BLOCK = 128
DECAY_RANGE_LIMIT = 110.0
LOG2E = 1.4426950408889634
VMEM_LIMIT = 48 * 1024 * 1024
INPROJ_VMEM_LIMIT = 62 * 1024 * 1024

_NT = (((1,), (1,)), ((), ()))
_TN = (((0,), (0,)), ((), ()))


def _sigmoid(x):
    return 0.5 + 0.5 * jnp.tanh(0.5 * x)


def _silu_of_half(h):
    return h + h * jnp.tanh(h)


def _rms(x, w):
    return x * lax.rsqrt(jnp.mean(x * x, axis=-1, keepdims=True) + EPS) * w


def _is_half_panel(half_panels):
    return functools.reduce(jnp.logical_or, [pl.program_id(0) == jp for jp in half_panels])


def _cast_weight_panel(w_ref, nw_ref, wb_ref):
    @pl.when(pl.program_id(1) == 0)
    def _():
        wb_ref[...] = (w_ref[...] * nw_ref[...]).astype(jnp.bfloat16)


def _inproj_first_kernel(x_ref, nw_ref, w_ref, o_ref, xb_ref, rs_ref, wb_ref, *, half_panels):
    bf = jnp.bfloat16
    _cast_weight_panel(w_ref, nw_ref, wb_ref)

    def body(first_panel):
        x = x_ref[...]
        r = lax.rsqrt(jnp.mean(x * x, axis=-1, keepdims=True) + EPS)
        xb = x.astype(bf)
        o_ref[...] = (lax.dot_general(xb, wb_ref[...], _NT, preferred_element_type=jnp.float32)
                      * (r * jnp.where(_is_half_panel(half_panels), 0.5, 1.0))).astype(o_ref.dtype)
        if first_panel:
            xb_ref[...] = xb
            rs_ref[...] = jnp.broadcast_to(r, rs_ref.shape)

    first = pl.program_id(0) == 0
    pl.when(first)(lambda: body(True))
    pl.when(jnp.logical_not(first))(lambda: body(False))


def _inproj_rest_kernel(xb_ref, rs_ref, nw_ref, w_ref, o_ref, wb_ref, *, half_panels):
    _cast_weight_panel(w_ref, nw_ref, wb_ref)
    r = rs_ref[:, 0:1] * jnp.where(_is_half_panel(half_panels), 0.5, 1.0)
    o_ref[...] = (lax.dot_general(xb_ref[...], wb_ref[...], _NT, preferred_element_type=jnp.float32)
                  * r).astype(o_ref.dtype)


def _inproj_params():
    return pltpu.CompilerParams(dimension_semantics=("parallel", "arbitrary"),
                                vmem_limit_bytes=INPROJ_VMEM_LIMIT)


def _first_panel_only(n_rows):
    return lambda j, i: (jnp.where(j == 0, i, n_rows - 1), 0)


def _inproj_first(x2, norm_w, w_t, panel_of, n_panels, half_panels, out_dtype, name, tm=1024):
    T, D = x2.shape
    n_rows = T // tm
    return pl.pallas_call(
        functools.partial(_inproj_first_kernel, half_panels=half_panels),
        grid=(n_panels, n_rows),
        in_specs=[pl.BlockSpec((tm, D), lambda j, i: (i, 0)),
                  pl.BlockSpec((1, D), lambda j, i: (0, 0)),
                  pl.BlockSpec((PANEL, D), lambda j, i: (panel_of(j), 0))],
        out_specs=[pl.BlockSpec((tm, PANEL), lambda j, i: (i, j)),
                   pl.BlockSpec((tm, D), _first_panel_only(n_rows)),
                   pl.BlockSpec((tm, LANE), _first_panel_only(n_rows))],
        out_shape=[jax.ShapeDtypeStruct((T, n_panels * PANEL), out_dtype),
                   jax.ShapeDtypeStruct((T, D), jnp.bfloat16),
                   jax.ShapeDtypeStruct((T, LANE), jnp.float32)],
        scratch_shapes=[pltpu.VMEM((PANEL, D), jnp.bfloat16)],
        compiler_params=_inproj_params(),
        name=name,
    )(x2, norm_w, w_t)


def _inproj_rest(xb, rs, norm_w, w_t, panel_of, n_panels, half_panels, out_dtype, name, tm=2048):
    T, D = xb.shape
    return pl.pallas_call(
        functools.partial(_inproj_rest_kernel, half_panels=half_panels),
        grid=(n_panels, T // tm),
        in_specs=[pl.BlockSpec((tm, D), lambda j, i: (i, 0)),
                  pl.BlockSpec((tm, LANE), lambda j, i: (i, 0)),
                  pl.BlockSpec((1, D), lambda j, i: (0, 0)),
                  pl.BlockSpec((PANEL, D), lambda j, i: (panel_of(j), 0))],
        out_specs=pl.BlockSpec((tm, PANEL), lambda j, i: (i, j)),
        out_shape=jax.ShapeDtypeStruct((T, n_panels * PANEL), out_dtype),
        scratch_shapes=[pltpu.VMEM((PANEL, D), jnp.bfloat16)],
        compiler_params=_inproj_params(),
        name=name,
    )(xb, rs, norm_w, w_t)


def _scan_stages(qkg, v, state, finish):
    bf = jnp.bfloat16
    f32 = jnp.float32
    dv, dk = state.shape
    blk = ATTN_BLOCK
    row = lax.broadcasted_iota(jnp.int32, (blk, blk), 0)
    col = lax.broadcasted_iota(jnp.int32, (blk, blk), 1)
    causal = (col <= row) & (col >= (row & -CHUNK))
    res = {}

    def gates():
        q, k, g = qkg()
        g_hi = g.astype(bf)
        g_lo = (g - g_hi.astype(f32)).astype(bf)
        tri = causal.astype(bf)
        tri2 = jnp.concatenate([tri, tri], axis=1)
        res.update(q=q, k=k, n_blk=g.shape[0] // blk, n_chk=g.shape[0] // CHUNK)
        res['b'] = [jnp.dot(tri2, jnp.concatenate([g_hi[r * blk:(r + 1) * blk],
                                                     g_lo[r * blk:(r + 1) * blk]], axis=0),
                            preferred_element_type=f32) for r in range(res['n_blk'])]

    def decay():
        b = jnp.concatenate(res.pop('b'), axis=0)
        eb = jnp.exp2(b)
        q, k = res.pop('q'), res.pop('k')
        res['qh'] = q * eb.astype(bf) if q.dtype == bf else (q * eb).astype(bf)
        res['kt'] = k * (1.0 / eb).astype(bf) if k.dtype == bf else (k / eb).astype(bf)
        res['vb'] = v().astype(bf)
        res['eb_last'] = [eb[(c + 1) * CHUNK - 1:(c + 1) * CHUNK, :] for c in range(res['n_chk'])]
        res['b_min'] = jnp.min(jnp.concatenate(
            [b[(c + 1) * CHUNK - 1:(c + 1) * CHUNK, :] for c in range(res['n_chk'])], axis=0))

    def scores():
        qh, kt, vb = res['qh'], res['kt'], res['vb']
        res['attn'] = [lax.dot_general(qh[r * blk:(r + 1) * blk], kt[r * blk:(r + 1) * blk], _NT,
                                       preferred_element_type=f32) for r in range(res['n_blk'])]
        res['upd'] = [lax.dot_general(vb[c * CHUNK:(c + 1) * CHUNK], kt[c * CHUNK:(c + 1) * CHUNK], _TN,
                                      preferred_element_type=f32) for c in range(res['n_chk'])]

    def carry():
        res['attn'] = [jnp.where(causal, a, 0.0).astype(bf) for a in res['attn']]
        st, st_in = state, []
        for upd, eb_last in zip(res.pop('upd'), res.pop('eb_last')):
            st_in.append(st.astype(bf))
            st = (st + upd) * eb_last
        res.update(st_in=st_in, state=st)

    def outputs():
        qh, vb = res.pop('qh'), res.pop('vb')
        res['intra'] = [jnp.dot(a, vb[r * blk:(r + 1) * blk], preferred_element_type=f32)
                        for r, a in enumerate(res.pop('attn'))]
        res['inter'] = [lax.dot_general(qh[c * CHUNK:(c + 1) * CHUNK], s, _NT, preferred_element_type=f32)
                        for c, s in enumerate(res.pop('st_in'))]

    def done():
        finish(jnp.concatenate(res.pop('intra'), axis=0) + jnp.concatenate(res.pop('inter'), axis=0))

    return [gates, decay, scores, carry, outputs, done], res


def _scan_tile_exact(qkg_ref, v_ref, st_ref, o_ref):
    bf = jnp.bfloat16
    f32 = jnp.float32
    n_heads, dv, dk = st_ref.shape
    tt = o_ref.shape[0]
    rowk = lax.broadcasted_iota(jnp.int32, (CHUNK, dk), 0)
    rowc = lax.broadcasted_iota(jnp.int32, (CHUNK, CHUNK), 0)
    colc = lax.broadcasted_iota(jnp.int32, (CHUNK, CHUNK), 1)
    tri = (colc <= rowc).astype(bf)

    for h in range(n_heads):
        ks = slice(h * dk, (h + 1) * dk)
        vs = slice(h * dv, (h + 1) * dv)

        def chunk(c, carry, ks=ks, vs=vs, h=h):
            r0 = pl.multiple_of(c * CHUNK, CHUNK)
            rows = pl.ds(r0, CHUNK)
            qc = qkg_ref[0, rows, ks]
            kc = qkg_ref[1, rows, ks]
            gc = qkg_ref[2, rows, ks]
            g_hi = gc.astype(bf)
            g_lo = (gc - g_hi.astype(f32)).astype(bf)
            b = (jnp.dot(tri, g_hi, preferred_element_type=f32)
                 + jnp.dot(tri, g_lo, preferred_element_type=f32))
            qkg_ref[2, rows, ks] = b

            def columns(jg, attn):
                j0 = pl.multiple_of(jg * SUBLANE, SUBLANE)
                b_rows = qkg_ref[2, pl.ds(r0 + j0, SUBLANE), ks]
                k_rows = qkg_ref[1, pl.ds(r0 + j0, SUBLANE), ks]
                for r in range(SUBLANE):
                    j = j0 + r
                    decay = jnp.exp2(jnp.where(rowk >= j, b - b_rows[r:r + 1], 0.0))
                    colv = jnp.sum(qc * k_rows[r:r + 1] * decay, axis=1, keepdims=True)
                    attn = jnp.where(colc == j, colv, attn)
                return attn

            attn = lax.fori_loop(0, CHUNK // SUBLANE, columns, jnp.zeros((CHUNK, CHUNK), f32))
            attn = jnp.where(colc <= rowc, attn, 0.0).astype(bf)
            vb = v_ref[rows, vs].astype(bf)
            st = st_ref[h]
            b_last = b[CHUNK - 1:CHUNK, :]
            o_ref[rows, vs] = (
                jnp.dot(attn, vb, preferred_element_type=f32)
                + lax.dot_general((qc * jnp.exp2(b)).astype(bf), st.astype(bf), _NT,
                                  preferred_element_type=f32))
            k_dec = (kc * jnp.exp2(b_last - b)).astype(bf)
            st_ref[h] = st * jnp.exp2(b_last) + lax.dot_general(vb, k_dec, _TN,
                                                                preferred_element_type=f32)
            return carry

        lax.fori_loop(0, tt // CHUNK, chunk, 0)


def _readout(o, z, n_heads, dk):
    dv = o.shape[1] // n_heads

    def head(oh):
        return oh * lax.rsqrt(jnp.mean(oh * oh, axis=-1, keepdims=True) + EPS * dk)

    on = jnp.concatenate([head(o[:, h * dv:(h + 1) * dv]) for h in range(n_heads)], axis=1)
    return on.astype(z.dtype) * _silu_of_half(z)


def _scan_and_readout(qkg, v_ref, z_ref, y_ref, st_ref, qkg_ref, o_ref, side_work=None):
    @pl.when(pl.program_id(1) == 0)
    def _():
        st_ref[...] = jnp.zeros_like(st_ref)

    if side_work is not None:
        side_work()
    n_heads, dv, dk = st_ref.shape
    heads = []
    for h in range(n_heads):
        ks = slice(h * dk, (h + 1) * dk)
        vs = slice(h * dv, (h + 1) * dv)

        def finish(o, vs=vs):
            y_ref[:, vs] = _readout(o, z_ref[:, vs], 1, dk).astype(y_ref.dtype)

        heads.append(_scan_stages(lambda ks=ks: qkg(ks), lambda vs=vs: v_ref[:, vs], st_ref[h], finish))
    n_stages = len(heads[0][0])
    for tick in range(n_heads + n_stages - 1):
        for h in range(n_heads):
            if 0 <= tick - h < n_stages:
                heads[h][0][tick - h]()
    st_new = [res['state'] for _, res in heads]
    b_min = heads[0][1]['b_min']
    for _, res in heads[1:]:
        b_min = jnp.minimum(b_min, res['b_min'])
    in_range = b_min >= -DECAY_RANGE_LIMIT

    @pl.when(in_range)
    def _():
        for h in range(n_heads):
            st_ref[h] = st_new[h]

    @pl.when(jnp.logical_not(in_range))
    def _():
        q, k, g = qkg(slice(None))
        qkg_ref[0] = q.astype(jnp.float32)
        qkg_ref[1] = k.astype(jnp.float32)
        qkg_ref[2] = g
        _scan_tile_exact(qkg_ref, v_ref, st_ref, o_ref)
        y_ref[...] = _readout(o_ref[...], z_ref[...], n_heads, dk).astype(y_ref.dtype)


def _hgrn_kernel(lb_ref, hq_ref, hf_ref, hi_ref, hz_ref, xb_ref, rs_ref, wlr_ref, nw_ref,
                 y_ref, lr_ref, st_ref, qkg_ref, o_ref):
    def gate_rank_projection():
        wlr = (wlr_ref[...] * nw_ref[...]).astype(jnp.bfloat16)
        lr_ref[...] = lax.dot_general(xb_ref[...], wlr, _NT,
                                      preferred_element_type=jnp.float32) * rs_ref[:, 0:1]

    def qkg(cols):
        lbp = lb_ref[:, cols]
        e = jnp.exp(lbp - jnp.max(lbp, axis=0, keepdims=True))
        lb = e[0:1, :] / jnp.sum(e, axis=0, keepdims=True)
        f = (0.5 + 0.5 * lb) + (0.5 - 0.5 * lb) * jnp.tanh(hf_ref[:, cols])
        return _silu_of_half(hq_ref[:, cols]), 1.0 - f, jnp.log2(f)

    _scan_and_readout(qkg, hi_ref, hz_ref, y_ref, st_ref, qkg_ref, o_ref, gate_rank_projection)


def _gla_kernel(gq_ref, gk_ref, gv_ref, gz_ref, lr_ref, w2_ref, gb_ref, y_ref, st_ref,
                qkg_ref, o_ref):
    def qkg(cols):
        logit = jnp.dot(lr_ref[...].astype(jnp.bfloat16), w2_ref[:, cols].astype(jnp.bfloat16),
                        preferred_element_type=jnp.float32) + gb_ref[:, cols]
        log2_sig = (jnp.minimum(logit, 0.0) * LOG2E
                    - jnp.log2(1.0 + jnp.exp2(jnp.abs(logit) * -LOG2E)))
        return gq_ref[:, cols], gk_ref[:, cols], log2_sig / GLA_GATE_TAU

    _scan_and_readout(qkg, gv_ref, gz_ref, y_ref, st_ref, qkg_ref, o_ref)


def _scan_params():
    return pltpu.CompilerParams(dimension_semantics=("parallel", "arbitrary"),
                                vmem_limit_bytes=VMEM_LIMIT)


def _hgrn(hf, pb, hgrn_lb, xb, rs, wlr_t, norm_w, tt=512):
    T, D = xb.shape
    w = D_HGRN
    blk = lambda sec: pl.BlockSpec((tt, w), lambda h, t: (t, sec))
    rows = lambda width: pl.BlockSpec((tt, width), lambda h, t: (t, 0))
    const = lambda a: pl.BlockSpec(a.shape, lambda h, t: (0, 0))
    return pl.pallas_call(
        _hgrn_kernel,
        grid=(1, T // tt),
        in_specs=[const(hgrn_lb), blk(0), blk(0), blk(1), blk(2),
                  rows(D), rows(LANE), const(wlr_t), const(norm_w)],
        out_specs=[pl.BlockSpec((tt, w), lambda h, t: (t, 0)), rows(LANE)],
        out_shape=[jax.ShapeDtypeStruct((T, D_HGRN), jnp.bfloat16),
                   jax.ShapeDtypeStruct((T, LANE), jnp.float32)],
        scratch_shapes=[pltpu.VMEM((HGRN_HEADS, HGRN_DV, HGRN_DK), jnp.float32),
                        pltpu.VMEM((3, tt, w), jnp.float32),
                        pltpu.VMEM((tt, w), jnp.float32)],
        compiler_params=_scan_params(),
        name="hgrn_scan",
    )(hgrn_lb, pb, hf, pb, pb, xb, rs, wlr_t, norm_w)


def _gla(pb, lr, w2_pad, gate_b, tt=512, hp=4):
    T = pb.shape[0]
    wk, wv = hp * GLA_DK, hp * GLA_DV
    q0 = 3 * PANEL // wk
    k0 = q0 + GLA_DK_TOTAL // wk
    v0 = 4 * PANEL // wv
    z0 = v0 + D_GLA // wv
    return pl.pallas_call(
        _gla_kernel,
        grid=(GLA_HEADS // hp, T // tt),
        in_specs=[
            pl.BlockSpec((tt, wk), lambda h, t: (t, q0 + h)),
            pl.BlockSpec((tt, wk), lambda h, t: (t, k0 + h)),
            pl.BlockSpec((tt, wv), lambda h, t: (t, v0 + h)),
            pl.BlockSpec((tt, wv), lambda h, t: (t, z0 + h)),
            pl.BlockSpec((tt, LANE), lambda h, t: (t, 0)),
            pl.BlockSpec((LANE, wk), lambda h, t: (0, h)),
            pl.BlockSpec((1, wk), lambda h, t: (0, h)),
        ],
        out_specs=pl.BlockSpec((tt, wv), lambda h, t: (t, h)),
        out_shape=jax.ShapeDtypeStruct((T, D_GLA), jnp.bfloat16),
        scratch_shapes=[pltpu.VMEM((hp, GLA_DV, GLA_DK), jnp.float32),
                        pltpu.VMEM((3, tt, wk), jnp.float32),
                        pltpu.VMEM((tt, wv), jnp.float32)],
        compiler_params=_scan_params(),
        name="gla_scan",
    )(pb, pb, pb, pb, lr, w2_pad, gate_b)


OUT_SUB = 2
OUT_VMEM_LIMIT = 58 * 1024 * 1024


def _out_kernel(x_ref, yh_ref, yg_ref, p_ref, woh_ref, wog_ref, pp_ref, pnw_ref, gnw_ref,
                gw_ref, gb_ref, fnw_ref, o_ref):
    f32 = jnp.float32
    bf = jnp.bfloat16
    rows = o_ref.shape[0] // OUT_SUB
    subs = [slice(s * rows, (s + 1) * rows) for s in range(OUT_SUB)]
    h = [x_ref[r, :]
         + jnp.dot(yh_ref[r, :], woh_ref[...], preferred_element_type=f32)
         + jnp.dot(yg_ref[r, :], wog_ref[...], preferred_element_type=f32) for r in subs]
    e = [_rms(jnp.dot(p_ref[r, :].astype(bf), pp_ref[...], preferred_element_type=f32), pnw_ref[...])
         for r in subs]
    hn = [_rms(hs, gnw_ref[...]).astype(bf) for hs in h]
    acc = [jnp.dot(hs, gw_ref[...], preferred_element_type=f32) for hs in hn]
    for r, hs, es, a in zip(subs, h, e, acc):
        o_ref[r, :] = _rms(hs + _sigmoid(a + gb_ref[...]) * es, fnw_ref[...])


def _outproj(x2, yh, yg, p2, wo, pp, pnw, gnw, gw, gb, fnw, tm=512):
    T, D = x2.shape
    row = lambda w: pl.BlockSpec((tm, w), lambda i: (i, 0))
    once = pl.Buffered(1)
    full = lambda a: pl.BlockSpec(a.shape, lambda i: (0, 0), pipeline_mode=once)
    return pl.pallas_call(
        _out_kernel,
        grid=(T // tm,),
        in_specs=[row(D), row(D_HGRN), row(D_GLA), row(PLE_DIM),
                  pl.BlockSpec((D_HGRN, D), lambda i: (0, 0), pipeline_mode=once),
                  pl.BlockSpec((D_GLA, D), lambda i: (D_HGRN // D_GLA, 0), pipeline_mode=once),
                  full(pp), full(pnw), full(gnw), full(gw), full(gb), full(fnw)],
        out_specs=row(D),
        out_shape=jax.ShapeDtypeStruct((T, D), jnp.float32),
        compiler_params=pltpu.CompilerParams(
            dimension_semantics=("parallel",),
            vmem_limit_bytes=OUT_VMEM_LIMIT),
        name="outproj_ple",
    )(x2, yh, yg, p2, wo, wo, pp, pnw, gnw, gw, gb, fnw)


def kernel(x, p, norm_mix_w, w_in, hgrn_lb, gla_gate_w2, gla_gate_b, hgrn_norm_w, gla_norm_w,
           w_out, ple_proj, ple_norm_w, ple_gate_norm_w, ple_gate_w, ple_gate_b, final_norm_w):
    B, T, D = x.shape
    bf = jnp.bfloat16
    x2 = x.reshape(B * T, D)
    p2 = p[0].reshape(B * T, PLE_DIM)
    row = lambda a: a.reshape(1, -1)

    w_t = jnp.swapaxes(w_in[0], 0, 1)
    wlr_t = jnp.pad(w_t[D_MAIN:], ((0, LANE - GLA_GATE_RANK), (0, 0)))
    nw = row(norm_mix_w[0])
    hf, xb, rs = _inproj_first(x2, nw, w_t, lambda j: j + 1, 1, (0,), jnp.float32, "inproj_f")
    pb = _inproj_rest(xb, rs, nw, w_t, lambda j: j + jnp.minimum(j, 1), 6, (0, 2, 5), bf,
                      "inproj_b")

    y_h, lr = _hgrn(hf, pb, hgrn_lb, xb, rs, wlr_t, nw)
    w2_pad = jnp.pad(gla_gate_w2[0], ((0, LANE - GLA_GATE_RANK), (0, 0)))
    y_g = _gla(pb, lr, w2_pad, row(gla_gate_b[0]))

    head_gain = jnp.concatenate([hgrn_norm_w[0], gla_norm_w[0]])[:, None]
    out = _outproj(x2, y_h, y_g, p2, (w_out[0] * head_gain).astype(bf), ple_proj[0].astype(bf),
                   row(ple_norm_w[0]), row(ple_gate_norm_w[0]), ple_gate_w[0].astype(bf),
                   row(ple_gate_b[0]), row(final_norm_w))
    return out.reshape(B, T, D)
```

```python
import functools

import jax
import jax.numpy as jnp
from jax import lax
from jax.experimental import pallas as pl
from jax.experimental.pallas import tpu as pltpu

D_MODEL = 2048
D_HGRN = 1024
HGRN_HEADS = 8
HGRN_DK = 128
HGRN_DV = 128
GLA_HEADS = 4
GLA_DK = 128
GLA_DV = 256
GLA_DK_TOTAL = 512
D_GLA = 1024
GLA_GATE_RANK = 16
GLA_GATE_TAU = 16.0
PLE_DIM = 256
EPS = 1e-6

LANE = 128
SUBLANE = 8
D_MAIN = 4 * D_HGRN + 2 * GLA_DK_TOTAL + 2 * D_GLA
PANEL = 1024
CHUNK = 128
ATTN_BLOCK = 128
DECAY_RANGE_LIMIT = 110.0
LOG2E = 1.4426950408889634
VMEM_LIMIT = 48 * 1024 * 1024
INPROJ_VMEM_LIMIT = 62 * 1024 * 1024

_NT = (((1,), (1,)), ((), ()))
_TN = (((0,), (0,)), ((), ()))


def _sigmoid(x):
    return 0.5 + 0.5 * jnp.tanh(0.5 * x)


def _silu_of_half(h):
    return h + h * jnp.tanh(h)


def _rms(x, w):
    return x * lax.rsqrt(jnp.mean(x * x, axis=-1, keepdims=True) + EPS) * w


def _is_half_panel(half_panels):
    return functools.reduce(jnp.logical_or, [pl.program_id(0) == jp for jp in half_panels])


def _cast_weight_panel(w_ref, nw_ref, wb_ref):
    @pl.when(pl.program_id(1) == 0)
    def _():
        wb_ref[...] = (w_ref[...] * nw_ref[...]).astype(jnp.bfloat16)


def _inproj_first_kernel(x_ref, nw_ref, w_ref, o_ref, xb_ref, rs_ref, wb_ref, *, half_panels):
    bf = jnp.bfloat16
    _cast_weight_panel(w_ref, nw_ref, wb_ref)

    def body(first_panel):
        x = x_ref[...]
        r = lax.rsqrt(jnp.mean(x * x, axis=-1, keepdims=True) + EPS)
        xb = x.astype(bf)
        o_ref[...] = (lax.dot_general(xb, wb_ref[...], _NT, preferred_element_type=jnp.float32)
                      * (r * jnp.where(_is_half_panel(half_panels), 0.5, 1.0))).astype(o_ref.dtype)
        if first_panel:
            xb_ref[...] = xb
            rs_ref[...] = jnp.broadcast_to(r, rs_ref.shape)

    first = pl.program_id(0) == 0
    pl.when(first)(lambda: body(True))
    pl.when(jnp.logical_not(first))(lambda: body(False))


def _inproj_rest_kernel(xb_ref, rs_ref, nw_ref, w_ref, o_ref, wb_ref, *, half_panels):
    _cast_weight_panel(w_ref, nw_ref, wb_ref)
    r = rs_ref[:, 0:1] * jnp.where(_is_half_panel(half_panels), 0.5, 1.0)
    o_ref[...] = (lax.dot_general(xb_ref[...], wb_ref[...], _NT, preferred_element_type=jnp.float32)
                  * r).astype(o_ref.dtype)


def _inproj_params():
    return pltpu.CompilerParams(dimension_semantics=("parallel", "arbitrary"),
                                vmem_limit_bytes=INPROJ_VMEM_LIMIT)


def _first_panel_only(n_rows):
    return lambda j, i: (jnp.where(j == 0, i, n_rows - 1), 0)


def _inproj_first(x2, norm_w, w_t, panel_of, n_panels, half_panels, out_dtype, name, tm=1024):
    T, D = x2.shape
    n_rows = T // tm
    return pl.pallas_call(
        functools.partial(_inproj_first_kernel, half_panels=half_panels),
        grid=(n_panels, n_rows),
        in_specs=[pl.BlockSpec((tm, D), lambda j, i: (i, 0)),
                  pl.BlockSpec((1, D), lambda j, i: (0, 0)),
                  pl.BlockSpec((PANEL, D), lambda j, i: (panel_of(j), 0))],
        out_specs=[pl.BlockSpec((tm, PANEL), lambda j, i: (i, j)),
                   pl.BlockSpec((tm, D), _first_panel_only(n_rows)),
                   pl.BlockSpec((tm, LANE), _first_panel_only(n_rows))],
        out_shape=[jax.ShapeDtypeStruct((T, n_panels * PANEL), out_dtype),
                   jax.ShapeDtypeStruct((T, D), jnp.bfloat16),
                   jax.ShapeDtypeStruct((T, LANE), jnp.float32)],
        scratch_shapes=[pltpu.VMEM((PANEL, D), jnp.bfloat16)],
        compiler_params=_inproj_params(),
        name=name,
    )(x2, norm_w, w_t)


def _inproj_rest(xb, rs, norm_w, w_t, panel_of, n_panels, half_panels, out_dtype, name, tm=2048):
    T, D = xb.shape
    return pl.pallas_call(
        functools.partial(_inproj_rest_kernel, half_panels=half_panels),
        grid=(n_panels, T // tm),
        in_specs=[pl.BlockSpec((tm, D), lambda j, i: (i, 0)),
                  pl.BlockSpec((tm, LANE), lambda j, i: (i, 0)),
                  pl.BlockSpec((1, D), lambda j, i: (0, 0)),
                  pl.BlockSpec((PANEL, D), lambda j, i: (panel_of(j), 0))],
        out_specs=pl.BlockSpec((tm, PANEL), lambda j, i: (i, j)),
        out_shape=jax.ShapeDtypeStruct((T, n_panels * PANEL), out_dtype),
        scratch_shapes=[pltpu.VMEM((PANEL, D), jnp.bfloat16)],
        compiler_params=_inproj_params(),
        name=name,
    )(xb, rs, norm_w, w_t)


def _scan_stages(qkg, v, state, finish):
    bf = jnp.bfloat16
    f32 = jnp.float32
    dv, dk = state.shape
    blk = ATTN_BLOCK
    row = lax.broadcasted_iota(jnp.int32, (blk, blk), 0)
    col = lax.broadcasted_iota(jnp.int32, (blk, blk), 1)
    causal = (col <= row) & (col >= (row & -CHUNK))
    res = {}

    def gates():
        q, k, g = qkg()
        g_hi = g.astype(bf)
        g_lo = (g - g_hi.astype(f32)).astype(bf)
        tri = causal.astype(bf)
        tri2 = jnp.concatenate([tri, tri], axis=1)
        res.update(q=q, k=k, n_blk=g.shape[0] // blk, n_chk=g.shape[0] // CHUNK)
        res['b'] = [jnp.dot(tri2, jnp.concatenate([g_hi[r * blk:(r + 1) * blk],
                                                     g_lo[r * blk:(r + 1) * blk]], axis=0),
                            preferred_element_type=f32) for r in range(res['n_blk'])]

    def decay():
        b = jnp.concatenate(res.pop('b'), axis=0)
        eb = jnp.exp2(b)
        q, k = res.pop('q'), res.pop('k')
        res['qh'] = q * eb.astype(bf) if q.dtype == bf else (q * eb).astype(bf)
        res['kt'] = k * (1.0 / eb).astype(bf) if k.dtype == bf else (k / eb).astype(bf)
        res['vb'] = v().astype(bf)
        res['eb_last'] = [eb[(c + 1) * CHUNK - 1:(c + 1) * CHUNK, :] for c in range(res['n_chk'])]
        res['b_min'] = jnp.min(jnp.concatenate(
            [b[(c + 1) * CHUNK - 1:(c + 1) * CHUNK, :] for c in range(res['n_chk'])], axis=0))

    def scores():
        qh, kt, vb = res['qh'], res['kt'], res['vb']
        res['attn'] = [lax.dot_general(qh[r * blk:(r + 1) * blk], kt[r * blk:(r + 1) * blk], _NT,
                                       preferred_element_type=f32) for r in range(res['n_blk'])]
        res['upd'] = [lax.dot_general(vb[c * CHUNK:(c + 1) * CHUNK], kt[c * CHUNK:(c + 1) * CHUNK], _TN,
                                      preferred_element_type=f32) for c in range(res['n_chk'])]

    def carry():
        res['attn'] = [jnp.where(causal, a, 0.0).astype(bf) for a in res['attn']]
        st, st_in = state, []
        for upd, eb_last in zip(res.pop('upd'), res.pop('eb_last')):
            st_in.append(st.astype(bf))
            st = (st + upd) * eb_last
        res.update(st_in=st_in, state=st)

    def outputs():
        qh, vb = res.pop('qh'), res.pop('vb')
        res['intra'] = [jnp.dot(a, vb[r * blk:(r + 1) * blk], preferred_element_type=f32)
                        for r, a in enumerate(res.pop('attn'))]
        res['inter'] = [lax.dot_general(qh[c * CHUNK:(c + 1) * CHUNK], s, _NT, preferred_element_type=f32)
                        for c, s in enumerate(res.pop('st_in'))]

    def done():
        finish(jnp.concatenate(res.pop('intra'), axis=0) + jnp.concatenate(res.pop('inter'), axis=0))

    return [gates, decay, scores, carry, outputs, done], res


def _scan_tile_exact(qkg_ref, v_ref, st_ref, o_ref):
    bf = jnp.bfloat16
    f32 = jnp.float32
    n_heads, dv, dk = st_ref.shape
    tt = o_ref.shape[0]
    rowk = lax.broadcasted_iota(jnp.int32, (CHUNK, dk), 0)
    rowc = lax.broadcasted_iota(jnp.int32, (CHUNK, CHUNK), 0)
    colc = lax.broadcasted_iota(jnp.int32, (CHUNK, CHUNK), 1)
    tri = (colc <= rowc).astype(bf)

    for h in range(n_heads):
        ks = slice(h * dk, (h + 1) * dk)
        vs = slice(h * dv, (h + 1) * dv)

        def chunk(c, carry, ks=ks, vs=vs, h=h):
            r0 = pl.multiple_of(c * CHUNK, CHUNK)
            rows = pl.ds(r0, CHUNK)
            qc = qkg_ref[0, rows, ks]
            kc = qkg_ref[1, rows, ks]
            gc = qkg_ref[2, rows, ks]
            g_hi = gc.astype(bf)
            g_lo = (gc - g_hi.astype(f32)).astype(bf)
            b = (jnp.dot(tri, g_hi, preferred_element_type=f32)
                 + jnp.dot(tri, g_lo, preferred_element_type=f32))
            qkg_ref[2, rows, ks] = b

            def columns(jg, attn):
                j0 = pl.multiple_of(jg * SUBLANE, SUBLANE)
                b_rows = qkg_ref[2, pl.ds(r0 + j0, SUBLANE), ks]
                k_rows = qkg_ref[1, pl.ds(r0 + j0, SUBLANE), ks]
                for r in range(SUBLANE):
                    j = j0 + r
                    decay = jnp.exp2(jnp.where(rowk >= j, b - b_rows[r:r + 1], 0.0))
                    colv = jnp.sum(qc * k_rows[r:r + 1] * decay, axis=1, keepdims=True)
                    attn = jnp.where(colc == j, colv, attn)
                return attn

            attn = lax.fori_loop(0, CHUNK // SUBLANE, columns, jnp.zeros((CHUNK, CHUNK), f32))
            attn = jnp.where(colc <= rowc, attn, 0.0).astype(bf)
            vb = v_ref[rows, vs].astype(bf)
            st = st_ref[h]
            b_last = b[CHUNK - 1:CHUNK, :]
            o_ref[rows, vs] = (
                jnp.dot(attn, vb, preferred_element_type=f32)
                + lax.dot_general((qc * jnp.exp2(b)).astype(bf), st.astype(bf), _NT,
                                  preferred_element_type=f32))
            k_dec = (kc * jnp.exp2(b_last - b)).astype(bf)
            st_ref[h] = st * jnp.exp2(b_last) + lax.dot_general(vb, k_dec, _TN,
                                                                preferred_element_type=f32)
            return carry

        lax.fori_loop(0, tt // CHUNK, chunk, 0)


def _readout(o, z, n_heads, dk):
    dv = o.shape[1] // n_heads

    def head(oh):
        return oh * lax.rsqrt(jnp.mean(oh * oh, axis=-1, keepdims=True) + EPS * dk)

    on = jnp.concatenate([head(o[:, h * dv:(h + 1) * dv]) for h in range(n_heads)], axis=1)
    return on.astype(z.dtype) * _silu_of_half(z)


def _scan_and_readout(qkg, v_ref, z_ref, y_ref, st_ref, qkg_ref, o_ref, side_work=None):
    @pl.when(pl.program_id(1) == 0)
    def _():
        st_ref[...] = jnp.zeros_like(st_ref)

    if side_work is not None:
        side_work()
    n_heads, dv, dk = st_ref.shape
    heads = []
    for h in range(n_heads):
        ks = slice(h * dk, (h + 1) * dk)
        vs = slice(h * dv, (h + 1) * dv)

        def finish(o, vs=vs):
            y_ref[:, vs] = _readout(o, z_ref[:, vs], 1, dk).astype(y_ref.dtype)

        heads.append(_scan_stages(lambda ks=ks: qkg(ks), lambda vs=vs: v_ref[:, vs], st_ref[h], finish))
    n_stages = len(heads[0][0])
    for tick in range(n_heads + n_stages - 1):
        for h in range(n_heads):
            if 0 <= tick - h < n_stages:
                heads[h][0][tick - h]()
    st_new = [res['state'] for _, res in heads]
    b_min = heads[0][1]['b_min']
    for _, res in heads[1:]:
        b_min = jnp.minimum(b_min, res['b_min'])
    in_range = b_min >= -DECAY_RANGE_LIMIT

    @pl.when(in_range)
    def _():
        for h in range(n_heads):
            st_ref[h] = st_new[h]

    @pl.when(jnp.logical_not(in_range))
    def _():
        q, k, g = qkg(slice(None))
        qkg_ref[0] = q.astype(jnp.float32)
        qkg_ref[1] = k.astype(jnp.float32)
        qkg_ref[2] = g
        _scan_tile_exact(qkg_ref, v_ref, st_ref, o_ref)
        y_ref[...] = _readout(o_ref[...], z_ref[...], n_heads, dk).astype(y_ref.dtype)


def _hgrn_kernel(lb_ref, hq_ref, hf_ref, hi_ref, hz_ref, xb_ref, rs_ref, wlr_ref, nw_ref,
                 wo_ref, hg_ref, pgw_ref, pp_ref,
                 y_ref, lr_ref, wo_bf_ref, pgw_bf_ref, pp_bf_ref, st_ref, qkg_ref, o_ref):
    def side_work():
        wlr = (wlr_ref[...] * nw_ref[...]).astype(jnp.bfloat16)
        lr_ref[...] = lax.dot_general(xb_ref[...], wlr, _NT,
                                      preferred_element_type=jnp.float32) * rs_ref[:, 0:1]
        wo_bf_ref[...] = (wo_ref[...] * hg_ref[:, 0:1]).astype(jnp.bfloat16)
        pgw_bf_ref[...] = pgw_ref[...].astype(jnp.bfloat16)
        pp_bf_ref[...] = pp_ref[...].astype(jnp.bfloat16)

    def qkg(cols):
        lbp = lb_ref[:, cols]
        e = jnp.exp(lbp - jnp.max(lbp, axis=0, keepdims=True))
        lb = e[0:1, :] / jnp.sum(e, axis=0, keepdims=True)
        f = (0.5 + 0.5 * lb) + (0.5 - 0.5 * lb) * jnp.tanh(hf_ref[:, cols])
        return _silu_of_half(hq_ref[:, cols]), 1.0 - f, jnp.log2(f)

    _scan_and_readout(qkg, hi_ref, hz_ref, y_ref, st_ref, qkg_ref, o_ref, side_work)


def _gla_kernel(gq_ref, gk_ref, gv_ref, gz_ref, lr_ref, w2_ref, gb_ref, y_ref, st_ref,
                qkg_ref, o_ref):
    def qkg(cols):
        logit = jnp.dot(lr_ref[...].astype(jnp.bfloat16), w2_ref[:, cols].astype(jnp.bfloat16),
                        preferred_element_type=jnp.float32) + gb_ref[:, cols]
        log2_sig = (jnp.minimum(logit, 0.0) * LOG2E
                    - jnp.log2(1.0 + jnp.exp2(jnp.abs(logit) * -LOG2E)))
        return gq_ref[:, cols], gk_ref[:, cols], log2_sig / GLA_GATE_TAU

    _scan_and_readout(qkg, gv_ref, gz_ref, y_ref, st_ref, qkg_ref, o_ref)


def _scan_params():
    return pltpu.CompilerParams(dimension_semantics=("parallel", "arbitrary"),
                                vmem_limit_bytes=VMEM_LIMIT)


def _hgrn(hf, pb, hgrn_lb, xb, rs, wlr_t, norm_w, w_out, head_gain, pgw, pp, tt=512):
    T, D = xb.shape
    n = T // tt
    w = D_HGRN
    blk = lambda sec: pl.BlockSpec((tt, w), lambda h, t: (t, sec))
    rows = lambda width: pl.BlockSpec((tt, width), lambda h, t: (t, 0))
    const = lambda a: pl.BlockSpec(a.shape, lambda h, t: (0, 0))
    piece = lambda a: pl.BlockSpec((a.shape[0] // n, a.shape[1]), lambda h, t: (t, 0))
    bf16_like = lambda a: jax.ShapeDtypeStruct(a.shape, jnp.bfloat16)
    assert all(a.shape[0] % (16 * n) == 0 for a in (w_out, pgw, pp))
    return pl.pallas_call(
        _hgrn_kernel,
        grid=(1, n),
        in_specs=[const(hgrn_lb), blk(0), blk(0), blk(1), blk(2),
                  rows(D), rows(LANE), const(wlr_t), const(norm_w),
                  piece(w_out), piece(head_gain), piece(pgw), piece(pp)],
        out_specs=[pl.BlockSpec((tt, w), lambda h, t: (t, 0)), rows(LANE),
                   piece(w_out), piece(pgw), piece(pp)],
        out_shape=[jax.ShapeDtypeStruct((T, D_HGRN), jnp.bfloat16),
                   jax.ShapeDtypeStruct((T, LANE), jnp.float32),
                   bf16_like(w_out), bf16_like(pgw), bf16_like(pp)],
        scratch_shapes=[pltpu.VMEM((HGRN_HEADS, HGRN_DV, HGRN_DK), jnp.float32),
                        pltpu.VMEM((3, tt, w), jnp.float32),
                        pltpu.VMEM((tt, w), jnp.float32)],
        compiler_params=_scan_params(),
        name="hgrn_scan",
    )(hgrn_lb, pb, hf, pb, pb, xb, rs, wlr_t, norm_w, w_out, head_gain, pgw, pp)


def _gla(pb, lr, w2_pad, gate_b, tt=512, hp=4):
    T = pb.shape[0]
    wk, wv = hp * GLA_DK, hp * GLA_DV
    q0 = 3 * PANEL // wk
    k0 = q0 + GLA_DK_TOTAL // wk
    v0 = 4 * PANEL // wv
    z0 = v0 + D_GLA // wv
    return pl.pallas_call(
        _gla_kernel,
        grid=(GLA_HEADS // hp, T // tt),
        in_specs=[
            pl.BlockSpec((tt, wk), lambda h, t: (t, q0 + h)),
            pl.BlockSpec((tt, wk), lambda h, t: (t, k0 + h)),
            pl.BlockSpec((tt, wv), lambda h, t: (t, v0 + h)),
            pl.BlockSpec((tt, wv), lambda h, t: (t, z0 + h)),
            pl.BlockSpec((tt, LANE), lambda h, t: (t, 0)),
            pl.BlockSpec((LANE, wk), lambda h, t: (0, h)),
            pl.BlockSpec((1, wk), lambda h, t: (0, h)),
        ],
        out_specs=pl.BlockSpec((tt, wv), lambda h, t: (t, h)),
        out_shape=jax.ShapeDtypeStruct((T, D_GLA), jnp.bfloat16),
        scratch_shapes=[pltpu.VMEM((hp, GLA_DV, GLA_DK), jnp.float32),
                        pltpu.VMEM((3, tt, wk), jnp.float32),
                        pltpu.VMEM((tt, wv), jnp.float32)],
        compiler_params=_scan_params(),
        name="gla_scan",
    )(pb, pb, pb, pb, lr, w2_pad, gate_b)


OUT_SUB = 2
OUT_VMEM_LIMIT = 58 * 1024 * 1024


def _out_kernel(x_ref, yh_ref, yg_ref, p_ref, woh_ref, wog_ref, pp_ref, pnw_ref, gnw_ref,
                gw_ref, gb_ref, fnw_ref, o_ref):
    f32 = jnp.float32
    bf = jnp.bfloat16
    rows = o_ref.shape[0] // OUT_SUB
    subs = [slice(s * rows, (s + 1) * rows) for s in range(OUT_SUB)]
    h = [x_ref[r, :]
         + jnp.dot(yh_ref[r, :], woh_ref[...], preferred_element_type=f32)
         + jnp.dot(yg_ref[r, :], wog_ref[...], preferred_element_type=f32) for r in subs]
    e = [_rms(jnp.dot(p_ref[r, :].astype(bf), pp_ref[...], preferred_element_type=f32), pnw_ref[...])
         for r in subs]
    hn = [_rms(hs, gnw_ref[...]).astype(bf) for hs in h]
    acc = [jnp.dot(hs, gw_ref[...], preferred_element_type=f32) for hs in hn]
    for r, hs, es, a in zip(subs, h, e, acc):
        o_ref[r, :] = _rms(hs + _sigmoid(a + gb_ref[...]) * es, fnw_ref[...])


def _outproj(x2, yh, yg, p2, wo, pp, pnw, gnw, gw, gb, fnw, tm=512):
    T, D = x2.shape
    row = lambda w: pl.BlockSpec((tm, w), lambda i: (i, 0))
    once = pl.Buffered(1)
    full = lambda a: pl.BlockSpec(a.shape, lambda i: (0, 0), pipeline_mode=once)
    return pl.pallas_call(
        _out_kernel,
        grid=(T // tm,),
        in_specs=[row(D), row(D_HGRN), row(D_GLA), row(PLE_DIM),
                  pl.BlockSpec((D_HGRN, D), lambda i: (0, 0), pipeline_mode=once),
                  pl.BlockSpec((D_GLA, D), lambda i: (D_HGRN // D_GLA, 0), pipeline_mode=once),
                  full(pp), full(pnw), full(gnw), full(gw), full(gb), full(fnw)],
        out_specs=row(D),
        out_shape=jax.ShapeDtypeStruct((T, D), jnp.float32),
        compiler_params=pltpu.CompilerParams(
            dimension_semantics=("parallel",),
            vmem_limit_bytes=OUT_VMEM_LIMIT),
        name="outproj_ple",
    )(x2, yh, yg, p2, wo, wo, pp, pnw, gnw, gw, gb, fnw)


def kernel(x, p, norm_mix_w, w_in, hgrn_lb, gla_gate_w2, gla_gate_b, hgrn_norm_w, gla_norm_w,
           w_out, ple_proj, ple_norm_w, ple_gate_norm_w, ple_gate_w, ple_gate_b, final_norm_w):
    B, T, D = x.shape
    bf = jnp.bfloat16
    x2 = x.reshape(B * T, D)
    p2 = p[0].reshape(B * T, PLE_DIM)
    row = lambda a: a.reshape(1, -1)

    w_t = jnp.swapaxes(w_in[0], 0, 1)
    wlr_t = jnp.pad(w_t[D_MAIN:], ((0, LANE - GLA_GATE_RANK), (0, 0)))
    nw = row(norm_mix_w[0])
    hf, xb, rs = _inproj_first(x2, nw, w_t, lambda j: j + 1, 1, (0,), jnp.float32, "inproj_f")
    pb = _inproj_rest(xb, rs, nw, w_t, lambda j: j + jnp.minimum(j, 1), 6, (0, 2, 5), bf,
                      "inproj_b")

    head_gain = jnp.broadcast_to(jnp.concatenate([hgrn_norm_w[0], gla_norm_w[0]])[:, None],
                                 (D_HGRN + D_GLA, LANE))
    y_h, lr, wo_bf, pgw_bf, pp_bf = _hgrn(hf, pb, hgrn_lb, xb, rs, wlr_t, nw,
                                          w_out[0], head_gain, ple_gate_w[0], ple_proj[0])
    w2_pad = jnp.pad(gla_gate_w2[0], ((0, LANE - GLA_GATE_RANK), (0, 0)))
    y_g = _gla(pb, lr, w2_pad, row(gla_gate_b[0]))

    out = _outproj(x2, y_h, y_g, p2, wo_bf, pp_bf, row(ple_norm_w[0]), row(ple_gate_norm_w[0]), pgw_bf,
                   row(ple_gate_b[0]), row(final_norm_w))
    return out.reshape(B, T, D)
```

```python
import functools

import jax
import jax.numpy as jnp
from jax import lax
from jax.experimental import pallas as pl
from jax.experimental.pallas import tpu as pltpu

D_MODEL = 2048
D_HGRN = 1024
HGRN_HEADS = 8
HGRN_DK = 128
HGRN_DV = 128
GLA_HEADS = 4
GLA_DK = 128
GLA_DV = 256
GLA_DK_TOTAL = 512
D_GLA = 1024
GLA_GATE_RANK = 16
GLA_GATE_TAU = 16.0
PLE_DIM = 256
EPS = 1e-6

LANE = 128
SUBLANE = 8
D_MAIN = 4 * D_HGRN + 2 * GLA_DK_TOTAL + 2 * D_GLA
PANEL = 1024
CHUNK = 128
ATTN_BLOCK = 128
DECAY_RANGE_LIMIT = 110.0
LOG2E = 1.4426950408889634
VMEM_LIMIT = 48 * 1024 * 1024
INPROJ_VMEM_LIMIT = 62 * 1024 * 1024

_NT = (((1,), (1,)), ((), ()))
_TN = (((0,), (0,)), ((), ()))


def _sigmoid(x):
    return 0.5 + 0.5 * jnp.tanh(0.5 * x)


def _silu_of_half(h):
    return h + h * jnp.tanh(h)


def _rms(x, w):
    return x * lax.rsqrt(jnp.mean(x * x, axis=-1, keepdims=True) + EPS) * w


def _is_half_panel(half_panels):
    return functools.reduce(jnp.logical_or, [pl.program_id(0) == jp for jp in half_panels])


def _cast_weight_panel(w_ref, nw_ref, wb_ref):
    @pl.when(pl.program_id(1) == 0)
    def _():
        wb_ref[...] = (w_ref[...] * nw_ref[...]).astype(jnp.bfloat16)


def _inproj_first_kernel(x_ref, nw_ref, w_ref, o_ref, xb_ref, rs_ref, wb_ref, *, half_panels):
    bf = jnp.bfloat16
    _cast_weight_panel(w_ref, nw_ref, wb_ref)

    def body(first_panel):
        x = x_ref[...]
        r = lax.rsqrt(jnp.mean(x * x, axis=-1, keepdims=True) + EPS)
        xb = x.astype(bf)
        o_ref[...] = (lax.dot_general(xb, wb_ref[...], _NT, preferred_element_type=jnp.float32)
                      * (r * jnp.where(_is_half_panel(half_panels), 0.5, 1.0))).astype(o_ref.dtype)
        if first_panel:
            xb_ref[...] = xb
            rs_ref[...] = jnp.broadcast_to(r, rs_ref.shape)

    first = pl.program_id(0) == 0
    pl.when(first)(lambda: body(True))
    pl.when(jnp.logical_not(first))(lambda: body(False))


def _inproj_rest_kernel(xb_ref, rs_ref, nw_ref, w_ref, o_ref, wb_ref, *, half_panels):
    _cast_weight_panel(w_ref, nw_ref, wb_ref)
    r = rs_ref[:, 0:1] * jnp.where(_is_half_panel(half_panels), 0.5, 1.0)
    o_ref[...] = (lax.dot_general(xb_ref[...], wb_ref[...], _NT, preferred_element_type=jnp.float32)
                  * r).astype(o_ref.dtype)


def _inproj_params():
    return pltpu.CompilerParams(dimension_semantics=("parallel", "arbitrary"),
                                vmem_limit_bytes=INPROJ_VMEM_LIMIT)


def _first_panel_only(n_rows):
    return lambda j, i: (jnp.where(j == 0, i, n_rows - 1), 0)


def _inproj_first(x2, norm_w, w_t, panel_of, n_panels, half_panels, out_dtype, name, tm=1024):
    T, D = x2.shape
    n_rows = T // tm
    return pl.pallas_call(
        functools.partial(_inproj_first_kernel, half_panels=half_panels),
        grid=(n_panels, n_rows),
        in_specs=[pl.BlockSpec((tm, D), lambda j, i: (i, 0)),
                  pl.BlockSpec((1, D), lambda j, i: (0, 0)),
                  pl.BlockSpec((PANEL, D), lambda j, i: (panel_of(j), 0))],
        out_specs=[pl.BlockSpec((tm, PANEL), lambda j, i: (i, j)),
                   pl.BlockSpec((tm, D), _first_panel_only(n_rows)),
                   pl.BlockSpec((tm, LANE), _first_panel_only(n_rows))],
        out_shape=[jax.ShapeDtypeStruct((T, n_panels * PANEL), out_dtype),
                   jax.ShapeDtypeStruct((T, D), jnp.bfloat16),
                   jax.ShapeDtypeStruct((T, LANE), jnp.float32)],
        scratch_shapes=[pltpu.VMEM((PANEL, D), jnp.bfloat16)],
        compiler_params=_inproj_params(),
        name=name,
    )(x2, norm_w, w_t)


def _inproj_rest(xb, rs, norm_w, w_t, panel_of, n_panels, half_panels, out_dtype, name, tm=2048):
    T, D = xb.shape
    return pl.pallas_call(
        functools.partial(_inproj_rest_kernel, half_panels=half_panels),
        grid=(n_panels, T // tm),
        in_specs=[pl.BlockSpec((tm, D), lambda j, i: (i, 0)),
                  pl.BlockSpec((tm, LANE), lambda j, i: (i, 0)),
                  pl.BlockSpec((1, D), lambda j, i: (0, 0)),
                  pl.BlockSpec((PANEL, D), lambda j, i: (panel_of(j), 0))],
        out_specs=pl.BlockSpec((tm, PANEL), lambda j, i: (i, j)),
        out_shape=jax.ShapeDtypeStruct((T, n_panels * PANEL), out_dtype),
        scratch_shapes=[pltpu.VMEM((PANEL, D), jnp.bfloat16)],
        compiler_params=_inproj_params(),
        name=name,
    )(xb, rs, norm_w, w_t)


def _scan_stages(qkg, v, state, finish):
    bf = jnp.bfloat16
    f32 = jnp.float32
    dv, dk = state.shape
    blk = ATTN_BLOCK
    row = lax.broadcasted_iota(jnp.int32, (blk, blk), 0)
    col = lax.broadcasted_iota(jnp.int32, (blk, blk), 1)
    causal = (col <= row) & (col >= (row & -CHUNK))
    res = {}

    def gates():
        q, k, g = qkg()
        g_hi = g.astype(bf)
        g_lo = (g - g_hi.astype(f32)).astype(bf)
        tri = causal.astype(bf)
        tri2 = jnp.concatenate([tri, tri], axis=1)
        res.update(q=q, k=k, n_blk=g.shape[0] // blk, n_chk=g.shape[0] // CHUNK)
        res['b'] = [jnp.dot(tri2, jnp.concatenate([g_hi[r * blk:(r + 1) * blk],
                                                     g_lo[r * blk:(r + 1) * blk]], axis=0),
                            preferred_element_type=f32) for r in range(res['n_blk'])]

    def decay():
        b = jnp.concatenate(res.pop('b'), axis=0)
        eb = jnp.exp2(b)
        q, k = res.pop('q'), res.pop('k')
        res['qh'] = q * eb.astype(bf) if q.dtype == bf else (q * eb).astype(bf)
        res['kt'] = k * (1.0 / eb).astype(bf) if k.dtype == bf else (k / eb).astype(bf)
        res['vb'] = v().astype(bf)
        res['eb_last'] = [eb[(c + 1) * CHUNK - 1:(c + 1) * CHUNK, :] for c in range(res['n_chk'])]
        res['b_min'] = jnp.min(jnp.concatenate(
            [b[(c + 1) * CHUNK - 1:(c + 1) * CHUNK, :] for c in range(res['n_chk'])], axis=0))

    def scores():
        qh, kt, vb = res['qh'], res['kt'], res['vb']
        res['attn'] = [lax.dot_general(qh[r * blk:(r + 1) * blk], kt[r * blk:(r + 1) * blk], _NT,
                                       preferred_element_type=f32) for r in range(res['n_blk'])]
        res['upd'] = [lax.dot_general(vb[c * CHUNK:(c + 1) * CHUNK], kt[c * CHUNK:(c + 1) * CHUNK], _TN,
                                      preferred_element_type=f32) for c in range(res['n_chk'])]

    def carry():
        res['attn'] = [jnp.where(causal, a, 0.0).astype(bf) for a in res['attn']]
        st, st_in = state, []
        for upd, eb_last in zip(res.pop('upd'), res.pop('eb_last')):
            st_in.append(st.astype(bf))
            st = (st + upd) * eb_last
        res.update(st_in=st_in, state=st)

    def outputs():
        qh, vb = res.pop('qh'), res.pop('vb')
        res['intra'] = [jnp.dot(a, vb[r * blk:(r + 1) * blk], preferred_element_type=f32)
                        for r, a in enumerate(res.pop('attn'))]
        res['inter'] = [lax.dot_general(qh[c * CHUNK:(c + 1) * CHUNK], s, _NT, preferred_element_type=f32)
                        for c, s in enumerate(res.pop('st_in'))]

    def done():
        finish(jnp.concatenate(res.pop('intra'), axis=0) + jnp.concatenate(res.pop('inter'), axis=0))

    return [gates, decay, scores, carry, outputs, done], res


def _scan_tile_exact(qkg_ref, v_ref, st_ref, o_ref):
    bf = jnp.bfloat16
    f32 = jnp.float32
    n_heads, dv, dk = st_ref.shape
    tt = o_ref.shape[0]
    rowk = lax.broadcasted_iota(jnp.int32, (CHUNK, dk), 0)
    rowc = lax.broadcasted_iota(jnp.int32, (CHUNK, CHUNK), 0)
    colc = lax.broadcasted_iota(jnp.int32, (CHUNK, CHUNK), 1)
    tri = (colc <= rowc).astype(bf)

    for h in range(n_heads):
        ks = slice(h * dk, (h + 1) * dk)
        vs = slice(h * dv, (h + 1) * dv)

        def chunk(c, carry, ks=ks, vs=vs, h=h):
            r0 = pl.multiple_of(c * CHUNK, CHUNK)
            rows = pl.ds(r0, CHUNK)
            qc = qkg_ref[0, rows, ks]
            kc = qkg_ref[1, rows, ks]
            gc = qkg_ref[2, rows, ks]
            g_hi = gc.astype(bf)
            g_lo = (gc - g_hi.astype(f32)).astype(bf)
            b = (jnp.dot(tri, g_hi, preferred_element_type=f32)
                 + jnp.dot(tri, g_lo, preferred_element_type=f32))
            qkg_ref[2, rows, ks] = b

            def columns(jg, attn):
                j0 = pl.multiple_of(jg * SUBLANE, SUBLANE)
                b_rows = qkg_ref[2, pl.ds(r0 + j0, SUBLANE), ks]
                k_rows = qkg_ref[1, pl.ds(r0 + j0, SUBLANE), ks]
                for r in range(SUBLANE):
                    j = j0 + r
                    decay = jnp.exp2(jnp.where(rowk >= j, b - b_rows[r:r + 1], 0.0))
                    colv = jnp.sum(qc * k_rows[r:r + 1] * decay, axis=1, keepdims=True)
                    attn = jnp.where(colc == j, colv, attn)
                return attn

            attn = lax.fori_loop(0, CHUNK // SUBLANE, columns, jnp.zeros((CHUNK, CHUNK), f32))
            attn = jnp.where(colc <= rowc, attn, 0.0).astype(bf)
            vb = v_ref[rows, vs].astype(bf)
            st = st_ref[h]
            b_last = b[CHUNK - 1:CHUNK, :]
            o_ref[rows, vs] = (
                jnp.dot(attn, vb, preferred_element_type=f32)
                + lax.dot_general((qc * jnp.exp2(b)).astype(bf), st.astype(bf), _NT,
                                  preferred_element_type=f32))
            k_dec = (kc * jnp.exp2(b_last - b)).astype(bf)
            st_ref[h] = st * jnp.exp2(b_last) + lax.dot_general(vb, k_dec, _TN,
                                                                preferred_element_type=f32)
            return carry

        lax.fori_loop(0, tt // CHUNK, chunk, 0)


def _readout(o, z, n_heads, dk):
    dv = o.shape[1] // n_heads

    def head(oh):
        return oh * lax.rsqrt(jnp.mean(oh * oh, axis=-1, keepdims=True) + EPS * dk)

    on = jnp.concatenate([head(o[:, h * dv:(h + 1) * dv]) for h in range(n_heads)], axis=1)
    return on.astype(z.dtype) * _silu_of_half(z)


def _scan_and_readout(qkg, v_ref, z_ref, y_ref, st_ref, qkg_ref, o_ref, side_work=None):
    @pl.when(pl.program_id(1) == 0)
    def _():
        st_ref[...] = jnp.zeros_like(st_ref)

    if side_work is not None:
        side_work()
    n_heads, dv, dk = st_ref.shape
    heads = []
    for h in range(n_heads):
        ks = slice(h * dk, (h + 1) * dk)
        vs = slice(h * dv, (h + 1) * dv)

        def finish(o, vs=vs):
            y_ref[:, vs] = _readout(o, z_ref[:, vs], 1, dk).astype(y_ref.dtype)

        heads.append(_scan_stages(lambda ks=ks: qkg(ks), lambda vs=vs: v_ref[:, vs], st_ref[h], finish))
    n_stages = len(heads[0][0])
    for tick in range(n_heads + n_stages - 1):
        for h in range(n_heads):
            if 0 <= tick - h < n_stages:
                heads[h][0][tick - h]()
    st_new = [res['state'] for _, res in heads]
    b_min = heads[0][1]['b_min']
    for _, res in heads[1:]:
        b_min = jnp.minimum(b_min, res['b_min'])
    in_range = b_min >= -DECAY_RANGE_LIMIT

    @pl.when(in_range)
    def _():
        for h in range(n_heads):
            st_ref[h] = st_new[h]

    @pl.when(jnp.logical_not(in_range))
    def _():
        q, k, g = qkg(slice(None))
        qkg_ref[0] = q.astype(jnp.float32)
        qkg_ref[1] = k.astype(jnp.float32)
        qkg_ref[2] = g
        _scan_tile_exact(qkg_ref, v_ref, st_ref, o_ref)
        y_ref[...] = _readout(o_ref[...], z_ref[...], n_heads, dk).astype(y_ref.dtype)


def _hgrn_kernel(lb_ref, hq_ref, hf_ref, hi_ref, hz_ref, xb_ref, rs_ref, wlr_ref, nw_ref,
                 wo_ref, hg_ref, y_ref, lr_ref, wo_bf_ref, st_ref, qkg_ref, o_ref):
    def side_work():
        wlr = (wlr_ref[...] * nw_ref[...]).astype(jnp.bfloat16)
        lr_ref[...] = lax.dot_general(xb_ref[...], wlr, _NT,
                                      preferred_element_type=jnp.float32) * rs_ref[:, 0:1]
        wo_bf_ref[...] = (wo_ref[...] * hg_ref[:, 0:1]).astype(jnp.bfloat16)

    def qkg(cols):
        lbp = lb_ref[:, cols]
        e = jnp.exp(lbp - jnp.max(lbp, axis=0, keepdims=True))
        lb = e[0:1, :] / jnp.sum(e, axis=0, keepdims=True)
        f = (0.5 + 0.5 * lb) + (0.5 - 0.5 * lb) * jnp.tanh(hf_ref[:, cols])
        return _silu_of_half(hq_ref[:, cols]), 1.0 - f, jnp.log2(f)

    _scan_and_readout(qkg, hi_ref, hz_ref, y_ref, st_ref, qkg_ref, o_ref, side_work)


def _gla_kernel(gq_ref, gk_ref, gv_ref, gz_ref, lr_ref, w2_ref, gb_ref, pgw_ref, pp_ref,
                y_ref, pgw_bf_ref, pp_bf_ref, st_ref, qkg_ref, o_ref):
    def side_work():
        pgw_bf_ref[...] = pgw_ref[...].astype(jnp.bfloat16)
        pp_bf_ref[...] = pp_ref[...].astype(jnp.bfloat16)

    def qkg(cols):
        logit = jnp.dot(lr_ref[...].astype(jnp.bfloat16), w2_ref[:, cols].astype(jnp.bfloat16),
                        preferred_element_type=jnp.float32) + gb_ref[:, cols]
        log2_sig = (jnp.minimum(logit, 0.0) * LOG2E
                    - jnp.log2(1.0 + jnp.exp2(jnp.abs(logit) * -LOG2E)))
        return gq_ref[:, cols], gk_ref[:, cols], log2_sig / GLA_GATE_TAU

    _scan_and_readout(qkg, gv_ref, gz_ref, y_ref, st_ref, qkg_ref, o_ref, side_work)


def _row_slices(n):
    def spec(a):
        assert a.shape[0] % (16 * n) == 0
        return pl.BlockSpec((a.shape[0] // n, a.shape[1]), lambda h, t: (t, 0))
    return spec


def _scan_params():
    return pltpu.CompilerParams(dimension_semantics=("parallel", "arbitrary"),
                                vmem_limit_bytes=VMEM_LIMIT)


def _hgrn(hf, pb, hgrn_lb, xb, rs, wlr_t, norm_w, w_out, head_gain, tt=512):
    T, D = xb.shape
    n = T // tt
    w = D_HGRN
    blk = lambda sec: pl.BlockSpec((tt, w), lambda h, t: (t, sec))
    rows = lambda width: pl.BlockSpec((tt, width), lambda h, t: (t, 0))
    const = lambda a: pl.BlockSpec(a.shape, lambda h, t: (0, 0))
    piece = _row_slices(n)
    return pl.pallas_call(
        _hgrn_kernel,
        grid=(1, n),
        in_specs=[const(hgrn_lb), blk(0), blk(0), blk(1), blk(2),
                  rows(D), rows(LANE), const(wlr_t), const(norm_w),
                  piece(w_out), piece(head_gain)],
        out_specs=[pl.BlockSpec((tt, w), lambda h, t: (t, 0)), rows(LANE), piece(w_out)],
        out_shape=[jax.ShapeDtypeStruct((T, D_HGRN), jnp.bfloat16),
                   jax.ShapeDtypeStruct((T, LANE), jnp.float32),
                   jax.ShapeDtypeStruct(w_out.shape, jnp.bfloat16)],
        scratch_shapes=[pltpu.VMEM((HGRN_HEADS, HGRN_DV, HGRN_DK), jnp.float32),
                        pltpu.VMEM((3, tt, w), jnp.float32),
                        pltpu.VMEM((tt, w), jnp.float32)],
        compiler_params=_scan_params(),
        name="hgrn_scan",
    )(hgrn_lb, pb, hf, pb, pb, xb, rs, wlr_t, norm_w, w_out, head_gain)


def _gla(pb, lr, w2_pad, gate_b, pgw, pp, tt=512):
    T = pb.shape[0]
    hp = GLA_HEADS
    piece = _row_slices(T // tt)
    wk, wv = hp * GLA_DK, hp * GLA_DV
    q0 = 3 * PANEL // wk
    k0 = q0 + GLA_DK_TOTAL // wk
    v0 = 4 * PANEL // wv
    z0 = v0 + D_GLA // wv
    return pl.pallas_call(
        _gla_kernel,
        grid=(GLA_HEADS // hp, T // tt),
        in_specs=[
            pl.BlockSpec((tt, wk), lambda h, t: (t, q0 + h)),
            pl.BlockSpec((tt, wk), lambda h, t: (t, k0 + h)),
            pl.BlockSpec((tt, wv), lambda h, t: (t, v0 + h)),
            pl.BlockSpec((tt, wv), lambda h, t: (t, z0 + h)),
            pl.BlockSpec((tt, LANE), lambda h, t: (t, 0)),
            pl.BlockSpec((LANE, wk), lambda h, t: (0, h)),
            pl.BlockSpec((1, wk), lambda h, t: (0, h)),
            piece(pgw), piece(pp),
        ],
        out_specs=[pl.BlockSpec((tt, wv), lambda h, t: (t, h)), piece(pgw), piece(pp)],
        out_shape=[jax.ShapeDtypeStruct((T, D_GLA), jnp.bfloat16),
                   jax.ShapeDtypeStruct(pgw.shape, jnp.bfloat16),
                   jax.ShapeDtypeStruct(pp.shape, jnp.bfloat16)],
        scratch_shapes=[pltpu.VMEM((hp, GLA_DV, GLA_DK), jnp.float32),
                        pltpu.VMEM((3, tt, wk), jnp.float32),
                        pltpu.VMEM((tt, wv), jnp.float32)],
        compiler_params=_scan_params(),
        name="gla_scan",
    )(pb, pb, pb, pb, lr, w2_pad, gate_b, pgw, pp)


OUT_SUB = 2
OUT_VMEM_LIMIT = 58 * 1024 * 1024


def _out_kernel(x_ref, yh_ref, yg_ref, p_ref, woh_ref, wog_ref, pp_ref, pnw_ref, gnw_ref,
                gw_ref, gb_ref, fnw_ref, o_ref):
    f32 = jnp.float32
    bf = jnp.bfloat16
    rows = o_ref.shape[0] // OUT_SUB
    subs = [slice(s * rows, (s + 1) * rows) for s in range(OUT_SUB)]
    h = [x_ref[r, :]
         + jnp.dot(yh_ref[r, :], woh_ref[...], preferred_element_type=f32)
         + jnp.dot(yg_ref[r, :], wog_ref[...], preferred_element_type=f32) for r in subs]
    e = [_rms(jnp.dot(p_ref[r, :].astype(bf), pp_ref[...], preferred_element_type=f32), pnw_ref[...])
         for r in subs]
    hn = [_rms(hs, gnw_ref[...]).astype(bf) for hs in h]
    acc = [jnp.dot(hs, gw_ref[...], preferred_element_type=f32) for hs in hn]
    for r, hs, es, a in zip(subs, h, e, acc):
        o_ref[r, :] = _rms(hs + _sigmoid(a + gb_ref[...]) * es, fnw_ref[...])


def _outproj(x2, yh, yg, p2, wo, pp, pnw, gnw, gw, gb, fnw, tm=512):
    T, D = x2.shape
    row = lambda w: pl.BlockSpec((tm, w), lambda i: (i, 0))
    once = pl.Buffered(1)
    full = lambda a: pl.BlockSpec(a.shape, lambda i: (0, 0), pipeline_mode=once)
    return pl.pallas_call(
        _out_kernel,
        grid=(T // tm,),
        in_specs=[row(D), row(D_HGRN), row(D_GLA), row(PLE_DIM),
                  pl.BlockSpec((D_HGRN, D), lambda i: (0, 0), pipeline_mode=once),
                  pl.BlockSpec((D_GLA, D), lambda i: (D_HGRN // D_GLA, 0), pipeline_mode=once),
                  full(pp), full(pnw), full(gnw), full(gw), full(gb), full(fnw)],
        out_specs=row(D),
        out_shape=jax.ShapeDtypeStruct((T, D), jnp.float32),
        compiler_params=pltpu.CompilerParams(
            dimension_semantics=("parallel",),
            vmem_limit_bytes=OUT_VMEM_LIMIT),
        name="outproj_ple",
    )(x2, yh, yg, p2, wo, wo, pp, pnw, gnw, gw, gb, fnw)


def kernel(x, p, norm_mix_w, w_in, hgrn_lb, gla_gate_w2, gla_gate_b, hgrn_norm_w, gla_norm_w,
           w_out, ple_proj, ple_norm_w, ple_gate_norm_w, ple_gate_w, ple_gate_b, final_norm_w):
    B, T, D = x.shape
    bf = jnp.bfloat16
    x2 = x.reshape(B * T, D)
    p2 = p[0].reshape(B * T, PLE_DIM)
    row = lambda a: a.reshape(1, -1)

    w_t = jnp.swapaxes(w_in[0], 0, 1)
    wlr_t = jnp.pad(w_t[D_MAIN:], ((0, LANE - GLA_GATE_RANK), (0, 0)))
    nw = row(norm_mix_w[0])
    hf, xb, rs = _inproj_first(x2, nw, w_t, lambda j: j + 1, 1, (0,), jnp.float32, "inproj_f")
    pb = _inproj_rest(xb, rs, nw, w_t, lambda j: j + jnp.minimum(j, 1), 6, (0, 2, 5), bf,
                      "inproj_b")

    head_gain = jnp.broadcast_to(jnp.concatenate([hgrn_norm_w[0], gla_norm_w[0]])[:, None],
                                 (D_HGRN + D_GLA, LANE))
    y_h, lr, wo_bf = _hgrn(hf, pb, hgrn_lb, xb, rs, wlr_t, nw, w_out[0], head_gain)
    w2_pad = jnp.pad(gla_gate_w2[0], ((0, LANE - GLA_GATE_RANK), (0, 0)))
    y_g, pgw_bf, pp_bf = _gla(pb, lr, w2_pad, row(gla_gate_b[0]), ple_gate_w[0], ple_proj[0])

    out = _outproj(x2, y_h, y_g, p2, wo_bf, pp_bf, row(ple_norm_w[0]), row(ple_gate_norm_w[0]), pgw_bf,
                   row(ple_gate_b[0]), row(final_norm_w))
    return out.reshape(B, T, D)
```

```python
import functools

import jax
import jax.numpy as jnp
from jax import lax
from jax.experimental import pallas as pl
from jax.experimental.pallas import tpu as pltpu

D_MODEL = 2048
D_HGRN = 1024
HGRN_HEADS = 8
HGRN_DK = 128
HGRN_DV = 128
GLA_HEADS = 4
GLA_DK = 128
GLA_DV = 256
GLA_DK_TOTAL = 512
D_GLA = 1024
GLA_GATE_RANK = 16
GLA_GATE_TAU = 16.0
PLE_DIM = 256
EPS = 1e-6

LANE = 128
SUBLANE = 8
D_MAIN = 4 * D_HGRN + 2 * GLA_DK_TOTAL + 2 * D_GLA
PANEL = 1024
CHUNK = 128
ATTN_BLOCK = 128
DECAY_RANGE_LIMIT = 110.0
LOG2E = 1.4426950408889634
VMEM_LIMIT = 48 * 1024 * 1024
INPROJ_VMEM_LIMIT = 62 * 1024 * 1024

_NT = (((1,), (1,)), ((), ()))
_TN = (((0,), (0,)), ((), ()))


def _sigmoid(x):
    return 0.5 + 0.5 * jnp.tanh(0.5 * x)


def _silu_of_half(h):
    return h + h * jnp.tanh(h)


def _rms(x, w):
    return x * lax.rsqrt(jnp.mean(x * x, axis=-1, keepdims=True) + EPS) * w


def _is_half_panel(half_panels):
    return functools.reduce(jnp.logical_or, [pl.program_id(0) == jp for jp in half_panels])


def _cast_weight_panel(w_ref, nw_ref, wb_ref):
    @pl.when(pl.program_id(1) == 0)
    def _():
        wb_ref[...] = (w_ref[...] * nw_ref[...]).astype(jnp.bfloat16)


def _inproj_first_kernel(x_ref, nw_ref, w_ref, o_ref, xb_ref, rs_ref, wb_ref, *, half_panels):
    bf = jnp.bfloat16
    _cast_weight_panel(w_ref, nw_ref, wb_ref)

    def body(first_panel):
        x = x_ref[...]
        r = lax.rsqrt(jnp.mean(x * x, axis=-1, keepdims=True) + EPS)
        xb = x.astype(bf)
        o_ref[...] = (lax.dot_general(xb, wb_ref[...], _NT, preferred_element_type=jnp.float32)
                      * (r * jnp.where(_is_half_panel(half_panels), 0.5, 1.0))).astype(o_ref.dtype)
        if first_panel:
            xb_ref[...] = xb
            rs_ref[...] = jnp.broadcast_to(r, rs_ref.shape)

    first = pl.program_id(0) == 0
    pl.when(first)(lambda: body(True))
    pl.when(jnp.logical_not(first))(lambda: body(False))


def _inproj_rest_kernel(xb_ref, rs_ref, nw_ref, w_ref, o_ref, wb_ref, *, half_panels):
    _cast_weight_panel(w_ref, nw_ref, wb_ref)
    r = rs_ref[:, 0:1] * jnp.where(_is_half_panel(half_panels), 0.5, 1.0)
    o_ref[...] = (lax.dot_general(xb_ref[...], wb_ref[...], _NT, preferred_element_type=jnp.float32)
                  * r).astype(o_ref.dtype)


def _inproj_params():
    return pltpu.CompilerParams(dimension_semantics=("parallel", "arbitrary"),
                                vmem_limit_bytes=INPROJ_VMEM_LIMIT)


def _first_panel_only(n_rows):
    return lambda j, i: (jnp.where(j == 0, i, n_rows - 1), 0)


def _inproj_first(x2, norm_w, w_t, panel_of, n_panels, half_panels, out_dtype, name, tm=1024):
    T, D = x2.shape
    n_rows = T // tm
    return pl.pallas_call(
        functools.partial(_inproj_first_kernel, half_panels=half_panels),
        grid=(n_panels, n_rows),
        in_specs=[pl.BlockSpec((tm, D), lambda j, i: (i, 0)),
                  pl.BlockSpec((1, D), lambda j, i: (0, 0)),
                  pl.BlockSpec((PANEL, D), lambda j, i: (panel_of(j), 0))],
        out_specs=[pl.BlockSpec((tm, PANEL), lambda j, i: (i, j)),
                   pl.BlockSpec((tm, D), _first_panel_only(n_rows)),
                   pl.BlockSpec((tm, LANE), _first_panel_only(n_rows))],
        out_shape=[jax.ShapeDtypeStruct((T, n_panels * PANEL), out_dtype),
                   jax.ShapeDtypeStruct((T, D), jnp.bfloat16),
                   jax.ShapeDtypeStruct((T, LANE), jnp.float32)],
        scratch_shapes=[pltpu.VMEM((PANEL, D), jnp.bfloat16)],
        compiler_params=_inproj_params(),
        name=name,
    )(x2, norm_w, w_t)


def _inproj_rest(xb, rs, norm_w, w_t, panel_of, n_panels, half_panels, out_dtype, name, tm=2048):
    T, D = xb.shape
    return pl.pallas_call(
        functools.partial(_inproj_rest_kernel, half_panels=half_panels),
        grid=(n_panels, T // tm),
        in_specs=[pl.BlockSpec((tm, D), lambda j, i: (i, 0)),
                  pl.BlockSpec((tm, LANE), lambda j, i: (i, 0)),
                  pl.BlockSpec((1, D), lambda j, i: (0, 0)),
                  pl.BlockSpec((PANEL, D), lambda j, i: (panel_of(j), 0))],
        out_specs=pl.BlockSpec((tm, PANEL), lambda j, i: (i, j)),
        out_shape=jax.ShapeDtypeStruct((T, n_panels * PANEL), out_dtype),
        scratch_shapes=[pltpu.VMEM((PANEL, D), jnp.bfloat16)],
        compiler_params=_inproj_params(),
        name=name,
    )(xb, rs, norm_w, w_t)


def _scan_stages(qkg, v, state, finish):
    bf = jnp.bfloat16
    f32 = jnp.float32
    dv, dk = state.shape
    blk = ATTN_BLOCK
    row = lax.broadcasted_iota(jnp.int32, (blk, blk), 0)
    col = lax.broadcasted_iota(jnp.int32, (blk, blk), 1)
    causal = (col <= row) & (col >= (row & -CHUNK))
    res = {}

    def gates():
        q, k, g = qkg()
        g_hi = g.astype(bf)
        g_lo = (g - g_hi.astype(f32)).astype(bf)
        tri = causal.astype(bf)
        tri2 = jnp.concatenate([tri, tri], axis=1)
        res.update(q=q, k=k, n_blk=g.shape[0] // blk, n_chk=g.shape[0] // CHUNK)
        res['b'] = [jnp.dot(tri2, jnp.concatenate([g_hi[r * blk:(r + 1) * blk],
                                                     g_lo[r * blk:(r + 1) * blk]], axis=0),
                            preferred_element_type=f32) for r in range(res['n_blk'])]

    def decay():
        b = jnp.concatenate(res.pop('b'), axis=0)
        eb = jnp.exp2(b)
        q, k = res.pop('q'), res.pop('k')
        res['qh'] = q * eb.astype(bf) if q.dtype == bf else (q * eb).astype(bf)
        res['kt'] = k * (1.0 / eb).astype(bf) if k.dtype == bf else (k / eb).astype(bf)
        res['vb'] = v().astype(bf)
        res['eb_last'] = [eb[(c + 1) * CHUNK - 1:(c + 1) * CHUNK, :] for c in range(res['n_chk'])]
        res['b_min'] = jnp.min(jnp.concatenate(
            [b[(c + 1) * CHUNK - 1:(c + 1) * CHUNK, :] for c in range(res['n_chk'])], axis=0))

    def scores():
        qh, kt, vb = res['qh'], res['kt'], res['vb']
        res['attn'] = [lax.dot_general(qh[r * blk:(r + 1) * blk], kt[r * blk:(r + 1) * blk], _NT,
                                       preferred_element_type=f32) for r in range(res['n_blk'])]
        res['upd'] = [lax.dot_general(vb[c * CHUNK:(c + 1) * CHUNK], kt[c * CHUNK:(c + 1) * CHUNK], _TN,
                                      preferred_element_type=f32) for c in range(res['n_chk'])]

    def carry():
        res['attn'] = [jnp.where(causal, a, 0.0).astype(bf) for a in res['attn']]
        st, st_in = state, []
        for upd, eb_last in zip(res.pop('upd'), res.pop('eb_last')):
            st_in.append(st.astype(bf))
            st = (st + upd) * eb_last
        res.update(st_in=st_in, state=st)

    def outputs():
        qh, vb = res.pop('qh'), res.pop('vb')
        res['intra'] = [jnp.dot(a, vb[r * blk:(r + 1) * blk], preferred_element_type=f32)
                        for r, a in enumerate(res.pop('attn'))]
        res['inter'] = [lax.dot_general(qh[c * CHUNK:(c + 1) * CHUNK], s, _NT, preferred_element_type=f32)
                        for c, s in enumerate(res.pop('st_in'))]

    def done():
        finish(jnp.concatenate(res.pop('intra'), axis=0) + jnp.concatenate(res.pop('inter'), axis=0))

    return [gates, decay, scores, carry, outputs, done], res


def _scan_tile_exact(qkg_ref, v_ref, st_ref, o_ref):
    bf = jnp.bfloat16
    f32 = jnp.float32
    n_heads, dv, dk = st_ref.shape
    tt = o_ref.shape[0]
    rowk = lax.broadcasted_iota(jnp.int32, (CHUNK, dk), 0)
    rowc = lax.broadcasted_iota(jnp.int32, (CHUNK, CHUNK), 0)
    colc = lax.broadcasted_iota(jnp.int32, (CHUNK, CHUNK), 1)
    tri = (colc <= rowc).astype(bf)

    for h in range(n_heads):
        ks = slice(h * dk, (h + 1) * dk)
        vs = slice(h * dv, (h + 1) * dv)

        def chunk(c, carry, ks=ks, vs=vs, h=h):
            r0 = pl.multiple_of(c * CHUNK, CHUNK)
            rows = pl.ds(r0, CHUNK)
            qc = qkg_ref[0, rows, ks]
            kc = qkg_ref[1, rows, ks]
            gc = qkg_ref[2, rows, ks]
            g_hi = gc.astype(bf)
            g_lo = (gc - g_hi.astype(f32)).astype(bf)
            b = (jnp.dot(tri, g_hi, preferred_element_type=f32)
                 + jnp.dot(tri, g_lo, preferred_element_type=f32))
            qkg_ref[2, rows, ks] = b

            def columns(jg, attn):
                j0 = pl.multiple_of(jg * SUBLANE, SUBLANE)
                b_rows = qkg_ref[2, pl.ds(r0 + j0, SUBLANE), ks]
                k_rows = qkg_ref[1, pl.ds(r0 + j0, SUBLANE), ks]
                for r in range(SUBLANE):
                    j = j0 + r
                    decay = jnp.exp2(jnp.where(rowk >= j, b - b_rows[r:r + 1], 0.0))
                    colv = jnp.sum(qc * k_rows[r:r + 1] * decay, axis=1, keepdims=True)
                    attn = jnp.where(colc == j, colv, attn)
                return attn

            attn = lax.fori_loop(0, CHUNK // SUBLANE, columns, jnp.zeros((CHUNK, CHUNK), f32))
            attn = jnp.where(colc <= rowc, attn, 0.0).astype(bf)
            vb = v_ref[rows, vs].astype(bf)
            st = st_ref[h]
            b_last = b[CHUNK - 1:CHUNK, :]
            o_ref[rows, vs] = (
                jnp.dot(attn, vb, preferred_element_type=f32)
                + lax.dot_general((qc * jnp.exp2(b)).astype(bf), st.astype(bf), _NT,
                                  preferred_element_type=f32))
            k_dec = (kc * jnp.exp2(b_last - b)).astype(bf)
            st_ref[h] = st * jnp.exp2(b_last) + lax.dot_general(vb, k_dec, _TN,
                                                                preferred_element_type=f32)
            return carry

        lax.fori_loop(0, tt // CHUNK, chunk, 0)


def _readout(o, z, n_heads, dk):
    dv = o.shape[1] // n_heads

    def head(oh):
        return oh * lax.rsqrt(jnp.mean(oh * oh, axis=-1, keepdims=True) + EPS * dk)

    on = jnp.concatenate([head(o[:, h * dv:(h + 1) * dv]) for h in range(n_heads)], axis=1)
    return on.astype(z.dtype) * _silu_of_half(z)


def _scan_and_readout(groups, qkg_ref, o_ref, side_work=None):
    @pl.when(pl.program_id(1) == 0)
    def _():
        for *_, st_ref in groups:
            st_ref[...] = jnp.zeros_like(st_ref)

    if side_work is not None:
        side_work()
    units = []
    for qkg, v_ref, z_ref, y_ref, st_ref in groups:
        n_heads, dv, dk = st_ref.shape
        for h in range(n_heads):
            ks = slice(h * dk, (h + 1) * dk)
            vs = slice(h * dv, (h + 1) * dv)

            def finish(o, vs=vs, y_ref=y_ref, z_ref=z_ref, dk=dk):
                y_ref[:, vs] = _readout(o, z_ref[:, vs], 1, dk).astype(y_ref.dtype)

            units.append(_scan_stages(lambda qkg=qkg, ks=ks: qkg(ks), lambda v_ref=v_ref, vs=vs: v_ref[:, vs],
                                      st_ref[h], finish))
    n_stages = len(units[0][0])
    for tick in range(len(units) + n_stages - 1):
        for u in range(len(units)):
            if 0 <= tick - u < n_stages:
                units[u][0][tick - u]()
    b_min = units[0][1]['b_min']
    for _, res in units[1:]:
        b_min = jnp.minimum(b_min, res['b_min'])
    in_range = b_min >= -DECAY_RANGE_LIMIT

    @pl.when(in_range)
    def _():
        new_states = iter(res['state'] for _, res in units)
        for *_, st_ref in groups:
            for h in range(st_ref.shape[0]):
                st_ref[h] = next(new_states)

    @pl.when(jnp.logical_not(in_range))
    def _():
        for qkg, v_ref, z_ref, y_ref, st_ref in groups:
            n_heads, dv, dk = st_ref.shape
            q, k, g = qkg(slice(0, n_heads * dk))
            qkg_ref[0, :, :n_heads * dk] = q.astype(jnp.float32)
            qkg_ref[1, :, :n_heads * dk] = k.astype(jnp.float32)
            qkg_ref[2, :, :n_heads * dk] = g
            _scan_tile_exact(qkg_ref, v_ref, st_ref, o_ref)
            y_ref[...] = _readout(o_ref[:, :n_heads * dv], z_ref[...], n_heads, dk).astype(y_ref.dtype)


def _scan_kernel(lb_ref, hq_ref, hf_ref, hi_ref, hz_ref, gq_ref, gk_ref, gv_ref, gz_ref, w2_ref, gb_ref,
                 xb_ref, rs_ref, wlr_ref, nw_ref, wo_ref, hg_ref, pgw_ref, pp_ref,
                 yh_ref, yg_ref, wo_bf_ref, pgw_bf_ref, pp_bf_ref, sth_ref, stg_ref, qkg_ref, o_ref):
    bf = jnp.bfloat16
    side = {}

    def side_work():
        wlr = (wlr_ref[...] * nw_ref[...]).astype(bf)
        side['lr'] = (lax.dot_general(xb_ref[...], wlr, _NT, preferred_element_type=jnp.float32)
                      * rs_ref[:, 0:1]).astype(bf)
        wo_bf_ref[...] = (wo_ref[...] * hg_ref[:, 0:1]).astype(bf)
        pgw_bf_ref[...] = pgw_ref[...].astype(bf)
        pp_bf_ref[...] = pp_ref[...].astype(bf)

    def hgrn_qkg(cols):
        lbp = lb_ref[:, cols]
        e = jnp.exp(lbp - jnp.max(lbp, axis=0, keepdims=True))
        lb = e[0:1, :] / jnp.sum(e, axis=0, keepdims=True)
        f = (0.5 + 0.5 * lb) + (0.5 - 0.5 * lb) * jnp.tanh(hf_ref[:, cols])
        return _silu_of_half(hq_ref[:, cols]), 1.0 - f, jnp.log2(f)

    def gla_qkg(cols):
        logit = jnp.dot(side['lr'], w2_ref[:, cols].astype(bf),
                        preferred_element_type=jnp.float32) + gb_ref[:, cols]
        log2_sig = (jnp.minimum(logit, 0.0) * LOG2E
                    - jnp.log2(1.0 + jnp.exp2(jnp.abs(logit) * -LOG2E)))
        return gq_ref[:, cols], gk_ref[:, cols], log2_sig / GLA_GATE_TAU

    _scan_and_readout([(hgrn_qkg, hi_ref, hz_ref, yh_ref, sth_ref),
                       (gla_qkg, gv_ref, gz_ref, yg_ref, stg_ref)], qkg_ref, o_ref, side_work)


def _row_slices(n):
    def spec(a):
        assert a.shape[0] % (16 * n) == 0
        return pl.BlockSpec((a.shape[0] // n, a.shape[1]), lambda h, t: (t, 0))
    return spec


def _scans(hf, pb, hgrn_lb, w2_pad, gate_b, xb, rs, wlr_t, norm_w, w_out, head_gain, pgw, pp, tt=512):
    T, D = xb.shape
    n = T // tt
    panel = lambda c: pl.BlockSpec((tt, PANEL), lambda h, t: (t, c))
    half = lambda c: pl.BlockSpec((tt, GLA_DK_TOTAL), lambda h, t: (t, c))
    rows = lambda width: pl.BlockSpec((tt, width), lambda h, t: (t, 0))
    const = lambda a: pl.BlockSpec(a.shape, lambda h, t: (0, 0))
    piece = _row_slices(n)
    gq0 = 3 * PANEL // GLA_DK_TOTAL
    return pl.pallas_call(
        _scan_kernel,
        grid=(1, n),
        in_specs=[const(hgrn_lb), panel(0), panel(0), panel(1), panel(2),
                  half(gq0), half(gq0 + 1), panel(4), panel(5), const(w2_pad), const(gate_b),
                  rows(D), rows(LANE), const(wlr_t), const(norm_w),
                  piece(w_out), piece(head_gain), piece(pgw), piece(pp)],
        out_specs=[rows(D_HGRN), rows(D_GLA), piece(w_out), piece(pgw), piece(pp)],
        out_shape=[jax.ShapeDtypeStruct((T, D_HGRN), jnp.bfloat16),
                   jax.ShapeDtypeStruct((T, D_GLA), jnp.bfloat16),
                   jax.ShapeDtypeStruct(w_out.shape, jnp.bfloat16),
                   jax.ShapeDtypeStruct(pgw.shape, jnp.bfloat16),
                   jax.ShapeDtypeStruct(pp.shape, jnp.bfloat16)],
        scratch_shapes=[pltpu.VMEM((HGRN_HEADS, HGRN_DV, HGRN_DK), jnp.float32),
                        pltpu.VMEM((GLA_HEADS, GLA_DV, GLA_DK), jnp.float32),
                        pltpu.VMEM((3, tt, D_HGRN), jnp.float32),
                        pltpu.VMEM((tt, max(D_HGRN, D_GLA)), jnp.float32)],
        compiler_params=pltpu.CompilerParams(dimension_semantics=("parallel", "arbitrary"),
                                             vmem_limit_bytes=VMEM_LIMIT),
        name="scans",
    )(hgrn_lb, pb, hf, pb, pb, pb, pb, pb, pb, w2_pad, gate_b, xb, rs, wlr_t, norm_w,
      w_out, head_gain, pgw, pp)


OUT_SUB = 2
OUT_VMEM_LIMIT = 58 * 1024 * 1024


def _out_kernel(x_ref, yh_ref, yg_ref, p_ref, woh_ref, wog_ref, pp_ref, pnw_ref, gnw_ref,
                gw_ref, gb_ref, fnw_ref, o_ref):
    f32 = jnp.float32
    bf = jnp.bfloat16
    rows = o_ref.shape[0] // OUT_SUB
    subs = [slice(s * rows, (s + 1) * rows) for s in range(OUT_SUB)]
    h = [x_ref[r, :]
         + jnp.dot(yh_ref[r, :], woh_ref[...], preferred_element_type=f32)
         + jnp.dot(yg_ref[r, :], wog_ref[...], preferred_element_type=f32) for r in subs]
    e = [_rms(jnp.dot(p_ref[r, :].astype(bf), pp_ref[...], preferred_element_type=f32), pnw_ref[...])
         for r in subs]
    hn = [_rms(hs, gnw_ref[...]).astype(bf) for hs in h]
    acc = [jnp.dot(hs, gw_ref[...], preferred_element_type=f32) for hs in hn]
    for r, hs, es, a in zip(subs, h, e, acc):
        o_ref[r, :] = _rms(hs + _sigmoid(a + gb_ref[...]) * es, fnw_ref[...])


def _outproj(x2, yh, yg, p2, wo, pp, pnw, gnw, gw, gb, fnw, tm=512):
    T, D = x2.shape
    row = lambda w: pl.BlockSpec((tm, w), lambda i: (i, 0))
    once = pl.Buffered(1)
    full = lambda a: pl.BlockSpec(a.shape, lambda i: (0, 0), pipeline_mode=once)
    return pl.pallas_call(
        _out_kernel,
        grid=(T // tm,),
        in_specs=[row(D), row(D_HGRN), row(D_GLA), row(PLE_DIM),
                  pl.BlockSpec((D_HGRN, D), lambda i: (0, 0), pipeline_mode=once),
                  pl.BlockSpec((D_GLA, D), lambda i: (D_HGRN // D_GLA, 0), pipeline_mode=once),
                  full(pp), full(pnw), full(gnw), full(gw), full(gb), full(fnw)],
        out_specs=row(D),
        out_shape=jax.ShapeDtypeStruct((T, D), jnp.float32),
        compiler_params=pltpu.CompilerParams(
            dimension_semantics=("parallel",),
            vmem_limit_bytes=OUT_VMEM_LIMIT),
        name="outproj_ple",
    )(x2, yh, yg, p2, wo, wo, pp, pnw, gnw, gw, gb, fnw)


def kernel(x, p, norm_mix_w, w_in, hgrn_lb, gla_gate_w2, gla_gate_b, hgrn_norm_w, gla_norm_w,
           w_out, ple_proj, ple_norm_w, ple_gate_norm_w, ple_gate_w, ple_gate_b, final_norm_w):
    B, T, D = x.shape
    bf = jnp.bfloat16
    x2 = x.reshape(B * T, D)
    p2 = p[0].reshape(B * T, PLE_DIM)
    row = lambda a: a.reshape(1, -1)

    w_t = jnp.swapaxes(w_in[0], 0, 1)
    wlr_t = jnp.pad(w_t[D_MAIN:], ((0, LANE - GLA_GATE_RANK), (0, 0)))
    nw = row(norm_mix_w[0])
    hf, xb, rs = _inproj_first(x2, nw, w_t, lambda j: j + 1, 1, (0,), jnp.float32, "inproj_f")
    pb = _inproj_rest(xb, rs, nw, w_t, lambda j: j + jnp.minimum(j, 1), 6, (0, 2, 5), bf,
                      "inproj_b")

    head_gain = jnp.broadcast_to(jnp.concatenate([hgrn_norm_w[0], gla_norm_w[0]])[:, None],
                                 (D_HGRN + D_GLA, LANE))
    w2_pad = jnp.pad(gla_gate_w2[0], ((0, LANE - GLA_GATE_RANK), (0, 0)))
    y_h, y_g, wo_bf, pgw_bf, pp_bf = _scans(hf, pb, hgrn_lb, w2_pad, row(gla_gate_b[0]), xb, rs, wlr_t, nw,
                                            w_out[0], head_gain, ple_gate_w[0], ple_proj[0])

    out = _outproj(x2, y_h, y_g, p2, wo_bf, pp_bf, row(ple_norm_w[0]), row(ple_gate_norm_w[0]), pgw_bf,
                   row(ple_gate_b[0]), row(final_norm_w))
    return out.reshape(B, T, D)
```

```python
import functools

import jax
import jax.numpy as jnp
from jax import lax
from jax.experimental import pallas as pl
from jax.experimental.pallas import tpu as pltpu

D_MODEL = 2048
D_HGRN = 1024
HGRN_HEADS = 8
HGRN_DK = 128
HGRN_DV = 128
GLA_HEADS = 4
GLA_DK = 128
GLA_DV = 256
GLA_DK_TOTAL = 512
D_GLA = 1024
GLA_GATE_RANK = 16
GLA_GATE_TAU = 16.0
PLE_DIM = 256
EPS = 1e-6

LANE = 128
SUBLANE = 8
D_MAIN = 4 * D_HGRN + 2 * GLA_DK_TOTAL + 2 * D_GLA
PANEL = 1024
CHUNK = 128
ATTN_BLOCK = 128
DECAY_RANGE_LIMIT = 110.0
LOG2E = 1.4426950408889634
VMEM_LIMIT = 48 * 1024 * 1024
INPROJ_VMEM_LIMIT = 62 * 1024 * 1024

_NT = (((1,), (1,)), ((), ()))
_TN = (((0,), (0,)), ((), ()))


def _sigmoid(x):
    return 0.5 + 0.5 * jnp.tanh(0.5 * x)


def _silu_of_half(h):
    return h + h * jnp.tanh(h)


def _rms(x, w):
    return x * lax.rsqrt(jnp.mean(x * x, axis=-1, keepdims=True) + EPS) * w


def _is_half_panel(half_panels):
    return functools.reduce(jnp.logical_or, [pl.program_id(0) == jp for jp in half_panels])


def _cast_weight_panel(w_ref, nw_ref, wb_ref):
    @pl.when(pl.program_id(1) == 0)
    def _():
        wb_ref[...] = (w_ref[...] * nw_ref[...]).astype(jnp.bfloat16)


def _inproj_first_kernel(x_ref, nw_ref, w_ref, o_ref, xb_ref, rs_ref, wb_ref, *, half_panels):
    bf = jnp.bfloat16
    _cast_weight_panel(w_ref, nw_ref, wb_ref)

    def body(first_panel):
        x = x_ref[...]
        r = lax.rsqrt(jnp.mean(x * x, axis=-1, keepdims=True) + EPS)
        xb = x.astype(bf)
        o_ref[...] = (lax.dot_general(xb, wb_ref[...], _NT, preferred_element_type=jnp.float32)
                      * (r * jnp.where(_is_half_panel(half_panels), 0.5, 1.0))).astype(o_ref.dtype)
        if first_panel:
            xb_ref[...] = xb
            rs_ref[...] = jnp.broadcast_to(r, rs_ref.shape)

    first = pl.program_id(0) == 0
    pl.when(first)(lambda: body(True))
    pl.when(jnp.logical_not(first))(lambda: body(False))


def _inproj_rest_kernel(xb_ref, rs_ref, nw_ref, w_ref, o_ref, wb_ref, *, half_panels):
    _cast_weight_panel(w_ref, nw_ref, wb_ref)
    r = rs_ref[:, 0:1] * jnp.where(_is_half_panel(half_panels), 0.5, 1.0)
    o_ref[...] = (lax.dot_general(xb_ref[...], wb_ref[...], _NT, preferred_element_type=jnp.float32)
                  * r).astype(o_ref.dtype)


def _inproj_params():
    return pltpu.CompilerParams(dimension_semantics=("parallel", "arbitrary"),
                                vmem_limit_bytes=INPROJ_VMEM_LIMIT)


def _first_panel_only(n_rows):
    return lambda j, i: (jnp.where(j == 0, i, n_rows - 1), 0)


def _inproj_first(x2, norm_w, w_t, panel_of, n_panels, half_panels, out_dtype, name, tm=1024):
    T, D = x2.shape
    assert T % tm == 0
    n_rows = T // tm
    return pl.pallas_call(
        functools.partial(_inproj_first_kernel, half_panels=half_panels),
        grid=(n_panels, n_rows),
        in_specs=[pl.BlockSpec((tm, D), lambda j, i: (i, 0)),
                  pl.BlockSpec((1, D), lambda j, i: (0, 0)),
                  pl.BlockSpec((PANEL, D), lambda j, i: (panel_of(j), 0))],
        out_specs=[pl.BlockSpec((tm, PANEL), lambda j, i: (i, j)),
                   pl.BlockSpec((tm, D), _first_panel_only(n_rows)),
                   pl.BlockSpec((tm, LANE), _first_panel_only(n_rows))],
        out_shape=[jax.ShapeDtypeStruct((T, n_panels * PANEL), out_dtype),
                   jax.ShapeDtypeStruct((T, D), jnp.bfloat16),
                   jax.ShapeDtypeStruct((T, LANE), jnp.float32)],
        scratch_shapes=[pltpu.VMEM((PANEL, D), jnp.bfloat16)],
        compiler_params=_inproj_params(),
        name=name,
    )(x2, norm_w, w_t)


def _inproj_rest(xb, rs, norm_w, w_t, panel_of, n_panels, half_panels, out_dtype, name, tm=2048):
    T, D = xb.shape
    assert T % tm == 0
    return pl.pallas_call(
        functools.partial(_inproj_rest_kernel, half_panels=half_panels),
        grid=(n_panels, T // tm),
        in_specs=[pl.BlockSpec((tm, D), lambda j, i: (i, 0)),
                  pl.BlockSpec((tm, LANE), lambda j, i: (i, 0)),
                  pl.BlockSpec((1, D), lambda j, i: (0, 0)),
                  pl.BlockSpec((PANEL, D), lambda j, i: (panel_of(j), 0))],
        out_specs=pl.BlockSpec((tm, PANEL), lambda j, i: (i, j)),
        out_shape=jax.ShapeDtypeStruct((T, n_panels * PANEL), out_dtype),
        scratch_shapes=[pltpu.VMEM((PANEL, D), jnp.bfloat16)],
        compiler_params=_inproj_params(),
        name=name,
    )(xb, rs, norm_w, w_t)


def _scan_stages(qkg, v, state, finish):
    bf = jnp.bfloat16
    f32 = jnp.float32
    dv, dk = state.shape
    blk = ATTN_BLOCK
    row = lax.broadcasted_iota(jnp.int32, (blk, blk), 0)
    col = lax.broadcasted_iota(jnp.int32, (blk, blk), 1)
    causal = (col <= row) & (col >= (row & -CHUNK))
    res = {}

    def gates():
        q, k, g = qkg()
        g_hi = g.astype(bf)
        g_lo = (g - g_hi.astype(f32)).astype(bf)
        tri = causal.astype(bf)
        tri2 = jnp.concatenate([tri, tri], axis=1)
        res.update(q=q, k=k, n_blk=g.shape[0] // blk, n_chk=g.shape[0] // CHUNK)
        res['b'] = [jnp.dot(tri2, jnp.concatenate([g_hi[r * blk:(r + 1) * blk],
                                                     g_lo[r * blk:(r + 1) * blk]], axis=0),
                            preferred_element_type=f32) for r in range(res['n_blk'])]

    def decay():
        b = jnp.concatenate(res.pop('b'), axis=0)
        eb = jnp.exp2(b)
        q, k = res.pop('q'), res.pop('k')
        res['qh'] = q * eb.astype(bf) if q.dtype == bf else (q * eb).astype(bf)
        res['kt'] = k * (1.0 / eb).astype(bf) if k.dtype == bf else (k / eb).astype(bf)
        res['vb'] = v().astype(bf)
        res['eb_last'] = [eb[(c + 1) * CHUNK - 1:(c + 1) * CHUNK, :] for c in range(res['n_chk'])]
        res['b_min'] = jnp.min(jnp.concatenate(
            [b[(c + 1) * CHUNK - 1:(c + 1) * CHUNK, :] for c in range(res['n_chk'])], axis=0))

    def scores():
        qh, kt, vb = res['qh'], res['kt'], res['vb']
        res['attn'] = [lax.dot_general(qh[r * blk:(r + 1) * blk], kt[r * blk:(r + 1) * blk], _NT,
                                       preferred_element_type=f32) for r in range(res['n_blk'])]
        res['upd'] = [lax.dot_general(vb[c * CHUNK:(c + 1) * CHUNK], kt[c * CHUNK:(c + 1) * CHUNK], _TN,
                                      preferred_element_type=f32) for c in range(res['n_chk'])]

    def carry():
        res['attn'] = [jnp.where(causal, a, 0.0).astype(bf) for a in res['attn']]
        st, st_in = state, []
        for upd, eb_last in zip(res.pop('upd'), res.pop('eb_last')):
            st_in.append(st.astype(bf))
            st = (st + upd) * eb_last
        res.update(st_in=st_in, state=st)

    def outputs():
        qh, vb = res.pop('qh'), res.pop('vb')
        res['intra'] = [jnp.dot(a, vb[r * blk:(r + 1) * blk], preferred_element_type=f32)
                        for r, a in enumerate(res.pop('attn'))]
        res['inter'] = [lax.dot_general(qh[c * CHUNK:(c + 1) * CHUNK], s, _NT, preferred_element_type=f32)
                        for c, s in enumerate(res.pop('st_in'))]

    def done():
        finish(jnp.concatenate(res.pop('intra'), axis=0) + jnp.concatenate(res.pop('inter'), axis=0))

    return [gates, decay, scores, carry, outputs, done], res


def _scan_tile_exact(qkg_ref, v_ref, st_ref, o_ref):
    bf = jnp.bfloat16
    f32 = jnp.float32
    n_heads, dv, dk = st_ref.shape
    tt = o_ref.shape[0]
    rowk = lax.broadcasted_iota(jnp.int32, (CHUNK, dk), 0)
    rowc = lax.broadcasted_iota(jnp.int32, (CHUNK, CHUNK), 0)
    colc = lax.broadcasted_iota(jnp.int32, (CHUNK, CHUNK), 1)
    tri = (colc <= rowc).astype(bf)

    for h in range(n_heads):
        ks = slice(h * dk, (h + 1) * dk)
        vs = slice(h * dv, (h + 1) * dv)

        def chunk(c, carry, ks=ks, vs=vs, h=h):
            r0 = pl.multiple_of(c * CHUNK, CHUNK)
            rows = pl.ds(r0, CHUNK)
            qc = qkg_ref[0, rows, ks]
            kc = qkg_ref[1, rows, ks]
            gc = qkg_ref[2, rows, ks]
            g_hi = gc.astype(bf)
            g_lo = (gc - g_hi.astype(f32)).astype(bf)
            b = (jnp.dot(tri, g_hi, preferred_element_type=f32)
                 + jnp.dot(tri, g_lo, preferred_element_type=f32))
            qkg_ref[2, rows, ks] = b

            def columns(jg, attn):
                j0 = pl.multiple_of(jg * SUBLANE, SUBLANE)
                b_rows = qkg_ref[2, pl.ds(r0 + j0, SUBLANE), ks]
                k_rows = qkg_ref[1, pl.ds(r0 + j0, SUBLANE), ks]
                for r in range(SUBLANE):
                    j = j0 + r
                    decay = jnp.exp2(jnp.where(rowk >= j, b - b_rows[r:r + 1], 0.0))
                    colv = jnp.sum(qc * k_rows[r:r + 1] * decay, axis=1, keepdims=True)
                    attn = jnp.where(colc == j, colv, attn)
                return attn

            attn = lax.fori_loop(0, CHUNK // SUBLANE, columns, jnp.zeros((CHUNK, CHUNK), f32))
            attn = jnp.where(colc <= rowc, attn, 0.0).astype(bf)
            vb = v_ref[rows, vs].astype(bf)
            st = st_ref[h]
            b_last = b[CHUNK - 1:CHUNK, :]
            o_ref[rows, vs] = (
                jnp.dot(attn, vb, preferred_element_type=f32)
                + lax.dot_general((qc * jnp.exp2(b)).astype(bf), st.astype(bf), _NT,
                                  preferred_element_type=f32))
            k_dec = (kc * jnp.exp2(b_last - b)).astype(bf)
            st_ref[h] = st * jnp.exp2(b_last) + lax.dot_general(vb, k_dec, _TN,
                                                                preferred_element_type=f32)
            return carry

        lax.fori_loop(0, tt // CHUNK, chunk, 0)


def _readout(o, z, n_heads, dk):
    dv = o.shape[1] // n_heads

    def head(oh):
        return oh * lax.rsqrt(jnp.mean(oh * oh, axis=-1, keepdims=True) + EPS * dk)

    on = jnp.concatenate([head(o[:, h * dv:(h + 1) * dv]) for h in range(n_heads)], axis=1)
    return on.astype(z.dtype) * _silu_of_half(z)


def _scan_and_readout(groups, qkg_ref, o_ref, side_work=None):
    @pl.when(pl.program_id(1) == 0)
    def _():
        for *_, st_ref in groups:
            st_ref[...] = jnp.zeros_like(st_ref)

    if side_work is not None:
        side_work()
    units = []
    for qkg, v_ref, z_ref, y_ref, st_ref in groups:
        n_heads, dv, dk = st_ref.shape
        for h in range(n_heads):
            ks = slice(h * dk, (h + 1) * dk)
            vs = slice(h * dv, (h + 1) * dv)

            def finish(o, vs=vs, y_ref=y_ref, z_ref=z_ref, dk=dk):
                y_ref[:, vs] = _readout(o, z_ref[:, vs], 1, dk).astype(y_ref.dtype)

            units.append(_scan_stages(lambda qkg=qkg, ks=ks: qkg(ks), lambda v_ref=v_ref, vs=vs: v_ref[:, vs],
                                      st_ref[h], finish))
    n_stages = len(units[0][0])
    for tick in range(len(units) + n_stages - 1):
        for u in range(len(units)):
            if 0 <= tick - u < n_stages:
                units[u][0][tick - u]()
    b_min = units[0][1]['b_min']
    for _, res in units[1:]:
        b_min = jnp.minimum(b_min, res['b_min'])
    in_range = b_min >= -DECAY_RANGE_LIMIT

    @pl.when(in_range)
    def _():
        new_states = iter(res['state'] for _, res in units)
        for *_, st_ref in groups:
            for h in range(st_ref.shape[0]):
                st_ref[h] = next(new_states)

    @pl.when(jnp.logical_not(in_range))
    def _():
        for qkg, v_ref, z_ref, y_ref, st_ref in groups:
            n_heads, dv, dk = st_ref.shape
            q, k, g = qkg(slice(0, n_heads * dk))
            qkg_ref[0, :, :n_heads * dk] = q.astype(jnp.float32)
            qkg_ref[1, :, :n_heads * dk] = k.astype(jnp.float32)
            qkg_ref[2, :, :n_heads * dk] = g
            _scan_tile_exact(qkg_ref, v_ref, st_ref, o_ref)
            y_ref[...] = _readout(o_ref[:, :n_heads * dv], z_ref[...], n_heads, dk).astype(y_ref.dtype)


def _scan_kernel(lb_ref, hq_ref, hf_ref, hi_ref, hz_ref, gq_ref, gk_ref, gv_ref, gz_ref, w2_ref, gb_ref,
                 xb_ref, rs_ref, wlr_ref, nw_ref, wo_ref, hg_ref, pgw_ref, pp_ref,
                 yh_ref, yg_ref, wo_bf_ref, pgw_bf_ref, pp_bf_ref, sth_ref, stg_ref, qkg_ref, o_ref):
    bf = jnp.bfloat16
    side = {}

    def side_work():
        wlr = (wlr_ref[...] * nw_ref[...]).astype(bf)
        side['lr'] = (lax.dot_general(xb_ref[...], wlr, _NT, preferred_element_type=jnp.float32)
                      * rs_ref[:, 0:1]).astype(bf)
        wo_bf_ref[...] = (wo_ref[...] * hg_ref[:, 0:1]).astype(bf)
        pgw_bf_ref[...] = pgw_ref[...].astype(bf)
        pp_bf_ref[...] = pp_ref[...].astype(bf)

    def hgrn_qkg(cols):
        lbp = lb_ref[:, cols]
        e = jnp.exp(lbp - jnp.max(lbp, axis=0, keepdims=True))
        lb = e[0:1, :] / jnp.sum(e, axis=0, keepdims=True)
        f = (0.5 + 0.5 * lb) + (0.5 - 0.5 * lb) * jnp.tanh(hf_ref[:, cols])
        return _silu_of_half(hq_ref[:, cols]), 1.0 - f, jnp.log2(f)

    def gla_qkg(cols):
        logit = jnp.dot(side['lr'], w2_ref[:, cols].astype(bf),
                        preferred_element_type=jnp.float32) + gb_ref[:, cols]
        log2_sig = (jnp.minimum(logit, 0.0) * LOG2E
                    - jnp.log2(1.0 + jnp.exp2(jnp.abs(logit) * -LOG2E)))
        return gq_ref[:, cols], gk_ref[:, cols], log2_sig / GLA_GATE_TAU

    _scan_and_readout([(hgrn_qkg, hi_ref, hz_ref, yh_ref, sth_ref),
                       (gla_qkg, gv_ref, gz_ref, yg_ref, stg_ref)], qkg_ref, o_ref, side_work)


def _row_slices(n):
    def spec(a):
        assert a.shape[0] % (16 * n) == 0
        return pl.BlockSpec((a.shape[0] // n, a.shape[1]), lambda h, t: (t, 0))
    return spec


def _scans(hf, pb, hgrn_lb, w2_pad, gate_b, xb, rs, wlr_t, norm_w, w_out, head_gain, pgw, pp, tt=512):
    T, D = xb.shape
    assert T % tt == 0 and tt % ATTN_BLOCK == 0 and ATTN_BLOCK % CHUNK == 0
    n = T // tt
    panel = lambda c: pl.BlockSpec((tt, PANEL), lambda h, t: (t, c))
    half = lambda c: pl.BlockSpec((tt, GLA_DK_TOTAL), lambda h, t: (t, c))
    rows = lambda width: pl.BlockSpec((tt, width), lambda h, t: (t, 0))
    const = lambda a: pl.BlockSpec(a.shape, lambda h, t: (0, 0))
    piece = _row_slices(n)
    gq0 = 3 * PANEL // GLA_DK_TOTAL
    return pl.pallas_call(
        _scan_kernel,
        grid=(1, n),
        in_specs=[const(hgrn_lb), panel(0), panel(0), panel(1), panel(2),
                  half(gq0), half(gq0 + 1), panel(4), panel(5), const(w2_pad), const(gate_b),
                  rows(D), rows(LANE), const(wlr_t), const(norm_w),
                  piece(w_out), piece(head_gain), piece(pgw), piece(pp)],
        out_specs=[rows(D_HGRN), rows(D_GLA), piece(w_out), piece(pgw), piece(pp)],
        out_shape=[jax.ShapeDtypeStruct((T, D_HGRN), jnp.bfloat16),
                   jax.ShapeDtypeStruct((T, D_GLA), jnp.bfloat16),
                   jax.ShapeDtypeStruct(w_out.shape, jnp.bfloat16),
                   jax.ShapeDtypeStruct(pgw.shape, jnp.bfloat16),
                   jax.ShapeDtypeStruct(pp.shape, jnp.bfloat16)],
        scratch_shapes=[pltpu.VMEM((HGRN_HEADS, HGRN_DV, HGRN_DK), jnp.float32),
                        pltpu.VMEM((GLA_HEADS, GLA_DV, GLA_DK), jnp.float32),
                        pltpu.VMEM((3, tt, D_HGRN), jnp.float32),
                        pltpu.VMEM((tt, max(D_HGRN, D_GLA)), jnp.float32)],
        compiler_params=pltpu.CompilerParams(dimension_semantics=("parallel", "arbitrary"),
                                             vmem_limit_bytes=VMEM_LIMIT),
        name="scans",
    )(hgrn_lb, pb, hf, pb, pb, pb, pb, pb, pb, w2_pad, gate_b, xb, rs, wlr_t, norm_w,
      w_out, head_gain, pgw, pp)


OUT_SUB = 2
OUT_VMEM_LIMIT = 58 * 1024 * 1024


def _out_kernel(x_ref, yh_ref, yg_ref, p_ref, woh_ref, wog_ref, pp_ref, pnw_ref, gnw_ref,
                gw_ref, gb_ref, fnw_ref, o_ref):
    f32 = jnp.float32
    bf = jnp.bfloat16
    rows = o_ref.shape[0] // OUT_SUB
    subs = [slice(s * rows, (s + 1) * rows) for s in range(OUT_SUB)]
    h = [x_ref[r, :]
         + jnp.dot(yh_ref[r, :], woh_ref[...], preferred_element_type=f32)
         + jnp.dot(yg_ref[r, :], wog_ref[...], preferred_element_type=f32) for r in subs]
    e = [_rms(jnp.dot(p_ref[r, :].astype(bf), pp_ref[...], preferred_element_type=f32), pnw_ref[...])
         for r in subs]
    hn = [_rms(hs, gnw_ref[...]).astype(bf) for hs in h]
    acc = [jnp.dot(hs, gw_ref[...], preferred_element_type=f32) for hs in hn]
    for r, hs, es, a in zip(subs, h, e, acc):
        o_ref[r, :] = _rms(hs + _sigmoid(a + gb_ref[...]) * es, fnw_ref[...])


def _outproj(x2, yh, yg, p2, wo, pp, pnw, gnw, gw, gb, fnw, tm=512):
    T, D = x2.shape
    assert T % tm == 0 and tm % OUT_SUB == 0
    row = lambda w: pl.BlockSpec((tm, w), lambda i: (i, 0))
    once = pl.Buffered(1)
    full = lambda a: pl.BlockSpec(a.shape, lambda i: (0, 0), pipeline_mode=once)
    return pl.pallas_call(
        _out_kernel,
        grid=(T // tm,),
        in_specs=[row(D), row(D_HGRN), row(D_GLA), row(PLE_DIM),
                  pl.BlockSpec((D_HGRN, D), lambda i: (0, 0), pipeline_mode=once),
                  pl.BlockSpec((D_GLA, D), lambda i: (D_HGRN // D_GLA, 0), pipeline_mode=once),
                  full(pp), full(pnw), full(gnw), full(gw), full(gb), full(fnw)],
        out_specs=row(D),
        out_shape=jax.ShapeDtypeStruct((T, D), jnp.float32),
        compiler_params=pltpu.CompilerParams(
            dimension_semantics=("parallel",),
            vmem_limit_bytes=OUT_VMEM_LIMIT),
        name="outproj_ple",
    )(x2, yh, yg, p2, wo, wo, pp, pnw, gnw, gw, gb, fnw)


def kernel(x, p, norm_mix_w, w_in, hgrn_lb, gla_gate_w2, gla_gate_b, hgrn_norm_w, gla_norm_w,
           w_out, ple_proj, ple_norm_w, ple_gate_norm_w, ple_gate_w, ple_gate_b, final_norm_w):
    B, T, D = x.shape
    bf = jnp.bfloat16
    x2 = x.reshape(B * T, D)
    p2 = p[0].reshape(B * T, PLE_DIM)
    row = lambda a: a.reshape(1, -1)

    w_t = jnp.swapaxes(w_in[0], 0, 1)
    wlr_t = jnp.pad(w_t[D_MAIN:], ((0, LANE - GLA_GATE_RANK), (0, 0)))
    nw = row(norm_mix_w[0])
    hf, xb, rs = _inproj_first(x2, nw, w_t, lambda j: j + 1, 1, (0,), jnp.float32, "inproj_f")
    pb = _inproj_rest(xb, rs, nw, w_t, lambda j: j + jnp.minimum(j, 1), 6, (0, 2, 5), bf,
                      "inproj_b")

    head_gain = jnp.broadcast_to(jnp.concatenate([hgrn_norm_w[0], gla_norm_w[0]])[:, None],
                                 (D_HGRN + D_GLA, LANE))
    w2_pad = jnp.pad(gla_gate_w2[0], ((0, LANE - GLA_GATE_RANK), (0, 0)))
    y_h, y_g, wo_bf, pgw_bf, pp_bf = _scans(hf, pb, hgrn_lb, w2_pad, row(gla_gate_b[0]), xb, rs, wlr_t, nw,
                                            w_out[0], head_gain, ple_gate_w[0], ple_proj[0])

    out = _outproj(x2, y_h, y_g, p2, wo_bf, pp_bf, row(ple_norm_w[0]), row(ple_gate_norm_w[0]), pgw_bf,
                   row(ple_gate_b[0]), row(final_norm_w))
    return out.reshape(B, T, D)
```

```python
import functools

import jax
import jax.numpy as jnp
from jax import lax
from jax.experimental import pallas as pl
from jax.experimental.pallas import tpu as pltpu

D_MODEL = 2048
D_HGRN = 1024
HGRN_HEADS = 8
HGRN_DK = 128
HGRN_DV = 128
GLA_HEADS = 4
GLA_DK = 128
GLA_DV = 256
GLA_DK_TOTAL = 512
D_GLA = 1024
GLA_GATE_RANK = 16
GLA_GATE_TAU = 16.0
PLE_DIM = 256
EPS = 1e-6

LANE = 128
SUBLANE = 8
D_MAIN = 4 * D_HGRN + 2 * GLA_DK_TOTAL + 2 * D_GLA
PANEL = 1024
CHUNK = 128
ATTN_BLOCK = 128
DECAY_RANGE_LIMIT = 110.0
LOG2E = 1.4426950408889634
VMEM_LIMIT = 48 * 1024 * 1024
INPROJ_VMEM_LIMIT = 62 * 1024 * 1024

_NT = (((1,), (1,)), ((), ()))
_TN = (((0,), (0,)), ((), ()))


def _sigmoid(x):
    return 0.5 + 0.5 * jnp.tanh(0.5 * x)


def _silu_of_half(h):
    return h + h * jnp.tanh(h)


def _rms(x, w):
    return x * lax.rsqrt(jnp.mean(x * x, axis=-1, keepdims=True) + EPS) * w


def _is_half_panel(half_panels):
    return functools.reduce(jnp.logical_or, [pl.program_id(0) == jp for jp in half_panels])


def _cast_weight_panel(w_ref, nw_ref, wb_ref):
    @pl.when(pl.program_id(1) == 0)
    def _():
        wb_ref[...] = (w_ref[...] * nw_ref[...]).astype(jnp.bfloat16)


def _inproj_first_kernel(x_ref, nw_ref, w_ref, o_ref, xb_ref, rs_ref, wb_ref, *, half_panels):
    bf = jnp.bfloat16
    _cast_weight_panel(w_ref, nw_ref, wb_ref)

    def body(first_panel):
        x = x_ref[...]
        r = lax.rsqrt(jnp.mean(x * x, axis=-1, keepdims=True) + EPS)
        xb = x.astype(bf)
        o_ref[...] = (lax.dot_general(xb, wb_ref[...], _NT, preferred_element_type=jnp.float32)
                      * (r * jnp.where(_is_half_panel(half_panels), 0.5, 1.0))).astype(o_ref.dtype)
        if first_panel:
            xb_ref[...] = xb
            rs_ref[...] = jnp.broadcast_to(r, rs_ref.shape)

    first = pl.program_id(0) == 0
    pl.when(first)(lambda: body(True))
    pl.when(jnp.logical_not(first))(lambda: body(False))


def _inproj_rest_kernel(xb_ref, rs_ref, nw_ref, w_ref, o_ref, wb_ref, *, half_panels):
    _cast_weight_panel(w_ref, nw_ref, wb_ref)
    r = rs_ref[:, 0:1] * jnp.where(_is_half_panel(half_panels), 0.5, 1.0)
    o_ref[...] = (lax.dot_general(xb_ref[...], wb_ref[...], _NT, preferred_element_type=jnp.float32)
                  * r).astype(o_ref.dtype)


def _inproj_params():
    return pltpu.CompilerParams(dimension_semantics=("parallel", "arbitrary"),
                                vmem_limit_bytes=INPROJ_VMEM_LIMIT)


def _first_panel_only(n_rows):
    return lambda j, i: (jnp.where(j == 0, i, n_rows - 1), 0)


def _inproj_first(x2, norm_w, w_t, panel_of, n_panels, half_panels, out_dtype, name, tm=512):
    T, D = x2.shape
    assert T % tm == 0
    n_rows = T // tm
    return pl.pallas_call(
        functools.partial(_inproj_first_kernel, half_panels=half_panels),
        grid=(n_panels, n_rows),
        in_specs=[pl.BlockSpec((tm, D), lambda j, i: (i, 0)),
                  pl.BlockSpec((1, D), lambda j, i: (0, 0)),
                  pl.BlockSpec((PANEL, D), lambda j, i: (panel_of(j), 0))],
        out_specs=[pl.BlockSpec((tm, PANEL), lambda j, i: (i, j)),
                   pl.BlockSpec((tm, D), _first_panel_only(n_rows)),
                   pl.BlockSpec((tm, LANE), _first_panel_only(n_rows))],
        out_shape=[jax.ShapeDtypeStruct((T, n_panels * PANEL), out_dtype),
                   jax.ShapeDtypeStruct((T, D), jnp.bfloat16),
                   jax.ShapeDtypeStruct((T, LANE), jnp.float32)],
        scratch_shapes=[pltpu.VMEM((PANEL, D), jnp.bfloat16)],
        compiler_params=_inproj_params(),
        name=name,
    )(x2, norm_w, w_t)


def _inproj_rest(xb, rs, norm_w, w_t, panel_of, n_panels, half_panels, out_dtype, name, tm=2048):
    T, D = xb.shape
    assert T % tm == 0
    return pl.pallas_call(
        functools.partial(_inproj_rest_kernel, half_panels=half_panels),
        grid=(n_panels, T // tm),
        in_specs=[pl.BlockSpec((tm, D), lambda j, i: (i, 0)),
                  pl.BlockSpec((tm, LANE), lambda j, i: (i, 0)),
                  pl.BlockSpec((1, D), lambda j, i: (0, 0)),
                  pl.BlockSpec((PANEL, D), lambda j, i: (panel_of(j), 0))],
        out_specs=pl.BlockSpec((tm, PANEL), lambda j, i: (i, j)),
        out_shape=jax.ShapeDtypeStruct((T, n_panels * PANEL), out_dtype),
        scratch_shapes=[pltpu.VMEM((PANEL, D), jnp.bfloat16)],
        compiler_params=_inproj_params(),
        name=name,
    )(xb, rs, norm_w, w_t)


def _scan_stages(qkg, v, state, finish):
    bf = jnp.bfloat16
    f32 = jnp.float32
    dv, dk = state.shape
    blk = ATTN_BLOCK
    row = lax.broadcasted_iota(jnp.int32, (blk, blk), 0)
    col = lax.broadcasted_iota(jnp.int32, (blk, blk), 1)
    causal = (col <= row) & (col >= (row & -CHUNK))
    res = {}

    def gates():
        q, k, g = qkg()
        g_hi = g.astype(bf)
        g_lo = (g - g_hi.astype(f32)).astype(bf)
        tri = causal.astype(bf)
        tri2 = jnp.concatenate([tri, tri], axis=1)
        res.update(q=q, k=k, n_blk=g.shape[0] // blk, n_chk=g.shape[0] // CHUNK)
        res['b'] = [jnp.dot(tri2, jnp.concatenate([g_hi[r * blk:(r + 1) * blk],
                                                     g_lo[r * blk:(r + 1) * blk]], axis=0),
                            preferred_element_type=f32) for r in range(res['n_blk'])]

    def decay():
        b = jnp.concatenate(res.pop('b'), axis=0)
        eb = jnp.exp2(b)
        q, k = res.pop('q'), res.pop('k')
        res['qh'] = q * eb.astype(bf) if q.dtype == bf else (q * eb).astype(bf)
        res['kt'] = k * (1.0 / eb).astype(bf) if k.dtype == bf else (k / eb).astype(bf)
        res['vb'] = v().astype(bf)
        res['eb_last'] = [eb[(c + 1) * CHUNK - 1:(c + 1) * CHUNK, :] for c in range(res['n_chk'])]
        res['b_min'] = jnp.min(jnp.concatenate(
            [b[(c + 1) * CHUNK - 1:(c + 1) * CHUNK, :] for c in range(res['n_chk'])], axis=0))

    def scores():
        qh, kt, vb = res['qh'], res['kt'], res['vb']
        res['attn'] = [lax.dot_general(qh[r * blk:(r + 1) * blk], kt[r * blk:(r + 1) * blk], _NT,
                                       preferred_element_type=f32) for r in range(res['n_blk'])]
        res['upd'] = [lax.dot_general(vb[c * CHUNK:(c + 1) * CHUNK], kt[c * CHUNK:(c + 1) * CHUNK], _TN,
                                      preferred_element_type=f32) for c in range(res['n_chk'])]

    def carry():
        res['attn'] = [jnp.where(causal, a, 0.0).astype(bf) for a in res['attn']]
        st, st_in = state, []
        for upd, eb_last in zip(res.pop('upd'), res.pop('eb_last')):
            st_in.append(st.astype(bf))
            st = (st + upd) * eb_last
        res.update(st_in=st_in, state=st)

    def outputs():
        qh, vb = res.pop('qh'), res.pop('vb')
        res['intra'] = [jnp.dot(a, vb[r * blk:(r + 1) * blk], preferred_element_type=f32)
                        for r, a in enumerate(res.pop('attn'))]
        res['inter'] = [lax.dot_general(qh[c * CHUNK:(c + 1) * CHUNK], s, _NT, preferred_element_type=f32)
                        for c, s in enumerate(res.pop('st_in'))]

    def done():
        finish(jnp.concatenate(res.pop('intra'), axis=0) + jnp.concatenate(res.pop('inter'), axis=0))

    return [gates, decay, scores, carry, outputs, done], res


def _scan_tile_exact(qkg_ref, v_ref, st_ref, o_ref):
    bf = jnp.bfloat16
    f32 = jnp.float32
    n_heads, dv, dk = st_ref.shape
    tt = o_ref.shape[0]
    rowk = lax.broadcasted_iota(jnp.int32, (CHUNK, dk), 0)
    rowc = lax.broadcasted_iota(jnp.int32, (CHUNK, CHUNK), 0)
    colc = lax.broadcasted_iota(jnp.int32, (CHUNK, CHUNK), 1)
    tri = (colc <= rowc).astype(bf)

    for h in range(n_heads):
        ks = slice(h * dk, (h + 1) * dk)
        vs = slice(h * dv, (h + 1) * dv)

        def chunk(c, carry, ks=ks, vs=vs, h=h):
            r0 = pl.multiple_of(c * CHUNK, CHUNK)
            rows = pl.ds(r0, CHUNK)
            qc = qkg_ref[0, rows, ks]
            kc = qkg_ref[1, rows, ks]
            gc = qkg_ref[2, rows, ks]
            g_hi = gc.astype(bf)
            g_lo = (gc - g_hi.astype(f32)).astype(bf)
            b = (jnp.dot(tri, g_hi, preferred_element_type=f32)
                 + jnp.dot(tri, g_lo, preferred_element_type=f32))
            qkg_ref[2, rows, ks] = b

            def columns(jg, attn):
                j0 = pl.multiple_of(jg * SUBLANE, SUBLANE)
                b_rows = qkg_ref[2, pl.ds(r0 + j0, SUBLANE), ks]
                k_rows = qkg_ref[1, pl.ds(r0 + j0, SUBLANE), ks]
                for r in range(SUBLANE):
                    j = j0 + r
                    decay = jnp.exp2(jnp.where(rowk >= j, b - b_rows[r:r + 1], 0.0))
                    colv = jnp.sum(qc * k_rows[r:r + 1] * decay, axis=1, keepdims=True)
                    attn = jnp.where(colc == j, colv, attn)
                return attn

            attn = lax.fori_loop(0, CHUNK // SUBLANE, columns, jnp.zeros((CHUNK, CHUNK), f32))
            attn = jnp.where(colc <= rowc, attn, 0.0).astype(bf)
            vb = v_ref[rows, vs].astype(bf)
            st = st_ref[h]
            b_last = b[CHUNK - 1:CHUNK, :]
            o_ref[rows, vs] = (
                jnp.dot(attn, vb, preferred_element_type=f32)
                + lax.dot_general((qc * jnp.exp2(b)).astype(bf), st.astype(bf), _NT,
                                  preferred_element_type=f32))
            k_dec = (kc * jnp.exp2(b_last - b)).astype(bf)
            st_ref[h] = st * jnp.exp2(b_last) + lax.dot_general(vb, k_dec, _TN,
                                                                preferred_element_type=f32)
            return carry

        lax.fori_loop(0, tt // CHUNK, chunk, 0)


def _readout(o, z, n_heads, dk):
    dv = o.shape[1] // n_heads

    def head(oh):
        return oh * lax.rsqrt(jnp.mean(oh * oh, axis=-1, keepdims=True) + EPS * dk)

    on = jnp.concatenate([head(o[:, h * dv:(h + 1) * dv]) for h in range(n_heads)], axis=1)
    return on.astype(z.dtype) * _silu_of_half(z)


def _scan_and_readout(groups, qkg_ref, o_ref, side_work=None):
    @pl.when(pl.program_id(1) == 0)
    def _():
        for *_, st_ref in groups:
            st_ref[...] = jnp.zeros_like(st_ref)

    if side_work is not None:
        side_work()
    units = []
    for qkg, v_ref, z_ref, y_ref, st_ref in groups:
        n_heads, dv, dk = st_ref.shape
        for h in range(n_heads):
            ks = slice(h * dk, (h + 1) * dk)
            vs = slice(h * dv, (h + 1) * dv)

            def finish(o, vs=vs, y_ref=y_ref, z_ref=z_ref, dk=dk):
                y_ref[:, vs] = _readout(o, z_ref[:, vs], 1, dk).astype(y_ref.dtype)

            units.append(_scan_stages(lambda qkg=qkg, ks=ks: qkg(ks), lambda v_ref=v_ref, vs=vs: v_ref[:, vs],
                                      st_ref[h], finish))
    n_stages = len(units[0][0])
    for tick in range(len(units) + n_stages - 1):
        for u in range(len(units)):
            if 0 <= tick - u < n_stages:
                units[u][0][tick - u]()
    b_min = units[0][1]['b_min']
    for _, res in units[1:]:
        b_min = jnp.minimum(b_min, res['b_min'])
    in_range = b_min >= -DECAY_RANGE_LIMIT

    @pl.when(in_range)
    def _():
        new_states = iter(res['state'] for _, res in units)
        for *_, st_ref in groups:
            for h in range(st_ref.shape[0]):
                st_ref[h] = next(new_states)

    @pl.when(jnp.logical_not(in_range))
    def _():
        for qkg, v_ref, z_ref, y_ref, st_ref in groups:
            n_heads, dv, dk = st_ref.shape
            q, k, g = qkg(slice(0, n_heads * dk))
            qkg_ref[0, :, :n_heads * dk] = q.astype(jnp.float32)
            qkg_ref[1, :, :n_heads * dk] = k.astype(jnp.float32)
            qkg_ref[2, :, :n_heads * dk] = g
            _scan_tile_exact(qkg_ref, v_ref, st_ref, o_ref)
            y_ref[...] = _readout(o_ref[:, :n_heads * dv], z_ref[...], n_heads, dk).astype(y_ref.dtype)


def _scan_kernel(lb_ref, hq_ref, hf_ref, hi_ref, hz_ref, gq_ref, gk_ref, gv_ref, gz_ref, w2_ref, gb_ref,
                 xb_ref, rs_ref, wlr_ref, nw_ref, wo_ref, hg_ref, pgw_ref, pp_ref,
                 yh_ref, yg_ref, wo_bf_ref, pgw_bf_ref, pp_bf_ref, sth_ref, stg_ref, qkg_ref, o_ref):
    bf = jnp.bfloat16
    side = {}

    def side_work():
        wlr = (wlr_ref[...] * nw_ref[...]).astype(bf)
        side['lr'] = (lax.dot_general(xb_ref[...], wlr, _NT, preferred_element_type=jnp.float32)
                      * rs_ref[:, 0:1]).astype(bf)
        wo_bf_ref[...] = (wo_ref[...] * hg_ref[:, 0:1]).astype(bf)
        pgw_bf_ref[...] = pgw_ref[...].astype(bf)
        pp_bf_ref[...] = pp_ref[...].astype(bf)

    def hgrn_qkg(cols):
        lbp = lb_ref[:, cols]
        e = jnp.exp(lbp - jnp.max(lbp, axis=0, keepdims=True))
        lb = e[0:1, :] / jnp.sum(e, axis=0, keepdims=True)
        f = (0.5 + 0.5 * lb) + (0.5 - 0.5 * lb) * jnp.tanh(hf_ref[:, cols])
        return _silu_of_half(hq_ref[:, cols]), 1.0 - f, jnp.log2(f)

    def gla_qkg(cols):
        logit = jnp.dot(side['lr'], w2_ref[:, cols].astype(bf),
                        preferred_element_type=jnp.float32) + gb_ref[:, cols]
        log2_sig = (jnp.minimum(logit, 0.0) * LOG2E
                    - jnp.log2(1.0 + jnp.exp2(jnp.abs(logit) * -LOG2E)))
        return gq_ref[:, cols], gk_ref[:, cols], log2_sig / GLA_GATE_TAU

    _scan_and_readout([(hgrn_qkg, hi_ref, hz_ref, yh_ref, sth_ref),
                       (gla_qkg, gv_ref, gz_ref, yg_ref, stg_ref)], qkg_ref, o_ref, side_work)


def _row_slices(n):
    def spec(a):
        assert a.shape[0] % (16 * n) == 0
        return pl.BlockSpec((a.shape[0] // n, a.shape[1]), lambda h, t: (t, 0))
    return spec


def _scans(hf, pb, hgrn_lb, w2_pad, gate_b, xb, rs, wlr_t, norm_w, w_out, head_gain, pgw, pp, tt=512):
    T, D = xb.shape
    assert T % tt == 0 and tt % ATTN_BLOCK == 0 and ATTN_BLOCK % CHUNK == 0
    n = T // tt
    panel = lambda c: pl.BlockSpec((tt, PANEL), lambda h, t: (t, c))
    half = lambda c: pl.BlockSpec((tt, GLA_DK_TOTAL), lambda h, t: (t, c))
    rows = lambda width: pl.BlockSpec((tt, width), lambda h, t: (t, 0))
    const = lambda a: pl.BlockSpec(a.shape, lambda h, t: (0, 0))
    piece = _row_slices(n)
    gq0 = 3 * PANEL // GLA_DK_TOTAL
    return pl.pallas_call(
        _scan_kernel,
        grid=(1, n),
        in_specs=[const(hgrn_lb), panel(0), panel(0), panel(1), panel(2),
                  half(gq0), half(gq0 + 1), panel(4), panel(5), const(w2_pad), const(gate_b),
                  rows(D), rows(LANE), const(wlr_t), const(norm_w),
                  piece(w_out), piece(head_gain), piece(pgw), piece(pp)],
        out_specs=[rows(D_HGRN), rows(D_GLA), piece(w_out), piece(pgw), piece(pp)],
        out_shape=[jax.ShapeDtypeStruct((T, D_HGRN), jnp.bfloat16),
                   jax.ShapeDtypeStruct((T, D_GLA), jnp.bfloat16),
                   jax.ShapeDtypeStruct(w_out.shape, jnp.bfloat16),
                   jax.ShapeDtypeStruct(pgw.shape, jnp.bfloat16),
                   jax.ShapeDtypeStruct(pp.shape, jnp.bfloat16)],
        scratch_shapes=[pltpu.VMEM((HGRN_HEADS, HGRN_DV, HGRN_DK), jnp.float32),
                        pltpu.VMEM((GLA_HEADS, GLA_DV, GLA_DK), jnp.float32),
                        pltpu.VMEM((3, tt, D_HGRN), jnp.float32),
                        pltpu.VMEM((tt, max(D_HGRN, D_GLA)), jnp.float32)],
        compiler_params=pltpu.CompilerParams(dimension_semantics=("parallel", "arbitrary"),
                                             vmem_limit_bytes=VMEM_LIMIT),
        name="scans",
    )(hgrn_lb, pb, hf, pb, pb, pb, pb, pb, pb, w2_pad, gate_b, xb, rs, wlr_t, norm_w,
      w_out, head_gain, pgw, pp)


OUT_SUB = 2
OUT_VMEM_LIMIT = 58 * 1024 * 1024


def _out_kernel(x_ref, yh_ref, yg_ref, p_ref, woh_ref, wog_ref, pp_ref, pnw_ref, gnw_ref,
                gw_ref, gb_ref, fnw_ref, o_ref):
    f32 = jnp.float32
    bf = jnp.bfloat16
    rows = o_ref.shape[0] // OUT_SUB
    subs = [slice(s * rows, (s + 1) * rows) for s in range(OUT_SUB)]
    h = [x_ref[r, :]
         + jnp.dot(yh_ref[r, :], woh_ref[...], preferred_element_type=f32)
         + jnp.dot(yg_ref[r, :], wog_ref[...], preferred_element_type=f32) for r in subs]
    e = [_rms(jnp.dot(p_ref[r, :].astype(bf), pp_ref[...], preferred_element_type=f32), pnw_ref[...])
         for r in subs]
    hn = [_rms(hs, gnw_ref[...]).astype(bf) for hs in h]
    acc = [jnp.dot(hs, gw_ref[...], preferred_element_type=f32) for hs in hn]
    for r, hs, es, a in zip(subs, h, e, acc):
        o_ref[r, :] = _rms(hs + _sigmoid(a + gb_ref[...]) * es, fnw_ref[...])


def _outproj(x2, yh, yg, p2, wo, pp, pnw, gnw, gw, gb, fnw, tm=512):
    T, D = x2.shape
    assert T % tm == 0 and tm % OUT_SUB == 0
    row = lambda w: pl.BlockSpec((tm, w), lambda i: (i, 0))
    once = pl.Buffered(1)
    full = lambda a: pl.BlockSpec(a.shape, lambda i: (0, 0), pipeline_mode=once)
    return pl.pallas_call(
        _out_kernel,
        grid=(T // tm,),
        in_specs=[row(D), row(D_HGRN), row(D_GLA), row(PLE_DIM),
                  pl.BlockSpec((D_HGRN, D), lambda i: (0, 0), pipeline_mode=once),
                  pl.BlockSpec((D_GLA, D), lambda i: (D_HGRN // D_GLA, 0), pipeline_mode=once),
                  full(pp), full(pnw), full(gnw), full(gw), full(gb), full(fnw)],
        out_specs=row(D),
        out_shape=jax.ShapeDtypeStruct((T, D), jnp.float32),
        compiler_params=pltpu.CompilerParams(
            dimension_semantics=("parallel",),
            vmem_limit_bytes=OUT_VMEM_LIMIT),
        name="outproj_ple",
    )(x2, yh, yg, p2, wo, wo, pp, pnw, gnw, gw, gb, fnw)


def kernel(x, p, norm_mix_w, w_in, hgrn_lb, gla_gate_w2, gla_gate_b, hgrn_norm_w, gla_norm_w,
           w_out, ple_proj, ple_norm_w, ple_gate_norm_w, ple_gate_w, ple_gate_b, final_norm_w):
    B, T, D = x.shape
    bf = jnp.bfloat16
    x2 = x.reshape(B * T, D)
    p2 = p[0].reshape(B * T, PLE_DIM)
    row = lambda a: a.reshape(1, -1)

    w_t = jnp.swapaxes(w_in[0], 0, 1)
    wlr_t = jnp.pad(w_t[D_MAIN:], ((0, LANE - GLA_GATE_RANK), (0, 0)))
    nw = row(norm_mix_w[0])
    hf, xb, rs = _inproj_first(x2, nw, w_t, lambda j: j + 1, 1, (0,), jnp.float32, "inproj_f")
    pb = _inproj_rest(xb, rs, nw, w_t, lambda j: j + jnp.minimum(j, 1), 6, (0, 2, 5), bf,
                      "inproj_b")

    head_gain = jnp.broadcast_to(jnp.concatenate([hgrn_norm_w[0], gla_norm_w[0]])[:, None],
                                 (D_HGRN + D_GLA, LANE))
    w2_pad = jnp.pad(gla_gate_w2[0], ((0, LANE - GLA_GATE_RANK), (0, 0)))
    y_h, y_g, wo_bf, pgw_bf, pp_bf = _scans(hf, pb, hgrn_lb, w2_pad, row(gla_gate_b[0]), xb, rs, wlr_t, nw,
                                            w_out[0], head_gain, ple_gate_w[0], ple_proj[0])

    out = _outproj(x2, y_h, y_g, p2, wo_bf, pp_bf, row(ple_norm_w[0]), row(ple_gate_norm_w[0]), pgw_bf,
                   row(ple_gate_b[0]), row(final_norm_w))
    return out.reshape(B, T, D)
```

```python
import functools

import jax
import jax.numpy as jnp
from jax import lax
from jax.experimental import pallas as pl
from jax.experimental.pallas import tpu as pltpu

D_MODEL = 2048
D_HGRN = 1024
HGRN_HEADS = 8
HGRN_DK = 128
HGRN_DV = 128
GLA_HEADS = 4
GLA_DK = 128
GLA_DV = 256
GLA_DK_TOTAL = 512
D_GLA = 1024
GLA_GATE_RANK = 16
GLA_GATE_TAU = 16.0
PLE_DIM = 256
EPS = 1e-6

LANE = 128
SUBLANE = 8
D_MAIN = 4 * D_HGRN + 2 * GLA_DK_TOTAL + 2 * D_GLA
PANEL = 1024
CHUNK = 128
ATTN_BLOCK = 128
DECAY_RANGE_LIMIT = 110.0
LOG2E = 1.4426950408889634
SCAN_VMEM_LIMIT = 56 * 1024 * 1024
INPROJ_VMEM_LIMIT = 62 * 1024 * 1024

_NT = (((1,), (1,)), ((), ()))
_TN = (((0,), (0,)), ((), ()))


def _sigmoid(x):
    return 0.5 + 0.5 * jnp.tanh(0.5 * x)


def _silu_of_half(h):
    return h + h * jnp.tanh(h)


def _rms(x, w):
    return x * lax.rsqrt(jnp.mean(x * x, axis=-1, keepdims=True) + EPS) * w


def _is_half_panel(half_panels):
    return functools.reduce(jnp.logical_or, [pl.program_id(0) == jp for jp in half_panels])


def _cast_weight_panel(w_ref, nw_ref, wb_ref):
    @pl.when(pl.program_id(1) == 0)
    def _():
        wb_ref[...] = (w_ref[...] * nw_ref[...]).astype(jnp.bfloat16)


def _inproj_first_kernel(x_ref, nw_ref, w_ref, o_ref, xb_ref, rs_ref, wb_ref, *, half_panels):
    bf = jnp.bfloat16
    _cast_weight_panel(w_ref, nw_ref, wb_ref)

    def body(first_panel):
        x = x_ref[...]
        r = lax.rsqrt(jnp.mean(x * x, axis=-1, keepdims=True) + EPS)
        xb = x.astype(bf)
        o_ref[...] = (lax.dot_general(xb, wb_ref[...], _NT, preferred_element_type=jnp.float32)
                      * (r * jnp.where(_is_half_panel(half_panels), 0.5, 1.0))).astype(o_ref.dtype)
        if first_panel:
            xb_ref[...] = xb
            rs_ref[...] = jnp.broadcast_to(r, rs_ref.shape)

    first = pl.program_id(0) == 0
    pl.when(first)(lambda: body(True))
    pl.when(jnp.logical_not(first))(lambda: body(False))


def _inproj_rest_kernel(xb_ref, rs_ref, nw_ref, w_ref, o_ref, wb_ref, *, half_panels):
    _cast_weight_panel(w_ref, nw_ref, wb_ref)
    r = rs_ref[:, 0:1] * jnp.where(_is_half_panel(half_panels), 0.5, 1.0)
    o_ref[...] = (lax.dot_general(xb_ref[...], wb_ref[...], _NT, preferred_element_type=jnp.float32)
                  * r).astype(o_ref.dtype)


def _inproj_params():
    return pltpu.CompilerParams(dimension_semantics=("parallel", "arbitrary"),
                                vmem_limit_bytes=INPROJ_VMEM_LIMIT)


def _first_panel_only(n_rows):
    return lambda j, i: (jnp.where(j == 0, i, n_rows - 1), 0)


def _inproj_first(x2, norm_w, w_t, panel_of, n_panels, half_panels, out_dtype, name, tm=1024):
    T, D = x2.shape
    assert T % tm == 0
    n_rows = T // tm
    return pl.pallas_call(
        functools.partial(_inproj_first_kernel, half_panels=half_panels),
        grid=(n_panels, n_rows),
        in_specs=[pl.BlockSpec((tm, D), lambda j, i: (i, 0)),
                  pl.BlockSpec((1, D), lambda j, i: (0, 0)),
                  pl.BlockSpec((PANEL, D), lambda j, i: (panel_of(j), 0))],
        out_specs=[pl.BlockSpec((tm, PANEL), lambda j, i: (i, j)),
                   pl.BlockSpec((tm, D), _first_panel_only(n_rows)),
                   pl.BlockSpec((tm, LANE), _first_panel_only(n_rows))],
        out_shape=[jax.ShapeDtypeStruct((T, n_panels * PANEL), out_dtype),
                   jax.ShapeDtypeStruct((T, D), jnp.bfloat16),
                   jax.ShapeDtypeStruct((T, LANE), jnp.float32)],
        scratch_shapes=[pltpu.VMEM((PANEL, D), jnp.bfloat16)],
        compiler_params=_inproj_params(),
        name=name,
    )(x2, norm_w, w_t)


def _inproj_rest(xb, rs, norm_w, w_t, panel_of, n_panels, half_panels, out_dtype, name, tm=2048):
    T, D = xb.shape
    assert T % tm == 0
    return pl.pallas_call(
        functools.partial(_inproj_rest_kernel, half_panels=half_panels),
        grid=(n_panels, T // tm),
        in_specs=[pl.BlockSpec((tm, D), lambda j, i: (i, 0)),
                  pl.BlockSpec((tm, LANE), lambda j, i: (i, 0)),
                  pl.BlockSpec((1, D), lambda j, i: (0, 0)),
                  pl.BlockSpec((PANEL, D), lambda j, i: (panel_of(j), 0))],
        out_specs=pl.BlockSpec((tm, PANEL), lambda j, i: (i, j)),
        out_shape=jax.ShapeDtypeStruct((T, n_panels * PANEL), out_dtype),
        scratch_shapes=[pltpu.VMEM((PANEL, D), jnp.bfloat16)],
        compiler_params=_inproj_params(),
        name=name,
    )(xb, rs, norm_w, w_t)


def _scan_stages(qkg, v, state, finish):
    bf = jnp.bfloat16
    f32 = jnp.float32
    dv, dk = state.shape
    blk = ATTN_BLOCK
    row = lax.broadcasted_iota(jnp.int32, (blk, blk), 0)
    col = lax.broadcasted_iota(jnp.int32, (blk, blk), 1)
    causal = (col <= row) & (col >= (row & -CHUNK))
    res = {}

    def gates0():
        res['qkg'] = qkg()

    def gates():
        q, k, g = res.pop('qkg')
        g_hi = g.astype(bf)
        g_lo = (g - g_hi.astype(f32)).astype(bf)
        tri = causal.astype(bf)
        tri2 = jnp.concatenate([tri, tri], axis=1)
        res.update(q=q, k=k, n_blk=g.shape[0] // blk, n_chk=g.shape[0] // CHUNK)
        res['b'] = [jnp.dot(tri2, jnp.concatenate([g_hi[r * blk:(r + 1) * blk],
                                                     g_lo[r * blk:(r + 1) * blk]], axis=0),
                            preferred_element_type=f32) for r in range(res['n_blk'])]

    def decay():
        b = jnp.concatenate(res.pop('b'), axis=0)
        eb = jnp.exp2(b)
        q, k = res.pop('q'), res.pop('k')
        res['qh'] = q * eb.astype(bf) if q.dtype == bf else (q * eb).astype(bf)
        res['kt'] = k * (1.0 / eb).astype(bf) if k.dtype == bf else (k / eb).astype(bf)
        res['vb'] = v().astype(bf)
        res['eb_last'] = [eb[(c + 1) * CHUNK - 1:(c + 1) * CHUNK, :] for c in range(res['n_chk'])]
        res['b_min'] = jnp.min(jnp.concatenate(
            [b[(c + 1) * CHUNK - 1:(c + 1) * CHUNK, :] for c in range(res['n_chk'])], axis=0))

    def scores():
        qh, kt, vb = res['qh'], res['kt'], res['vb']
        res['attn'] = [lax.dot_general(qh[r * blk:(r + 1) * blk], kt[r * blk:(r + 1) * blk], _NT,
                                       preferred_element_type=f32) for r in range(res['n_blk'])]
        res['upd'] = [lax.dot_general(vb[c * CHUNK:(c + 1) * CHUNK], kt[c * CHUNK:(c + 1) * CHUNK], _TN,
                                      preferred_element_type=f32) for c in range(res['n_chk'])]

    def carry():
        res['attn'] = [jnp.where(causal, a, 0.0).astype(bf) for a in res['attn']]
        st, st_in = state, []
        for upd, eb_last in zip(res.pop('upd'), res.pop('eb_last')):
            st_in.append(st.astype(bf))
            st = (st + upd) * eb_last
        res.update(st_in=st_in, state=st)

    def outputs():
        qh, vb = res.pop('qh'), res.pop('vb')
        res['intra'] = [jnp.dot(a, vb[r * blk:(r + 1) * blk], preferred_element_type=f32)
                        for r, a in enumerate(res.pop('attn'))]
        res['inter'] = [lax.dot_general(qh[c * CHUNK:(c + 1) * CHUNK], s, _NT, preferred_element_type=f32)
                        for c, s in enumerate(res.pop('st_in'))]

    def done():
        finish(jnp.concatenate(res.pop('intra'), axis=0) + jnp.concatenate(res.pop('inter'), axis=0))

    return [gates0, gates, decay, scores, carry, outputs, done], res


def _scan_tile_exact(qkg_ref, v_ref, st_ref, o_ref):
    bf = jnp.bfloat16
    f32 = jnp.float32
    n_heads, dv, dk = st_ref.shape
    tt = o_ref.shape[0]
    rowk = lax.broadcasted_iota(jnp.int32, (CHUNK, dk), 0)
    rowc = lax.broadcasted_iota(jnp.int32, (CHUNK, CHUNK), 0)
    colc = lax.broadcasted_iota(jnp.int32, (CHUNK, CHUNK), 1)
    tri = (colc <= rowc).astype(bf)

    for h in range(n_heads):
        ks = slice(h * dk, (h + 1) * dk)
        vs = slice(h * dv, (h + 1) * dv)

        def chunk(c, carry, ks=ks, vs=vs, h=h):
            r0 = pl.multiple_of(c * CHUNK, CHUNK)
            rows = pl.ds(r0, CHUNK)
            qc = qkg_ref[0, rows, ks]
            kc = qkg_ref[1, rows, ks]
            gc = qkg_ref[2, rows, ks]
            g_hi = gc.astype(bf)
            g_lo = (gc - g_hi.astype(f32)).astype(bf)
            b = (jnp.dot(tri, g_hi, preferred_element_type=f32)
                 + jnp.dot(tri, g_lo, preferred_element_type=f32))
            qkg_ref[2, rows, ks] = b

            def columns(jg, attn):
                j0 = pl.multiple_of(jg * SUBLANE, SUBLANE)
                b_rows = qkg_ref[2, pl.ds(r0 + j0, SUBLANE), ks]
                k_rows = qkg_ref[1, pl.ds(r0 + j0, SUBLANE), ks]
                for r in range(SUBLANE):
                    j = j0 + r
                    decay = jnp.exp2(jnp.where(rowk >= j, b - b_rows[r:r + 1], 0.0))
                    colv = jnp.sum(qc * k_rows[r:r + 1] * decay, axis=1, keepdims=True)
                    attn = jnp.where(colc == j, colv, attn)
                return attn

            attn = lax.fori_loop(0, CHUNK // SUBLANE, columns, jnp.zeros((CHUNK, CHUNK), f32))
            attn = jnp.where(colc <= rowc, attn, 0.0).astype(bf)
            vb = v_ref[rows, vs].astype(bf)
            st = st_ref[h]
            b_last = b[CHUNK - 1:CHUNK, :]
            o_ref[rows, vs] = (
                jnp.dot(attn, vb, preferred_element_type=f32)
                + lax.dot_general((qc * jnp.exp2(b)).astype(bf), st.astype(bf), _NT,
                                  preferred_element_type=f32))
            k_dec = (kc * jnp.exp2(b_last - b)).astype(bf)
            st_ref[h] = st * jnp.exp2(b_last) + lax.dot_general(vb, k_dec, _TN,
                                                                preferred_element_type=f32)
            return carry

        lax.fori_loop(0, tt // CHUNK, chunk, 0)


def _readout(o, z, n_heads, dk):
    dv = o.shape[1] // n_heads

    def head(oh):
        return oh * lax.rsqrt(jnp.mean(oh * oh, axis=-1, keepdims=True) + EPS * dk)

    on = jnp.concatenate([head(o[:, h * dv:(h + 1) * dv]) for h in range(n_heads)], axis=1)
    return on.astype(z.dtype) * _silu_of_half(z)


def _scan_and_readout(groups, qkg_ref, o_ref, side_work=None):
    @pl.when(pl.program_id(1) == 0)
    def _():
        for *_, st_ref in groups:
            st_ref[...] = jnp.zeros_like(st_ref)

    if side_work is not None:
        side_work()
    units = []
    for qkg, v_ref, z_ref, y_ref, st_ref in groups:
        n_heads, dv, dk = st_ref.shape
        for h in range(n_heads):
            ks = slice(h * dk, (h + 1) * dk)
            vs = slice(h * dv, (h + 1) * dv)

            def finish(o, vs=vs, y_ref=y_ref, z_ref=z_ref, dk=dk):
                y_ref[:, vs] = _readout(o, z_ref[:, vs], 1, dk).astype(y_ref.dtype)

            units.append(_scan_stages(lambda qkg=qkg, ks=ks: qkg(ks), lambda v_ref=v_ref, vs=vs: v_ref[:, vs],
                                      st_ref[h], finish))
    n_stages = len(units[0][0])
    for tick in range(len(units) + n_stages - 1):
        for u in range(len(units)):
            if 0 <= tick - u < n_stages:
                units[u][0][tick - u]()
    b_min = units[0][1]['b_min']
    for _, res in units[1:]:
        b_min = jnp.minimum(b_min, res['b_min'])
    in_range = b_min >= -DECAY_RANGE_LIMIT

    @pl.when(in_range)
    def _():
        new_states = iter(res['state'] for _, res in units)
        for *_, st_ref in groups:
            for h in range(st_ref.shape[0]):
                st_ref[h] = next(new_states)

    @pl.when(jnp.logical_not(in_range))
    def _():
        for qkg, v_ref, z_ref, y_ref, st_ref in groups:
            n_heads, dv, dk = st_ref.shape
            q, k, g = qkg(slice(0, n_heads * dk))
            qkg_ref[0, :, :n_heads * dk] = q.astype(jnp.float32)
            qkg_ref[1, :, :n_heads * dk] = k.astype(jnp.float32)
            qkg_ref[2, :, :n_heads * dk] = g
            _scan_tile_exact(qkg_ref, v_ref, st_ref, o_ref)
            y_ref[...] = _readout(o_ref[:, :n_heads * dv], z_ref[...], n_heads, dk).astype(y_ref.dtype)


def _scan_kernel(lb_ref, hq_ref, hf_ref, hi_ref, hz_ref, gq_ref, gk_ref, gv_ref, gz_ref, w2_ref, gb_ref,
                 xb_ref, rs_ref, wlr_ref, nw_ref, wo_ref, hg_ref, pgw_ref, pp_ref,
                 yh_ref, yg_ref, wo_bf_ref, pgw_bf_ref, pp_bf_ref, sth_ref, stg_ref, qkg_ref, o_ref):
    bf = jnp.bfloat16
    side = {}

    def side_work():
        wlr = (wlr_ref[...] * nw_ref[...]).astype(bf)
        side['lr'] = (lax.dot_general(xb_ref[...], wlr, _NT, preferred_element_type=jnp.float32)
                      * rs_ref[:, 0:1]).astype(bf)
        wo_bf_ref[...] = (wo_ref[...] * hg_ref[:, 0:1]).astype(bf)
        pgw_bf_ref[...] = pgw_ref[...].astype(bf)
        pp_bf_ref[...] = pp_ref[...].astype(bf)

    def hgrn_qkg(cols):
        lbp = lb_ref[:, cols]
        e = jnp.exp(lbp - jnp.max(lbp, axis=0, keepdims=True))
        lb = e[0:1, :] / jnp.sum(e, axis=0, keepdims=True)
        f = (0.5 + 0.5 * lb) + (0.5 - 0.5 * lb) * jnp.tanh(hf_ref[:, cols])
        return _silu_of_half(hq_ref[:, cols]), 1.0 - f, jnp.log2(f)

    def gla_qkg(cols):
        logit = jnp.dot(side['lr'], w2_ref[:, cols].astype(bf),
                        preferred_element_type=jnp.float32) + gb_ref[:, cols]
        log2_sig = (jnp.minimum(logit, 0.0) * LOG2E
                    - jnp.log2(1.0 + jnp.exp2(jnp.abs(logit) * -LOG2E)))
        return gq_ref[:, cols], gk_ref[:, cols], log2_sig / GLA_GATE_TAU

    _scan_and_readout([(hgrn_qkg, hi_ref, hz_ref, yh_ref, sth_ref),
                       (gla_qkg, gv_ref, gz_ref, yg_ref, stg_ref)], qkg_ref, o_ref, side_work)


def _row_slices(n):
    def spec(a):
        assert a.shape[0] % (16 * n) == 0
        return pl.BlockSpec((a.shape[0] // n, a.shape[1]), lambda h, t: (t, 0))
    return spec


def _scans(hf, pb, hgrn_lb, w2_pad, gate_b, xb, rs, wlr_t, norm_w, w_out, head_gain, pgw, pp, tt=512):
    T, D = xb.shape
    assert T % tt == 0 and tt % ATTN_BLOCK == 0 and ATTN_BLOCK % CHUNK == 0
    n = T // tt
    panel = lambda c: pl.BlockSpec((tt, PANEL), lambda h, t: (t, c))
    half = lambda c: pl.BlockSpec((tt, GLA_DK_TOTAL), lambda h, t: (t, c))
    rows = lambda width: pl.BlockSpec((tt, width), lambda h, t: (t, 0))
    const = lambda a: pl.BlockSpec(a.shape, lambda h, t: (0, 0))
    piece = _row_slices(n)
    gq0 = 3 * PANEL // GLA_DK_TOTAL
    return pl.pallas_call(
        _scan_kernel,
        grid=(1, n),
        in_specs=[const(hgrn_lb), panel(0), panel(0), panel(1), panel(2),
                  half(gq0), half(gq0 + 1), panel(4), panel(5), const(w2_pad), const(gate_b),
                  rows(D), rows(LANE), const(wlr_t), const(norm_w),
                  piece(w_out), piece(head_gain), piece(pgw), piece(pp)],
        out_specs=[rows(D_HGRN), rows(D_GLA), piece(w_out), piece(pgw), piece(pp)],
        out_shape=[jax.ShapeDtypeStruct((T, D_HGRN), jnp.bfloat16),
                   jax.ShapeDtypeStruct((T, D_GLA), jnp.bfloat16),
                   jax.ShapeDtypeStruct(w_out.shape, jnp.bfloat16),
                   jax.ShapeDtypeStruct(pgw.shape, jnp.bfloat16),
                   jax.ShapeDtypeStruct(pp.shape, jnp.bfloat16)],
        scratch_shapes=[pltpu.VMEM((HGRN_HEADS, HGRN_DV, HGRN_DK), jnp.float32),
                        pltpu.VMEM((GLA_HEADS, GLA_DV, GLA_DK), jnp.float32),
                        pltpu.VMEM((3, tt, D_HGRN), jnp.float32),
                        pltpu.VMEM((tt, max(D_HGRN, D_GLA)), jnp.float32)],
        compiler_params=pltpu.CompilerParams(dimension_semantics=("parallel", "arbitrary"),
                                             vmem_limit_bytes=SCAN_VMEM_LIMIT),
        name="scans",
    )(hgrn_lb, pb, hf, pb, pb, pb, pb, pb, pb, w2_pad, gate_b, xb, rs, wlr_t, norm_w,
      w_out, head_gain, pgw, pp)


OUT_SUB = 2
OUT_VMEM_LIMIT = 58 * 1024 * 1024


def _out_kernel(x_ref, yh_ref, yg_ref, p_ref, woh_ref, wog_ref, pp_ref, pnw_ref, gnw_ref,
                gw_ref, gb_ref, fnw_ref, o_ref):
    f32 = jnp.float32
    bf = jnp.bfloat16
    rows = o_ref.shape[0] // OUT_SUB
    subs = [slice(s * rows, (s + 1) * rows) for s in range(OUT_SUB)]
    h = [x_ref[r, :]
         + jnp.dot(yh_ref[r, :], woh_ref[...], preferred_element_type=f32)
         + jnp.dot(yg_ref[r, :], wog_ref[...], preferred_element_type=f32) for r in subs]
    e = [_rms(jnp.dot(p_ref[r, :].astype(bf), pp_ref[...], preferred_element_type=f32), pnw_ref[...])
         for r in subs]
    hn = [_rms(hs, gnw_ref[...]).astype(bf) for hs in h]
    acc = [jnp.dot(hs, gw_ref[...], preferred_element_type=f32) for hs in hn]
    for r, hs, es, a in zip(subs, h, e, acc):
        o_ref[r, :] = _rms(hs + _sigmoid(a + gb_ref[...]) * es, fnw_ref[...])


def _outproj(x2, yh, yg, p2, wo, pp, pnw, gnw, gw, gb, fnw, tm=512):
    T, D = x2.shape
    assert T % tm == 0 and tm % OUT_SUB == 0
    row = lambda w: pl.BlockSpec((tm, w), lambda i: (i, 0))
    once = pl.Buffered(1)
    full = lambda a: pl.BlockSpec(a.shape, lambda i: (0, 0), pipeline_mode=once)
    return pl.pallas_call(
        _out_kernel,
        grid=(T // tm,),
        in_specs=[row(D), row(D_HGRN), row(D_GLA), row(PLE_DIM),
                  pl.BlockSpec((D_HGRN, D), lambda i: (0, 0), pipeline_mode=once),
                  pl.BlockSpec((D_GLA, D), lambda i: (D_HGRN // D_GLA, 0), pipeline_mode=once),
                  full(pp), full(pnw), full(gnw), full(gw), full(gb), full(fnw)],
        out_specs=row(D),
        out_shape=jax.ShapeDtypeStruct((T, D), jnp.float32),
        compiler_params=pltpu.CompilerParams(
            dimension_semantics=("parallel",),
            vmem_limit_bytes=OUT_VMEM_LIMIT),
        name="outproj_ple",
    )(x2, yh, yg, p2, wo, wo, pp, pnw, gnw, gw, gb, fnw)


def kernel(x, p, norm_mix_w, w_in, hgrn_lb, gla_gate_w2, gla_gate_b, hgrn_norm_w, gla_norm_w,
           w_out, ple_proj, ple_norm_w, ple_gate_norm_w, ple_gate_w, ple_gate_b, final_norm_w):
    B, T, D = x.shape
    bf = jnp.bfloat16
    x2 = x.reshape(B * T, D)
    p2 = p[0].reshape(B * T, PLE_DIM)
    row = lambda a: a.reshape(1, -1)

    w_t = jnp.swapaxes(w_in[0], 0, 1)
    wlr_t = jnp.pad(w_t[D_MAIN:], ((0, LANE - GLA_GATE_RANK), (0, 0)))
    nw = row(norm_mix_w[0])
    hf, xb, rs = _inproj_first(x2, nw, w_t, lambda j: j + 1, 1, (0,), jnp.float32, "inproj_f")
    pb = _inproj_rest(xb, rs, nw, w_t, lambda j: j + jnp.minimum(j, 1), 6, (0, 2, 5), bf,
                      "inproj_b")

    head_gain = jnp.broadcast_to(jnp.concatenate([hgrn_norm_w[0], gla_norm_w[0]])[:, None],
                                 (D_HGRN + D_GLA, LANE))
    w2_pad = jnp.pad(gla_gate_w2[0], ((0, LANE - GLA_GATE_RANK), (0, 0)))
    y_h, y_g, wo_bf, pgw_bf, pp_bf = _scans(hf, pb, hgrn_lb, w2_pad, row(gla_gate_b[0]), xb, rs, wlr_t, nw,
                                            w_out[0], head_gain, ple_gate_w[0], ple_proj[0])

    out = _outproj(x2, y_h, y_g, p2, wo_bf, pp_bf, row(ple_norm_w[0]), row(ple_gate_norm_w[0]), pgw_bf,
                   row(ple_gate_b[0]), row(final_norm_w))
    return out.reshape(B, T, D)
```

```python
import functools

import jax
import jax.numpy as jnp
from jax import lax
from jax.experimental import pallas as pl
from jax.experimental.pallas import tpu as pltpu

D_MODEL = 2048
D_HGRN = 1024
HGRN_HEADS = 8
HGRN_DK = 128
HGRN_DV = 128
GLA_HEADS = 4
GLA_DK = 128
GLA_DV = 256
GLA_DK_TOTAL = 512
D_GLA = 1024
GLA_GATE_RANK = 16
GLA_GATE_TAU = 16.0
PLE_DIM = 256
EPS = 1e-6

LANE = 128
SUBLANE = 8
D_MAIN = 4 * D_HGRN + 2 * GLA_DK_TOTAL + 2 * D_GLA
PANEL = 1024
CHUNK = 128
ATTN_BLOCK = 128
DECAY_RANGE_LIMIT = 110.0
LOG2E = 1.4426950408889634
SCAN_VMEM_LIMIT = 56 * 1024 * 1024
INPROJ_VMEM_LIMIT = 62 * 1024 * 1024

_NT = (((1,), (1,)), ((), ()))
_TN = (((0,), (0,)), ((), ()))


def _sigmoid(x):
    return 0.5 + 0.5 * jnp.tanh(0.5 * x)


def _silu_of_half(h):
    return h + h * jnp.tanh(h)


def _rms(x, w):
    return x * lax.rsqrt(jnp.mean(x * x, axis=-1, keepdims=True) + EPS) * w


def _is_half_panel(half_panels):
    return functools.reduce(jnp.logical_or, [pl.program_id(0) == jp for jp in half_panels])


def _cast_weight_panel(w_ref, nw_ref, wb_ref):
    @pl.when(pl.program_id(1) == 0)
    def _():
        wb_ref[...] = (w_ref[...] * nw_ref[...]).astype(jnp.bfloat16)


def _inproj_first_kernel(x_ref, nw_ref, w_ref, wlr_ref, o_ref, xb_ref, rs_ref, lr_ref, wb_ref, wlrb_ref, *,
                         half_panels):
    bf = jnp.bfloat16
    _cast_weight_panel(w_ref, nw_ref, wb_ref)

    _cast_weight_panel(wlr_ref, nw_ref, wlrb_ref)

    def body(first_panel):
        x = x_ref[...]
        r = lax.rsqrt(jnp.mean(x * x, axis=-1, keepdims=True) + EPS)
        xb = x.astype(bf)
        o_ref[...] = (lax.dot_general(xb, wb_ref[...], _NT, preferred_element_type=jnp.float32)
                      * (r * jnp.where(_is_half_panel(half_panels), 0.5, 1.0))).astype(o_ref.dtype)
        if first_panel:
            xb_ref[...] = xb
            rs_ref[...] = jnp.broadcast_to(r, rs_ref.shape)
            lr_ref[...] = (lax.dot_general(xb, wlrb_ref[...], _NT, preferred_element_type=jnp.float32)
                           * r).astype(bf)

    first = pl.program_id(0) == 0
    pl.when(first)(lambda: body(True))
    pl.when(jnp.logical_not(first))(lambda: body(False))


def _inproj_rest_kernel(xb_ref, rs_ref, nw_ref, w_ref, o_ref, wb_ref, *, half_panels):
    _cast_weight_panel(w_ref, nw_ref, wb_ref)
    r = rs_ref[:, 0:1] * jnp.where(_is_half_panel(half_panels), 0.5, 1.0)
    o_ref[...] = (lax.dot_general(xb_ref[...], wb_ref[...], _NT, preferred_element_type=jnp.float32)
                  * r).astype(o_ref.dtype)


def _inproj_params():
    return pltpu.CompilerParams(dimension_semantics=("parallel", "arbitrary"),
                                vmem_limit_bytes=INPROJ_VMEM_LIMIT)


def _first_panel_only(n_rows):
    return lambda j, i: (jnp.where(j == 0, i, n_rows - 1), 0)


def _inproj_first(x2, norm_w, w_t, wlr_t, panel_of, n_panels, half_panels, out_dtype, name, tm=1024):
    T, D = x2.shape
    assert T % tm == 0 and wlr_t.shape == (LANE, D)
    n_rows = T // tm
    return pl.pallas_call(
        functools.partial(_inproj_first_kernel, half_panels=half_panels),
        grid=(n_panels, n_rows),
        in_specs=[pl.BlockSpec((tm, D), lambda j, i: (i, 0)),
                  pl.BlockSpec((1, D), lambda j, i: (0, 0)),
                  pl.BlockSpec((PANEL, D), lambda j, i: (panel_of(j), 0)),
                  pl.BlockSpec((LANE, D), lambda j, i: (0, 0))],
        out_specs=[pl.BlockSpec((tm, PANEL), lambda j, i: (i, j)),
                   pl.BlockSpec((tm, D), _first_panel_only(n_rows)),
                   pl.BlockSpec((tm, LANE), _first_panel_only(n_rows)),
                   pl.BlockSpec((tm, LANE), _first_panel_only(n_rows))],
        out_shape=[jax.ShapeDtypeStruct((T, n_panels * PANEL), out_dtype),
                   jax.ShapeDtypeStruct((T, D), jnp.bfloat16),
                   jax.ShapeDtypeStruct((T, LANE), jnp.float32),
                   jax.ShapeDtypeStruct((T, LANE), jnp.bfloat16)],
        scratch_shapes=[pltpu.VMEM((PANEL, D), jnp.bfloat16),
                        pltpu.VMEM((LANE, D), jnp.bfloat16)],
        compiler_params=_inproj_params(),
        name=name,
    )(x2, norm_w, w_t, wlr_t)


def _inproj_rest(xb, rs, norm_w, w_t, panel_of, n_panels, half_panels, out_dtype, name, tm=2048):
    T, D = xb.shape
    assert T % tm == 0
    return pl.pallas_call(
        functools.partial(_inproj_rest_kernel, half_panels=half_panels),
        grid=(n_panels, T // tm),
        in_specs=[pl.BlockSpec((tm, D), lambda j, i: (i, 0)),
                  pl.BlockSpec((tm, LANE), lambda j, i: (i, 0)),
                  pl.BlockSpec((1, D), lambda j, i: (0, 0)),
                  pl.BlockSpec((PANEL, D), lambda j, i: (panel_of(j), 0))],
        out_specs=pl.BlockSpec((tm, PANEL), lambda j, i: (i, j)),
        out_shape=jax.ShapeDtypeStruct((T, n_panels * PANEL), out_dtype),
        scratch_shapes=[pltpu.VMEM((PANEL, D), jnp.bfloat16)],
        compiler_params=_inproj_params(),
        name=name,
    )(xb, rs, norm_w, w_t)


def _scan_stages(qkg, v, state, finish):
    bf = jnp.bfloat16
    f32 = jnp.float32
    dv, dk = state.shape
    blk = ATTN_BLOCK
    row = lax.broadcasted_iota(jnp.int32, (blk, blk), 0)
    col = lax.broadcasted_iota(jnp.int32, (blk, blk), 1)
    causal = (col <= row) & (col >= (row & -CHUNK))
    res = {}

    def gates0():
        res['qkg'] = qkg()

    def gates():
        q, k, g = res.pop('qkg')
        g_hi = g.astype(bf)
        g_lo = (g - g_hi.astype(f32)).astype(bf)
        tri = causal.astype(bf)
        tri2 = jnp.concatenate([tri, tri], axis=1)
        res.update(q=q, k=k, n_blk=g.shape[0] // blk, n_chk=g.shape[0] // CHUNK)
        res['b'] = [jnp.dot(tri2, jnp.concatenate([g_hi[r * blk:(r + 1) * blk],
                                                     g_lo[r * blk:(r + 1) * blk]], axis=0),
                            preferred_element_type=f32) for r in range(res['n_blk'])]

    def decay():
        b = jnp.concatenate(res.pop('b'), axis=0)
        eb = jnp.exp2(b)
        q, k = res.pop('q'), res.pop('k')
        res['qh'] = q * eb.astype(bf) if q.dtype == bf else (q * eb).astype(bf)
        res['kt'] = k * (1.0 / eb).astype(bf) if k.dtype == bf else (k / eb).astype(bf)
        res['vb'] = v().astype(bf)
        res['eb_last'] = [eb[(c + 1) * CHUNK - 1:(c + 1) * CHUNK, :] for c in range(res['n_chk'])]
        res['b_min'] = jnp.min(jnp.concatenate(
            [b[(c + 1) * CHUNK - 1:(c + 1) * CHUNK, :] for c in range(res['n_chk'])], axis=0))

    def scores():
        qh, kt, vb = res['qh'], res['kt'], res['vb']
        res['attn'] = [lax.dot_general(qh[r * blk:(r + 1) * blk], kt[r * blk:(r + 1) * blk], _NT,
                                       preferred_element_type=f32) for r in range(res['n_blk'])]
        res['upd'] = [lax.dot_general(vb[c * CHUNK:(c + 1) * CHUNK], kt[c * CHUNK:(c + 1) * CHUNK], _TN,
                                      preferred_element_type=f32) for c in range(res['n_chk'])]

    def carry():
        res['attn'] = [jnp.where(causal, a, 0.0).astype(bf) for a in res['attn']]
        st, st_in = state, []
        for upd, eb_last in zip(res.pop('upd'), res.pop('eb_last')):
            st_in.append(st.astype(bf))
            st = (st + upd) * eb_last
        res.update(st_in=st_in, state=st)

    def outputs():
        qh, vb = res.pop('qh'), res.pop('vb')
        res['intra'] = [jnp.dot(a, vb[r * blk:(r + 1) * blk], preferred_element_type=f32)
                        for r, a in enumerate(res.pop('attn'))]
        res['inter'] = [lax.dot_general(qh[c * CHUNK:(c + 1) * CHUNK], s, _NT, preferred_element_type=f32)
                        for c, s in enumerate(res.pop('st_in'))]

    def done():
        finish(jnp.concatenate(res.pop('intra'), axis=0) + jnp.concatenate(res.pop('inter'), axis=0))

    return [gates0, gates, decay, scores, carry, outputs, done], res


def _scan_tile_exact(qkg_ref, v_ref, st_ref, o_ref):
    bf = jnp.bfloat16
    f32 = jnp.float32
    n_heads, dv, dk = st_ref.shape
    tt = o_ref.shape[0]
    rowk = lax.broadcasted_iota(jnp.int32, (CHUNK, dk), 0)
    rowc = lax.broadcasted_iota(jnp.int32, (CHUNK, CHUNK), 0)
    colc = lax.broadcasted_iota(jnp.int32, (CHUNK, CHUNK), 1)
    tri = (colc <= rowc).astype(bf)

    for h in range(n_heads):
        ks = slice(h * dk, (h + 1) * dk)
        vs = slice(h * dv, (h + 1) * dv)

        def chunk(c, carry, ks=ks, vs=vs, h=h):
            r0 = pl.multiple_of(c * CHUNK, CHUNK)
            rows = pl.ds(r0, CHUNK)
            qc = qkg_ref[0, rows, ks]
            kc = qkg_ref[1, rows, ks]
            gc = qkg_ref[2, rows, ks]
            g_hi = gc.astype(bf)
            g_lo = (gc - g_hi.astype(f32)).astype(bf)
            b = (jnp.dot(tri, g_hi, preferred_element_type=f32)
                 + jnp.dot(tri, g_lo, preferred_element_type=f32))
            qkg_ref[2, rows, ks] = b

            def columns(jg, attn):
                j0 = pl.multiple_of(jg * SUBLANE, SUBLANE)
                b_rows = qkg_ref[2, pl.ds(r0 + j0, SUBLANE), ks]
                k_rows = qkg_ref[1, pl.ds(r0 + j0, SUBLANE), ks]
                for r in range(SUBLANE):
                    j = j0 + r
                    decay = jnp.exp2(jnp.where(rowk >= j, b - b_rows[r:r + 1], 0.0))
                    colv = jnp.sum(qc * k_rows[r:r + 1] * decay, axis=1, keepdims=True)
                    attn = jnp.where(colc == j, colv, attn)
                return attn

            attn = lax.fori_loop(0, CHUNK // SUBLANE, columns, jnp.zeros((CHUNK, CHUNK), f32))
            attn = jnp.where(colc <= rowc, attn, 0.0).astype(bf)
            vb = v_ref[rows, vs].astype(bf)
            st = st_ref[h]
            b_last = b[CHUNK - 1:CHUNK, :]
            o_ref[rows, vs] = (
                jnp.dot(attn, vb, preferred_element_type=f32)
                + lax.dot_general((qc * jnp.exp2(b)).astype(bf), st.astype(bf), _NT,
                                  preferred_element_type=f32))
            k_dec = (kc * jnp.exp2(b_last - b)).astype(bf)
            st_ref[h] = st * jnp.exp2(b_last) + lax.dot_general(vb, k_dec, _TN,
                                                                preferred_element_type=f32)
            return carry

        lax.fori_loop(0, tt // CHUNK, chunk, 0)


def _readout(o, z, n_heads, dk):
    dv = o.shape[1] // n_heads

    def head(oh):
        return oh * lax.rsqrt(jnp.mean(oh * oh, axis=-1, keepdims=True) + EPS * dk)

    on = jnp.concatenate([head(o[:, h * dv:(h + 1) * dv]) for h in range(n_heads)], axis=1)
    return on.astype(z.dtype) * _silu_of_half(z)


def _scan_and_readout(groups, qkg_ref, o_ref, side_work=None):
    @pl.when(pl.program_id(1) == 0)
    def _():
        for *_, st_ref in groups:
            st_ref[...] = jnp.zeros_like(st_ref)

    if side_work is not None:
        side_work()
    units = []
    for qkg, v_ref, z_ref, y_ref, st_ref in groups:
        n_heads, dv, dk = st_ref.shape
        for h in range(n_heads):
            ks = slice(h * dk, (h + 1) * dk)
            vs = slice(h * dv, (h + 1) * dv)

            def finish(o, vs=vs, y_ref=y_ref, z_ref=z_ref, dk=dk):
                y_ref[:, vs] = _readout(o, z_ref[:, vs], 1, dk).astype(y_ref.dtype)

            units.append(_scan_stages(lambda qkg=qkg, ks=ks: qkg(ks), lambda v_ref=v_ref, vs=vs: v_ref[:, vs],
                                      st_ref[h], finish))
    n_stages = len(units[0][0])
    for tick in range(len(units) + n_stages - 1):
        for u in range(len(units)):
            if 0 <= tick - u < n_stages:
                units[u][0][tick - u]()
    b_min = units[0][1]['b_min']
    for _, res in units[1:]:
        b_min = jnp.minimum(b_min, res['b_min'])
    in_range = b_min >= -DECAY_RANGE_LIMIT

    @pl.when(in_range)
    def _():
        new_states = iter(res['state'] for _, res in units)
        for *_, st_ref in groups:
            for h in range(st_ref.shape[0]):
                st_ref[h] = next(new_states)

    @pl.when(jnp.logical_not(in_range))
    def _():
        for qkg, v_ref, z_ref, y_ref, st_ref in groups:
            n_heads, dv, dk = st_ref.shape
            q, k, g = qkg(slice(0, n_heads * dk))
            qkg_ref[0, :, :n_heads * dk] = q.astype(jnp.float32)
            qkg_ref[1, :, :n_heads * dk] = k.astype(jnp.float32)
            qkg_ref[2, :, :n_heads * dk] = g
            _scan_tile_exact(qkg_ref, v_ref, st_ref, o_ref)
            y_ref[...] = _readout(o_ref[:, :n_heads * dv], z_ref[...], n_heads, dk).astype(y_ref.dtype)


def _scan_kernel(lb_ref, hq_ref, hf_ref, hi_ref, hz_ref, gq_ref, gk_ref, gv_ref, gz_ref, w2_ref, gb_ref,
                 lr_ref, wo_ref, hg_ref, pgw_ref, pp_ref,
                 yh_ref, yg_ref, wo_bf_ref, pgw_bf_ref, pp_bf_ref, sth_ref, stg_ref, qkg_ref, o_ref):
    bf = jnp.bfloat16

    def side_work():
        wo_bf_ref[...] = (wo_ref[...] * hg_ref[:, 0:1]).astype(bf)
        pgw_bf_ref[...] = pgw_ref[...].astype(bf)
        pp_bf_ref[...] = pp_ref[...].astype(bf)

    def hgrn_qkg(cols):
        lbp = lb_ref[:, cols]
        e = jnp.exp(lbp - jnp.max(lbp, axis=0, keepdims=True))
        lb = e[0:1, :] / jnp.sum(e, axis=0, keepdims=True)
        f = (0.5 + 0.5 * lb) + (0.5 - 0.5 * lb) * jnp.tanh(hf_ref[:, cols])
        return _silu_of_half(hq_ref[:, cols]), 1.0 - f, jnp.log2(f)

    def gla_qkg(cols):
        logit = jnp.dot(lr_ref[...], w2_ref[:, cols].astype(bf),
                        preferred_element_type=jnp.float32) + gb_ref[:, cols]
        log2_sig = (jnp.minimum(logit, 0.0) * LOG2E
                    - jnp.log2(1.0 + jnp.exp2(jnp.abs(logit) * -LOG2E)))
        return gq_ref[:, cols], gk_ref[:, cols], log2_sig / GLA_GATE_TAU

    _scan_and_readout([(hgrn_qkg, hi_ref, hz_ref, yh_ref, sth_ref),
                       (gla_qkg, gv_ref, gz_ref, yg_ref, stg_ref)], qkg_ref, o_ref, side_work)


def _row_slices(n):
    def spec(a):
        assert a.shape[0] % (16 * n) == 0
        return pl.BlockSpec((a.shape[0] // n, a.shape[1]), lambda h, t: (t, 0))
    return spec


def _scans(hf, pb, hgrn_lb, w2_pad, gate_b, lr, w_out, head_gain, pgw, pp, tt=512):
    T = lr.shape[0]
    assert T % tt == 0 and tt % ATTN_BLOCK == 0 and ATTN_BLOCK % CHUNK == 0
    n = T // tt
    panel = lambda c: pl.BlockSpec((tt, PANEL), lambda h, t: (t, c))
    half = lambda c: pl.BlockSpec((tt, GLA_DK_TOTAL), lambda h, t: (t, c))
    rows = lambda width: pl.BlockSpec((tt, width), lambda h, t: (t, 0))
    const = lambda a: pl.BlockSpec(a.shape, lambda h, t: (0, 0))
    piece = _row_slices(n)
    gq0 = 3 * PANEL // GLA_DK_TOTAL
    return pl.pallas_call(
        _scan_kernel,
        grid=(1, n),
        in_specs=[const(hgrn_lb), panel(0), panel(0), panel(1), panel(2),
                  half(gq0), half(gq0 + 1), panel(4), panel(5), const(w2_pad), const(gate_b),
                  rows(LANE), piece(w_out), piece(head_gain), piece(pgw), piece(pp)],
        out_specs=[rows(D_HGRN), rows(D_GLA), piece(w_out), piece(pgw), piece(pp)],
        out_shape=[jax.ShapeDtypeStruct((T, D_HGRN), jnp.bfloat16),
                   jax.ShapeDtypeStruct((T, D_GLA), jnp.bfloat16),
                   jax.ShapeDtypeStruct(w_out.shape, jnp.bfloat16),
                   jax.ShapeDtypeStruct(pgw.shape, jnp.bfloat16),
                   jax.ShapeDtypeStruct(pp.shape, jnp.bfloat16)],
        scratch_shapes=[pltpu.VMEM((HGRN_HEADS, HGRN_DV, HGRN_DK), jnp.float32),
                        pltpu.VMEM((GLA_HEADS, GLA_DV, GLA_DK), jnp.float32),
                        pltpu.VMEM((3, tt, D_HGRN), jnp.float32),
                        pltpu.VMEM((tt, max(D_HGRN, D_GLA)), jnp.float32)],
        compiler_params=pltpu.CompilerParams(dimension_semantics=("parallel", "arbitrary"),
                                             vmem_limit_bytes=SCAN_VMEM_LIMIT),
        name="scans",
    )(hgrn_lb, pb, hf, pb, pb, pb, pb, pb, pb, w2_pad, gate_b, lr, w_out, head_gain, pgw, pp)


OUT_SUB = 2
OUT_VMEM_LIMIT = 58 * 1024 * 1024


def _out_kernel(x_ref, yh_ref, yg_ref, p_ref, woh_ref, wog_ref, pp_ref, pnw_ref, gnw_ref,
                gw_ref, gb_ref, fnw_ref, o_ref):
    f32 = jnp.float32
    bf = jnp.bfloat16
    rows = o_ref.shape[0] // OUT_SUB
    subs = [slice(s * rows, (s + 1) * rows) for s in range(OUT_SUB)]
    h = [x_ref[r, :]
         + jnp.dot(yh_ref[r, :], woh_ref[...], preferred_element_type=f32)
         + jnp.dot(yg_ref[r, :], wog_ref[...], preferred_element_type=f32) for r in subs]
    e = [_rms(jnp.dot(p_ref[r, :].astype(bf), pp_ref[...], preferred_element_type=f32), pnw_ref[...])
         for r in subs]
    hn = [_rms(hs, gnw_ref[...]).astype(bf) for hs in h]
    acc = [jnp.dot(hs, gw_ref[...], preferred_element_type=f32) for hs in hn]
    for r, hs, es, a in zip(subs, h, e, acc):
        o_ref[r, :] = _rms(hs + _sigmoid(a + gb_ref[...]) * es, fnw_ref[...])


def _outproj(x2, yh, yg, p2, wo, pp, pnw, gnw, gw, gb, fnw, tm=512):
    T, D = x2.shape
    assert T % tm == 0 and tm % OUT_SUB == 0
    row = lambda w: pl.BlockSpec((tm, w), lambda i: (i, 0))
    once = pl.Buffered(1)
    full = lambda a: pl.BlockSpec(a.shape, lambda i: (0, 0), pipeline_mode=once)
    return pl.pallas_call(
        _out_kernel,
        grid=(T // tm,),
        in_specs=[row(D), row(D_HGRN), row(D_GLA), row(PLE_DIM),
                  pl.BlockSpec((D_HGRN, D), lambda i: (0, 0), pipeline_mode=once),
                  pl.BlockSpec((D_GLA, D), lambda i: (D_HGRN // D_GLA, 0), pipeline_mode=once),
                  full(pp), full(pnw), full(gnw), full(gw), full(gb), full(fnw)],
        out_specs=row(D),
        out_shape=jax.ShapeDtypeStruct((T, D), jnp.float32),
        compiler_params=pltpu.CompilerParams(
            dimension_semantics=("parallel",),
            vmem_limit_bytes=OUT_VMEM_LIMIT),
        name="outproj_ple",
    )(x2, yh, yg, p2, wo, wo, pp, pnw, gnw, gw, gb, fnw)


def kernel(x, p, norm_mix_w, w_in, hgrn_lb, gla_gate_w2, gla_gate_b, hgrn_norm_w, gla_norm_w,
           w_out, ple_proj, ple_norm_w, ple_gate_norm_w, ple_gate_w, ple_gate_b, final_norm_w):
    B, T, D = x.shape
    bf = jnp.bfloat16
    x2 = x.reshape(B * T, D)
    p2 = p[0].reshape(B * T, PLE_DIM)
    row = lambda a: a.reshape(1, -1)

    w_t = jnp.swapaxes(w_in[0], 0, 1)
    wlr_t = jnp.pad(w_t[D_MAIN:], ((0, LANE - GLA_GATE_RANK), (0, 0)))
    nw = row(norm_mix_w[0])
    hf, xb, rs, lr = _inproj_first(x2, nw, w_t, wlr_t, lambda j: j + 1, 1, (0,), jnp.float32, "inproj_f")
    pb = _inproj_rest(xb, rs, nw, w_t, lambda j: j + jnp.minimum(j, 1), 6, (0, 2, 5), bf,
                      "inproj_b")

    head_gain = jnp.broadcast_to(jnp.concatenate([hgrn_norm_w[0], gla_norm_w[0]])[:, None],
                                 (D_HGRN + D_GLA, LANE))
    w2_pad = jnp.pad(gla_gate_w2[0], ((0, LANE - GLA_GATE_RANK), (0, 0)))
    y_h, y_g, wo_bf, pgw_bf, pp_bf = _scans(hf, pb, hgrn_lb, w2_pad, row(gla_gate_b[0]), lr,
                                            w_out[0], head_gain, ple_gate_w[0], ple_proj[0])

    out = _outproj(x2, y_h, y_g, p2, wo_bf, pp_bf, row(ple_norm_w[0]), row(ple_gate_norm_w[0]), pgw_bf,
                   row(ple_gate_b[0]), row(final_norm_w))
    return out.reshape(B, T, D)
```

```python
import functools

import jax
import jax.numpy as jnp
from jax import lax
from jax.experimental import pallas as pl
from jax.experimental.pallas import tpu as pltpu

D_MODEL = 2048
D_HGRN = 1024
HGRN_HEADS = 8
HGRN_DK = 128
HGRN_DV = 128
GLA_HEADS = 4
GLA_DK = 128
GLA_DV = 256
GLA_DK_TOTAL = 512
D_GLA = 1024
GLA_GATE_RANK = 16
GLA_GATE_TAU = 16.0
PLE_DIM = 256
EPS = 1e-6

LANE = 128
SUBLANE = 8
D_MAIN = 4 * D_HGRN + 2 * GLA_DK_TOTAL + 2 * D_GLA
PANEL = 1024
CHUNK = 128
ATTN_BLOCK = 128
DECAY_RANGE_LIMIT = 110.0
LOG2E = 1.4426950408889634
SCAN_VMEM_LIMIT = 56 * 1024 * 1024
INPROJ_VMEM_LIMIT = 62 * 1024 * 1024

_NT = (((1,), (1,)), ((), ()))
_TN = (((0,), (0,)), ((), ()))


def _sigmoid(x):
    return 0.5 + 0.5 * jnp.tanh(0.5 * x)


def _silu_of_half(h):
    return h + h * jnp.tanh(h)


def _rms(x, w):
    return x * lax.rsqrt(jnp.mean(x * x, axis=-1, keepdims=True) + EPS) * w


def _is_half_panel(half_panels):
    return functools.reduce(jnp.logical_or, [pl.program_id(0) == jp for jp in half_panels])


def _cast_weight_panel(w_ref, nw_ref, wb_ref):
    @pl.when(pl.program_id(1) == 0)
    def _():
        wb_ref[...] = (w_ref[...] * nw_ref[...]).astype(jnp.bfloat16)


X_SLOTS = 3


def _inproj_first_kernel(x_hbm, nw_ref, w_ref, o_ref, xb_ref, rs_ref, wb_ref, xbuf_ref, sem, *, scale):
    bf = jnp.bfloat16
    i = pl.program_id(0)
    n = pl.num_programs(0)
    tm = xbuf_ref.shape[1]
    ahead = X_SLOTS - 1

    def tile_copy(step):
        row0 = step * tm if isinstance(step, int) else pl.multiple_of(step * tm, tm)
        slot = step % X_SLOTS
        return pltpu.make_async_copy(x_hbm.at[pl.ds(row0, tm), :], xbuf_ref.at[slot], sem.at[slot])

    @pl.when(i == 0)
    def _():
        for step in range(ahead):
            tile_copy(step).start()
        wb_ref[...] = (w_ref[...] * nw_ref[...]).astype(bf)

    @pl.when(i + ahead < n)
    def _():
        tile_copy(i + ahead).start()

    tile_copy(i).wait()
    x = xbuf_ref[i % X_SLOTS]
    r = lax.rsqrt(jnp.mean(x * x, axis=-1, keepdims=True) + EPS)
    xb = x.astype(bf)
    o_ref[...] = lax.dot_general(xb, wb_ref[...], _NT, preferred_element_type=jnp.float32) * (r * scale)
    xb_ref[...] = xb
    rs_ref[...] = jnp.broadcast_to(r, rs_ref.shape)


def _inproj_rest_kernel(xb_ref, rs_ref, nw_ref, w_ref, o_ref, wb_ref, *, half_panels):
    _cast_weight_panel(w_ref, nw_ref, wb_ref)
    r = rs_ref[:, 0:1] * jnp.where(_is_half_panel(half_panels), 0.5, 1.0)
    o_ref[...] = (lax.dot_general(xb_ref[...], wb_ref[...], _NT, preferred_element_type=jnp.float32)
                  * r).astype(o_ref.dtype)


def _inproj_params():
    return pltpu.CompilerParams(dimension_semantics=("parallel", "arbitrary"),
                                vmem_limit_bytes=INPROJ_VMEM_LIMIT)


def _inproj_first(x2, norm_w, w_t, panel, scale, name, tm=1024):
    T, D = x2.shape
    assert T % tm == 0 and T // tm >= X_SLOTS - 1
    return pl.pallas_call(
        functools.partial(_inproj_first_kernel, scale=scale),
        grid=(T // tm,),
        in_specs=[pl.BlockSpec(memory_space=pl.ANY),
                  pl.BlockSpec((1, D), lambda i: (0, 0)),
                  pl.BlockSpec((PANEL, D), lambda i: (panel, 0), pipeline_mode=pl.Buffered(1))],
        out_specs=[pl.BlockSpec((tm, PANEL), lambda i: (i, 0)),
                   pl.BlockSpec((tm, D), lambda i: (i, 0)),
                   pl.BlockSpec((tm, LANE), lambda i: (i, 0))],
        out_shape=[jax.ShapeDtypeStruct((T, PANEL), jnp.float32),
                   jax.ShapeDtypeStruct((T, D), jnp.bfloat16),
                   jax.ShapeDtypeStruct((T, LANE), jnp.float32)],
        scratch_shapes=[pltpu.VMEM((PANEL, D), jnp.bfloat16),
                        pltpu.VMEM((X_SLOTS, tm, D), jnp.float32),
                        pltpu.SemaphoreType.DMA((X_SLOTS,))],
        compiler_params=pltpu.CompilerParams(dimension_semantics=("arbitrary",),
                                             vmem_limit_bytes=INPROJ_VMEM_LIMIT),
        name=name,
    )(x2, norm_w, w_t)


def _inproj_rest(xb, rs, norm_w, w_t, panel_of, n_panels, half_panels, out_dtype, name, tm=2048):
    T, D = xb.shape
    assert T % tm == 0
    return pl.pallas_call(
        functools.partial(_inproj_rest_kernel, half_panels=half_panels),
        grid=(n_panels, T // tm),
        in_specs=[pl.BlockSpec((tm, D), lambda j, i: (i, 0)),
                  pl.BlockSpec((tm, LANE), lambda j, i: (i, 0)),
                  pl.BlockSpec((1, D), lambda j, i: (0, 0)),
                  pl.BlockSpec((PANEL, D), lambda j, i: (panel_of(j), 0))],
        out_specs=pl.BlockSpec((tm, PANEL), lambda j, i: (i, j)),
        out_shape=jax.ShapeDtypeStruct((T, n_panels * PANEL), out_dtype),
        scratch_shapes=[pltpu.VMEM((PANEL, D), jnp.bfloat16)],
        compiler_params=_inproj_params(),
        name=name,
    )(xb, rs, norm_w, w_t)


def _scan_stages(qkg, v, state, finish):
    bf = jnp.bfloat16
    f32 = jnp.float32
    dv, dk = state.shape
    blk = ATTN_BLOCK
    row = lax.broadcasted_iota(jnp.int32, (blk, blk), 0)
    col = lax.broadcasted_iota(jnp.int32, (blk, blk), 1)
    causal = (col <= row) & (col >= (row & -CHUNK))
    res = {}

    def gates0():
        res['qkg'] = qkg()

    def gates():
        q, k, g = res.pop('qkg')
        g_hi = g.astype(bf)
        g_lo = (g - g_hi.astype(f32)).astype(bf)
        tri = causal.astype(bf)
        tri2 = jnp.concatenate([tri, tri], axis=1)
        res.update(q=q, k=k, n_blk=g.shape[0] // blk, n_chk=g.shape[0] // CHUNK)
        res['b'] = [jnp.dot(tri2, jnp.concatenate([g_hi[r * blk:(r + 1) * blk],
                                                     g_lo[r * blk:(r + 1) * blk]], axis=0),
                            preferred_element_type=f32) for r in range(res['n_blk'])]

    def decay():
        b = jnp.concatenate(res.pop('b'), axis=0)
        eb = jnp.exp2(b)
        q, k = res.pop('q'), res.pop('k')
        res['qh'] = q * eb.astype(bf) if q.dtype == bf else (q * eb).astype(bf)
        res['kt'] = k * (1.0 / eb).astype(bf) if k.dtype == bf else (k / eb).astype(bf)
        res['vb'] = v().astype(bf)
        res['eb_last'] = [eb[(c + 1) * CHUNK - 1:(c + 1) * CHUNK, :] for c in range(res['n_chk'])]
        res['b_min'] = jnp.min(jnp.concatenate(
            [b[(c + 1) * CHUNK - 1:(c + 1) * CHUNK, :] for c in range(res['n_chk'])], axis=0))

    def scores():
        qh, kt, vb = res['qh'], res['kt'], res['vb']
        res['attn'] = [lax.dot_general(qh[r * blk:(r + 1) * blk], kt[r * blk:(r + 1) * blk], _NT,
                                       preferred_element_type=f32) for r in range(res['n_blk'])]
        res['upd'] = [lax.dot_general(vb[c * CHUNK:(c + 1) * CHUNK], kt[c * CHUNK:(c + 1) * CHUNK], _TN,
                                      preferred_element_type=f32) for c in range(res['n_chk'])]

    def carry():
        res['attn'] = [jnp.where(causal, a, 0.0).astype(bf) for a in res['attn']]
        st, st_in = state, []
        for upd, eb_last in zip(res.pop('upd'), res.pop('eb_last')):
            st_in.append(st.astype(bf))
            st = (st + upd) * eb_last
        res.update(st_in=st_in, state=st)

    def outputs():
        qh, vb = res.pop('qh'), res.pop('vb')
        res['intra'] = [jnp.dot(a, vb[r * blk:(r + 1) * blk], preferred_element_type=f32)
                        for r, a in enumerate(res.pop('attn'))]
        res['inter'] = [lax.dot_general(qh[c * CHUNK:(c + 1) * CHUNK], s, _NT, preferred_element_type=f32)
                        for c, s in enumerate(res.pop('st_in'))]

    def done():
        finish(jnp.concatenate(res.pop('intra'), axis=0) + jnp.concatenate(res.pop('inter'), axis=0))

    return [gates0, gates, decay, scores, carry, outputs, done], res


def _scan_tile_exact(qkg_ref, v_ref, st_ref, o_ref):
    bf = jnp.bfloat16
    f32 = jnp.float32
    n_heads, dv, dk = st_ref.shape
    tt = o_ref.shape[0]
    rowk = lax.broadcasted_iota(jnp.int32, (CHUNK, dk), 0)
    rowc = lax.broadcasted_iota(jnp.int32, (CHUNK, CHUNK), 0)
    colc = lax.broadcasted_iota(jnp.int32, (CHUNK, CHUNK), 1)
    tri = (colc <= rowc).astype(bf)

    for h in range(n_heads):
        ks = slice(h * dk, (h + 1) * dk)
        vs = slice(h * dv, (h + 1) * dv)

        def chunk(c, carry, ks=ks, vs=vs, h=h):
            r0 = pl.multiple_of(c * CHUNK, CHUNK)
            rows = pl.ds(r0, CHUNK)
            qc = qkg_ref[0, rows, ks]
            kc = qkg_ref[1, rows, ks]
            gc = qkg_ref[2, rows, ks]
            g_hi = gc.astype(bf)
            g_lo = (gc - g_hi.astype(f32)).astype(bf)
            b = (jnp.dot(tri, g_hi, preferred_element_type=f32)
                 + jnp.dot(tri, g_lo, preferred_element_type=f32))
            qkg_ref[2, rows, ks] = b

            def columns(jg, attn):
                j0 = pl.multiple_of(jg * SUBLANE, SUBLANE)
                b_rows = qkg_ref[2, pl.ds(r0 + j0, SUBLANE), ks]
                k_rows = qkg_ref[1, pl.ds(r0 + j0, SUBLANE), ks]
                for r in range(SUBLANE):
                    j = j0 + r
                    decay = jnp.exp2(jnp.where(rowk >= j, b - b_rows[r:r + 1], 0.0))
                    colv = jnp.sum(qc * k_rows[r:r + 1] * decay, axis=1, keepdims=True)
                    attn = jnp.where(colc == j, colv, attn)
                return attn

            attn = lax.fori_loop(0, CHUNK // SUBLANE, columns, jnp.zeros((CHUNK, CHUNK), f32))
            attn = jnp.where(colc <= rowc, attn, 0.0).astype(bf)
            vb = v_ref[rows, vs].astype(bf)
            st = st_ref[h]
            b_last = b[CHUNK - 1:CHUNK, :]
            o_ref[rows, vs] = (
                jnp.dot(attn, vb, preferred_element_type=f32)
                + lax.dot_general((qc * jnp.exp2(b)).astype(bf), st.astype(bf), _NT,
                                  preferred_element_type=f32))
            k_dec = (kc * jnp.exp2(b_last - b)).astype(bf)
            st_ref[h] = st * jnp.exp2(b_last) + lax.dot_general(vb, k_dec, _TN,
                                                                preferred_element_type=f32)
            return carry

        lax.fori_loop(0, tt // CHUNK, chunk, 0)


def _readout(o, z, n_heads, dk):
    dv = o.shape[1] // n_heads

    def head(oh):
        return oh * lax.rsqrt(jnp.mean(oh * oh, axis=-1, keepdims=True) + EPS * dk)

    on = jnp.concatenate([head(o[:, h * dv:(h + 1) * dv]) for h in range(n_heads)], axis=1)
    return on.astype(z.dtype) * _silu_of_half(z)


def _scan_and_readout(groups, qkg_ref, o_ref, side_work=None):
    @pl.when(pl.program_id(1) == 0)
    def _():
        for *_, st_ref in groups:
            st_ref[...] = jnp.zeros_like(st_ref)

    if side_work is not None:
        side_work()
    units = []
    for qkg, v_ref, z_ref, y_ref, st_ref in groups:
        n_heads, dv, dk = st_ref.shape
        for h in range(n_heads):
            ks = slice(h * dk, (h + 1) * dk)
            vs = slice(h * dv, (h + 1) * dv)

            def finish(o, vs=vs, y_ref=y_ref, z_ref=z_ref, dk=dk):
                y_ref[:, vs] = _readout(o, z_ref[:, vs], 1, dk).astype(y_ref.dtype)

            units.append(_scan_stages(lambda qkg=qkg, ks=ks: qkg(ks), lambda v_ref=v_ref, vs=vs: v_ref[:, vs],
                                      st_ref[h], finish))
    n_stages = len(units[0][0])
    for tick in range(len(units) + n_stages - 1):
        for u in range(len(units)):
            if 0 <= tick - u < n_stages:
                units[u][0][tick - u]()
    b_min = units[0][1]['b_min']
    for _, res in units[1:]:
        b_min = jnp.minimum(b_min, res['b_min'])
    in_range = b_min >= -DECAY_RANGE_LIMIT

    @pl.when(in_range)
    def _():
        new_states = iter(res['state'] for _, res in units)
        for *_, st_ref in groups:
            for h in range(st_ref.shape[0]):
                st_ref[h] = next(new_states)

    @pl.when(jnp.logical_not(in_range))
    def _():
        for qkg, v_ref, z_ref, y_ref, st_ref in groups:
            n_heads, dv, dk = st_ref.shape
            q, k, g = qkg(slice(0, n_heads * dk))
            qkg_ref[0, :, :n_heads * dk] = q.astype(jnp.float32)
            qkg_ref[1, :, :n_heads * dk] = k.astype(jnp.float32)
            qkg_ref[2, :, :n_heads * dk] = g
            _scan_tile_exact(qkg_ref, v_ref, st_ref, o_ref)
            y_ref[...] = _readout(o_ref[:, :n_heads * dv], z_ref[...], n_heads, dk).astype(y_ref.dtype)


def _scan_kernel(lb_ref, hq_ref, hf_ref, hi_ref, hz_ref, gq_ref, gk_ref, gv_ref, gz_ref, w2_ref, gb_ref,
                 xb_ref, rs_ref, wlr_ref, nw_ref, wo_ref, hg_ref, pgw_ref, pp_ref,
                 yh_ref, yg_ref, wo_bf_ref, pgw_bf_ref, pp_bf_ref, sth_ref, stg_ref, qkg_ref, o_ref):
    bf = jnp.bfloat16
    side = {}

    def side_work():
        wlr = (wlr_ref[...] * nw_ref[...]).astype(bf)
        side['lr'] = (lax.dot_general(xb_ref[...], wlr, _NT, preferred_element_type=jnp.float32)
                      * rs_ref[:, 0:1]).astype(bf)
        wo_bf_ref[...] = (wo_ref[...] * hg_ref[:, 0:1]).astype(bf)
        pgw_bf_ref[...] = pgw_ref[...].astype(bf)
        pp_bf_ref[...] = pp_ref[...].astype(bf)

    def hgrn_qkg(cols):
        lbp = lb_ref[:, cols]
        e = jnp.exp(lbp - jnp.max(lbp, axis=0, keepdims=True))
        lb = e[0:1, :] / jnp.sum(e, axis=0, keepdims=True)
        f = (0.5 + 0.5 * lb) + (0.5 - 0.5 * lb) * jnp.tanh(hf_ref[:, cols])
        return _silu_of_half(hq_ref[:, cols]), 1.0 - f, jnp.log2(f)

    def gla_qkg(cols):
        logit = jnp.dot(side['lr'], w2_ref[:, cols].astype(bf),
                        preferred_element_type=jnp.float32) + gb_ref[:, cols]
        log2_sig = (jnp.minimum(logit, 0.0) * LOG2E
                    - jnp.log2(1.0 + jnp.exp2(jnp.abs(logit) * -LOG2E)))
        return gq_ref[:, cols], gk_ref[:, cols], log2_sig / GLA_GATE_TAU

    _scan_and_readout([(hgrn_qkg, hi_ref, hz_ref, yh_ref, sth_ref),
                       (gla_qkg, gv_ref, gz_ref, yg_ref, stg_ref)], qkg_ref, o_ref, side_work)


def _row_slices(n):
    def spec(a):
        assert a.shape[0] % (16 * n) == 0
        return pl.BlockSpec((a.shape[0] // n, a.shape[1]), lambda h, t: (t, 0))
    return spec


def _scans(hf, pb, hgrn_lb, w2_pad, gate_b, xb, rs, wlr_t, norm_w, w_out, head_gain, pgw, pp, tt=512):
    T, D = xb.shape
    assert T % tt == 0 and tt % ATTN_BLOCK == 0 and ATTN_BLOCK % CHUNK == 0
    n = T // tt
    panel = lambda c: pl.BlockSpec((tt, PANEL), lambda h, t: (t, c))
    half = lambda c: pl.BlockSpec((tt, GLA_DK_TOTAL), lambda h, t: (t, c))
    rows = lambda width: pl.BlockSpec((tt, width), lambda h, t: (t, 0))
    const = lambda a: pl.BlockSpec(a.shape, lambda h, t: (0, 0))
    piece = _row_slices(n)
    gq0 = 3 * PANEL // GLA_DK_TOTAL
    return pl.pallas_call(
        _scan_kernel,
        grid=(1, n),
        in_specs=[const(hgrn_lb), panel(0), panel(0), panel(1), panel(2),
                  half(gq0), half(gq0 + 1), panel(4), panel(5), const(w2_pad), const(gate_b),
                  rows(D), rows(LANE), const(wlr_t), const(norm_w),
                  piece(w_out), piece(head_gain), piece(pgw), piece(pp)],
        out_specs=[rows(D_HGRN), rows(D_GLA), piece(w_out), piece(pgw), piece(pp)],
        out_shape=[jax.ShapeDtypeStruct((T, D_HGRN), jnp.bfloat16),
                   jax.ShapeDtypeStruct((T, D_GLA), jnp.bfloat16),
                   jax.ShapeDtypeStruct(w_out.shape, jnp.bfloat16),
                   jax.ShapeDtypeStruct(pgw.shape, jnp.bfloat16),
                   jax.ShapeDtypeStruct(pp.shape, jnp.bfloat16)],
        scratch_shapes=[pltpu.VMEM((HGRN_HEADS, HGRN_DV, HGRN_DK), jnp.float32),
                        pltpu.VMEM((GLA_HEADS, GLA_DV, GLA_DK), jnp.float32),
                        pltpu.VMEM((3, tt, D_HGRN), jnp.float32),
                        pltpu.VMEM((tt, max(D_HGRN, D_GLA)), jnp.float32)],
        compiler_params=pltpu.CompilerParams(dimension_semantics=("parallel", "arbitrary"),
                                             vmem_limit_bytes=SCAN_VMEM_LIMIT),
        name="scans",
    )(hgrn_lb, pb, hf, pb, pb, pb, pb, pb, pb, w2_pad, gate_b, xb, rs, wlr_t, norm_w,
      w_out, head_gain, pgw, pp)


OUT_SUB = 2
OUT_VMEM_LIMIT = 58 * 1024 * 1024


def _out_kernel(x_ref, yh_ref, yg_ref, p_ref, woh_ref, wog_ref, pp_ref, pnw_ref, gnw_ref,
                gw_ref, gb_ref, fnw_ref, o_ref):
    f32 = jnp.float32
    bf = jnp.bfloat16
    rows = o_ref.shape[0] // OUT_SUB
    subs = [slice(s * rows, (s + 1) * rows) for s in range(OUT_SUB)]
    h = [x_ref[r, :]
         + jnp.dot(yh_ref[r, :], woh_ref[...], preferred_element_type=f32)
         + jnp.dot(yg_ref[r, :], wog_ref[...], preferred_element_type=f32) for r in subs]
    e = [_rms(jnp.dot(p_ref[r, :].astype(bf), pp_ref[...], preferred_element_type=f32), pnw_ref[...])
         for r in subs]
    hn = [_rms(hs, gnw_ref[...]).astype(bf) for hs in h]
    acc = [jnp.dot(hs, gw_ref[...], preferred_element_type=f32) for hs in hn]
    for r, hs, es, a in zip(subs, h, e, acc):
        o_ref[r, :] = _rms(hs + _sigmoid(a + gb_ref[...]) * es, fnw_ref[...])


def _outproj(x2, yh, yg, p2, wo, pp, pnw, gnw, gw, gb, fnw, tm=512):
    T, D = x2.shape
    assert T % tm == 0 and tm % OUT_SUB == 0
    row = lambda w: pl.BlockSpec((tm, w), lambda i: (i, 0))
    once = pl.Buffered(1)
    full = lambda a: pl.BlockSpec(a.shape, lambda i: (0, 0), pipeline_mode=once)
    return pl.pallas_call(
        _out_kernel,
        grid=(T // tm,),
        in_specs=[row(D), row(D_HGRN), row(D_GLA), row(PLE_DIM),
                  pl.BlockSpec((D_HGRN, D), lambda i: (0, 0), pipeline_mode=once),
                  pl.BlockSpec((D_GLA, D), lambda i: (D_HGRN // D_GLA, 0), pipeline_mode=once),
                  full(pp), full(pnw), full(gnw), full(gw), full(gb), full(fnw)],
        out_specs=row(D),
        out_shape=jax.ShapeDtypeStruct((T, D), jnp.float32),
        compiler_params=pltpu.CompilerParams(
            dimension_semantics=("parallel",),
            vmem_limit_bytes=OUT_VMEM_LIMIT),
        name="outproj_ple",
    )(x2, yh, yg, p2, wo, wo, pp, pnw, gnw, gw, gb, fnw)


def kernel(x, p, norm_mix_w, w_in, hgrn_lb, gla_gate_w2, gla_gate_b, hgrn_norm_w, gla_norm_w,
           w_out, ple_proj, ple_norm_w, ple_gate_norm_w, ple_gate_w, ple_gate_b, final_norm_w):
    B, T, D = x.shape
    bf = jnp.bfloat16
    x2 = x.reshape(B * T, D)
    p2 = p[0].reshape(B * T, PLE_DIM)
    row = lambda a: a.reshape(1, -1)

    w_t = jnp.swapaxes(w_in[0], 0, 1)
    wlr_t = jnp.pad(w_t[D_MAIN:], ((0, LANE - GLA_GATE_RANK), (0, 0)))
    nw = row(norm_mix_w[0])
    hf, xb, rs = _inproj_first(x2, nw, w_t, 1, 0.5, "inproj_f")
    pb = _inproj_rest(xb, rs, nw, w_t, lambda j: j + jnp.minimum(j, 1), 6, (0, 2, 5), bf,
                      "inproj_b")

    head_gain = jnp.broadcast_to(jnp.concatenate([hgrn_norm_w[0], gla_norm_w[0]])[:, None],
                                 (D_HGRN + D_GLA, LANE))
    w2_pad = jnp.pad(gla_gate_w2[0], ((0, LANE - GLA_GATE_RANK), (0, 0)))
    y_h, y_g, wo_bf, pgw_bf, pp_bf = _scans(hf, pb, hgrn_lb, w2_pad, row(gla_gate_b[0]), xb, rs, wlr_t, nw,
                                            w_out[0], head_gain, ple_gate_w[0], ple_proj[0])

    out = _outproj(x2, y_h, y_g, p2, wo_bf, pp_bf, row(ple_norm_w[0]), row(ple_gate_norm_w[0]), pgw_bf,
                   row(ple_gate_b[0]), row(final_norm_w))
    return out.reshape(B, T, D)
```

```python
import functools

import jax
import jax.numpy as jnp
from jax import lax
from jax.experimental import pallas as pl
from jax.experimental.pallas import tpu as pltpu

D_MODEL = 2048
D_HGRN = 1024
HGRN_HEADS = 8
HGRN_DK = 128
HGRN_DV = 128
GLA_HEADS = 4
GLA_DK = 128
GLA_DV = 256
GLA_DK_TOTAL = 512
D_GLA = 1024
GLA_GATE_RANK = 16
GLA_GATE_TAU = 16.0
PLE_DIM = 256
EPS = 1e-6

LANE = 128
SUBLANE = 8
D_MAIN = 4 * D_HGRN + 2 * GLA_DK_TOTAL + 2 * D_GLA
PANEL = 1024
CHUNK = 128
ATTN_BLOCK = 128
DECAY_RANGE_LIMIT = 110.0
LOG2E = 1.4426950408889634
SCAN_VMEM_LIMIT = 56 * 1024 * 1024
INPROJ_VMEM_LIMIT = 62 * 1024 * 1024

_NT = (((1,), (1,)), ((), ()))
_TN = (((0,), (0,)), ((), ()))


def _sigmoid(x):
    return 0.5 + 0.5 * jnp.tanh(0.5 * x)


def _silu_of_half(h):
    return h + h * jnp.tanh(h)


def _rms(x, w):
    return x * lax.rsqrt(jnp.mean(x * x, axis=-1, keepdims=True) + EPS) * w


def _is_half_panel(half_panels):
    return functools.reduce(jnp.logical_or, [pl.program_id(0) == jp for jp in half_panels])


def _cast_weight_panel(w_ref, nw_ref, wb_ref):
    @pl.when(pl.program_id(1) == 0)
    def _():
        wb_ref[...] = (w_ref[...] * nw_ref[...]).astype(jnp.bfloat16)


X_SLOTS = 3


def _inproj_first_kernel(x_hbm, nw_ref, w_ref, o_ref, xb_ref, rs_ref, wb_ref, xbuf_ref, sem, *, scale):
    bf = jnp.bfloat16
    i = pl.program_id(0)
    n = pl.num_programs(0)
    tm = xbuf_ref.shape[1]
    ahead = X_SLOTS - 1

    def tile_copy(step):
        row0 = step * tm if isinstance(step, int) else pl.multiple_of(step * tm, tm)
        slot = step % X_SLOTS
        return pltpu.make_async_copy(x_hbm.at[pl.ds(row0, tm), :], xbuf_ref.at[slot], sem.at[slot])

    @pl.when(i == 0)
    def _():
        for step in range(ahead):
            tile_copy(step).start()
        wb_ref[...] = (w_ref[...] * nw_ref[...]).astype(bf)

    @pl.when(i + ahead < n)
    def _():
        tile_copy(i + ahead).start()

    tile_copy(i).wait()
    x = xbuf_ref[i % X_SLOTS]
    r = lax.rsqrt(jnp.mean(x * x, axis=-1, keepdims=True) + EPS)
    xb = x.astype(bf)
    o_ref[...] = lax.dot_general(xb, wb_ref[...], _NT, preferred_element_type=jnp.float32) * (r * scale)
    xb_ref[...] = xb
    rs_ref[...] = jnp.broadcast_to(r, rs_ref.shape)


def _inproj_rest_kernel(xb_ref, rs_ref, nw_ref, w_ref, o_ref, wb_ref, *, half_panels):
    _cast_weight_panel(w_ref, nw_ref, wb_ref)
    r = rs_ref[:, 0:1] * jnp.where(_is_half_panel(half_panels), 0.5, 1.0)
    o_ref[...] = (lax.dot_general(xb_ref[...], wb_ref[...], _NT, preferred_element_type=jnp.float32)
                  * r).astype(o_ref.dtype)


def _inproj_params():
    return pltpu.CompilerParams(dimension_semantics=("parallel", "arbitrary"),
                                vmem_limit_bytes=INPROJ_VMEM_LIMIT)


def _inproj_first(x2, norm_w, w_t, panel, scale, name, tm=1024):
    T, D = x2.shape
    assert T % tm == 0 and T // tm >= X_SLOTS - 1
    return pl.pallas_call(
        functools.partial(_inproj_first_kernel, scale=scale),
        grid=(T // tm,),
        in_specs=[pl.BlockSpec(memory_space=pl.ANY),
                  pl.BlockSpec((1, D), lambda i: (0, 0)),
                  pl.BlockSpec((PANEL, D), lambda i: (panel, 0), pipeline_mode=pl.Buffered(1))],
        out_specs=[pl.BlockSpec((tm, PANEL), lambda i: (i, 0)),
                   pl.BlockSpec((tm, D), lambda i: (i, 0)),
                   pl.BlockSpec((tm, LANE), lambda i: (i, 0))],
        out_shape=[jax.ShapeDtypeStruct((T, PANEL), jnp.float32),
                   jax.ShapeDtypeStruct((T, D), jnp.bfloat16),
                   jax.ShapeDtypeStruct((T, LANE), jnp.float32)],
        scratch_shapes=[pltpu.VMEM((PANEL, D), jnp.bfloat16),
                        pltpu.VMEM((X_SLOTS, tm, D), jnp.float32),
                        pltpu.SemaphoreType.DMA((X_SLOTS,))],
        compiler_params=pltpu.CompilerParams(dimension_semantics=("arbitrary",),
                                             vmem_limit_bytes=INPROJ_VMEM_LIMIT),
        name=name,
    )(x2, norm_w, w_t)


def _inproj_rest(xb, rs, norm_w, w_t, panel_of, n_panels, half_panels, out_dtype, name, tm=2048):
    T, D = xb.shape
    assert T % tm == 0
    return pl.pallas_call(
        functools.partial(_inproj_rest_kernel, half_panels=half_panels),
        grid=(n_panels, T // tm),
        in_specs=[pl.BlockSpec((tm, D), lambda j, i: (i, 0)),
                  pl.BlockSpec((tm, LANE), lambda j, i: (i, 0)),
                  pl.BlockSpec((1, D), lambda j, i: (0, 0)),
                  pl.BlockSpec((PANEL, D), lambda j, i: (panel_of(j), 0))],
        out_specs=pl.BlockSpec((tm, PANEL), lambda j, i: (i, j)),
        out_shape=jax.ShapeDtypeStruct((T, n_panels * PANEL), out_dtype),
        scratch_shapes=[pltpu.VMEM((PANEL, D), jnp.bfloat16)],
        compiler_params=_inproj_params(),
        name=name,
    )(xb, rs, norm_w, w_t)


def _scan_stages(qkg, v, state, finish):
    bf = jnp.bfloat16
    f32 = jnp.float32
    dv, dk = state.shape
    blk = ATTN_BLOCK
    row = lax.broadcasted_iota(jnp.int32, (blk, blk), 0)
    col = lax.broadcasted_iota(jnp.int32, (blk, blk), 1)
    causal = (col <= row) & (col >= (row & -CHUNK))
    res = {}

    def gates0():
        res['qkg'] = qkg()

    def gates():
        q, k, g = res.pop('qkg')
        g_hi = g.astype(bf)
        g_lo = (g - g_hi.astype(f32)).astype(bf)
        tri = causal.astype(bf)
        tri2 = jnp.concatenate([tri, tri], axis=1)
        res.update(q=q, k=k, n_blk=g.shape[0] // blk, n_chk=g.shape[0] // CHUNK)
        res['b'] = [jnp.dot(tri2, jnp.concatenate([g_hi[r * blk:(r + 1) * blk],
                                                     g_lo[r * blk:(r + 1) * blk]], axis=0),
                            preferred_element_type=f32) for r in range(res['n_blk'])]

    def decay():
        b = jnp.concatenate(res.pop('b'), axis=0)
        eb = jnp.exp2(b)
        q, k = res.pop('q'), res.pop('k')
        res['qh'] = q * eb.astype(bf) if q.dtype == bf else (q * eb).astype(bf)
        res['kt'] = k * (1.0 / eb).astype(bf) if k.dtype == bf else (k / eb).astype(bf)
        res['vb'] = v().astype(bf)
        res['eb_last'] = [eb[(c + 1) * CHUNK - 1:(c + 1) * CHUNK, :] for c in range(res['n_chk'])]
        res['b_min'] = jnp.min(jnp.concatenate(
            [b[(c + 1) * CHUNK - 1:(c + 1) * CHUNK, :] for c in range(res['n_chk'])], axis=0))

    def scores():
        qh, kt, vb = res['qh'], res['kt'], res['vb']
        res['attn'] = [lax.dot_general(qh[r * blk:(r + 1) * blk], kt[r * blk:(r + 1) * blk], _NT,
                                       preferred_element_type=f32) for r in range(res['n_blk'])]
        res['upd'] = [lax.dot_general(vb[c * CHUNK:(c + 1) * CHUNK], kt[c * CHUNK:(c + 1) * CHUNK], _TN,
                                      preferred_element_type=f32) for c in range(res['n_chk'])]

    def carry():
        res['attn'] = [jnp.where(causal, a, 0.0).astype(bf) for a in res['attn']]
        st, st_in = state, []
        for upd, eb_last in zip(res.pop('upd'), res.pop('eb_last')):
            st_in.append(st.astype(bf))
            st = (st + upd) * eb_last
        res.update(st_in=st_in, state=st)

    def outputs():
        qh, vb = res.pop('qh'), res.pop('vb')
        res['intra'] = [jnp.dot(a, vb[r * blk:(r + 1) * blk], preferred_element_type=f32)
                        for r, a in enumerate(res.pop('attn'))]
        res['inter'] = [lax.dot_general(qh[c * CHUNK:(c + 1) * CHUNK], s, _NT, preferred_element_type=f32)
                        for c, s in enumerate(res.pop('st_in'))]

    def done():
        finish(jnp.concatenate(res.pop('intra'), axis=0) + jnp.concatenate(res.pop('inter'), axis=0))

    return [gates0, gates, decay, scores, carry, outputs, done], res


def _scan_tile_exact(qkg_ref, v_ref, st_ref, o_ref):
    bf = jnp.bfloat16
    f32 = jnp.float32
    n_heads, dv, dk = st_ref.shape
    tt = o_ref.shape[0]
    rowk = lax.broadcasted_iota(jnp.int32, (CHUNK, dk), 0)
    rowc = lax.broadcasted_iota(jnp.int32, (CHUNK, CHUNK), 0)
    colc = lax.broadcasted_iota(jnp.int32, (CHUNK, CHUNK), 1)
    tri = (colc <= rowc).astype(bf)

    for h in range(n_heads):
        ks = slice(h * dk, (h + 1) * dk)
        vs = slice(h * dv, (h + 1) * dv)

        def chunk(c, carry, ks=ks, vs=vs, h=h):
            r0 = pl.multiple_of(c * CHUNK, CHUNK)
            rows = pl.ds(r0, CHUNK)
            qc = qkg_ref[0, rows, ks]
            kc = qkg_ref[1, rows, ks]
            gc = qkg_ref[2, rows, ks]
            g_hi = gc.astype(bf)
            g_lo = (gc - g_hi.astype(f32)).astype(bf)
            b = (jnp.dot(tri, g_hi, preferred_element_type=f32)
                 + jnp.dot(tri, g_lo, preferred_element_type=f32))
            qkg_ref[2, rows, ks] = b

            def columns(jg, attn):
                j0 = pl.multiple_of(jg * SUBLANE, SUBLANE)
                b_rows = qkg_ref[2, pl.ds(r0 + j0, SUBLANE), ks]
                k_rows = qkg_ref[1, pl.ds(r0 + j0, SUBLANE), ks]
                for r in range(SUBLANE):
                    j = j0 + r
                    decay = jnp.exp2(jnp.where(rowk >= j, b - b_rows[r:r + 1], 0.0))
                    colv = jnp.sum(qc * k_rows[r:r + 1] * decay, axis=1, keepdims=True)
                    attn = jnp.where(colc == j, colv, attn)
                return attn

            attn = lax.fori_loop(0, CHUNK // SUBLANE, columns, jnp.zeros((CHUNK, CHUNK), f32))
            attn = jnp.where(colc <= rowc, attn, 0.0).astype(bf)
            vb = v_ref[rows, vs].astype(bf)
            st = st_ref[h]
            b_last = b[CHUNK - 1:CHUNK, :]
            o_ref[rows, vs] = (
                jnp.dot(attn, vb, preferred_element_type=f32)
                + lax.dot_general((qc * jnp.exp2(b)).astype(bf), st.astype(bf), _NT,
                                  preferred_element_type=f32))
            k_dec = (kc * jnp.exp2(b_last - b)).astype(bf)
            st_ref[h] = st * jnp.exp2(b_last) + lax.dot_general(vb, k_dec, _TN,
                                                                preferred_element_type=f32)
            return carry

        lax.fori_loop(0, tt // CHUNK, chunk, 0)


def _readout(o, z, n_heads, dk):
    dv = o.shape[1] // n_heads

    def head(oh):
        return oh * lax.rsqrt(jnp.mean(oh * oh, axis=-1, keepdims=True) + EPS * dk)

    on = jnp.concatenate([head(o[:, h * dv:(h + 1) * dv]) for h in range(n_heads)], axis=1)
    return on.astype(z.dtype) * _silu_of_half(z)


def _scan_and_readout(groups, qkg_ref, o_ref, side_work=None):
    @pl.when(pl.program_id(1) == 0)
    def _():
        for *_, st_ref in groups:
            st_ref[...] = jnp.zeros_like(st_ref)

    if side_work is not None:
        side_work()
    units = []
    for qkg, v_ref, z_ref, y_ref, st_ref in groups:
        n_heads, dv, dk = st_ref.shape
        for h in range(n_heads):
            ks = slice(h * dk, (h + 1) * dk)
            vs = slice(h * dv, (h + 1) * dv)

            def finish(o, vs=vs, y_ref=y_ref, z_ref=z_ref, dk=dk):
                y_ref[:, vs] = _readout(o, z_ref[:, vs], 1, dk).astype(y_ref.dtype)

            units.append(_scan_stages(lambda qkg=qkg, ks=ks: qkg(ks), lambda v_ref=v_ref, vs=vs: v_ref[:, vs],
                                      st_ref[h], finish))
    n_stages = len(units[0][0])
    for tick in range(len(units) + n_stages - 1):
        for u in range(len(units)):
            if 0 <= tick - u < n_stages:
                units[u][0][tick - u]()
    b_min = units[0][1]['b_min']
    for _, res in units[1:]:
        b_min = jnp.minimum(b_min, res['b_min'])
    in_range = b_min >= -DECAY_RANGE_LIMIT

    @pl.when(in_range)
    def _():
        new_states = iter(res['state'] for _, res in units)
        for *_, st_ref in groups:
            for h in range(st_ref.shape[0]):
                st_ref[h] = next(new_states)

    @pl.when(jnp.logical_not(in_range))
    def _():
        for qkg, v_ref, z_ref, y_ref, st_ref in groups:
            n_heads, dv, dk = st_ref.shape
            q, k, g = qkg(slice(0, n_heads * dk))
            qkg_ref[0, :, :n_heads * dk] = q.astype(jnp.float32)
            qkg_ref[1, :, :n_heads * dk] = k.astype(jnp.float32)
            qkg_ref[2, :, :n_heads * dk] = g
            _scan_tile_exact(qkg_ref, v_ref, st_ref, o_ref)
            y_ref[...] = _readout(o_ref[:, :n_heads * dv], z_ref[...], n_heads, dk).astype(y_ref.dtype)


def _scan_kernel(lb_ref, hq_ref, hf_ref, hi_ref, hz_ref, gq_ref, gk_ref, gv_ref, gz_ref, w2_ref, gb_ref,
                 xb_ref, rs_ref, wlr_ref, nw_ref, wo_ref, hgain_ref, ggain_ref, pgw_ref, pp_ref,
                 yh_ref, yg_ref, wo_bf_ref, pgw_bf_ref, pp_bf_ref, sth_ref, stg_ref, qkg_ref, o_ref):
    bf = jnp.bfloat16
    side = {}
    rank = wlr_ref.shape[0]

    def pad_rank(a):
        return jnp.concatenate([a, jnp.zeros((LANE - rank, a.shape[1]), a.dtype)], axis=0)

    def side_work():
        wlr = pad_rank((wlr_ref[...] * nw_ref[...]).astype(bf))
        side['lr'] = (lax.dot_general(xb_ref[...], wlr, _NT, preferred_element_type=jnp.float32)
                      * rs_ref[:, 0:1]).astype(bf)
        n_rows = wo_ref.shape[0]
        gain = jnp.where(pl.program_id(1) < D_HGRN // n_rows, hgain_ref[...], ggain_ref[...])
        diag = (lax.broadcasted_iota(jnp.int32, (n_rows, n_rows), 0)
                == lax.broadcasted_iota(jnp.int32, (n_rows, n_rows), 1))
        gain_col = jnp.sum(jnp.where(diag, gain, 0.0), axis=1, keepdims=True)
        wo_bf_ref[...] = (wo_ref[...] * gain_col).astype(bf)
        pgw_bf_ref[...] = pgw_ref[...].astype(bf)
        pp_bf_ref[...] = pp_ref[...].astype(bf)

    def hgrn_qkg(cols):
        lbp = lb_ref[:, cols]
        e = jnp.exp(lbp - jnp.max(lbp, axis=0, keepdims=True))
        lb = e[0:1, :] / jnp.sum(e, axis=0, keepdims=True)
        f = (0.5 + 0.5 * lb) + (0.5 - 0.5 * lb) * jnp.tanh(hf_ref[:, cols])
        return _silu_of_half(hq_ref[:, cols]), 1.0 - f, jnp.log2(f)

    def gla_qkg(cols):
        logit = jnp.dot(side['lr'], pad_rank(w2_ref[:, cols].astype(bf)),
                        preferred_element_type=jnp.float32) + gb_ref[:, cols]
        log2_sig = (jnp.minimum(logit, 0.0) * LOG2E
                    - jnp.log2(1.0 + jnp.exp2(jnp.abs(logit) * -LOG2E)))
        return gq_ref[:, cols], gk_ref[:, cols], log2_sig / GLA_GATE_TAU

    _scan_and_readout([(hgrn_qkg, hi_ref, hz_ref, yh_ref, sth_ref),
                       (gla_qkg, gv_ref, gz_ref, yg_ref, stg_ref)], qkg_ref, o_ref, side_work)


def _row_slices(n):
    def spec(a):
        assert a.shape[0] % (16 * n) == 0
        return pl.BlockSpec((a.shape[0] // n, a.shape[1]), lambda h, t: (t, 0))
    return spec


def _scans(hf, pb, hgrn_lb, w2, gate_b, xb, rs, w_t, norm_w, w_out, hgrn_gain, gla_gain, pgw, pp, tt=512):
    T, D = xb.shape
    assert T % tt == 0 and tt % ATTN_BLOCK == 0 and ATTN_BLOCK % CHUNK == 0
    n = T // tt
    wo_rows = w_out.shape[0] // n
    assert wo_rows % LANE == 0 and D_HGRN % wo_rows == 0 and D_GLA % wo_rows == 0
    assert D_MAIN % GLA_GATE_RANK == 0 and w2.shape[0] == GLA_GATE_RANK
    n_hgrn = D_HGRN // wo_rows
    gate_rows = pl.BlockSpec((GLA_GATE_RANK, D), lambda h, t: (D_MAIN // GLA_GATE_RANK, 0))
    panel = lambda c: pl.BlockSpec((tt, PANEL), lambda h, t: (t, c))
    half = lambda c: pl.BlockSpec((tt, GLA_DK_TOTAL), lambda h, t: (t, c))
    rows = lambda width: pl.BlockSpec((tt, width), lambda h, t: (t, 0))
    const = lambda a: pl.BlockSpec(a.shape, lambda h, t: (0, 0))
    piece = _row_slices(n)
    gq0 = 3 * PANEL // GLA_DK_TOTAL
    return pl.pallas_call(
        _scan_kernel,
        grid=(1, n),
        in_specs=[const(hgrn_lb), panel(0), panel(0), panel(1), panel(2),
                  half(gq0), half(gq0 + 1), panel(4), panel(5), const(w2), const(gate_b),
                  rows(D), rows(LANE), gate_rows, const(norm_w),
                  piece(w_out),
                  pl.BlockSpec((1, wo_rows), lambda h, t: (0, jnp.minimum(t, n_hgrn - 1))),
                  pl.BlockSpec((1, wo_rows), lambda h, t: (0, jnp.maximum(t - n_hgrn, 0))),
                  piece(pgw), piece(pp)],
        out_specs=[rows(D_HGRN), rows(D_GLA), piece(w_out), piece(pgw), piece(pp)],
        out_shape=[jax.ShapeDtypeStruct((T, D_HGRN), jnp.bfloat16),
                   jax.ShapeDtypeStruct((T, D_GLA), jnp.bfloat16),
                   jax.ShapeDtypeStruct(w_out.shape, jnp.bfloat16),
                   jax.ShapeDtypeStruct(pgw.shape, jnp.bfloat16),
                   jax.ShapeDtypeStruct(pp.shape, jnp.bfloat16)],
        scratch_shapes=[pltpu.VMEM((HGRN_HEADS, HGRN_DV, HGRN_DK), jnp.float32),
                        pltpu.VMEM((GLA_HEADS, GLA_DV, GLA_DK), jnp.float32),
                        pltpu.VMEM((3, tt, D_HGRN), jnp.float32),
                        pltpu.VMEM((tt, max(D_HGRN, D_GLA)), jnp.float32)],
        compiler_params=pltpu.CompilerParams(dimension_semantics=("parallel", "arbitrary"),
                                             vmem_limit_bytes=SCAN_VMEM_LIMIT),
        name="scans",
    )(hgrn_lb, pb, hf, pb, pb, pb, pb, pb, pb, w2, gate_b, xb, rs, w_t, norm_w,
      w_out, hgrn_gain, gla_gain, pgw, pp)


OUT_SUB = 2
OUT_VMEM_LIMIT = 58 * 1024 * 1024


def _out_kernel(x_ref, yh_ref, yg_ref, p_ref, woh_ref, wog_ref, pp_ref, pnw_ref, gnw_ref,
                gw_ref, gb_ref, fnw_ref, o_ref):
    f32 = jnp.float32
    bf = jnp.bfloat16
    rows = o_ref.shape[0] // OUT_SUB
    subs = [slice(s * rows, (s + 1) * rows) for s in range(OUT_SUB)]
    h = [x_ref[r, :]
         + jnp.dot(yh_ref[r, :], woh_ref[...], preferred_element_type=f32)
         + jnp.dot(yg_ref[r, :], wog_ref[...], preferred_element_type=f32) for r in subs]
    e = [_rms(jnp.dot(p_ref[r, :].astype(bf), pp_ref[...], preferred_element_type=f32), pnw_ref[...])
         for r in subs]
    hn = [_rms(hs, gnw_ref[...]).astype(bf) for hs in h]
    acc = [jnp.dot(hs, gw_ref[...], preferred_element_type=f32) for hs in hn]
    for r, hs, es, a in zip(subs, h, e, acc):
        o_ref[r, :] = _rms(hs + _sigmoid(a + gb_ref[...]) * es, fnw_ref[...])


def _outproj(x2, yh, yg, p2, wo, pp, pnw, gnw, gw, gb, fnw, tm=512):
    T, D = x2.shape
    assert T % tm == 0 and tm % OUT_SUB == 0
    row = lambda w: pl.BlockSpec((tm, w), lambda i: (i, 0))
    once = pl.Buffered(1)
    full = lambda a: pl.BlockSpec(a.shape, lambda i: (0, 0), pipeline_mode=once)
    return pl.pallas_call(
        _out_kernel,
        grid=(T // tm,),
        in_specs=[row(D), row(D_HGRN), row(D_GLA), row(PLE_DIM),
                  pl.BlockSpec((D_HGRN, D), lambda i: (0, 0), pipeline_mode=once),
                  pl.BlockSpec((D_GLA, D), lambda i: (D_HGRN // D_GLA, 0), pipeline_mode=once),
                  full(pp), full(pnw), full(gnw), full(gw), full(gb), full(fnw)],
        out_specs=row(D),
        out_shape=jax.ShapeDtypeStruct((T, D), jnp.float32),
        compiler_params=pltpu.CompilerParams(
            dimension_semantics=("parallel",),
            vmem_limit_bytes=OUT_VMEM_LIMIT),
        name="outproj_ple",
    )(x2, yh, yg, p2, wo, wo, pp, pnw, gnw, gw, gb, fnw)


def kernel(x, p, norm_mix_w, w_in, hgrn_lb, gla_gate_w2, gla_gate_b, hgrn_norm_w, gla_norm_w,
           w_out, ple_proj, ple_norm_w, ple_gate_norm_w, ple_gate_w, ple_gate_b, final_norm_w):
    B, T, D = x.shape
    bf = jnp.bfloat16
    x2 = x.reshape(B * T, D)
    p2 = p[0].reshape(B * T, PLE_DIM)
    row = lambda a: a.reshape(1, -1)

    w_t = jnp.swapaxes(w_in[0], 0, 1)
    nw = row(norm_mix_w[0])
    hf, xb, rs = _inproj_first(x2, nw, w_t, 1, 0.5, "inproj_f")
    pb = _inproj_rest(xb, rs, nw, w_t, lambda j: j + jnp.minimum(j, 1), 6, (0, 2, 5), bf,
                      "inproj_b")

    y_h, y_g, wo_bf, pgw_bf, pp_bf = _scans(hf, pb, hgrn_lb, gla_gate_w2[0], row(gla_gate_b[0]), xb, rs, w_t, nw,
                                            w_out[0], row(hgrn_norm_w[0]), row(gla_norm_w[0]),
                                            ple_gate_w[0], ple_proj[0])

    out = _outproj(x2, y_h, y_g, p2, wo_bf, pp_bf, row(ple_norm_w[0]), row(ple_gate_norm_w[0]), pgw_bf,
                   row(ple_gate_b[0]), row(final_norm_w))
    return out.reshape(B, T, D)
```

```python
import functools

import jax
import jax.numpy as jnp
from jax import lax
from jax.experimental import pallas as pl
from jax.experimental.pallas import tpu as pltpu

D_MODEL = 2048
D_HGRN = 1024
HGRN_HEADS = 8
HGRN_DK = 128
HGRN_DV = 128
GLA_HEADS = 4
GLA_DK = 128
GLA_DV = 256
GLA_DK_TOTAL = 512
D_GLA = 1024
GLA_GATE_RANK = 16
GLA_GATE_TAU = 16.0
PLE_DIM = 256
EPS = 1e-6

LANE = 128
SUBLANE = 8
D_MAIN = 4 * D_HGRN + 2 * GLA_DK_TOTAL + 2 * D_GLA
PANEL = 1024
CHUNK = 128
ATTN_BLOCK = 128
DECAY_RANGE_LIMIT = 110.0
LOG2E = 1.4426950408889634
SCAN_VMEM_LIMIT = 56 * 1024 * 1024
INPROJ_VMEM_LIMIT = 62 * 1024 * 1024

_NT = (((1,), (1,)), ((), ()))
_TN = (((0,), (0,)), ((), ()))


def _sigmoid(x):
    return 0.5 + 0.5 * jnp.tanh(0.5 * x)


def _silu_of_half(h):
    return h + h * jnp.tanh(h)


def _rms(x, w):
    return x * lax.rsqrt(jnp.mean(x * x, axis=-1, keepdims=True) + EPS) * w


def _is_half_panel(half_panels):
    return functools.reduce(jnp.logical_or, [pl.program_id(0) == jp for jp in half_panels])


def _cast_weight_panel(w_ref, nw_ref, wb_ref):
    @pl.when(pl.program_id(1) == 0)
    def _():
        wb_ref[...] = (w_ref[...] * nw_ref[...]).astype(jnp.bfloat16)


X_SLOTS = 3


def _inproj_first_kernel(x_hbm, nw_ref, w_ref, o_ref, xb_ref, rs_ref, wb_ref, xbuf_ref, sem, *, scale):
    bf = jnp.bfloat16
    i = pl.program_id(0)
    n = pl.num_programs(0)
    tm = xbuf_ref.shape[1]
    ahead = X_SLOTS - 1

    def tile_copy(step):
        row0 = step * tm if isinstance(step, int) else pl.multiple_of(step * tm, tm)
        slot = step % X_SLOTS
        return pltpu.make_async_copy(x_hbm.at[pl.ds(row0, tm), :], xbuf_ref.at[slot], sem.at[slot])

    @pl.when(i == 0)
    def _():
        for step in range(ahead):
            tile_copy(step).start()
        wb_ref[...] = (w_ref[...] * nw_ref[...]).astype(bf)

    @pl.when(i + ahead < n)
    def _():
        tile_copy(i + ahead).start()

    tile_copy(i).wait()
    x = xbuf_ref[i % X_SLOTS]
    r = lax.rsqrt(jnp.mean(x * x, axis=-1, keepdims=True) + EPS)
    xb = x.astype(bf)
    o_ref[...] = lax.dot_general(xb, wb_ref[...], _NT, preferred_element_type=jnp.float32) * (r * scale)
    xb_ref[...] = xb
    rs_ref[...] = jnp.broadcast_to(r, rs_ref.shape)


def _inproj_rest_kernel(xb_ref, rs_ref, nw_ref, w_ref, o_ref, wb_ref, *, half_panels):
    _cast_weight_panel(w_ref, nw_ref, wb_ref)
    r = rs_ref[:, 0:1] * jnp.where(_is_half_panel(half_panels), 0.5, 1.0)
    o_ref[...] = (lax.dot_general(xb_ref[...], wb_ref[...], _NT, preferred_element_type=jnp.float32)
                  * r).astype(o_ref.dtype)


def _inproj_params():
    return pltpu.CompilerParams(dimension_semantics=("parallel", "arbitrary"),
                                vmem_limit_bytes=INPROJ_VMEM_LIMIT)


def _inproj_first(x2, norm_w, w_t, panel, scale, name, tm=1024):
    T, D = x2.shape
    assert T % tm == 0 and T // tm >= X_SLOTS - 1
    return pl.pallas_call(
        functools.partial(_inproj_first_kernel, scale=scale),
        grid=(T // tm,),
        in_specs=[pl.BlockSpec(memory_space=pl.ANY),
                  pl.BlockSpec((1, D), lambda i: (0, 0)),
                  pl.BlockSpec((PANEL, D), lambda i: (panel, 0), pipeline_mode=pl.Buffered(1))],
        out_specs=[pl.BlockSpec((tm, PANEL), lambda i: (i, 0)),
                   pl.BlockSpec((tm, D), lambda i: (i, 0)),
                   pl.BlockSpec((tm, LANE), lambda i: (i, 0))],
        out_shape=[jax.ShapeDtypeStruct((T, PANEL), jnp.float32),
                   jax.ShapeDtypeStruct((T, D), jnp.bfloat16),
                   jax.ShapeDtypeStruct((T, LANE), jnp.float32)],
        scratch_shapes=[pltpu.VMEM((PANEL, D), jnp.bfloat16),
                        pltpu.VMEM((X_SLOTS, tm, D), jnp.float32),
                        pltpu.SemaphoreType.DMA((X_SLOTS,))],
        compiler_params=pltpu.CompilerParams(dimension_semantics=("arbitrary",),
                                             vmem_limit_bytes=INPROJ_VMEM_LIMIT),
        name=name,
    )(x2, norm_w, w_t)


def _inproj_rest(xb, rs, norm_w, w_t, panel_of, n_panels, half_panels, out_dtype, name, tm=2048):
    T, D = xb.shape
    assert T % tm == 0
    return pl.pallas_call(
        functools.partial(_inproj_rest_kernel, half_panels=half_panels),
        grid=(n_panels, T // tm),
        in_specs=[pl.BlockSpec((tm, D), lambda j, i: (i, 0)),
                  pl.BlockSpec((tm, LANE), lambda j, i: (i, 0)),
                  pl.BlockSpec((1, D), lambda j, i: (0, 0)),
                  pl.BlockSpec((PANEL, D), lambda j, i: (panel_of(j), 0))],
        out_specs=pl.BlockSpec((tm, PANEL), lambda j, i: (i, j)),
        out_shape=jax.ShapeDtypeStruct((T, n_panels * PANEL), out_dtype),
        scratch_shapes=[pltpu.VMEM((PANEL, D), jnp.bfloat16)],
        compiler_params=_inproj_params(),
        name=name,
    )(xb, rs, norm_w, w_t)


def _scan_stages(qkg, v, state, finish):
    bf = jnp.bfloat16
    f32 = jnp.float32
    dv, dk = state.shape
    blk = ATTN_BLOCK
    row = lax.broadcasted_iota(jnp.int32, (blk, blk), 0)
    col = lax.broadcasted_iota(jnp.int32, (blk, blk), 1)
    causal = (col <= row) & (col >= (row & -CHUNK))
    res = {}

    def gates0():
        res['qkg'] = qkg()

    def gates():
        q, k, g = res.pop('qkg')
        g_hi = g.astype(bf)
        g_lo = (g - g_hi.astype(f32)).astype(bf)
        tri = causal.astype(bf)
        tri2 = jnp.concatenate([tri, tri], axis=1)
        res.update(q=q, k=k, n_blk=g.shape[0] // blk, n_chk=g.shape[0] // CHUNK)
        res['b'] = [jnp.dot(tri2, jnp.concatenate([g_hi[r * blk:(r + 1) * blk],
                                                     g_lo[r * blk:(r + 1) * blk]], axis=0),
                            preferred_element_type=f32) for r in range(res['n_blk'])]

    def decay():
        b = jnp.concatenate(res.pop('b'), axis=0)
        eb = jnp.exp2(b)
        q, k = res.pop('q'), res.pop('k')
        res['qh'] = q * eb.astype(bf) if q.dtype == bf else (q * eb).astype(bf)
        res['kt'] = k * (1.0 / eb).astype(bf) if k.dtype == bf else (k / eb).astype(bf)
        res['vb'] = v().astype(bf)
        res['eb_last'] = [eb[(c + 1) * CHUNK - 1:(c + 1) * CHUNK, :] for c in range(res['n_chk'])]
        res['b_min'] = jnp.min(jnp.concatenate(
            [b[(c + 1) * CHUNK - 1:(c + 1) * CHUNK, :] for c in range(res['n_chk'])], axis=0))

    def scores():
        qh, kt, vb = res['qh'], res['kt'], res['vb']
        res['attn'] = [lax.dot_general(qh[r * blk:(r + 1) * blk], kt[r * blk:(r + 1) * blk], _NT,
                                       preferred_element_type=f32) for r in range(res['n_blk'])]
        res['upd'] = [lax.dot_general(vb[c * CHUNK:(c + 1) * CHUNK], kt[c * CHUNK:(c + 1) * CHUNK], _TN,
                                      preferred_element_type=f32) for c in range(res['n_chk'])]

    def carry():
        res['attn'] = [jnp.where(causal, a, 0.0).astype(bf) for a in res['attn']]
        st, st_in = state, []
        for upd, eb_last in zip(res.pop('upd'), res.pop('eb_last')):
            st_in.append(st.astype(bf))
            st = (st + upd) * eb_last
        res.update(st_in=st_in, state=st)

    def outputs():
        qh, vb = res.pop('qh'), res.pop('vb')
        res['intra'] = [jnp.dot(a, vb[r * blk:(r + 1) * blk], preferred_element_type=f32)
                        for r, a in enumerate(res.pop('attn'))]
        res['inter'] = [lax.dot_general(qh[c * CHUNK:(c + 1) * CHUNK], s, _NT, preferred_element_type=f32)
                        for c, s in enumerate(res.pop('st_in'))]

    def done():
        finish(jnp.concatenate(res.pop('intra'), axis=0) + jnp.concatenate(res.pop('inter'), axis=0))

    return [gates0, gates, decay, scores, carry, outputs, done], res


def _scan_tile_exact(qkg_ref, v_ref, st_ref, o_ref):
    bf = jnp.bfloat16
    f32 = jnp.float32
    n_heads, dv, dk = st_ref.shape
    tt = o_ref.shape[0]
    rowk = lax.broadcasted_iota(jnp.int32, (CHUNK, dk), 0)
    rowc = lax.broadcasted_iota(jnp.int32, (CHUNK, CHUNK), 0)
    colc = lax.broadcasted_iota(jnp.int32, (CHUNK, CHUNK), 1)
    tri = (colc <= rowc).astype(bf)

    for h in range(n_heads):
        ks = slice(h * dk, (h + 1) * dk)
        vs = slice(h * dv, (h + 1) * dv)

        def chunk(c, carry, ks=ks, vs=vs, h=h):
            r0 = pl.multiple_of(c * CHUNK, CHUNK)
            rows = pl.ds(r0, CHUNK)
            qc = qkg_ref[0, rows, ks]
            kc = qkg_ref[1, rows, ks]
            gc = qkg_ref[2, rows, ks]
            g_hi = gc.astype(bf)
            g_lo = (gc - g_hi.astype(f32)).astype(bf)
            b = (jnp.dot(tri, g_hi, preferred_element_type=f32)
                 + jnp.dot(tri, g_lo, preferred_element_type=f32))
            qkg_ref[2, rows, ks] = b

            def columns(jg, attn):
                j0 = pl.multiple_of(jg * SUBLANE, SUBLANE)
                b_rows = qkg_ref[2, pl.ds(r0 + j0, SUBLANE), ks]
                k_rows = qkg_ref[1, pl.ds(r0 + j0, SUBLANE), ks]
                for r in range(SUBLANE):
                    j = j0 + r
                    decay = jnp.exp2(jnp.where(rowk >= j, b - b_rows[r:r + 1], 0.0))
                    colv = jnp.sum(qc * k_rows[r:r + 1] * decay, axis=1, keepdims=True)
                    attn = jnp.where(colc == j, colv, attn)
                return attn

            attn = lax.fori_loop(0, CHUNK // SUBLANE, columns, jnp.zeros((CHUNK, CHUNK), f32))
            attn = jnp.where(colc <= rowc, attn, 0.0).astype(bf)
            vb = v_ref[rows, vs].astype(bf)
            st = st_ref[h]
            b_last = b[CHUNK - 1:CHUNK, :]
            o_ref[rows, vs] = (
                jnp.dot(attn, vb, preferred_element_type=f32)
                + lax.dot_general((qc * jnp.exp2(b)).astype(bf), st.astype(bf), _NT,
                                  preferred_element_type=f32))
            k_dec = (kc * jnp.exp2(b_last - b)).astype(bf)
            st_ref[h] = st * jnp.exp2(b_last) + lax.dot_general(vb, k_dec, _TN,
                                                                preferred_element_type=f32)
            return carry

        lax.fori_loop(0, tt // CHUNK, chunk, 0)


def _readout(o, z, n_heads, dk):
    dv = o.shape[1] // n_heads

    def head(oh):
        return oh * lax.rsqrt(jnp.mean(oh * oh, axis=-1, keepdims=True) + EPS * dk)

    on = jnp.concatenate([head(o[:, h * dv:(h + 1) * dv]) for h in range(n_heads)], axis=1)
    return on.astype(z.dtype) * _silu_of_half(z)


def _scan_and_readout(groups, qkg_ref, o_ref, side_work=None):
    @pl.when(pl.program_id(1) == 0)
    def _():
        for *_, st_ref in groups:
            st_ref[...] = jnp.zeros_like(st_ref)

    if side_work is not None:
        side_work()
    units = []
    for qkg, v_ref, z_ref, y_ref, st_ref in groups:
        n_heads, dv, dk = st_ref.shape
        for h in range(n_heads):
            ks = slice(h * dk, (h + 1) * dk)
            vs = slice(h * dv, (h + 1) * dv)

            def finish(o, vs=vs, y_ref=y_ref, z_ref=z_ref, dk=dk):
                y_ref[:, vs] = _readout(o, z_ref[:, vs], 1, dk).astype(y_ref.dtype)

            units.append(_scan_stages(lambda qkg=qkg, ks=ks: qkg(ks), lambda v_ref=v_ref, vs=vs: v_ref[:, vs],
                                      st_ref[h], finish))
    n_stages = len(units[0][0])
    for tick in range(len(units) + n_stages - 1):
        for u in range(len(units)):
            if 0 <= tick - u < n_stages:
                units[u][0][tick - u]()
    b_min = units[0][1]['b_min']
    for _, res in units[1:]:
        b_min = jnp.minimum(b_min, res['b_min'])
    in_range = b_min >= -DECAY_RANGE_LIMIT

    @pl.when(in_range)
    def _():
        new_states = iter(res['state'] for _, res in units)
        for *_, st_ref in groups:
            for h in range(st_ref.shape[0]):
                st_ref[h] = next(new_states)

    @pl.when(jnp.logical_not(in_range))
    def _():
        for qkg, v_ref, z_ref, y_ref, st_ref in groups:
            n_heads, dv, dk = st_ref.shape
            q, k, g = qkg(slice(0, n_heads * dk))
            qkg_ref[0, :, :n_heads * dk] = q.astype(jnp.float32)
            qkg_ref[1, :, :n_heads * dk] = k.astype(jnp.float32)
            qkg_ref[2, :, :n_heads * dk] = g
            _scan_tile_exact(qkg_ref, v_ref, st_ref, o_ref)
            y_ref[...] = _readout(o_ref[:, :n_heads * dv], z_ref[...], n_heads, dk).astype(y_ref.dtype)


def _scan_kernel(lb_ref, hq_ref, hf_ref, hi_ref, hz_ref, gq_ref, gk_ref, gv_ref, gz_ref, w2_ref, gb_ref,
                 xb_ref, rs_ref, wlr_ref, nw_ref, wo_ref, hgain_ref, ggain_ref, pgw_ref, pp_ref,
                 yh_ref, yg_ref, wo_bf_ref, pgw_bf_ref, pp_bf_ref, sth_ref, stg_ref, qkg_ref, o_ref):
    bf = jnp.bfloat16
    side = {}
    rank = wlr_ref.shape[0]

    def pad_rank(a):
        return jnp.concatenate([a, jnp.zeros((LANE - rank, a.shape[1]), a.dtype)], axis=0)

    def side_work():
        wlr = pad_rank((wlr_ref[...] * nw_ref[...]).astype(bf))
        side['lr'] = (lax.dot_general(xb_ref[...], wlr, _NT, preferred_element_type=jnp.float32)
                      * rs_ref[:, 0:1]).astype(bf)
        n_rows = wo_ref.shape[0]
        gain = jnp.where(pl.program_id(1) < D_HGRN // n_rows, hgain_ref[...], ggain_ref[...])
        diag = (lax.broadcasted_iota(jnp.int32, (n_rows, n_rows), 0)
                == lax.broadcasted_iota(jnp.int32, (n_rows, n_rows), 1))
        gain_col = jnp.sum(jnp.where(diag, gain, 0.0), axis=1, keepdims=True)
        wo_bf_ref[...] = (wo_ref[...] * gain_col).astype(bf)
        pgw_bf_ref[...] = pgw_ref[...].astype(bf)
        pp_bf_ref[...] = pp_ref[...].astype(bf)

    def hgrn_qkg(cols):
        lbp = lb_ref[:, cols]
        e = jnp.exp(lbp - jnp.max(lbp, axis=0, keepdims=True))
        lb = e[0:1, :] / jnp.sum(e, axis=0, keepdims=True)
        f = (0.5 + 0.5 * lb) + (0.5 - 0.5 * lb) * jnp.tanh(hf_ref[:, cols])
        return _silu_of_half(hq_ref[:, cols]), 1.0 - f, jnp.log2(f)

    def gla_qkg(cols):
        logit = jnp.dot(side['lr'], pad_rank(w2_ref[:, cols].astype(bf)),
                        preferred_element_type=jnp.float32) + gb_ref[:, cols]
        log2_sig = (jnp.minimum(logit, 0.0) * LOG2E
                    - jnp.log2(1.0 + jnp.exp2(jnp.abs(logit) * -LOG2E)))
        return gq_ref[:, cols], gk_ref[:, cols], log2_sig / GLA_GATE_TAU

    _scan_and_readout([(hgrn_qkg, hi_ref, hz_ref, yh_ref, sth_ref),
                       (gla_qkg, gv_ref, gz_ref, yg_ref, stg_ref)], qkg_ref, o_ref, side_work)


def _row_slices(n):
    def spec(a):
        assert a.shape[0] % (16 * n) == 0
        return pl.BlockSpec((a.shape[0] // n, a.shape[1]), lambda h, t: (t, 0))
    return spec


def _scans(hf, pb, hgrn_lb, w2, gate_b, xb, rs, w_t, norm_w, w_out, hgrn_gain, gla_gain, pgw, pp, tt=512):
    T, D = xb.shape
    assert T % tt == 0 and tt % ATTN_BLOCK == 0 and ATTN_BLOCK % CHUNK == 0
    n = T // tt
    wo_rows = w_out.shape[0] // n
    assert wo_rows % LANE == 0 and D_HGRN % wo_rows == 0 and D_GLA % wo_rows == 0
    assert D_MAIN % GLA_GATE_RANK == 0 and w2.shape[0] == GLA_GATE_RANK
    n_hgrn = D_HGRN // wo_rows
    gate_rows = pl.BlockSpec((GLA_GATE_RANK, D), lambda h, t: (D_MAIN // GLA_GATE_RANK, 0))
    panel = lambda c: pl.BlockSpec((tt, PANEL), lambda h, t: (t, c))
    half = lambda c: pl.BlockSpec((tt, GLA_DK_TOTAL), lambda h, t: (t, c))
    rows = lambda width: pl.BlockSpec((tt, width), lambda h, t: (t, 0))
    const = lambda a: pl.BlockSpec(a.shape, lambda h, t: (0, 0))
    piece = _row_slices(n)
    gq0 = 3 * PANEL // GLA_DK_TOTAL
    return pl.pallas_call(
        _scan_kernel,
        grid=(1, n),
        in_specs=[const(hgrn_lb), panel(0), panel(0), panel(1), panel(2),
                  half(gq0), half(gq0 + 1), panel(4), panel(5), const(w2), const(gate_b),
                  rows(D), rows(LANE), gate_rows, const(norm_w),
                  piece(w_out),
                  pl.BlockSpec((1, wo_rows), lambda h, t: (0, jnp.minimum(t, n_hgrn - 1))),
                  pl.BlockSpec((1, wo_rows), lambda h, t: (0, jnp.maximum(t - n_hgrn, 0))),
                  piece(pgw), piece(pp)],
        out_specs=[rows(D_HGRN), rows(D_GLA), piece(w_out), piece(pgw), piece(pp)],
        out_shape=[jax.ShapeDtypeStruct((T, D_HGRN), jnp.bfloat16),
                   jax.ShapeDtypeStruct((T, D_GLA), jnp.bfloat16),
                   jax.ShapeDtypeStruct(w_out.shape, jnp.bfloat16),
                   jax.ShapeDtypeStruct(pgw.shape, jnp.bfloat16),
                   jax.ShapeDtypeStruct(pp.shape, jnp.bfloat16)],
        scratch_shapes=[pltpu.VMEM((HGRN_HEADS, HGRN_DV, HGRN_DK), jnp.float32),
                        pltpu.VMEM((GLA_HEADS, GLA_DV, GLA_DK), jnp.float32),
                        pltpu.VMEM((3, tt, D_HGRN), jnp.float32),
                        pltpu.VMEM((tt, max(D_HGRN, D_GLA)), jnp.float32)],
        compiler_params=pltpu.CompilerParams(dimension_semantics=("parallel", "arbitrary"),
                                             vmem_limit_bytes=SCAN_VMEM_LIMIT),
        name="scans",
    )(hgrn_lb, pb, hf, pb, pb, pb, pb, pb, pb, w2, gate_b, xb, rs, w_t, norm_w,
      w_out, hgrn_gain, gla_gain, pgw, pp)


OUT_SUB = 2
OUT_VMEM_LIMIT = 58 * 1024 * 1024


def _out_kernel(x_ref, yh_ref, yg_ref, p_ref, wo_hbm, pp_hbm, gw_hbm, pnw_ref, gnw_ref, gb_ref, fnw_ref,
                o_ref, wo_ref, pp_ref, gw_ref, sem):
    f32 = jnp.float32
    bf = jnp.bfloat16
    rows = o_ref.shape[0] // OUT_SUB
    subs = [slice(s * rows, (s + 1) * rows) for s in range(OUT_SUB)]
    d_h = yh_ref.shape[1]
    woh, wog = wo_ref.at[pl.ds(0, d_h), :], wo_ref.at[pl.ds(d_h, wo_ref.shape[0] - d_h), :]
    copies = [pltpu.make_async_copy(wo_hbm.at[pl.ds(0, d_h), :], woh, sem.at[0]),
              pltpu.make_async_copy(wo_hbm.at[pl.ds(d_h, wo_ref.shape[0] - d_h), :], wog, sem.at[1]),
              pltpu.make_async_copy(pp_hbm, pp_ref, sem.at[2]),
              pltpu.make_async_copy(gw_hbm, gw_ref, sem.at[3])]

    def body(first_step):
        def arrived(k):
            if first_step:
                copies[k].wait()

        if first_step:
            for c in copies:
                c.start()
        arrived(0)
        h = [x_ref[r, :] + jnp.dot(yh_ref[r, :], woh[...], preferred_element_type=f32) for r in subs]
        arrived(1)
        h = [hs + jnp.dot(yg_ref[r, :], wog[...], preferred_element_type=f32) for r, hs in zip(subs, h)]
        arrived(2)
        e = [_rms(jnp.dot(p_ref[r, :].astype(bf), pp_ref[...], preferred_element_type=f32), pnw_ref[...])
             for r in subs]
        hn = [_rms(hs, gnw_ref[...]).astype(bf) for hs in h]
        arrived(3)
        acc = [jnp.dot(hs, gw_ref[...], preferred_element_type=f32) for hs in hn]
        for r, hs, es, a in zip(subs, h, e, acc):
            o_ref[r, :] = _rms(hs + _sigmoid(a + gb_ref[...]) * es, fnw_ref[...])

    first = pl.program_id(0) == 0
    pl.when(first)(lambda: body(True))
    pl.when(jnp.logical_not(first))(lambda: body(False))


def _outproj(x2, yh, yg, p2, wo, pp, pnw, gnw, gw, gb, fnw, tm=512):
    T, D = x2.shape
    assert T % tm == 0 and tm % OUT_SUB == 0
    row = lambda w: pl.BlockSpec((tm, w), lambda i: (i, 0))
    full = lambda a: pl.BlockSpec(a.shape, lambda i: (0, 0), pipeline_mode=pl.Buffered(1))
    in_hbm = pl.BlockSpec(memory_space=pl.ANY)
    return pl.pallas_call(
        _out_kernel,
        grid=(T // tm,),
        in_specs=[row(D), row(D_HGRN), row(D_GLA), row(PLE_DIM), in_hbm, in_hbm, in_hbm,
                  full(pnw), full(gnw), full(gb), full(fnw)],
        out_specs=row(D),
        out_shape=jax.ShapeDtypeStruct((T, D), jnp.float32),
        scratch_shapes=[pltpu.VMEM(wo.shape, wo.dtype), pltpu.VMEM(pp.shape, pp.dtype),
                        pltpu.VMEM(gw.shape, gw.dtype), pltpu.SemaphoreType.DMA((4,))],
        compiler_params=pltpu.CompilerParams(
            dimension_semantics=("arbitrary",),
            vmem_limit_bytes=OUT_VMEM_LIMIT),
        name="outproj_ple",
    )(x2, yh, yg, p2, wo, pp, gw, pnw, gnw, gb, fnw)


def kernel(x, p, norm_mix_w, w_in, hgrn_lb, gla_gate_w2, gla_gate_b, hgrn_norm_w, gla_norm_w,
           w_out, ple_proj, ple_norm_w, ple_gate_norm_w, ple_gate_w, ple_gate_b, final_norm_w):
    B, T, D = x.shape
    bf = jnp.bfloat16
    x2 = x.reshape(B * T, D)
    p2 = p[0].reshape(B * T, PLE_DIM)
    row = lambda a: a.reshape(1, -1)

    w_t = jnp.swapaxes(w_in[0], 0, 1)
    nw = row(norm_mix_w[0])
    hf, xb, rs = _inproj_first(x2, nw, w_t, 1, 0.5, "inproj_f")
    pb = _inproj_rest(xb, rs, nw, w_t, lambda j: j + jnp.minimum(j, 1), 6, (0, 2, 5), bf,
                      "inproj_b")

    y_h, y_g, wo_bf, pgw_bf, pp_bf = _scans(hf, pb, hgrn_lb, gla_gate_w2[0], row(gla_gate_b[0]), xb, rs, w_t, nw,
                                            w_out[0], row(hgrn_norm_w[0]), row(gla_norm_w[0]),
                                            ple_gate_w[0], ple_proj[0])

    out = _outproj(x2, y_h, y_g, p2, wo_bf, pp_bf, row(ple_norm_w[0]), row(ple_gate_norm_w[0]), pgw_bf,
                   row(ple_gate_b[0]), row(final_norm_w))
    return out.reshape(B, T, D)
```

```python
import functools

import jax
import jax.numpy as jnp
from jax import lax
from jax.experimental import pallas as pl
from jax.experimental.pallas import tpu as pltpu

D_MODEL = 2048
D_HGRN = 1024
HGRN_HEADS = 8
HGRN_DK = 128
HGRN_DV = 128
GLA_HEADS = 4
GLA_DK = 128
GLA_DV = 256
GLA_DK_TOTAL = 512
D_GLA = 1024
GLA_GATE_RANK = 16
GLA_GATE_TAU = 16.0
PLE_DIM = 256
EPS = 1e-6

LANE = 128
SUBLANE = 8
D_MAIN = 4 * D_HGRN + 2 * GLA_DK_TOTAL + 2 * D_GLA
PANEL = 1024
CHUNK = 128
ATTN_BLOCK = 128
DECAY_RANGE_LIMIT = 110.0
LOG2E = 1.4426950408889634
SCAN_VMEM_LIMIT = 56 * 1024 * 1024
INPROJ_VMEM_LIMIT = 62 * 1024 * 1024

_NT = (((1,), (1,)), ((), ()))
_TN = (((0,), (0,)), ((), ()))


def _sigmoid(x):
    return 0.5 + 0.5 * jnp.tanh(0.5 * x)


def _silu_of_half(h):
    return h + h * jnp.tanh(h)


def _rms(x, w):
    return x * lax.rsqrt(jnp.mean(x * x, axis=-1, keepdims=True) + EPS) * w


def _is_half_panel(half_panels):
    return functools.reduce(jnp.logical_or, [pl.program_id(0) == jp for jp in half_panels])


def _cast_weight_panel(w_ref, nw_ref, wb_ref):
    @pl.when(pl.program_id(1) == 0)
    def _():
        wb_ref[...] = (w_ref[...] * nw_ref[...]).astype(jnp.bfloat16)


X_SLOTS = 3


def _inproj_first_kernel(x_hbm, nw_ref, w_ref, o_ref, xb_ref, rs_ref, wb_ref, xbuf_ref, sem, *, scale):
    bf = jnp.bfloat16
    i = pl.program_id(0)
    n = pl.num_programs(0)
    tm = xbuf_ref.shape[1]
    ahead = X_SLOTS - 1

    def tile_copy(step):
        row0 = step * tm if isinstance(step, int) else pl.multiple_of(step * tm, tm)
        slot = step % X_SLOTS
        return pltpu.make_async_copy(x_hbm.at[pl.ds(row0, tm), :], xbuf_ref.at[slot], sem.at[slot])

    @pl.when(i == 0)
    def _():
        for step in range(ahead):
            tile_copy(step).start()
        wb_ref[...] = (w_ref[...] * nw_ref[...]).astype(bf)

    @pl.when(i + ahead < n)
    def _():
        tile_copy(i + ahead).start()

    tile_copy(i).wait()
    x = xbuf_ref[i % X_SLOTS]
    r = lax.rsqrt(jnp.mean(x * x, axis=-1, keepdims=True) + EPS)
    xb = x.astype(bf)
    o_ref[...] = lax.dot_general(xb, wb_ref[...], _NT, preferred_element_type=jnp.float32) * (r * scale)
    xb_ref[...] = xb
    rs_ref[...] = jnp.broadcast_to(r, rs_ref.shape)


def _inproj_rest_kernel(xb_ref, rs_ref, nw_ref, w_ref, o_ref, wb_ref, *, half_panels):
    _cast_weight_panel(w_ref, nw_ref, wb_ref)
    r = rs_ref[:, 0:1] * jnp.where(_is_half_panel(half_panels), 0.5, 1.0)
    o_ref[...] = (lax.dot_general(xb_ref[...], wb_ref[...], _NT, preferred_element_type=jnp.float32)
                  * r).astype(o_ref.dtype)


def _inproj_params():
    return pltpu.CompilerParams(dimension_semantics=("parallel", "arbitrary"),
                                vmem_limit_bytes=INPROJ_VMEM_LIMIT)


def _inproj_first(x2, norm_w, w_t, panel, scale, name, tm=1024):
    T, D = x2.shape
    assert T % tm == 0 and T // tm >= X_SLOTS - 1
    return pl.pallas_call(
        functools.partial(_inproj_first_kernel, scale=scale),
        grid=(T // tm,),
        in_specs=[pl.BlockSpec(memory_space=pl.ANY),
                  pl.BlockSpec((1, D), lambda i: (0, 0)),
                  pl.BlockSpec((PANEL, D), lambda i: (panel, 0), pipeline_mode=pl.Buffered(1))],
        out_specs=[pl.BlockSpec((tm, PANEL), lambda i: (i, 0)),
                   pl.BlockSpec((tm, D), lambda i: (i, 0)),
                   pl.BlockSpec((tm, LANE), lambda i: (i, 0))],
        out_shape=[jax.ShapeDtypeStruct((T, PANEL), jnp.float32),
                   jax.ShapeDtypeStruct((T, D), jnp.bfloat16),
                   jax.ShapeDtypeStruct((T, LANE), jnp.float32)],
        scratch_shapes=[pltpu.VMEM((PANEL, D), jnp.bfloat16),
                        pltpu.VMEM((X_SLOTS, tm, D), jnp.float32),
                        pltpu.SemaphoreType.DMA((X_SLOTS,))],
        compiler_params=pltpu.CompilerParams(dimension_semantics=("arbitrary",),
                                             vmem_limit_bytes=INPROJ_VMEM_LIMIT),
        name=name,
    )(x2, norm_w, w_t)


def _inproj_rest(xb, rs, norm_w, w_t, panel_of, n_panels, half_panels, out_dtype, name, tm=2048):
    T, D = xb.shape
    assert T % tm == 0
    return pl.pallas_call(
        functools.partial(_inproj_rest_kernel, half_panels=half_panels),
        grid=(n_panels, T // tm),
        in_specs=[pl.BlockSpec((tm, D), lambda j, i: (i, 0)),
                  pl.BlockSpec((tm, LANE), lambda j, i: (i, 0)),
                  pl.BlockSpec((1, D), lambda j, i: (0, 0)),
                  pl.BlockSpec((PANEL, D), lambda j, i: (panel_of(j), 0))],
        out_specs=pl.BlockSpec((tm, PANEL), lambda j, i: (i, j)),
        out_shape=jax.ShapeDtypeStruct((T, n_panels * PANEL), out_dtype),
        scratch_shapes=[pltpu.VMEM((PANEL, D), jnp.bfloat16)],
        compiler_params=_inproj_params(),
        name=name,
    )(xb, rs, norm_w, w_t)


def _scan_stages(qkg, v, state, finish):
    bf = jnp.bfloat16
    f32 = jnp.float32
    dv, dk = state.shape
    blk = ATTN_BLOCK
    row = lax.broadcasted_iota(jnp.int32, (blk, blk), 0)
    col = lax.broadcasted_iota(jnp.int32, (blk, blk), 1)
    causal = (col <= row) & (col >= (row & -CHUNK))
    res = {}

    def gates0():
        res['qkg'] = qkg()

    def gates():
        q, k, g = res.pop('qkg')
        g_hi = g.astype(bf)
        g_lo = (g - g_hi.astype(f32)).astype(bf)
        tri = causal.astype(bf)
        tri2 = jnp.concatenate([tri, tri], axis=1)
        res.update(q=q, k=k, n_blk=g.shape[0] // blk, n_chk=g.shape[0] // CHUNK)
        res['b'] = [jnp.dot(tri2, jnp.concatenate([g_hi[r * blk:(r + 1) * blk],
                                                     g_lo[r * blk:(r + 1) * blk]], axis=0),
                            preferred_element_type=f32) for r in range(res['n_blk'])]

    def decay():
        b = jnp.concatenate(res.pop('b'), axis=0)
        eb = jnp.exp2(b)
        q, k = res.pop('q'), res.pop('k')
        res['qh'] = q * eb.astype(bf) if q.dtype == bf else (q * eb).astype(bf)
        res['kt'] = k * (1.0 / eb).astype(bf) if k.dtype == bf else (k / eb).astype(bf)
        res['vb'] = v().astype(bf)
        res['eb_last'] = [eb[(c + 1) * CHUNK - 1:(c + 1) * CHUNK, :] for c in range(res['n_chk'])]
        res['b_min'] = jnp.min(jnp.concatenate(
            [b[(c + 1) * CHUNK - 1:(c + 1) * CHUNK, :] for c in range(res['n_chk'])], axis=0))

    def scores():
        qh, kt, vb = res['qh'], res['kt'], res['vb']
        res['attn'] = [lax.dot_general(qh[r * blk:(r + 1) * blk], kt[r * blk:(r + 1) * blk], _NT,
                                       preferred_element_type=f32) for r in range(res['n_blk'])]
        res['upd'] = [lax.dot_general(vb[c * CHUNK:(c + 1) * CHUNK], kt[c * CHUNK:(c + 1) * CHUNK], _TN,
                                      preferred_element_type=f32) for c in range(res['n_chk'])]

    def carry():
        res['attn'] = [jnp.where(causal, a, 0.0).astype(bf) for a in res['attn']]
        st, st_in = state, []
        for upd, eb_last in zip(res.pop('upd'), res.pop('eb_last')):
            st_in.append(st.astype(bf))
            st = (st + upd) * eb_last
        res.update(st_in=st_in, state=st)

    def outputs():
        qh, vb = res.pop('qh'), res.pop('vb')
        res['intra'] = [jnp.dot(a, vb[r * blk:(r + 1) * blk], preferred_element_type=f32)
                        for r, a in enumerate(res.pop('attn'))]
        res['inter'] = [lax.dot_general(qh[c * CHUNK:(c + 1) * CHUNK], s, _NT, preferred_element_type=f32)
                        for c, s in enumerate(res.pop('st_in'))]

    def done():
        finish(jnp.concatenate(res.pop('intra'), axis=0) + jnp.concatenate(res.pop('inter'), axis=0))

    return [gates0, gates, decay, scores, carry, outputs, done], res


def _scan_tile_exact(qkg_ref, v_ref, st_ref, o_ref):
    bf = jnp.bfloat16
    f32 = jnp.float32
    n_heads, dv, dk = st_ref.shape
    tt = o_ref.shape[0]
    rowk = lax.broadcasted_iota(jnp.int32, (CHUNK, dk), 0)
    rowc = lax.broadcasted_iota(jnp.int32, (CHUNK, CHUNK), 0)
    colc = lax.broadcasted_iota(jnp.int32, (CHUNK, CHUNK), 1)
    tri = (colc <= rowc).astype(bf)

    for h in range(n_heads):
        ks = slice(h * dk, (h + 1) * dk)
        vs = slice(h * dv, (h + 1) * dv)

        def chunk(c, carry, ks=ks, vs=vs, h=h):
            r0 = pl.multiple_of(c * CHUNK, CHUNK)
            rows = pl.ds(r0, CHUNK)
            qc = qkg_ref[0, rows, ks]
            kc = qkg_ref[1, rows, ks]
            gc = qkg_ref[2, rows, ks]
            g_hi = gc.astype(bf)
            g_lo = (gc - g_hi.astype(f32)).astype(bf)
            b = (jnp.dot(tri, g_hi, preferred_element_type=f32)
                 + jnp.dot(tri, g_lo, preferred_element_type=f32))
            qkg_ref[2, rows, ks] = b

            def columns(jg, attn):
                j0 = pl.multiple_of(jg * SUBLANE, SUBLANE)
                b_rows = qkg_ref[2, pl.ds(r0 + j0, SUBLANE), ks]
                k_rows = qkg_ref[1, pl.ds(r0 + j0, SUBLANE), ks]
                for r in range(SUBLANE):
                    j = j0 + r
                    decay = jnp.exp2(jnp.where(rowk >= j, b - b_rows[r:r + 1], 0.0))
                    colv = jnp.sum(qc * k_rows[r:r + 1] * decay, axis=1, keepdims=True)
                    attn = jnp.where(colc == j, colv, attn)
                return attn

            attn = lax.fori_loop(0, CHUNK // SUBLANE, columns, jnp.zeros((CHUNK, CHUNK), f32))
            attn = jnp.where(colc <= rowc, attn, 0.0).astype(bf)
            vb = v_ref[rows, vs].astype(bf)
            st = st_ref[h]
            b_last = b[CHUNK - 1:CHUNK, :]
            o_ref[rows, vs] = (
                jnp.dot(attn, vb, preferred_element_type=f32)
                + lax.dot_general((qc * jnp.exp2(b)).astype(bf), st.astype(bf), _NT,
                                  preferred_element_type=f32))
            k_dec = (kc * jnp.exp2(b_last - b)).astype(bf)
            st_ref[h] = st * jnp.exp2(b_last) + lax.dot_general(vb, k_dec, _TN,
                                                                preferred_element_type=f32)
            return carry

        lax.fori_loop(0, tt // CHUNK, chunk, 0)


def _readout(o, z, n_heads, dk):
    dv = o.shape[1] // n_heads

    def head(oh):
        return oh * lax.rsqrt(jnp.mean(oh * oh, axis=-1, keepdims=True) + EPS * dk)

    on = jnp.concatenate([head(o[:, h * dv:(h + 1) * dv]) for h in range(n_heads)], axis=1)
    return on.astype(z.dtype) * _silu_of_half(z)


def _scan_and_readout(groups, qkg_ref, o_ref, side_work=None):
    @pl.when(pl.program_id(1) == 0)
    def _():
        for *_, st_ref in groups:
            st_ref[...] = jnp.zeros_like(st_ref)

    if side_work is not None:
        side_work()
    units = []
    for qkg, v_ref, z_ref, y_ref, st_ref in groups:
        n_heads, dv, dk = st_ref.shape
        for h in range(n_heads):
            ks = slice(h * dk, (h + 1) * dk)
            vs = slice(h * dv, (h + 1) * dv)

            def finish(o, vs=vs, y_ref=y_ref, z_ref=z_ref, dk=dk):
                y_ref[:, vs] = _readout(o, z_ref[:, vs], 1, dk).astype(y_ref.dtype)

            units.append(_scan_stages(lambda qkg=qkg, ks=ks: qkg(ks), lambda v_ref=v_ref, vs=vs: v_ref[:, vs],
                                      st_ref[h], finish))
    n_stages = len(units[0][0])
    for tick in range(len(units) + n_stages - 1):
        for u in range(len(units)):
            if 0 <= tick - u < n_stages:
                units[u][0][tick - u]()
    b_min = units[0][1]['b_min']
    for _, res in units[1:]:
        b_min = jnp.minimum(b_min, res['b_min'])
    in_range = b_min >= -DECAY_RANGE_LIMIT

    @pl.when(in_range)
    def _():
        new_states = iter(res['state'] for _, res in units)
        for *_, st_ref in groups:
            for h in range(st_ref.shape[0]):
                st_ref[h] = next(new_states)

    @pl.when(jnp.logical_not(in_range))
    def _():
        for qkg, v_ref, z_ref, y_ref, st_ref in groups:
            n_heads, dv, dk = st_ref.shape
            q, k, g = qkg(slice(0, n_heads * dk))
            qkg_ref[0, :, :n_heads * dk] = q.astype(jnp.float32)
            qkg_ref[1, :, :n_heads * dk] = k.astype(jnp.float32)
            qkg_ref[2, :, :n_heads * dk] = g
            _scan_tile_exact(qkg_ref, v_ref, st_ref, o_ref)
            y_ref[...] = _readout(o_ref[:, :n_heads * dv], z_ref[...], n_heads, dk).astype(y_ref.dtype)


def _scan_kernel(lb_ref, pb_ref, hf_ref, w2_ref, gb_ref,
                 xb_ref, rs_ref, wlr_ref, nw_ref, wo_ref, hgain_ref, ggain_ref, pgw_ref, pp_ref,
                 yh_ref, yg_ref, wo_bf_ref, pgw_bf_ref, pp_bf_ref, sth_ref, stg_ref, qkg_ref, o_ref):
    bf = jnp.bfloat16
    side = {}
    rank = wlr_ref.shape[0]
    columns = lambda start, width: pb_ref.at[:, pl.ds(start, width)]
    hq_ref, hi_ref, hz_ref = (columns(c * PANEL, PANEL) for c in range(3))
    gq_ref, gk_ref = (columns(3 * PANEL + c * GLA_DK_TOTAL, GLA_DK_TOTAL) for c in range(2))
    gv_ref, gz_ref = (columns(c * PANEL, PANEL) for c in (4, 5))

    def pad_rank(a):
        return jnp.concatenate([a, jnp.zeros((LANE - rank, a.shape[1]), a.dtype)], axis=0)

    def side_work():
        wlr = pad_rank((wlr_ref[...] * nw_ref[...]).astype(bf))
        side['lr'] = (lax.dot_general(xb_ref[...], wlr, _NT, preferred_element_type=jnp.float32)
                      * rs_ref[:, 0:1]).astype(bf)
        n_rows = wo_ref.shape[0]
        gain = jnp.where(pl.program_id(1) < D_HGRN // n_rows, hgain_ref[...], ggain_ref[...])
        diag = (lax.broadcasted_iota(jnp.int32, (n_rows, n_rows), 0)
                == lax.broadcasted_iota(jnp.int32, (n_rows, n_rows), 1))
        gain_col = jnp.sum(jnp.where(diag, gain, 0.0), axis=1, keepdims=True)
        wo_bf_ref[...] = (wo_ref[...] * gain_col).astype(bf)
        pgw_bf_ref[...] = pgw_ref[...].astype(bf)
        pp_bf_ref[...] = pp_ref[...].astype(bf)

    def hgrn_qkg(cols):
        lbp = lb_ref[:, cols]
        e = jnp.exp(lbp - jnp.max(lbp, axis=0, keepdims=True))
        lb = e[0:1, :] / jnp.sum(e, axis=0, keepdims=True)
        f = (0.5 + 0.5 * lb) + (0.5 - 0.5 * lb) * jnp.tanh(hf_ref[:, cols])
        return _silu_of_half(hq_ref[:, cols]), 1.0 - f, jnp.log2(f)

    def gla_qkg(cols):
        logit = jnp.dot(side['lr'], pad_rank(w2_ref[:, cols].astype(bf)),
                        preferred_element_type=jnp.float32) + gb_ref[:, cols]
        log2_sig = (jnp.minimum(logit, 0.0) * LOG2E
                    - jnp.log2(1.0 + jnp.exp2(jnp.abs(logit) * -LOG2E)))
        return gq_ref[:, cols], gk_ref[:, cols], log2_sig / GLA_GATE_TAU

    _scan_and_readout([(hgrn_qkg, hi_ref, hz_ref, yh_ref, sth_ref),
                       (gla_qkg, gv_ref, gz_ref, yg_ref, stg_ref)], qkg_ref, o_ref, side_work)


def _row_slices(n):
    def spec(a):
        assert a.shape[0] % (16 * n) == 0
        return pl.BlockSpec((a.shape[0] // n, a.shape[1]), lambda h, t: (t, 0))
    return spec


def _scans(hf, pb, hgrn_lb, w2, gate_b, xb, rs, w_t, norm_w, w_out, hgrn_gain, gla_gain, pgw, pp, tt=512):
    T, D = xb.shape
    assert T % tt == 0 and tt % ATTN_BLOCK == 0 and ATTN_BLOCK % CHUNK == 0
    n = T // tt
    wo_rows = w_out.shape[0] // n
    assert wo_rows % LANE == 0 and D_HGRN % wo_rows == 0 and D_GLA % wo_rows == 0
    assert D_MAIN % GLA_GATE_RANK == 0 and w2.shape[0] == GLA_GATE_RANK
    n_hgrn = D_HGRN // wo_rows
    gate_rows = pl.BlockSpec((GLA_GATE_RANK, D), lambda h, t: (D_MAIN // GLA_GATE_RANK, 0))
    rows = lambda width: pl.BlockSpec((tt, width), lambda h, t: (t, 0))
    const = lambda a: pl.BlockSpec(a.shape, lambda h, t: (0, 0))
    piece = _row_slices(n)
    assert pb.shape[1] == 6 * PANEL and hf.shape[1] == PANEL
    return pl.pallas_call(
        _scan_kernel,
        grid=(1, n),
        in_specs=[const(hgrn_lb), rows(6 * PANEL), rows(PANEL), const(w2), const(gate_b),
                  rows(D), rows(LANE), gate_rows, const(norm_w),
                  piece(w_out),
                  pl.BlockSpec((1, wo_rows), lambda h, t: (0, jnp.minimum(t, n_hgrn - 1))),
                  pl.BlockSpec((1, wo_rows), lambda h, t: (0, jnp.maximum(t - n_hgrn, 0))),
                  piece(pgw), piece(pp)],
        out_specs=[rows(D_HGRN), rows(D_GLA), piece(w_out), piece(pgw), piece(pp)],
        out_shape=[jax.ShapeDtypeStruct((T, D_HGRN), jnp.bfloat16),
                   jax.ShapeDtypeStruct((T, D_GLA), jnp.bfloat16),
                   jax.ShapeDtypeStruct(w_out.shape, jnp.bfloat16),
                   jax.ShapeDtypeStruct(pgw.shape, jnp.bfloat16),
                   jax.ShapeDtypeStruct(pp.shape, jnp.bfloat16)],
        scratch_shapes=[pltpu.VMEM((HGRN_HEADS, HGRN_DV, HGRN_DK), jnp.float32),
                        pltpu.VMEM((GLA_HEADS, GLA_DV, GLA_DK), jnp.float32),
                        pltpu.VMEM((3, tt, D_HGRN), jnp.float32),
                        pltpu.VMEM((tt, max(D_HGRN, D_GLA)), jnp.float32)],
        compiler_params=pltpu.CompilerParams(dimension_semantics=("parallel", "arbitrary"),
                                             vmem_limit_bytes=SCAN_VMEM_LIMIT),
        name="scans",
    )(hgrn_lb, pb, hf, w2, gate_b, xb, rs, w_t, norm_w, w_out, hgrn_gain, gla_gain, pgw, pp)


OUT_SUB = 2
OUT_VMEM_LIMIT = 58 * 1024 * 1024


def _out_kernel(x_ref, yh_ref, yg_ref, p_ref, woh_ref, wog_ref, pp_ref, pnw_ref, gnw_ref,
                gw_ref, gb_ref, fnw_ref, o_ref):
    f32 = jnp.float32
    bf = jnp.bfloat16
    rows = o_ref.shape[0] // OUT_SUB
    subs = [slice(s * rows, (s + 1) * rows) for s in range(OUT_SUB)]
    h = [x_ref[r, :]
         + jnp.dot(yh_ref[r, :], woh_ref[...], preferred_element_type=f32)
         + jnp.dot(yg_ref[r, :], wog_ref[...], preferred_element_type=f32) for r in subs]
    e = [_rms(jnp.dot(p_ref[r, :].astype(bf), pp_ref[...], preferred_element_type=f32), pnw_ref[...])
         for r in subs]
    hn = [_rms(hs, gnw_ref[...]).astype(bf) for hs in h]
    acc = [jnp.dot(hs, gw_ref[...], preferred_element_type=f32) for hs in hn]
    for r, hs, es, a in zip(subs, h, e, acc):
        o_ref[r, :] = _rms(hs + _sigmoid(a + gb_ref[...]) * es, fnw_ref[...])


def _outproj(x2, yh, yg, p2, wo, pp, pnw, gnw, gw, gb, fnw, tm=512):
    T, D = x2.shape
    assert T % tm == 0 and tm % OUT_SUB == 0
    row = lambda w: pl.BlockSpec((tm, w), lambda i: (i, 0))
    once = pl.Buffered(1)
    full = lambda a: pl.BlockSpec(a.shape, lambda i: (0, 0), pipeline_mode=once)
    return pl.pallas_call(
        _out_kernel,
        grid=(T // tm,),
        in_specs=[row(D), row(D_HGRN), row(D_GLA), row(PLE_DIM),
                  pl.BlockSpec((D_HGRN, D), lambda i: (0, 0), pipeline_mode=once),
                  pl.BlockSpec((D_GLA, D), lambda i: (D_HGRN // D_GLA, 0), pipeline_mode=once),
                  full(pp), full(pnw), full(gnw), full(gw), full(gb), full(fnw)],
        out_specs=row(D),
        out_shape=jax.ShapeDtypeStruct((T, D), jnp.float32),
        compiler_params=pltpu.CompilerParams(
            dimension_semantics=("parallel",),
            vmem_limit_bytes=OUT_VMEM_LIMIT),
        name="outproj_ple",
    )(x2, yh, yg, p2, wo, wo, pp, pnw, gnw, gw, gb, fnw)


def kernel(x, p, norm_mix_w, w_in, hgrn_lb, gla_gate_w2, gla_gate_b, hgrn_norm_w, gla_norm_w,
           w_out, ple_proj, ple_norm_w, ple_gate_norm_w, ple_gate_w, ple_gate_b, final_norm_w):
    B, T, D = x.shape
    bf = jnp.bfloat16
    x2 = x.reshape(B * T, D)
    p2 = p[0].reshape(B * T, PLE_DIM)
    row = lambda a: a.reshape(1, -1)

    w_t = jnp.swapaxes(w_in[0], 0, 1)
    nw = row(norm_mix_w[0])
    hf, xb, rs = _inproj_first(x2, nw, w_t, 1, 0.5, "inproj_f")
    pb = _inproj_rest(xb, rs, nw, w_t, lambda j: j + jnp.minimum(j, 1), 6, (0, 2, 5), bf,
                      "inproj_b")

    y_h, y_g, wo_bf, pgw_bf, pp_bf = _scans(hf, pb, hgrn_lb, gla_gate_w2[0], row(gla_gate_b[0]), xb, rs, w_t, nw,
                                            w_out[0], row(hgrn_norm_w[0]), row(gla_norm_w[0]),
                                            ple_gate_w[0], ple_proj[0])

    out = _outproj(x2, y_h, y_g, p2, wo_bf, pp_bf, row(ple_norm_w[0]), row(ple_gate_norm_w[0]), pgw_bf,
                   row(ple_gate_b[0]), row(final_norm_w))
    return out.reshape(B, T, D)
```

```python
import functools

import jax
import jax.numpy as jnp
from jax import lax
from jax.experimental import pallas as pl
from jax.experimental.pallas import tpu as pltpu

D_MODEL = 2048
D_HGRN = 1024
HGRN_HEADS = 8
HGRN_DK = 128
HGRN_DV = 128
GLA_HEADS = 4
GLA_DK = 128
GLA_DV = 256
GLA_DK_TOTAL = 512
D_GLA = 1024
GLA_GATE_RANK = 16
GLA_GATE_TAU = 16.0
PLE_DIM = 256
EPS = 1e-6

LANE = 128
SUBLANE = 8
D_MAIN = 4 * D_HGRN + 2 * GLA_DK_TOTAL + 2 * D_GLA
PANEL = 1024
CHUNK = 128
ATTN_BLOCK = 128
DECAY_RANGE_LIMIT = 110.0
LOG2E = 1.4426950408889634
SCAN_VMEM_LIMIT = 56 * 1024 * 1024
INPROJ_VMEM_LIMIT = 62 * 1024 * 1024

_NT = (((1,), (1,)), ((), ()))
_TN = (((0,), (0,)), ((), ()))


def _sigmoid(x):
    return 0.5 + 0.5 * jnp.tanh(0.5 * x)


def _silu_of_half(h):
    return h + h * jnp.tanh(h)


def _rms(x, w):
    return x * lax.rsqrt(jnp.mean(x * x, axis=-1, keepdims=True) + EPS) * w


def _is_half_panel(half_panels):
    return functools.reduce(jnp.logical_or, [pl.program_id(0) == jp for jp in half_panels])


def _cast_weight_panel(w_ref, nw_ref, wb_ref):
    @pl.when(pl.program_id(1) == 0)
    def _():
        wb_ref[...] = (w_ref[...] * nw_ref[...]).astype(jnp.bfloat16)


X_SLOTS = 3


def _inproj_first_kernel(x_hbm, nw_ref, w_ref, o_ref, xb_ref, rs_ref, wb_ref, xbuf_ref, sem, *, scale):
    bf = jnp.bfloat16
    i = pl.program_id(0)
    n = pl.num_programs(0)
    tm = xbuf_ref.shape[1]
    ahead = X_SLOTS - 1

    def tile_copy(step):
        row0 = step * tm if isinstance(step, int) else pl.multiple_of(step * tm, tm)
        slot = step % X_SLOTS
        return pltpu.make_async_copy(x_hbm.at[pl.ds(row0, tm), :], xbuf_ref.at[slot], sem.at[slot])

    @pl.when(i == 0)
    def _():
        for step in range(ahead):
            tile_copy(step).start()
        wb_ref[...] = (w_ref[...] * nw_ref[...]).astype(bf)

    @pl.when(i + ahead < n)
    def _():
        tile_copy(i + ahead).start()

    tile_copy(i).wait()
    x = xbuf_ref[i % X_SLOTS]
    r = lax.rsqrt(jnp.mean(x * x, axis=-1, keepdims=True) + EPS)
    xb = x.astype(bf)
    o_ref[...] = lax.dot_general(xb, wb_ref[...], _NT, preferred_element_type=jnp.float32) * (r * scale)
    xb_ref[...] = xb
    rs_ref[...] = jnp.broadcast_to(r, rs_ref.shape)


def _inproj_rest_kernel(xb_ref, rs_ref, nw_ref, w_ref, o_ref, wb_ref, *, half_panels):
    _cast_weight_panel(w_ref, nw_ref, wb_ref)
    r = rs_ref[:, 0:1] * jnp.where(_is_half_panel(half_panels), 0.5, 1.0)
    o_ref[...] = (lax.dot_general(xb_ref[...], wb_ref[...], _NT, preferred_element_type=jnp.float32)
                  * r).astype(o_ref.dtype)


def _inproj_params():
    return pltpu.CompilerParams(dimension_semantics=("parallel", "arbitrary"),
                                vmem_limit_bytes=INPROJ_VMEM_LIMIT)


def _inproj_first(x2, norm_w, w_t, panel, scale, name, tm=1024):
    T, D = x2.shape
    assert T % tm == 0 and T // tm >= X_SLOTS - 1
    return pl.pallas_call(
        functools.partial(_inproj_first_kernel, scale=scale),
        grid=(T // tm,),
        in_specs=[pl.BlockSpec(memory_space=pl.ANY),
                  pl.BlockSpec((1, D), lambda i: (0, 0)),
                  pl.BlockSpec((PANEL, D), lambda i: (panel, 0), pipeline_mode=pl.Buffered(1))],
        out_specs=[pl.BlockSpec((tm, PANEL), lambda i: (i, 0)),
                   pl.BlockSpec((tm, D), lambda i: (i, 0)),
                   pl.BlockSpec((tm, LANE), lambda i: (i, 0))],
        out_shape=[jax.ShapeDtypeStruct((T, PANEL), jnp.float32),
                   jax.ShapeDtypeStruct((T, D), jnp.bfloat16),
                   jax.ShapeDtypeStruct((T, LANE), jnp.float32)],
        scratch_shapes=[pltpu.VMEM((PANEL, D), jnp.bfloat16),
                        pltpu.VMEM((X_SLOTS, tm, D), jnp.float32),
                        pltpu.SemaphoreType.DMA((X_SLOTS,))],
        compiler_params=pltpu.CompilerParams(dimension_semantics=("arbitrary",),
                                             vmem_limit_bytes=INPROJ_VMEM_LIMIT),
        name=name,
    )(x2, norm_w, w_t)


def _inproj_rest(xb, rs, norm_w, w_t, panel_of, n_panels, half_panels, out_dtype, name, tm=2048):
    T, D = xb.shape
    assert T % tm == 0
    return pl.pallas_call(
        functools.partial(_inproj_rest_kernel, half_panels=half_panels),
        grid=(n_panels, T // tm),
        in_specs=[pl.BlockSpec((tm, D), lambda j, i: (i, 0)),
                  pl.BlockSpec((tm, LANE), lambda j, i: (i, 0)),
                  pl.BlockSpec((1, D), lambda j, i: (0, 0)),
                  pl.BlockSpec((PANEL, D), lambda j, i: (panel_of(j), 0))],
        out_specs=pl.BlockSpec((tm, PANEL), lambda j, i: (i, j)),
        out_shape=jax.ShapeDtypeStruct((T, n_panels * PANEL), out_dtype),
        scratch_shapes=[pltpu.VMEM((PANEL, D), jnp.bfloat16)],
        compiler_params=_inproj_params(),
        name=name,
    )(xb, rs, norm_w, w_t)


def _scan_stages(qkg, v, state, finish):
    bf = jnp.bfloat16
    f32 = jnp.float32
    dv, dk = state.shape
    blk = ATTN_BLOCK
    row = lax.broadcasted_iota(jnp.int32, (blk, blk), 0)
    col = lax.broadcasted_iota(jnp.int32, (blk, blk), 1)
    causal = (col <= row) & (col >= (row & -CHUNK))
    res = {}

    def gates0():
        res['qkg'] = qkg()

    def gates():
        q, k, g = res.pop('qkg')
        g_hi = g.astype(bf)
        g_lo = (g - g_hi.astype(f32)).astype(bf)
        tri = causal.astype(bf)
        tri2 = jnp.concatenate([tri, tri], axis=1)
        res.update(q=q, k=k, n_blk=g.shape[0] // blk, n_chk=g.shape[0] // CHUNK)
        res['b'] = [jnp.dot(tri2, jnp.concatenate([g_hi[r * blk:(r + 1) * blk],
                                                     g_lo[r * blk:(r + 1) * blk]], axis=0),
                            preferred_element_type=f32) for r in range(res['n_blk'])]

    def decay():
        b = jnp.concatenate(res.pop('b'), axis=0)
        eb = jnp.exp2(b)
        q, k = res.pop('q'), res.pop('k')
        res['qh'] = q * eb.astype(bf) if q.dtype == bf else (q * eb).astype(bf)
        res['kt'] = k * (1.0 / eb).astype(bf) if k.dtype == bf else (k / eb).astype(bf)
        res['vb'] = v().astype(bf)
        res['eb_last'] = [eb[(c + 1) * CHUNK - 1:(c + 1) * CHUNK, :] for c in range(res['n_chk'])]
        res['b_min'] = jnp.min(jnp.concatenate(
            [b[(c + 1) * CHUNK - 1:(c + 1) * CHUNK, :] for c in range(res['n_chk'])], axis=0))

    def scores():
        qh, kt, vb = res['qh'], res['kt'], res['vb']
        res['attn'] = [lax.dot_general(qh[r * blk:(r + 1) * blk], kt[r * blk:(r + 1) * blk], _NT,
                                       preferred_element_type=f32) for r in range(res['n_blk'])]
        res['upd'] = [lax.dot_general(vb[c * CHUNK:(c + 1) * CHUNK], kt[c * CHUNK:(c + 1) * CHUNK], _TN,
                                      preferred_element_type=f32) for c in range(res['n_chk'])]

    def carry():
        res['attn'] = [jnp.where(causal, a, 0.0).astype(bf) for a in res['attn']]
        st, st_in = state, []
        for upd, eb_last in zip(res.pop('upd'), res.pop('eb_last')):
            st_in.append(st.astype(bf))
            st = (st + upd) * eb_last
        res.update(st_in=st_in, state=st)

    def outputs():
        qh, vb = res.pop('qh'), res.pop('vb')
        res['intra'] = [jnp.dot(a, vb[r * blk:(r + 1) * blk], preferred_element_type=f32)
                        for r, a in enumerate(res.pop('attn'))]
        res['inter'] = [lax.dot_general(qh[c * CHUNK:(c + 1) * CHUNK], s, _NT, preferred_element_type=f32)
                        for c, s in enumerate(res.pop('st_in'))]

    def done():
        finish(jnp.concatenate(res.pop('intra'), axis=0) + jnp.concatenate(res.pop('inter'), axis=0))

    return [gates0, gates, decay, scores, carry, outputs, done], res


def _scan_tile_exact(qkg_ref, v_ref, st_ref, o_ref):
    bf = jnp.bfloat16
    f32 = jnp.float32
    n_heads, dv, dk = st_ref.shape
    tt = o_ref.shape[0]
    rowk = lax.broadcasted_iota(jnp.int32, (CHUNK, dk), 0)
    rowc = lax.broadcasted_iota(jnp.int32, (CHUNK, CHUNK), 0)
    colc = lax.broadcasted_iota(jnp.int32, (CHUNK, CHUNK), 1)
    tri = (colc <= rowc).astype(bf)

    for h in range(n_heads):
        ks = slice(h * dk, (h + 1) * dk)
        vs = slice(h * dv, (h + 1) * dv)

        def chunk(c, carry, ks=ks, vs=vs, h=h):
            r0 = pl.multiple_of(c * CHUNK, CHUNK)
            rows = pl.ds(r0, CHUNK)
            qc = qkg_ref[0, rows, ks]
            kc = qkg_ref[1, rows, ks]
            gc = qkg_ref[2, rows, ks]
            g_hi = gc.astype(bf)
            g_lo = (gc - g_hi.astype(f32)).astype(bf)
            b = (jnp.dot(tri, g_hi, preferred_element_type=f32)
                 + jnp.dot(tri, g_lo, preferred_element_type=f32))
            qkg_ref[2, rows, ks] = b

            def columns(jg, attn):
                j0 = pl.multiple_of(jg * SUBLANE, SUBLANE)
                b_rows = qkg_ref[2, pl.ds(r0 + j0, SUBLANE), ks]
                k_rows = qkg_ref[1, pl.ds(r0 + j0, SUBLANE), ks]
                for r in range(SUBLANE):
                    j = j0 + r
                    decay = jnp.exp2(jnp.where(rowk >= j, b - b_rows[r:r + 1], 0.0))
                    colv = jnp.sum(qc * k_rows[r:r + 1] * decay, axis=1, keepdims=True)
                    attn = jnp.where(colc == j, colv, attn)
                return attn

            attn = lax.fori_loop(0, CHUNK // SUBLANE, columns, jnp.zeros((CHUNK, CHUNK), f32))
            attn = jnp.where(colc <= rowc, attn, 0.0).astype(bf)
            vb = v_ref[rows, vs].astype(bf)
            st = st_ref[h]
            b_last = b[CHUNK - 1:CHUNK, :]
            o_ref[rows, vs] = (
                jnp.dot(attn, vb, preferred_element_type=f32)
                + lax.dot_general((qc * jnp.exp2(b)).astype(bf), st.astype(bf), _NT,
                                  preferred_element_type=f32))
            k_dec = (kc * jnp.exp2(b_last - b)).astype(bf)
            st_ref[h] = st * jnp.exp2(b_last) + lax.dot_general(vb, k_dec, _TN,
                                                                preferred_element_type=f32)
            return carry

        lax.fori_loop(0, tt // CHUNK, chunk, 0)


def _readout(o, z, n_heads, dk):
    dv = o.shape[1] // n_heads

    def head(oh):
        return oh * lax.rsqrt(jnp.mean(oh * oh, axis=-1, keepdims=True) + EPS * dk)

    on = jnp.concatenate([head(o[:, h * dv:(h + 1) * dv]) for h in range(n_heads)], axis=1)
    return on.astype(z.dtype) * _silu_of_half(z)


def _scan_and_readout(groups, qkg_ref, o_ref, side_work=None):
    @pl.when(pl.program_id(1) == 0)
    def _():
        for *_, st_ref in groups:
            st_ref[...] = jnp.zeros_like(st_ref)

    if side_work is not None:
        side_work()
    units = []
    for qkg, v_ref, z_ref, y_ref, st_ref in groups:
        n_heads, dv, dk = st_ref.shape
        for h in range(n_heads):
            ks = slice(h * dk, (h + 1) * dk)
            vs = slice(h * dv, (h + 1) * dv)

            def finish(o, vs=vs, y_ref=y_ref, z_ref=z_ref, dk=dk):
                y_ref[:, vs] = _readout(o, z_ref[:, vs], 1, dk).astype(y_ref.dtype)

            units.append(_scan_stages(lambda qkg=qkg, ks=ks: qkg(ks), lambda v_ref=v_ref, vs=vs: v_ref[:, vs],
                                      st_ref[h], finish))
    n_stages = len(units[0][0])
    for tick in range(len(units) + n_stages - 1):
        for u in range(len(units)):
            if 0 <= tick - u < n_stages:
                units[u][0][tick - u]()
    b_min = units[0][1]['b_min']
    for _, res in units[1:]:
        b_min = jnp.minimum(b_min, res['b_min'])
    in_range = b_min >= -DECAY_RANGE_LIMIT

    @pl.when(in_range)
    def _():
        new_states = iter(res['state'] for _, res in units)
        for *_, st_ref in groups:
            for h in range(st_ref.shape[0]):
                st_ref[h] = next(new_states)

    @pl.when(jnp.logical_not(in_range))
    def _():
        for qkg, v_ref, z_ref, y_ref, st_ref in groups:
            n_heads, dv, dk = st_ref.shape
            q, k, g = qkg(slice(0, n_heads * dk))
            qkg_ref[0, :, :n_heads * dk] = q.astype(jnp.float32)
            qkg_ref[1, :, :n_heads * dk] = k.astype(jnp.float32)
            qkg_ref[2, :, :n_heads * dk] = g
            _scan_tile_exact(qkg_ref, v_ref, st_ref, o_ref)
            y_ref[...] = _readout(o_ref[:, :n_heads * dv], z_ref[...], n_heads, dk).astype(y_ref.dtype)


def _scan_kernel(lb_ref, hq_ref, hfa_ref, hfb_ref, hi_ref, hz_ref, gq_ref, gk_ref, gv_ref, gz_ref, w2_ref,
                 gb_ref, xba_ref, xbb_ref, rs_ref, wlr_ref, nw_ref, wo_ref, hgain_ref, ggain_ref, pgw_ref,
                 pp_ref, yh_ref, yg_ref, wo_bf_ref, pgw_bf_ref, pp_bf_ref, sth_ref, stg_ref, qkg_ref, o_ref):
    bf = jnp.bfloat16
    side = {}
    rank = wlr_ref.shape[0]

    def half_forget_logits(cols):
        w = hfa_ref.shape[1]
        if cols.stop <= w:
            return hfa_ref[:, cols]
        if cols.start >= w:
            return hfb_ref[:, cols.start - w:cols.stop - w]
        return jnp.concatenate([hfa_ref[:, cols.start:], hfb_ref[:, :cols.stop - w]], axis=1)

    def pad_rank(a):
        return jnp.concatenate([a, jnp.zeros((LANE - rank, a.shape[1]), a.dtype)], axis=0)

    def side_work():
        wlr = pad_rank((wlr_ref[...] * nw_ref[...]).astype(bf))
        k = xba_ref.shape[1]
        side['lr'] = ((lax.dot_general(xba_ref[...], wlr[:, :k], _NT, preferred_element_type=jnp.float32)
                       + lax.dot_general(xbb_ref[...], wlr[:, k:], _NT, preferred_element_type=jnp.float32))
                      * rs_ref[:, 0:1]).astype(bf)
        n_rows = wo_ref.shape[0]
        gain = jnp.where(pl.program_id(1) < D_HGRN // n_rows, hgain_ref[...], ggain_ref[...])
        diag = (lax.broadcasted_iota(jnp.int32, (n_rows, n_rows), 0)
                == lax.broadcasted_iota(jnp.int32, (n_rows, n_rows), 1))
        gain_col = jnp.sum(jnp.where(diag, gain, 0.0), axis=1, keepdims=True)
        wo_bf_ref[...] = (wo_ref[...] * gain_col).astype(bf)
        pgw_bf_ref[...] = pgw_ref[...].astype(bf)
        pp_bf_ref[...] = pp_ref[...].astype(bf)

    def hgrn_qkg(cols):
        lbp = lb_ref[:, cols]
        e = jnp.exp(lbp - jnp.max(lbp, axis=0, keepdims=True))
        lb = e[0:1, :] / jnp.sum(e, axis=0, keepdims=True)
        f = (0.5 + 0.5 * lb) + (0.5 - 0.5 * lb) * jnp.tanh(half_forget_logits(cols))
        return _silu_of_half(hq_ref[:, cols]), 1.0 - f, jnp.log2(f)

    def gla_qkg(cols):
        logit = jnp.dot(side['lr'], pad_rank(w2_ref[:, cols].astype(bf)),
                        preferred_element_type=jnp.float32) + gb_ref[:, cols]
        log2_sig = (jnp.minimum(logit, 0.0) * LOG2E
                    - jnp.log2(1.0 + jnp.exp2(jnp.abs(logit) * -LOG2E)))
        return gq_ref[:, cols], gk_ref[:, cols], log2_sig / GLA_GATE_TAU

    _scan_and_readout([(hgrn_qkg, hi_ref, hz_ref, yh_ref, sth_ref),
                       (gla_qkg, gv_ref, gz_ref, yg_ref, stg_ref)], qkg_ref, o_ref, side_work)


def _row_slices(n):
    def spec(a):
        assert a.shape[0] % (16 * n) == 0
        return pl.BlockSpec((a.shape[0] // n, a.shape[1]), lambda h, t: (t, 0))
    return spec


def _scans(hf, pb, hgrn_lb, w2, gate_b, xb, rs, w_t, norm_w, w_out, hgrn_gain, gla_gain, pgw, pp, tt=512):
    T, D = xb.shape
    assert T % tt == 0 and tt % ATTN_BLOCK == 0 and ATTN_BLOCK % CHUNK == 0
    n = T // tt
    wo_rows = w_out.shape[0] // n
    assert wo_rows % LANE == 0 and D_HGRN % wo_rows == 0 and D_GLA % wo_rows == 0
    assert D_MAIN % GLA_GATE_RANK == 0 and w2.shape[0] == GLA_GATE_RANK
    n_hgrn = D_HGRN // wo_rows
    gate_rows = pl.BlockSpec((GLA_GATE_RANK, D), lambda h, t: (D_MAIN // GLA_GATE_RANK, 0))
    panel = lambda c: pl.BlockSpec((tt, PANEL), lambda h, t: (t, c))
    half = lambda c: pl.BlockSpec((tt, GLA_DK_TOTAL), lambda h, t: (t, c))
    rows = lambda width: pl.BlockSpec((tt, width), lambda h, t: (t, 0))
    const = lambda a: pl.BlockSpec(a.shape, lambda h, t: (0, 0))
    piece = _row_slices(n)
    gq0 = 3 * PANEL // GLA_DK_TOTAL
    return pl.pallas_call(
        _scan_kernel,
        grid=(1, n),
        in_specs=[const(hgrn_lb), panel(0), half(0), half(1), panel(1), panel(2),
                  half(gq0), half(gq0 + 1), panel(4), panel(5), const(w2), const(gate_b),
                  panel(0), panel(1), rows(LANE), gate_rows, const(norm_w),
                  piece(w_out),
                  pl.BlockSpec((1, wo_rows), lambda h, t: (0, jnp.minimum(t, n_hgrn - 1))),
                  pl.BlockSpec((1, wo_rows), lambda h, t: (0, jnp.maximum(t - n_hgrn, 0))),
                  piece(pgw), piece(pp)],
        out_specs=[rows(D_HGRN), rows(D_GLA), piece(w_out), piece(pgw), piece(pp)],
        out_shape=[jax.ShapeDtypeStruct((T, D_HGRN), jnp.bfloat16),
                   jax.ShapeDtypeStruct((T, D_GLA), jnp.bfloat16),
                   jax.ShapeDtypeStruct(w_out.shape, jnp.bfloat16),
                   jax.ShapeDtypeStruct(pgw.shape, jnp.bfloat16),
                   jax.ShapeDtypeStruct(pp.shape, jnp.bfloat16)],
        scratch_shapes=[pltpu.VMEM((HGRN_HEADS, HGRN_DV, HGRN_DK), jnp.float32),
                        pltpu.VMEM((GLA_HEADS, GLA_DV, GLA_DK), jnp.float32),
                        pltpu.VMEM((3, tt, D_HGRN), jnp.float32),
                        pltpu.VMEM((tt, max(D_HGRN, D_GLA)), jnp.float32)],
        compiler_params=pltpu.CompilerParams(dimension_semantics=("parallel", "arbitrary"),
                                             vmem_limit_bytes=SCAN_VMEM_LIMIT),
        name="scans",
    )(hgrn_lb, pb, hf, hf, pb, pb, pb, pb, pb, pb, w2, gate_b, xb, xb, rs, w_t, norm_w,
      w_out, hgrn_gain, gla_gain, pgw, pp)


OUT_SUB = 2
OUT_VMEM_LIMIT = 58 * 1024 * 1024


def _out_kernel(x_ref, yh_ref, yg_ref, p_ref, woh_ref, wog_ref, pp_ref, pnw_ref, gnw_ref,
                gw_ref, gb_ref, fnw_ref, o_ref):
    f32 = jnp.float32
    bf = jnp.bfloat16
    rows = o_ref.shape[0] // OUT_SUB
    subs = [slice(s * rows, (s + 1) * rows) for s in range(OUT_SUB)]
    h = [x_ref[r, :]
         + jnp.dot(yh_ref[r, :], woh_ref[...], preferred_element_type=f32)
         + jnp.dot(yg_ref[r, :], wog_ref[...], preferred_element_type=f32) for r in subs]
    e = [_rms(jnp.dot(p_ref[r, :].astype(bf), pp_ref[...], preferred_element_type=f32), pnw_ref[...])
         for r in subs]
    hn = [_rms(hs, gnw_ref[...]).astype(bf) for hs in h]
    acc = [jnp.dot(hs, gw_ref[...], preferred_element_type=f32) for hs in hn]
    for r, hs, es, a in zip(subs, h, e, acc):
        o_ref[r, :] = _rms(hs + _sigmoid(a + gb_ref[...]) * es, fnw_ref[...])


def _outproj(x2, yh, yg, p2, wo, pp, pnw, gnw, gw, gb, fnw, tm=512):
    T, D = x2.shape
    assert T % tm == 0 and tm % OUT_SUB == 0
    row = lambda w: pl.BlockSpec((tm, w), lambda i: (i, 0))
    once = pl.Buffered(1)
    full = lambda a: pl.BlockSpec(a.shape, lambda i: (0, 0), pipeline_mode=once)
    return pl.pallas_call(
        _out_kernel,
        grid=(T // tm,),
        in_specs=[row(D), row(D_HGRN), row(D_GLA), row(PLE_DIM),
                  pl.BlockSpec((D_HGRN, D), lambda i: (0, 0), pipeline_mode=once),
                  pl.BlockSpec((D_GLA, D), lambda i: (D_HGRN // D_GLA, 0), pipeline_mode=once),
                  full(pp), full(pnw), full(gnw), full(gw), full(gb), full(fnw)],
        out_specs=row(D),
        out_shape=jax.ShapeDtypeStruct((T, D), jnp.float32),
        compiler_params=pltpu.CompilerParams(
            dimension_semantics=("parallel",),
            vmem_limit_bytes=OUT_VMEM_LIMIT),
        name="outproj_ple",
    )(x2, yh, yg, p2, wo, wo, pp, pnw, gnw, gw, gb, fnw)


def kernel(x, p, norm_mix_w, w_in, hgrn_lb, gla_gate_w2, gla_gate_b, hgrn_norm_w, gla_norm_w,
           w_out, ple_proj, ple_norm_w, ple_gate_norm_w, ple_gate_w, ple_gate_b, final_norm_w):
    B, T, D = x.shape
    bf = jnp.bfloat16
    x2 = x.reshape(B * T, D)
    p2 = p[0].reshape(B * T, PLE_DIM)
    row = lambda a: a.reshape(1, -1)

    w_t = jnp.swapaxes(w_in[0], 0, 1)
    nw = row(norm_mix_w[0])
    hf, xb, rs = _inproj_first(x2, nw, w_t, 1, 0.5, "inproj_f")
    pb = _inproj_rest(xb, rs, nw, w_t, lambda j: j + jnp.minimum(j, 1), 6, (0, 2, 5), bf,
                      "inproj_b")

    y_h, y_g, wo_bf, pgw_bf, pp_bf = _scans(hf, pb, hgrn_lb, gla_gate_w2[0], row(gla_gate_b[0]), xb, rs, w_t, nw,
                                            w_out[0], row(hgrn_norm_w[0]), row(gla_norm_w[0]),
                                            ple_gate_w[0], ple_proj[0])

    out = _outproj(x2, y_h, y_g, p2, wo_bf, pp_bf, row(ple_norm_w[0]), row(ple_gate_norm_w[0]), pgw_bf,
                   row(ple_gate_b[0]), row(final_norm_w))
    return out.reshape(B, T, D)
```

```python
import functools

import jax
import jax.numpy as jnp
from jax import lax
from jax.experimental import pallas as pl
from jax.experimental.pallas import tpu as pltpu

D_MODEL = 2048
D_HGRN = 1024
HGRN_HEADS = 8
HGRN_DK = 128
HGRN_DV = 128
GLA_HEADS = 4
GLA_DK = 128
GLA_DV = 256
GLA_DK_TOTAL = 512
D_GLA = 1024
GLA_GATE_RANK = 16
GLA_GATE_TAU = 16.0
PLE_DIM = 256
EPS = 1e-6

LANE = 128
SUBLANE = 8
D_MAIN = 4 * D_HGRN + 2 * GLA_DK_TOTAL + 2 * D_GLA
PANEL = 1024
CHUNK = 128
ATTN_BLOCK = 128
DECAY_RANGE_LIMIT = 110.0
LOG2E = 1.4426950408889634
SCAN_VMEM_LIMIT = 56 * 1024 * 1024
INPROJ_VMEM_LIMIT = 62 * 1024 * 1024

_NT = (((1,), (1,)), ((), ()))
_TN = (((0,), (0,)), ((), ()))


def _sigmoid(x):
    return 0.5 + 0.5 * jnp.tanh(0.5 * x)


def _silu_of_half(h):
    return h + h * jnp.tanh(h)


def _rms(x, w):
    return x * lax.rsqrt(jnp.mean(x * x, axis=-1, keepdims=True) + EPS) * w


def _is_half_panel(half_panels):
    return functools.reduce(jnp.logical_or, [pl.program_id(0) == jp for jp in half_panels])


def _cast_weight_panel(w_ref, nw_ref, wb_ref):
    @pl.when(pl.program_id(1) == 0)
    def _():
        wb_ref[...] = (w_ref[...] * nw_ref[...]).astype(jnp.bfloat16)


X_SLOTS = 3


def _inproj_first_kernel(x_hbm, nw_ref, w_ref, o_ref, xb_ref, rs_ref, wb_ref, xbuf_ref, sem, *, scale):
    bf = jnp.bfloat16
    i = pl.program_id(0)
    n = pl.num_programs(0)
    tm = xbuf_ref.shape[1]
    ahead = X_SLOTS - 1

    def tile_copy(step):
        row0 = step * tm if isinstance(step, int) else pl.multiple_of(step * tm, tm)
        slot = step % X_SLOTS
        return pltpu.make_async_copy(x_hbm.at[pl.ds(row0, tm), :], xbuf_ref.at[slot], sem.at[slot])

    @pl.when(i == 0)
    def _():
        for step in range(ahead):
            tile_copy(step).start()
        wb_ref[...] = (w_ref[...] * nw_ref[...]).astype(bf)

    @pl.when(i + ahead < n)
    def _():
        tile_copy(i + ahead).start()

    tile_copy(i).wait()
    x = xbuf_ref[i % X_SLOTS]
    r = lax.rsqrt(jnp.mean(x * x, axis=-1, keepdims=True) + EPS)
    xb = x.astype(bf)
    o_ref[...] = lax.dot_general(xb, wb_ref[...], _NT, preferred_element_type=jnp.float32) * (r * scale)
    xb_ref[...] = xb
    rs_ref[...] = jnp.broadcast_to(r, rs_ref.shape)


def _inproj_rest_kernel(xb_hbm, rs_ref, nw_ref, w_ref, o_ref, wb_ref, xbuf_ref, sem, *, half_panels):
    tm = xbuf_ref.shape[1]
    n_i = xb_hbm.shape[0] // tm
    n = pl.num_programs(0) * n_i
    s = pl.program_id(0) * n_i + pl.program_id(1)
    ahead = X_SLOTS - 1

    def tile_copy(step):
        row0 = (step % n_i) * tm
        if not isinstance(step, int):
            row0 = pl.multiple_of(row0, tm)
        slot = step % X_SLOTS
        return pltpu.make_async_copy(xb_hbm.at[pl.ds(row0, tm), :], xbuf_ref.at[slot], sem.at[slot])

    @pl.when(s == 0)
    def _():
        for step in range(ahead):
            tile_copy(step).start()

    @pl.when(s + ahead < n)
    def _():
        tile_copy(s + ahead).start()

    _cast_weight_panel(w_ref, nw_ref, wb_ref)
    tile_copy(s).wait()
    r = rs_ref[:, 0:1] * jnp.where(_is_half_panel(half_panels), 0.5, 1.0)
    o_ref[...] = (lax.dot_general(xbuf_ref[s % X_SLOTS], wb_ref[...], _NT, preferred_element_type=jnp.float32)
                  * r).astype(o_ref.dtype)


def _inproj_params():
    return pltpu.CompilerParams(dimension_semantics=("parallel", "arbitrary"),
                                vmem_limit_bytes=INPROJ_VMEM_LIMIT)


def _inproj_first(x2, norm_w, w_t, panel, scale, name, tm=1024):
    T, D = x2.shape
    assert T % tm == 0 and T // tm >= X_SLOTS - 1
    return pl.pallas_call(
        functools.partial(_inproj_first_kernel, scale=scale),
        grid=(T // tm,),
        in_specs=[pl.BlockSpec(memory_space=pl.ANY),
                  pl.BlockSpec((1, D), lambda i: (0, 0)),
                  pl.BlockSpec((PANEL, D), lambda i: (panel, 0), pipeline_mode=pl.Buffered(1))],
        out_specs=[pl.BlockSpec((tm, PANEL), lambda i: (i, 0)),
                   pl.BlockSpec((tm, D), lambda i: (i, 0)),
                   pl.BlockSpec((tm, LANE), lambda i: (i, 0))],
        out_shape=[jax.ShapeDtypeStruct((T, PANEL), jnp.float32),
                   jax.ShapeDtypeStruct((T, D), jnp.bfloat16),
                   jax.ShapeDtypeStruct((T, LANE), jnp.float32)],
        scratch_shapes=[pltpu.VMEM((PANEL, D), jnp.bfloat16),
                        pltpu.VMEM((X_SLOTS, tm, D), jnp.float32),
                        pltpu.SemaphoreType.DMA((X_SLOTS,))],
        compiler_params=pltpu.CompilerParams(dimension_semantics=("arbitrary",),
                                             vmem_limit_bytes=INPROJ_VMEM_LIMIT),
        name=name,
    )(x2, norm_w, w_t)


def _inproj_rest(xb, rs, norm_w, w_t, panel_of, n_panels, half_panels, out_dtype, name, tm=1024):
    T, D = xb.shape
    assert T % tm == 0 and n_panels * (T // tm) >= X_SLOTS - 1
    return pl.pallas_call(
        functools.partial(_inproj_rest_kernel, half_panels=half_panels),
        grid=(n_panels, T // tm),
        in_specs=[pl.BlockSpec(memory_space=pl.ANY),
                  pl.BlockSpec((tm, LANE), lambda j, i: (i, 0)),
                  pl.BlockSpec((1, D), lambda j, i: (0, 0)),
                  pl.BlockSpec((PANEL, D), lambda j, i: (panel_of(j), 0))],
        out_specs=pl.BlockSpec((tm, PANEL), lambda j, i: (i, j)),
        out_shape=jax.ShapeDtypeStruct((T, n_panels * PANEL), out_dtype),
        scratch_shapes=[pltpu.VMEM((PANEL, D), jnp.bfloat16),
                        pltpu.VMEM((X_SLOTS, tm, D), jnp.bfloat16),
                        pltpu.SemaphoreType.DMA((X_SLOTS,))],
        compiler_params=pltpu.CompilerParams(dimension_semantics=("arbitrary", "arbitrary"),
                                             vmem_limit_bytes=INPROJ_VMEM_LIMIT),
        name=name,
    )(xb, rs, norm_w, w_t)


def _scan_stages(qkg, v, state, finish):
    bf = jnp.bfloat16
    f32 = jnp.float32
    dv, dk = state.shape
    blk = ATTN_BLOCK
    row = lax.broadcasted_iota(jnp.int32, (blk, blk), 0)
    col = lax.broadcasted_iota(jnp.int32, (blk, blk), 1)
    causal = (col <= row) & (col >= (row & -CHUNK))
    res = {}

    def gates0():
        res['qkg'] = qkg()

    def gates():
        q, k, g = res.pop('qkg')
        g_hi = g.astype(bf)
        g_lo = (g - g_hi.astype(f32)).astype(bf)
        tri = causal.astype(bf)
        tri2 = jnp.concatenate([tri, tri], axis=1)
        res.update(q=q, k=k, n_blk=g.shape[0] // blk, n_chk=g.shape[0] // CHUNK)
        res['b'] = [jnp.dot(tri2, jnp.concatenate([g_hi[r * blk:(r + 1) * blk],
                                                     g_lo[r * blk:(r + 1) * blk]], axis=0),
                            preferred_element_type=f32) for r in range(res['n_blk'])]

    def decay():
        b = jnp.concatenate(res.pop('b'), axis=0)
        eb = jnp.exp2(b)
        q, k = res.pop('q'), res.pop('k')
        res['qh'] = q * eb.astype(bf) if q.dtype == bf else (q * eb).astype(bf)
        res['kt'] = k * (1.0 / eb).astype(bf) if k.dtype == bf else (k / eb).astype(bf)
        res['vb'] = v().astype(bf)
        res['eb_last'] = [eb[(c + 1) * CHUNK - 1:(c + 1) * CHUNK, :] for c in range(res['n_chk'])]
        res['b_min'] = jnp.min(jnp.concatenate(
            [b[(c + 1) * CHUNK - 1:(c + 1) * CHUNK, :] for c in range(res['n_chk'])], axis=0))

    def scores():
        qh, kt, vb = res['qh'], res['kt'], res['vb']
        res['attn'] = [lax.dot_general(qh[r * blk:(r + 1) * blk], kt[r * blk:(r + 1) * blk], _NT,
                                       preferred_element_type=f32) for r in range(res['n_blk'])]
        res['upd'] = [lax.dot_general(vb[c * CHUNK:(c + 1) * CHUNK], kt[c * CHUNK:(c + 1) * CHUNK], _TN,
                                      preferred_element_type=f32) for c in range(res['n_chk'])]

    def carry():
        res['attn'] = [jnp.where(causal, a, 0.0).astype(bf) for a in res['attn']]
        st, st_in = state, []
        for upd, eb_last in zip(res.pop('upd'), res.pop('eb_last')):
            st_in.append(st.astype(bf))
            st = (st + upd) * eb_last
        res.update(st_in=st_in, state=st)

    def outputs():
        qh, vb = res.pop('qh'), res.pop('vb')
        res['intra'] = [jnp.dot(a, vb[r * blk:(r + 1) * blk], preferred_element_type=f32)
                        for r, a in enumerate(res.pop('attn'))]
        res['inter'] = [lax.dot_general(qh[c * CHUNK:(c + 1) * CHUNK], s, _NT, preferred_element_type=f32)
                        for c, s in enumerate(res.pop('st_in'))]

    def done():
        finish(jnp.concatenate(res.pop('intra'), axis=0) + jnp.concatenate(res.pop('inter'), axis=0))

    return [gates0, gates, decay, scores, carry, outputs, done], res


def _scan_tile_exact(qkg_ref, v_ref, st_ref, o_ref):
    bf = jnp.bfloat16
    f32 = jnp.float32
    n_heads, dv, dk = st_ref.shape
    tt = o_ref.shape[0]
    rowk = lax.broadcasted_iota(jnp.int32, (CHUNK, dk), 0)
    rowc = lax.broadcasted_iota(jnp.int32, (CHUNK, CHUNK), 0)
    colc = lax.broadcasted_iota(jnp.int32, (CHUNK, CHUNK), 1)
    tri = (colc <= rowc).astype(bf)

    for h in range(n_heads):
        ks = slice(h * dk, (h + 1) * dk)
        vs = slice(h * dv, (h + 1) * dv)

        def chunk(c, carry, ks=ks, vs=vs, h=h):
            r0 = pl.multiple_of(c * CHUNK, CHUNK)
            rows = pl.ds(r0, CHUNK)
            qc = qkg_ref[0, rows, ks]
            kc = qkg_ref[1, rows, ks]
            gc = qkg_ref[2, rows, ks]
            g_hi = gc.astype(bf)
            g_lo = (gc - g_hi.astype(f32)).astype(bf)
            b = (jnp.dot(tri, g_hi, preferred_element_type=f32)
                 + jnp.dot(tri, g_lo, preferred_element_type=f32))
            qkg_ref[2, rows, ks] = b

            def columns(jg, attn):
                j0 = pl.multiple_of(jg * SUBLANE, SUBLANE)
                b_rows = qkg_ref[2, pl.ds(r0 + j0, SUBLANE), ks]
                k_rows = qkg_ref[1, pl.ds(r0 + j0, SUBLANE), ks]
                for r in range(SUBLANE):
                    j = j0 + r
                    decay = jnp.exp2(jnp.where(rowk >= j, b - b_rows[r:r + 1], 0.0))
                    colv = jnp.sum(qc * k_rows[r:r + 1] * decay, axis=1, keepdims=True)
                    attn = jnp.where(colc == j, colv, attn)
                return attn

            attn = lax.fori_loop(0, CHUNK // SUBLANE, columns, jnp.zeros((CHUNK, CHUNK), f32))
            attn = jnp.where(colc <= rowc, attn, 0.0).astype(bf)
            vb = v_ref[rows, vs].astype(bf)
            st = st_ref[h]
            b_last = b[CHUNK - 1:CHUNK, :]
            o_ref[rows, vs] = (
                jnp.dot(attn, vb, preferred_element_type=f32)
                + lax.dot_general((qc * jnp.exp2(b)).astype(bf), st.astype(bf), _NT,
                                  preferred_element_type=f32))
            k_dec = (kc * jnp.exp2(b_last - b)).astype(bf)
            st_ref[h] = st * jnp.exp2(b_last) + lax.dot_general(vb, k_dec, _TN,
                                                                preferred_element_type=f32)
            return carry

        lax.fori_loop(0, tt // CHUNK, chunk, 0)


def _readout(o, z, n_heads, dk):
    dv = o.shape[1] // n_heads

    def head(oh):
        return oh * lax.rsqrt(jnp.mean(oh * oh, axis=-1, keepdims=True) + EPS * dk)

    on = jnp.concatenate([head(o[:, h * dv:(h + 1) * dv]) for h in range(n_heads)], axis=1)
    return on.astype(z.dtype) * _silu_of_half(z)


def _scan_and_readout(groups, qkg_ref, o_ref, side_work=None):
    @pl.when(pl.program_id(1) == 0)
    def _():
        for *_, st_ref in groups:
            st_ref[...] = jnp.zeros_like(st_ref)

    if side_work is not None:
        side_work()
    units = []
    for qkg, v_ref, z_ref, y_ref, st_ref in groups:
        n_heads, dv, dk = st_ref.shape
        for h in range(n_heads):
            ks = slice(h * dk, (h + 1) * dk)
            vs = slice(h * dv, (h + 1) * dv)

            def finish(o, vs=vs, y_ref=y_ref, z_ref=z_ref, dk=dk):
                y_ref[:, vs] = _readout(o, z_ref[:, vs], 1, dk).astype(y_ref.dtype)

            units.append(_scan_stages(lambda qkg=qkg, ks=ks: qkg(ks), lambda v_ref=v_ref, vs=vs: v_ref[:, vs],
                                      st_ref[h], finish))
    n_stages = len(units[0][0])
    for tick in range(len(units) + n_stages - 1):
        for u in range(len(units)):
            if 0 <= tick - u < n_stages:
                units[u][0][tick - u]()
    b_min = units[0][1]['b_min']
    for _, res in units[1:]:
        b_min = jnp.minimum(b_min, res['b_min'])
    in_range = b_min >= -DECAY_RANGE_LIMIT

    @pl.when(in_range)
    def _():
        new_states = iter(res['state'] for _, res in units)
        for *_, st_ref in groups:
            for h in range(st_ref.shape[0]):
                st_ref[h] = next(new_states)

    @pl.when(jnp.logical_not(in_range))
    def _():
        for qkg, v_ref, z_ref, y_ref, st_ref in groups:
            n_heads, dv, dk = st_ref.shape
            q, k, g = qkg(slice(0, n_heads * dk))
            qkg_ref[0, :, :n_heads * dk] = q.astype(jnp.float32)
            qkg_ref[1, :, :n_heads * dk] = k.astype(jnp.float32)
            qkg_ref[2, :, :n_heads * dk] = g
            _scan_tile_exact(qkg_ref, v_ref, st_ref, o_ref)
            y_ref[...] = _readout(o_ref[:, :n_heads * dv], z_ref[...], n_heads, dk).astype(y_ref.dtype)


def _scan_kernel(lb_ref, hq_ref, hf_ref, hi_ref, hz_ref, gq_ref, gk_ref, gv_ref, gz_ref, w2_ref, gb_ref,
                 xb_ref, rs_ref, wlr_ref, nw_ref, wo_ref, hgain_ref, ggain_ref, pgw_ref, pp_ref,
                 yh_ref, yg_ref, wo_bf_ref, pgw_bf_ref, pp_bf_ref, sth_ref, stg_ref, qkg_ref, o_ref):
    bf = jnp.bfloat16
    side = {}
    rank = wlr_ref.shape[0]

    def pad_rank(a):
        return jnp.concatenate([a, jnp.zeros((LANE - rank, a.shape[1]), a.dtype)], axis=0)

    def side_work():
        wlr = pad_rank((wlr_ref[...] * nw_ref[...]).astype(bf))
        side['lr'] = (lax.dot_general(xb_ref[...], wlr, _NT, preferred_element_type=jnp.float32)
                      * rs_ref[:, 0:1]).astype(bf)
        n_rows = wo_ref.shape[0]
        gain = jnp.where(pl.program_id(1) < D_HGRN // n_rows, hgain_ref[...], ggain_ref[...])
        diag = (lax.broadcasted_iota(jnp.int32, (n_rows, n_rows), 0)
                == lax.broadcasted_iota(jnp.int32, (n_rows, n_rows), 1))
        gain_col = jnp.sum(jnp.where(diag, gain, 0.0), axis=1, keepdims=True)
        wo_bf_ref[...] = (wo_ref[...] * gain_col).astype(bf)
        pgw_bf_ref[...] = pgw_ref[...].astype(bf)
        pp_bf_ref[...] = pp_ref[...].astype(bf)

    def hgrn_qkg(cols):
        lbp = lb_ref[:, cols]
        e = jnp.exp(lbp - jnp.max(lbp, axis=0, keepdims=True))
        lb = e[0:1, :] / jnp.sum(e, axis=0, keepdims=True)
        f = (0.5 + 0.5 * lb) + (0.5 - 0.5 * lb) * jnp.tanh(hf_ref[:, cols])
        return _silu_of_half(hq_ref[:, cols]), 1.0 - f, jnp.log2(f)

    def gla_qkg(cols):
        logit = jnp.dot(side['lr'], pad_rank(w2_ref[:, cols].astype(bf)),
                        preferred_element_type=jnp.float32) + gb_ref[:, cols]
        log2_sig = (jnp.minimum(logit, 0.0) * LOG2E
                    - jnp.log2(1.0 + jnp.exp2(jnp.abs(logit) * -LOG2E)))
        return gq_ref[:, cols], gk_ref[:, cols], log2_sig / GLA_GATE_TAU

    _scan_and_readout([(hgrn_qkg, hi_ref, hz_ref, yh_ref, sth_ref),
                       (gla_qkg, gv_ref, gz_ref, yg_ref, stg_ref)], qkg_ref, o_ref, side_work)


def _row_slices(n):
    def spec(a):
        assert a.shape[0] % (16 * n) == 0
        return pl.BlockSpec((a.shape[0] // n, a.shape[1]), lambda h, t: (t, 0))
    return spec


def _scans(hf, pb, hgrn_lb, w2, gate_b, xb, rs, w_t, norm_w, w_out, hgrn_gain, gla_gain, pgw, pp, tt=512):
    T, D = xb.shape
    assert T % tt == 0 and tt % ATTN_BLOCK == 0 and ATTN_BLOCK % CHUNK == 0
    n = T // tt
    wo_rows = w_out.shape[0] // n
    assert wo_rows % LANE == 0 and D_HGRN % wo_rows == 0 and D_GLA % wo_rows == 0
    assert D_MAIN % GLA_GATE_RANK == 0 and w2.shape[0] == GLA_GATE_RANK
    n_hgrn = D_HGRN // wo_rows
    gate_rows = pl.BlockSpec((GLA_GATE_RANK, D), lambda h, t: (D_MAIN // GLA_GATE_RANK, 0))
    panel = lambda c: pl.BlockSpec((tt, PANEL), lambda h, t: (t, c))
    half = lambda c: pl.BlockSpec((tt, GLA_DK_TOTAL), lambda h, t: (t, c))
    rows = lambda width: pl.BlockSpec((tt, width), lambda h, t: (t, 0))
    const = lambda a: pl.BlockSpec(a.shape, lambda h, t: (0, 0))
    piece = _row_slices(n)
    gq0 = 3 * PANEL // GLA_DK_TOTAL
    return pl.pallas_call(
        _scan_kernel,
        grid=(1, n),
        in_specs=[const(hgrn_lb), panel(0), panel(0), panel(1), panel(2),
                  half(gq0), half(gq0 + 1), panel(4), panel(5), const(w2), const(gate_b),
                  rows(D), rows(LANE), gate_rows, const(norm_w),
                  piece(w_out),
                  pl.BlockSpec((1, wo_rows), lambda h, t: (0, jnp.minimum(t, n_hgrn - 1))),
                  pl.BlockSpec((1, wo_rows), lambda h, t: (0, jnp.maximum(t - n_hgrn, 0))),
                  piece(pgw), piece(pp)],
        out_specs=[rows(D_HGRN), rows(D_GLA), piece(w_out), piece(pgw), piece(pp)],
        out_shape=[jax.ShapeDtypeStruct((T, D_HGRN), jnp.bfloat16),
                   jax.ShapeDtypeStruct((T, D_GLA), jnp.bfloat16),
                   jax.ShapeDtypeStruct(w_out.shape, jnp.bfloat16),
                   jax.ShapeDtypeStruct(pgw.shape, jnp.bfloat16),
                   jax.ShapeDtypeStruct(pp.shape, jnp.bfloat16)],
        scratch_shapes=[pltpu.VMEM((HGRN_HEADS, HGRN_DV, HGRN_DK), jnp.float32),
                        pltpu.VMEM((GLA_HEADS, GLA_DV, GLA_DK), jnp.float32),
                        pltpu.VMEM((3, tt, D_HGRN), jnp.float32),
                        pltpu.VMEM((tt, max(D_HGRN, D_GLA)), jnp.float32)],
        compiler_params=pltpu.CompilerParams(dimension_semantics=("parallel", "arbitrary"),
                                             vmem_limit_bytes=SCAN_VMEM_LIMIT),
        name="scans",
    )(hgrn_lb, pb, hf, pb, pb, pb, pb, pb, pb, w2, gate_b, xb, rs, w_t, norm_w,
      w_out, hgrn_gain, gla_gain, pgw, pp)


OUT_SUB = 2
OUT_VMEM_LIMIT = 58 * 1024 * 1024


def _out_kernel(x_ref, yh_ref, yg_ref, p_ref, woh_ref, wog_ref, pp_ref, pnw_ref, gnw_ref,
                gw_ref, gb_ref, fnw_ref, o_ref):
    f32 = jnp.float32
    bf = jnp.bfloat16
    rows = o_ref.shape[0] // OUT_SUB
    subs = [slice(s * rows, (s + 1) * rows) for s in range(OUT_SUB)]
    h = [x_ref[r, :]
         + jnp.dot(yh_ref[r, :], woh_ref[...], preferred_element_type=f32)
         + jnp.dot(yg_ref[r, :], wog_ref[...], preferred_element_type=f32) for r in subs]
    e = [_rms(jnp.dot(p_ref[r, :].astype(bf), pp_ref[...], preferred_element_type=f32), pnw_ref[...])
         for r in subs]
    hn = [_rms(hs, gnw_ref[...]).astype(bf) for hs in h]
    acc = [jnp.dot(hs, gw_ref[...], preferred_element_type=f32) for hs in hn]
    for r, hs, es, a in zip(subs, h, e, acc):
        o_ref[r, :] = _rms(hs + _sigmoid(a + gb_ref[...]) * es, fnw_ref[...])


def _outproj(x2, yh, yg, p2, wo, pp, pnw, gnw, gw, gb, fnw, tm=512):
    T, D = x2.shape
    assert T % tm == 0 and tm % OUT_SUB == 0
    row = lambda w: pl.BlockSpec((tm, w), lambda i: (i, 0))
    once = pl.Buffered(1)
    full = lambda a: pl.BlockSpec(a.shape, lambda i: (0, 0), pipeline_mode=once)
    return pl.pallas_call(
        _out_kernel,
        grid=(T // tm,),
        in_specs=[row(D), row(D_HGRN), row(D_GLA), row(PLE_DIM),
                  pl.BlockSpec((D_HGRN, D), lambda i: (0, 0), pipeline_mode=once),
                  pl.BlockSpec((D_GLA, D), lambda i: (D_HGRN // D_GLA, 0), pipeline_mode=once),
                  full(pp), full(pnw), full(gnw), full(gw), full(gb), full(fnw)],
        out_specs=row(D),
        out_shape=jax.ShapeDtypeStruct((T, D), jnp.float32),
        compiler_params=pltpu.CompilerParams(
            dimension_semantics=("parallel",),
            vmem_limit_bytes=OUT_VMEM_LIMIT),
        name="outproj_ple",
    )(x2, yh, yg, p2, wo, wo, pp, pnw, gnw, gw, gb, fnw)


def kernel(x, p, norm_mix_w, w_in, hgrn_lb, gla_gate_w2, gla_gate_b, hgrn_norm_w, gla_norm_w,
           w_out, ple_proj, ple_norm_w, ple_gate_norm_w, ple_gate_w, ple_gate_b, final_norm_w):
    B, T, D = x.shape
    bf = jnp.bfloat16
    x2 = x.reshape(B * T, D)
    p2 = p[0].reshape(B * T, PLE_DIM)
    row = lambda a: a.reshape(1, -1)

    w_t = jnp.swapaxes(w_in[0], 0, 1)
    nw = row(norm_mix_w[0])
    hf, xb, rs = _inproj_first(x2, nw, w_t, 1, 0.5, "inproj_f")
    pb = _inproj_rest(xb, rs, nw, w_t, lambda j: j + jnp.minimum(j, 1), 6, (0, 2, 5), bf,
                      "inproj_b")

    y_h, y_g, wo_bf, pgw_bf, pp_bf = _scans(hf, pb, hgrn_lb, gla_gate_w2[0], row(gla_gate_b[0]), xb, rs, w_t, nw,
                                            w_out[0], row(hgrn_norm_w[0]), row(gla_norm_w[0]),
                                            ple_gate_w[0], ple_proj[0])

    out = _outproj(x2, y_h, y_g, p2, wo_bf, pp_bf, row(ple_norm_w[0]), row(ple_gate_norm_w[0]), pgw_bf,
                   row(ple_gate_b[0]), row(final_norm_w))
    return out.reshape(B, T, D)
```

```python
import functools

import jax
import jax.numpy as jnp
from jax import lax
from jax.experimental import pallas as pl
from jax.experimental.pallas import tpu as pltpu

D_MODEL = 2048
D_HGRN = 1024
HGRN_HEADS = 8
HGRN_DK = 128
HGRN_DV = 128
GLA_HEADS = 4
GLA_DK = 128
GLA_DV = 256
GLA_DK_TOTAL = 512
D_GLA = 1024
GLA_GATE_RANK = 16
GLA_GATE_TAU = 16.0
PLE_DIM = 256
EPS = 1e-6

LANE = 128
SUBLANE = 8
D_MAIN = 4 * D_HGRN + 2 * GLA_DK_TOTAL + 2 * D_GLA
PANEL = 1024
CHUNK = 128
ATTN_BLOCK = 128
DECAY_RANGE_LIMIT = 110.0
LOG2E = 1.4426950408889634
SCAN_VMEM_LIMIT = 56 * 1024 * 1024
INPROJ_VMEM_LIMIT = 62 * 1024 * 1024

_NT = (((1,), (1,)), ((), ()))
_TN = (((0,), (0,)), ((), ()))


def _sigmoid(x):
    return 0.5 + 0.5 * jnp.tanh(0.5 * x)


def _silu_of_half(h):
    return h + h * jnp.tanh(h)


def _rms(x, w):
    return x * lax.rsqrt(jnp.mean(x * x, axis=-1, keepdims=True) + EPS) * w


def _is_half_panel(half_panels):
    return functools.reduce(jnp.logical_or, [pl.program_id(0) == jp for jp in half_panels])


def _cast_weight_panel(w_ref, nw_ref, wb_ref):
    @pl.when(pl.program_id(1) == 0)
    def _():
        wb_ref[...] = (w_ref[...] * nw_ref[...]).astype(jnp.bfloat16)


X_DMA_PRIORITY = 1
X_SLOTS = 3


def _inproj_first_kernel(x_hbm, nw_ref, w_ref, o_ref, xb_ref, rs_ref, wb_ref, xbuf_ref, sem, *, scale):
    bf = jnp.bfloat16
    i = pl.program_id(0)
    n = pl.num_programs(0)
    tm = xbuf_ref.shape[1]
    ahead = X_SLOTS - 1

    def tile_copy(step):
        row0 = step * tm if isinstance(step, int) else pl.multiple_of(step * tm, tm)
        slot = step % X_SLOTS
        return pltpu.make_async_copy(x_hbm.at[pl.ds(row0, tm), :], xbuf_ref.at[slot], sem.at[slot])

    @pl.when(i == 0)
    def _():
        for step in range(ahead):
            tile_copy(step).start(priority=X_DMA_PRIORITY)
        wb_ref[...] = (w_ref[...] * nw_ref[...]).astype(bf)

    @pl.when(i + ahead < n)
    def _():
        tile_copy(i + ahead).start(priority=X_DMA_PRIORITY)

    tile_copy(i).wait()
    x = xbuf_ref[i % X_SLOTS]
    r = lax.rsqrt(jnp.mean(x * x, axis=-1, keepdims=True) + EPS)
    xb = x.astype(bf)
    o_ref[...] = lax.dot_general(xb, wb_ref[...], _NT, preferred_element_type=jnp.float32) * (r * scale)
    xb_ref[...] = xb
    rs_ref[...] = jnp.broadcast_to(r, rs_ref.shape)


def _inproj_rest_kernel(xb_ref, rs_ref, nw_ref, w_ref, o_ref, wb_ref, *, half_panels):
    _cast_weight_panel(w_ref, nw_ref, wb_ref)
    r = rs_ref[:, 0:1] * jnp.where(_is_half_panel(half_panels), 0.5, 1.0)
    o_ref[...] = (lax.dot_general(xb_ref[...], wb_ref[...], _NT, preferred_element_type=jnp.float32)
                  * r).astype(o_ref.dtype)


def _inproj_params():
    return pltpu.CompilerParams(dimension_semantics=("parallel", "arbitrary"),
                                vmem_limit_bytes=INPROJ_VMEM_LIMIT)


def _inproj_first(x2, norm_w, w_t, panel, scale, name, tm=1024):
    T, D = x2.shape
    assert T % tm == 0 and T // tm >= X_SLOTS - 1
    return pl.pallas_call(
        functools.partial(_inproj_first_kernel, scale=scale),
        grid=(T // tm,),
        in_specs=[pl.BlockSpec(memory_space=pl.ANY),
                  pl.BlockSpec((1, D), lambda i: (0, 0)),
                  pl.BlockSpec((PANEL, D), lambda i: (panel, 0), pipeline_mode=pl.Buffered(1))],
        out_specs=[pl.BlockSpec((tm, PANEL), lambda i: (i, 0)),
                   pl.BlockSpec((tm, D), lambda i: (i, 0)),
                   pl.BlockSpec((tm, LANE), lambda i: (i, 0))],
        out_shape=[jax.ShapeDtypeStruct((T, PANEL), jnp.float32),
                   jax.ShapeDtypeStruct((T, D), jnp.bfloat16),
                   jax.ShapeDtypeStruct((T, LANE), jnp.float32)],
        scratch_shapes=[pltpu.VMEM((PANEL, D), jnp.bfloat16),
                        pltpu.VMEM((X_SLOTS, tm, D), jnp.float32),
                        pltpu.SemaphoreType.DMA((X_SLOTS,))],
        compiler_params=pltpu.CompilerParams(dimension_semantics=("arbitrary",),
                                             vmem_limit_bytes=INPROJ_VMEM_LIMIT),
        name=name,
    )(x2, norm_w, w_t)


def _inproj_rest(xb, rs, norm_w, w_t, panel_of, n_panels, half_panels, out_dtype, name, tm=2048):
    T, D = xb.shape
    assert T % tm == 0
    return pl.pallas_call(
        functools.partial(_inproj_rest_kernel, half_panels=half_panels),
        grid=(n_panels, T // tm),
        in_specs=[pl.BlockSpec((tm, D), lambda j, i: (i, 0)),
                  pl.BlockSpec((tm, LANE), lambda j, i: (i, 0)),
                  pl.BlockSpec((1, D), lambda j, i: (0, 0)),
                  pl.BlockSpec((PANEL, D), lambda j, i: (panel_of(j), 0))],
        out_specs=pl.BlockSpec((tm, PANEL), lambda j, i: (i, j)),
        out_shape=jax.ShapeDtypeStruct((T, n_panels * PANEL), out_dtype),
        scratch_shapes=[pltpu.VMEM((PANEL, D), jnp.bfloat16)],
        compiler_params=_inproj_params(),
        name=name,
    )(xb, rs, norm_w, w_t)


def _scan_stages(qkg, v, state, finish):
    bf = jnp.bfloat16
    f32 = jnp.float32
    dv, dk = state.shape
    blk = ATTN_BLOCK
    row = lax.broadcasted_iota(jnp.int32, (blk, blk), 0)
    col = lax.broadcasted_iota(jnp.int32, (blk, blk), 1)
    causal = (col <= row) & (col >= (row & -CHUNK))
    res = {}

    def gates0():
        res['qkg'] = qkg()

    def gates():
        q, k, g = res.pop('qkg')
        g_hi = g.astype(bf)
        g_lo = (g - g_hi.astype(f32)).astype(bf)
        tri = causal.astype(bf)
        tri2 = jnp.concatenate([tri, tri], axis=1)
        res.update(q=q, k=k, n_blk=g.shape[0] // blk, n_chk=g.shape[0] // CHUNK)
        res['b'] = [jnp.dot(tri2, jnp.concatenate([g_hi[r * blk:(r + 1) * blk],
                                                     g_lo[r * blk:(r + 1) * blk]], axis=0),
                            preferred_element_type=f32) for r in range(res['n_blk'])]

    def decay():
        b = jnp.concatenate(res.pop('b'), axis=0)
        eb = jnp.exp2(b)
        q, k = res.pop('q'), res.pop('k')
        res['qh'] = q * eb.astype(bf) if q.dtype == bf else (q * eb).astype(bf)
        res['kt'] = k * (1.0 / eb).astype(bf) if k.dtype == bf else (k / eb).astype(bf)
        res['vb'] = v().astype(bf)
        res['eb_last'] = [eb[(c + 1) * CHUNK - 1:(c + 1) * CHUNK, :] for c in range(res['n_chk'])]
        res['b_min'] = jnp.min(jnp.concatenate(
            [b[(c + 1) * CHUNK - 1:(c + 1) * CHUNK, :] for c in range(res['n_chk'])], axis=0))

    def scores():
        qh, kt, vb = res['qh'], res['kt'], res['vb']
        res['attn'] = [lax.dot_general(qh[r * blk:(r + 1) * blk], kt[r * blk:(r + 1) * blk], _NT,
                                       preferred_element_type=f32) for r in range(res['n_blk'])]
        res['upd'] = [lax.dot_general(vb[c * CHUNK:(c + 1) * CHUNK], kt[c * CHUNK:(c + 1) * CHUNK], _TN,
                                      preferred_element_type=f32) for c in range(res['n_chk'])]

    def carry():
        res['attn'] = [jnp.where(causal, a, 0.0).astype(bf) for a in res['attn']]
        st, st_in = state, []
        for upd, eb_last in zip(res.pop('upd'), res.pop('eb_last')):
            st_in.append(st.astype(bf))
            st = (st + upd) * eb_last
        res.update(st_in=st_in, state=st)

    def outputs():
        qh, vb = res.pop('qh'), res.pop('vb')
        res['intra'] = [jnp.dot(a, vb[r * blk:(r + 1) * blk], preferred_element_type=f32)
                        for r, a in enumerate(res.pop('attn'))]
        res['inter'] = [lax.dot_general(qh[c * CHUNK:(c + 1) * CHUNK], s, _NT, preferred_element_type=f32)
                        for c, s in enumerate(res.pop('st_in'))]

    def done():
        finish(jnp.concatenate(res.pop('intra'), axis=0) + jnp.concatenate(res.pop('inter'), axis=0))

    return [gates0, gates, decay, scores, carry, outputs, done], res


def _scan_tile_exact(qkg_ref, v_ref, st_ref, o_ref):
    bf = jnp.bfloat16
    f32 = jnp.float32
    n_heads, dv, dk = st_ref.shape
    tt = o_ref.shape[0]
    rowk = lax.broadcasted_iota(jnp.int32, (CHUNK, dk), 0)
    rowc = lax.broadcasted_iota(jnp.int32, (CHUNK, CHUNK), 0)
    colc = lax.broadcasted_iota(jnp.int32, (CHUNK, CHUNK), 1)
    tri = (colc <= rowc).astype(bf)

    for h in range(n_heads):
        ks = slice(h * dk, (h + 1) * dk)
        vs = slice(h * dv, (h + 1) * dv)

        def chunk(c, carry, ks=ks, vs=vs, h=h):
            r0 = pl.multiple_of(c * CHUNK, CHUNK)
            rows = pl.ds(r0, CHUNK)
            qc = qkg_ref[0, rows, ks]
            kc = qkg_ref[1, rows, ks]
            gc = qkg_ref[2, rows, ks]
            g_hi = gc.astype(bf)
            g_lo = (gc - g_hi.astype(f32)).astype(bf)
            b = (jnp.dot(tri, g_hi, preferred_element_type=f32)
                 + jnp.dot(tri, g_lo, preferred_element_type=f32))
            qkg_ref[2, rows, ks] = b

            def columns(jg, attn):
                j0 = pl.multiple_of(jg * SUBLANE, SUBLANE)
                b_rows = qkg_ref[2, pl.ds(r0 + j0, SUBLANE), ks]
                k_rows = qkg_ref[1, pl.ds(r0 + j0, SUBLANE), ks]
                for r in range(SUBLANE):
                    j = j0 + r
                    decay = jnp.exp2(jnp.where(rowk >= j, b - b_rows[r:r + 1], 0.0))
                    colv = jnp.sum(qc * k_rows[r:r + 1] * decay, axis=1, keepdims=True)
                    attn = jnp.where(colc == j, colv, attn)
                return attn

            attn = lax.fori_loop(0, CHUNK // SUBLANE, columns, jnp.zeros((CHUNK, CHUNK), f32))
            attn = jnp.where(colc <= rowc, attn, 0.0).astype(bf)
            vb = v_ref[rows, vs].astype(bf)
            st = st_ref[h]
            b_last = b[CHUNK - 1:CHUNK, :]
            o_ref[rows, vs] = (
                jnp.dot(attn, vb, preferred_element_type=f32)
                + lax.dot_general((qc * jnp.exp2(b)).astype(bf), st.astype(bf), _NT,
                                  preferred_element_type=f32))
            k_dec = (kc * jnp.exp2(b_last - b)).astype(bf)
            st_ref[h] = st * jnp.exp2(b_last) + lax.dot_general(vb, k_dec, _TN,
                                                                preferred_element_type=f32)
            return carry

        lax.fori_loop(0, tt // CHUNK, chunk, 0)


def _readout(o, z, n_heads, dk):
    dv = o.shape[1] // n_heads

    def head(oh):
        return oh * lax.rsqrt(jnp.mean(oh * oh, axis=-1, keepdims=True) + EPS * dk)

    on = jnp.concatenate([head(o[:, h * dv:(h + 1) * dv]) for h in range(n_heads)], axis=1)
    return on.astype(z.dtype) * _silu_of_half(z)


def _scan_and_readout(groups, qkg_ref, o_ref, side_work=None):
    @pl.when(pl.program_id(1) == 0)
    def _():
        for *_, st_ref in groups:
            st_ref[...] = jnp.zeros_like(st_ref)

    if side_work is not None:
        side_work()
    units = []
    for qkg, v_ref, z_ref, y_ref, st_ref in groups:
        n_heads, dv, dk = st_ref.shape
        for h in range(n_heads):
            ks = slice(h * dk, (h + 1) * dk)
            vs = slice(h * dv, (h + 1) * dv)

            def finish(o, vs=vs, y_ref=y_ref, z_ref=z_ref, dk=dk):
                y_ref[:, vs] = _readout(o, z_ref[:, vs], 1, dk).astype(y_ref.dtype)

            units.append(_scan_stages(lambda qkg=qkg, ks=ks: qkg(ks), lambda v_ref=v_ref, vs=vs: v_ref[:, vs],
                                      st_ref[h], finish))
    n_stages = len(units[0][0])
    for tick in range(len(units) + n_stages - 1):
        for u in range(len(units)):
            if 0 <= tick - u < n_stages:
                units[u][0][tick - u]()
    b_min = units[0][1]['b_min']
    for _, res in units[1:]:
        b_min = jnp.minimum(b_min, res['b_min'])
    in_range = b_min >= -DECAY_RANGE_LIMIT

    @pl.when(in_range)
    def _():
        new_states = iter(res['state'] for _, res in units)
        for *_, st_ref in groups:
            for h in range(st_ref.shape[0]):
                st_ref[h] = next(new_states)

    @pl.when(jnp.logical_not(in_range))
    def _():
        for qkg, v_ref, z_ref, y_ref, st_ref in groups:
            n_heads, dv, dk = st_ref.shape
            q, k, g = qkg(slice(0, n_heads * dk))
            qkg_ref[0, :, :n_heads * dk] = q.astype(jnp.float32)
            qkg_ref[1, :, :n_heads * dk] = k.astype(jnp.float32)
            qkg_ref[2, :, :n_heads * dk] = g
            _scan_tile_exact(qkg_ref, v_ref, st_ref, o_ref)
            y_ref[...] = _readout(o_ref[:, :n_heads * dv], z_ref[...], n_heads, dk).astype(y_ref.dtype)


def _scan_kernel(lb_ref, hq_ref, hf_ref, hi_ref, hz_ref, gq_ref, gk_ref, gv_ref, gz_ref, w2_ref, gb_ref,
                 xb_ref, rs_ref, wlr_ref, nw_ref, wo_ref, hgain_ref, ggain_ref, pgw_ref, pp_ref,
                 yh_ref, yg_ref, wo_bf_ref, pgw_bf_ref, pp_bf_ref, sth_ref, stg_ref, qkg_ref, o_ref):
    bf = jnp.bfloat16
    side = {}
    rank = wlr_ref.shape[0]

    def pad_rank(a):
        return jnp.concatenate([a, jnp.zeros((LANE - rank, a.shape[1]), a.dtype)], axis=0)

    def side_work():
        wlr = pad_rank((wlr_ref[...] * nw_ref[...]).astype(bf))
        side['lr'] = (lax.dot_general(xb_ref[...], wlr, _NT, preferred_element_type=jnp.float32)
                      * rs_ref[:, 0:1]).astype(bf)
        n_rows = wo_ref.shape[0]
        gain = jnp.where(pl.program_id(1) < D_HGRN // n_rows, hgain_ref[...], ggain_ref[...])
        diag = (lax.broadcasted_iota(jnp.int32, (n_rows, n_rows), 0)
                == lax.broadcasted_iota(jnp.int32, (n_rows, n_rows), 1))
        gain_col = jnp.sum(jnp.where(diag, gain, 0.0), axis=1, keepdims=True)
        wo_bf_ref[...] = (wo_ref[...] * gain_col).astype(bf)
        pgw_bf_ref[...] = pgw_ref[...].astype(bf)
        pp_bf_ref[...] = pp_ref[...].astype(bf)

    def hgrn_qkg(cols):
        lbp = lb_ref[:, cols]
        e = jnp.exp(lbp - jnp.max(lbp, axis=0, keepdims=True))
        lb = e[0:1, :] / jnp.sum(e, axis=0, keepdims=True)
        f = (0.5 + 0.5 * lb) + (0.5 - 0.5 * lb) * jnp.tanh(hf_ref[:, cols])
        return _silu_of_half(hq_ref[:, cols]), 1.0 - f, jnp.log2(f)

    def gla_qkg(cols):
        logit = jnp.dot(side['lr'], pad_rank(w2_ref[:, cols].astype(bf)),
                        preferred_element_type=jnp.float32) + gb_ref[:, cols]
        log2_sig = (jnp.minimum(logit, 0.0) * LOG2E
                    - jnp.log2(1.0 + jnp.exp2(jnp.abs(logit) * -LOG2E)))
        return gq_ref[:, cols], gk_ref[:, cols], log2_sig / GLA_GATE_TAU

    _scan_and_readout([(hgrn_qkg, hi_ref, hz_ref, yh_ref, sth_ref),
                       (gla_qkg, gv_ref, gz_ref, yg_ref, stg_ref)], qkg_ref, o_ref, side_work)


def _row_slices(n):
    def spec(a):
        assert a.shape[0] % (16 * n) == 0
        return pl.BlockSpec((a.shape[0] // n, a.shape[1]), lambda h, t: (t, 0))
    return spec


def _scans(hf, pb, hgrn_lb, w2, gate_b, xb, rs, w_t, norm_w, w_out, hgrn_gain, gla_gain, pgw, pp, tt=512):
    T, D = xb.shape
    assert T % tt == 0 and tt % ATTN_BLOCK == 0 and ATTN_BLOCK % CHUNK == 0
    n = T // tt
    wo_rows = w_out.shape[0] // n
    assert wo_rows % LANE == 0 and D_HGRN % wo_rows == 0 and D_GLA % wo_rows == 0
    assert D_MAIN % GLA_GATE_RANK == 0 and w2.shape[0] == GLA_GATE_RANK
    n_hgrn = D_HGRN // wo_rows
    gate_rows = pl.BlockSpec((GLA_GATE_RANK, D), lambda h, t: (D_MAIN // GLA_GATE_RANK, 0))
    panel = lambda c: pl.BlockSpec((tt, PANEL), lambda h, t: (t, c))
    half = lambda c: pl.BlockSpec((tt, GLA_DK_TOTAL), lambda h, t: (t, c))
    rows = lambda width: pl.BlockSpec((tt, width), lambda h, t: (t, 0))
    const = lambda a: pl.BlockSpec(a.shape, lambda h, t: (0, 0))
    piece = _row_slices(n)
    gq0 = 3 * PANEL // GLA_DK_TOTAL
    return pl.pallas_call(
        _scan_kernel,
        grid=(1, n),
        in_specs=[const(hgrn_lb), panel(0), panel(0), panel(1), panel(2),
                  half(gq0), half(gq0 + 1), panel(4), panel(5), const(w2), const(gate_b),
                  rows(D), rows(LANE), gate_rows, const(norm_w),
                  piece(w_out),
                  pl.BlockSpec((1, wo_rows), lambda h, t: (0, jnp.minimum(t, n_hgrn - 1))),
                  pl.BlockSpec((1, wo_rows), lambda h, t: (0, jnp.maximum(t - n_hgrn, 0))),
                  piece(pgw), piece(pp)],
        out_specs=[rows(D_HGRN), rows(D_GLA), piece(w_out), piece(pgw), piece(pp)],
        out_shape=[jax.ShapeDtypeStruct((T, D_HGRN), jnp.bfloat16),
                   jax.ShapeDtypeStruct((T, D_GLA), jnp.bfloat16),
                   jax.ShapeDtypeStruct(w_out.shape, jnp.bfloat16),
                   jax.ShapeDtypeStruct(pgw.shape, jnp.bfloat16),
                   jax.ShapeDtypeStruct(pp.shape, jnp.bfloat16)],
        scratch_shapes=[pltpu.VMEM((HGRN_HEADS, HGRN_DV, HGRN_DK), jnp.float32),
                        pltpu.VMEM((GLA_HEADS, GLA_DV, GLA_DK), jnp.float32),
                        pltpu.VMEM((3, tt, D_HGRN), jnp.float32),
                        pltpu.VMEM((tt, max(D_HGRN, D_GLA)), jnp.float32)],
        compiler_params=pltpu.CompilerParams(dimension_semantics=("parallel", "arbitrary"),
                                             vmem_limit_bytes=SCAN_VMEM_LIMIT),
        name="scans",
    )(hgrn_lb, pb, hf, pb, pb, pb, pb, pb, pb, w2, gate_b, xb, rs, w_t, norm_w,
      w_out, hgrn_gain, gla_gain, pgw, pp)


OUT_SUB = 2
OUT_VMEM_LIMIT = 58 * 1024 * 1024


def _out_kernel(x_ref, yh_ref, yg_ref, p_ref, woh_ref, wog_ref, pp_ref, pnw_ref, gnw_ref,
                gw_ref, gb_ref, fnw_ref, o_ref):
    f32 = jnp.float32
    bf = jnp.bfloat16
    rows = o_ref.shape[0] // OUT_SUB
    subs = [slice(s * rows, (s + 1) * rows) for s in range(OUT_SUB)]
    h = [x_ref[r, :]
         + jnp.dot(yh_ref[r, :], woh_ref[...], preferred_element_type=f32)
         + jnp.dot(yg_ref[r, :], wog_ref[...], preferred_element_type=f32) for r in subs]
    e = [_rms(jnp.dot(p_ref[r, :].astype(bf), pp_ref[...], preferred_element_type=f32), pnw_ref[...])
         for r in subs]
    hn = [_rms(hs, gnw_ref[...]).astype(bf) for hs in h]
    acc = [jnp.dot(hs, gw_ref[...], preferred_element_type=f32) for hs in hn]
    for r, hs, es, a in zip(subs, h, e, acc):
        o_ref[r, :] = _rms(hs + _sigmoid(a + gb_ref[...]) * es, fnw_ref[...])


def _outproj(x2, yh, yg, p2, wo, pp, pnw, gnw, gw, gb, fnw, tm=512):
    T, D = x2.shape
    assert T % tm == 0 and tm % OUT_SUB == 0
    row = lambda w: pl.BlockSpec((tm, w), lambda i: (i, 0))
    once = pl.Buffered(1)
    full = lambda a: pl.BlockSpec(a.shape, lambda i: (0, 0), pipeline_mode=once)
    return pl.pallas_call(
        _out_kernel,
        grid=(T // tm,),
        in_specs=[row(D), row(D_HGRN), row(D_GLA), row(PLE_DIM),
                  pl.BlockSpec((D_HGRN, D), lambda i: (0, 0), pipeline_mode=once),
                  pl.BlockSpec((D_GLA, D), lambda i: (D_HGRN // D_GLA, 0), pipeline_mode=once),
                  full(pp), full(pnw), full(gnw), full(gw), full(gb), full(fnw)],
        out_specs=row(D),
        out_shape=jax.ShapeDtypeStruct((T, D), jnp.float32),
        compiler_params=pltpu.CompilerParams(
            dimension_semantics=("parallel",),
            vmem_limit_bytes=OUT_VMEM_LIMIT),
        name="outproj_ple",
    )(x2, yh, yg, p2, wo, wo, pp, pnw, gnw, gw, gb, fnw)


def kernel(x, p, norm_mix_w, w_in, hgrn_lb, gla_gate_w2, gla_gate_b, hgrn_norm_w, gla_norm_w,
           w_out, ple_proj, ple_norm_w, ple_gate_norm_w, ple_gate_w, ple_gate_b, final_norm_w):
    B, T, D = x.shape
    bf = jnp.bfloat16
    x2 = x.reshape(B * T, D)
    p2 = p[0].reshape(B * T, PLE_DIM)
    row = lambda a: a.reshape(1, -1)

    w_t = jnp.swapaxes(w_in[0], 0, 1)
    nw = row(norm_mix_w[0])
    hf, xb, rs = _inproj_first(x2, nw, w_t, 1, 0.5, "inproj_f")
    pb = _inproj_rest(xb, rs, nw, w_t, lambda j: j + jnp.minimum(j, 1), 6, (0, 2, 5), bf,
                      "inproj_b")

    y_h, y_g, wo_bf, pgw_bf, pp_bf = _scans(hf, pb, hgrn_lb, gla_gate_w2[0], row(gla_gate_b[0]), xb, rs, w_t, nw,
                                            w_out[0], row(hgrn_norm_w[0]), row(gla_norm_w[0]),
                                            ple_gate_w[0], ple_proj[0])

    out = _outproj(x2, y_h, y_g, p2, wo_bf, pp_bf, row(ple_norm_w[0]), row(ple_gate_norm_w[0]), pgw_bf,
                   row(ple_gate_b[0]), row(final_norm_w))
    return out.reshape(B, T, D)
```
